```python
import math
import jax, jax.numpy as jnp
from jax import lax
import numpy as np

D_MODEL = 1024
BATCH = 4
SEQ = 4096
DEPTH = 1
DEC_BATCH = 32
DEC_SEQ = 8
PAST_LEN = 8192
PAGE_SIZE = 128

HEAD_DIM = 64
HEADS_PER_GROUP = 4
ATTN_GROUPS = ((128, 1), (512, 4), (2048, 16))
N_HEADS = HEADS_PER_GROUP * len(ATTN_GROUPS)
ATTN_WIDTH = N_HEADS * HEAD_DIM
ATTN_OUT_WIDTH = HEADS_PER_GROUP * HEAD_DIM
N_BUCKETS = 32
MAX_DISTANCE = 2048
QUERY_BLOCK = 128
POOL_WINDOWS = (2, 4, 8, 16)
POOL_WIDTH = D_MODEL // 2
POOL_GROUP_WIDTH = POOL_WIDTH // len(POOL_WINDOWS)
POOL_STATE = max(POOL_WINDOWS) - 1
D_FF = ((8 * D_MODEL // 3 + 127) // 128) * 128
EPS = 1e-6
IN_WIDTH = 3 * ATTN_WIDTH + POOL_WIDTH + 2 * D_MODEL
IN_SPLITS = (ATTN_WIDTH, 2 * ATTN_WIDTH, 3 * ATTN_WIDTH,
             3 * ATTN_WIDTH + POOL_WIDTH, 3 * ATTN_WIDTH + POOL_WIDTH + D_MODEL)

kernel_name = "hybrid_dilated_attn_pool_macaron_step"


def _rmsnorm(x, g):
    xf = x.astype(jnp.float32)
    y = xf * lax.rsqrt(jnp.mean(xf * xf, axis=-1, keepdims=True) + EPS)
    return (y * g.astype(jnp.float32)).astype(x.dtype)


def _head_rmsnorm(x, g):
    xf = x.astype(jnp.float32)
    y = xf * lax.rsqrt(jnp.mean(xf * xf, axis=-1, keepdims=True) + EPS)
    return (y * g.astype(jnp.float32)[None, None]).astype(x.dtype)


def _swiglu(x, w_up, w_down):
    gate, up = jnp.split(x @ w_up, 2, axis=-1)
    return (jax.nn.silu(gate) * up) @ w_down


def _t5_buckets(distance):
    max_exact = N_BUCKETS // 2
    d = np.asarray(distance, dtype=np.int32)
    ratio = np.log(np.maximum(d, 1).astype(np.float32) / np.float32(max_exact))
    large = max_exact + (ratio / np.float32(math.log(MAX_DISTANCE / max_exact))
                         * (N_BUCKETS - max_exact)).astype(np.int32)
    large = np.minimum(large, N_BUCKETS - 1)
    return np.where(d < max_exact, d, large).astype(np.int32)


def _dilated_attention(q, k, v, past_kv, rel_bias_table):
    B, T = q.shape[:2]
    qb = math.gcd(T, QUERY_BLOCK)
    n_blocks = T // qb
    scale = HEAD_DIM ** -0.5
    neg = jnp.finfo(jnp.float32).min
    kv_new = jnp.stack([k, v], axis=2)
    kv_alls, offsets, biases, new_state = [], [], [], []
    for g, (win, dil) in enumerate(ATTN_GROUPS):
        kv_g = kv_new[:, :, :, g * HEADS_PER_GROUP:(g + 1) * HEADS_PER_GROUP]
        if past_kv is None:
            kv_all = kv_g
            keep = min(win, T)
        else:
            kv_all = jnp.concatenate([past_kv[g].astype(kv_g.dtype), kv_g], axis=1)
            keep = past_kv[g].shape[1]
        new_state.append(kv_all[:, kv_all.shape[1] - keep:])
        kv_alls.append(kv_all)
        offsets.append(kv_all.shape[1] - T)
        dist = dil * np.arange(win // dil + 1)
        biases.append(rel_bias_table[_t5_buckets(dist)][:, g * HEADS_PER_GROUP:(g + 1) * HEADS_PER_GROUP]
                      .astype(jnp.float32))

    def block(i):
        start = i * qb
        q_blk = lax.dynamic_slice_in_dim(q, start, qb, axis=1)
        outs, lses = [], []
        for g, (win, dil) in enumerate(ATTN_GROUPS):
            n_keys = win // dil + 1
            idx = offsets[g] + start + jnp.arange(qb)[:, None] - dil * jnp.arange(n_keys)[None, :]
            valid = idx >= 0
            kv_blk = jnp.take(kv_alls[g], jnp.maximum(idx, 0), axis=1)
            qg = q_blk[:, :, g * HEADS_PER_GROUP:(g + 1) * HEADS_PER_GROUP].astype(jnp.float32)
            logits = jnp.einsum('bqhd,bqjhd->bqjh', qg, kv_blk[:, :, :, 0].astype(jnp.float32)) * scale
            logits = jnp.where(valid[None, :, :, None], logits + biases[g][None, None], neg)
            m = jnp.max(logits, axis=2, keepdims=True)
            p = jnp.exp(logits - m)
            denom = jnp.sum(p, axis=2)
            o = jnp.einsum('bqjh,bqjhd->bqhd', p, kv_blk[:, :, :, 1].astype(jnp.float32)) / denom[..., None]
            outs.append(o)
            lses.append(m[:, :, 0] + jnp.log(denom))
        wts = jax.nn.softmax(jnp.stack(lses, axis=0), axis=0)
        o = jnp.sum(wts[..., None] * jnp.stack(outs, axis=0), axis=0)
        return o.astype(q.dtype)

    o = lax.map(block, jnp.arange(n_blocks))
    o = o.transpose(1, 0, 2, 3, 4).reshape(B, T, ATTN_OUT_WIDTH)
    return o, new_state


def _multiscale_pool(u, past, pos0, w_group, scale):
    B, T, C = u.shape
    ue = jnp.concatenate([past.astype(u.dtype), u], axis=1)
    cs = jnp.cumsum(ue.astype(jnp.float32), axis=1)
    cs = jnp.concatenate([jnp.zeros((B, 1, C), jnp.float32), cs], axis=1)
    pos = pos0 + jnp.arange(T)
    outs = []
    for gi, win in enumerate(POOL_WINDOWS):
        sl = slice(gi * POOL_GROUP_WIDTH, (gi + 1) * POOL_GROUP_WIDTH)
        s = cs[:, POOL_STATE + 1:, sl] - cs[:, POOL_STATE + 1 - win:POOL_STATE + 1 - win + T, sl]
        cnt = jnp.minimum(pos + 1, win).astype(jnp.float32)[None, :, None]
        outs.append(s / cnt - u[:, :, sl].astype(jnp.float32))
    d = jnp.stack(outs, axis=2)
    y = jnp.einsum('btgc,gcd->btgd', d, w_group.astype(jnp.float32)).reshape(B, T, C)
    y = y * scale.astype(jnp.float32)
    return y.astype(u.dtype), ue[:, ue.shape[1] - POOL_STATE:]


def _layer(x, pos0, past_kv, past_pool, w, rel_bias_table):
    B, T, _ = x.shape
    x = x + 0.5 * _swiglu(_rmsnorm(x, w["norm_ffn1"]), w["ffn1_w_up"], w["ffn1_w_down"])
    h = _rmsnorm(x, w["norm_mix"])
    q, k, v, u, g_a, g_b = jnp.split(h @ w["w_in"], IN_SPLITS, axis=-1)
    q = _head_rmsnorm(q.reshape(B, T, N_HEADS, HEAD_DIM), w["q_norm"])
    k = _head_rmsnorm(k.reshape(B, T, N_HEADS, HEAD_DIM), w["k_norm"])
    v = v.reshape(B, T, N_HEADS, HEAD_DIM)
    attn, new_kv = _dilated_attention(q, k, v, past_kv, rel_bias_table)
    pooled, new_pool = _multiscale_pool(u, past_pool, pos0, w["pool_w_group"], w["pool_scale"])
    merged = (jax.nn.sigmoid(g_a) * (attn @ w["w_attn_branch"])
              + jax.nn.sigmoid(g_b) * (pooled @ w["w_pool_branch"]))
    x = x + merged @ w["w_out"]
    x = x + 0.5 * _swiglu(_rmsnorm(x, w["norm_ffn2"]), w["ffn2_w_up"], w["ffn2_w_down"])
    return x, new_kv, new_pool


def setup_inputs(seed: int = 0) -> dict:
    key = jax.random.key(seed)
    ks = jax.random.split(key, 24)
    f32 = jnp.float32

    def nrm(k, shape, s=1.0):
        return jax.random.normal(k, shape, f32) * s

    def gain(k, shape):
        return 1.0 + 0.05 * jax.random.normal(k, shape, f32)

    win_len = [min(win, PAST_LEN) for win, _ in ATTN_GROUPS]
    return {
        "x_prompt": nrm(ks[0], (BATCH, SEQ, D_MODEL)),
        "x_sample": nrm(ks[1], (DEC_BATCH, DEC_SEQ, D_MODEL)),
        "cache_kv_w128": nrm(ks[2], (DEPTH, DEC_BATCH, win_len[0], 2, HEADS_PER_GROUP, HEAD_DIM)),
        "cache_kv_w512": nrm(ks[3], (DEPTH, DEC_BATCH, win_len[1], 2, HEADS_PER_GROUP, HEAD_DIM)),
        "cache_kv_w2048": nrm(ks[4], (DEPTH, DEC_BATCH, win_len[2], 2, HEADS_PER_GROUP, HEAD_DIM)),
        "state_pool": nrm(ks[5], (DEPTH, DEC_BATCH, POOL_STATE, POOL_WIDTH)),
        "rel_bias_table": nrm(ks[6], (N_BUCKETS, N_HEADS), 0.5),
        "norm_ffn1": gain(ks[7], (DEPTH, D_MODEL)),
        "ffn1_w_up": nrm(ks[8], (DEPTH, D_MODEL, 2 * D_FF), D_MODEL ** -0.5),
        "ffn1_w_down": nrm(ks[9], (DEPTH, D_FF, D_MODEL), D_FF ** -0.5),
        "norm_mix": gain(ks[10], (DEPTH, D_MODEL)),
        "w_in": nrm(ks[11], (DEPTH, D_MODEL, IN_WIDTH), D_MODEL ** -0.5),
        "q_norm": gain(ks[12], (DEPTH, N_HEADS, HEAD_DIM)),
        "k_norm": gain(ks[13], (DEPTH, N_HEADS, HEAD_DIM)),
        "pool_w_group": nrm(ks[14], (DEPTH, len(POOL_WINDOWS), POOL_GROUP_WIDTH, POOL_GROUP_WIDTH),
                            POOL_GROUP_WIDTH ** -0.5),
        "pool_scale": gain(ks[15], (DEPTH, POOL_WIDTH)),
        "w_attn_branch": nrm(ks[16], (DEPTH, ATTN_OUT_WIDTH, D_MODEL), ATTN_OUT_WIDTH ** -0.5),
        "w_pool_branch": nrm(ks[17], (DEPTH, POOL_WIDTH, D_MODEL), POOL_WIDTH ** -0.5),
        "w_out": nrm(ks[18], (DEPTH, D_MODEL, D_MODEL), D_MODEL ** -0.5),
        "norm_ffn2": gain(ks[19], (DEPTH, D_MODEL)),
        "ffn2_w_up": nrm(ks[20], (DEPTH, D_MODEL, 2 * D_FF), D_MODEL ** -0.5),
        "ffn2_w_down": nrm(ks[21], (DEPTH, D_FF, D_MODEL), D_FF ** -0.5),
    }


def reference(x_prompt, x_sample, cache_kv_w128, cache_kv_w512, cache_kv_w2048, state_pool,
              rel_bias_table, norm_ffn1, ffn1_w_up, ffn1_w_down, norm_mix, w_in, q_norm, k_norm,
              pool_w_group, pool_scale, w_attn_branch, w_pool_branch, w_out,
              norm_ffn2, ffn2_w_up, ffn2_w_down):
    xp, xs = x_prompt, x_sample
    kv_p, kv_s = ([], [], []), ([], [], [])
    pool_p, pool_s = [], []
    for layer in range(DEPTH):
        w = dict(norm_ffn1=norm_ffn1[layer], ffn1_w_up=ffn1_w_up[layer], ffn1_w_down=ffn1_w_down[layer],
                 norm_mix=norm_mix[layer], w_in=w_in[layer], q_norm=q_norm[layer], k_norm=k_norm[layer],
                 pool_w_group=pool_w_group[layer], pool_scale=pool_scale[layer],
                 w_attn_branch=w_attn_branch[layer], w_pool_branch=w_pool_branch[layer],
                 w_out=w_out[layer], norm_ffn2=norm_ffn2[layer], ffn2_w_up=ffn2_w_up[layer],
                 ffn2_w_down=ffn2_w_down[layer])
        zero_pool = jnp.zeros((xp.shape[0], POOL_STATE, POOL_WIDTH), xp.dtype)
        xp, nkv_p, npool_p = _layer(xp, 0, None, zero_pool, w, rel_bias_table)
        past = (cache_kv_w128[layer], cache_kv_w512[layer], cache_kv_w2048[layer])
        xs, nkv_s, npool_s = _layer(xs, PAST_LEN, past, state_pool[layer], w, rel_bias_table)
        for g in range(len(ATTN_GROUPS)):
            kv_p[g].append(nkv_p[g])
            kv_s[g].append(nkv_s[g])
        pool_p.append(npool_p)
        pool_s.append(npool_s)
    kv128_prompt = jnp.stack(kv_p[0], axis=0)
    kv512_prompt = jnp.stack(kv_p[1], axis=0)
    kv2048_prompt = jnp.stack(kv_p[2], axis=0)
    kv128_sample = jnp.stack(kv_s[0], axis=0)
    kv512_sample = jnp.stack(kv_s[1], axis=0)
    kv2048_sample = jnp.stack(kv_s[2], axis=0)
    pool_prompt = jnp.stack(pool_p, axis=0)
    pool_sample = jnp.stack(pool_s, axis=0)
    return (xp, xs, kv128_prompt, kv512_prompt, kv2048_prompt, pool_prompt,
            kv128_sample, kv512_sample, kv2048_sample, pool_sample)
```

```python
import functools
import math

import numpy as np
import jax
import jax.numpy as jnp
from jax import lax
from jax.experimental import pallas as pl
from jax.experimental.pallas import tpu as pltpu

HEAD_DIM = 64
HEADS_PER_GROUP = 4
GROUP_WIDTH = HEADS_PER_GROUP * HEAD_DIM
ATTN_GROUPS = ((128, 1), (512, 4), (2048, 16))
N_GROUPS = len(ATTN_GROUPS)
ATTN_WIDTH = N_GROUPS * GROUP_WIDTH
N_BUCKETS = 32
MAX_DISTANCE = 2048
POOL_WINDOWS = (2, 4, 8, 16)
POOL_STATE = max(POOL_WINDOWS) - 1
POOL_HALO = 16
EPS = 1e-6
PAST_LEN = 8192
NEG = -1e30
QBLK = 128
PAD_KEYS = 128
VMEM_LIMIT = 50 * 1024 * 1024

F32 = jnp.float32
BF16 = jnp.bfloat16


def _t5_buckets(distance):
    max_exact = N_BUCKETS // 2
    d = np.asarray(distance, dtype=np.int32)
    ratio = np.log(np.maximum(d, 1).astype(np.float32) / np.float32(max_exact))
    large = max_exact + (ratio / np.float32(math.log(MAX_DISTANCE / max_exact))
                         * (N_BUCKETS - max_exact)).astype(np.int32)
    large = np.minimum(large, N_BUCKETS - 1)
    return np.where(d < max_exact, d, large).astype(np.int32)


def _params(n_grid_dims=1):
    return pltpu.CompilerParams(dimension_semantics=("arbitrary",) * n_grid_dims,
                                vmem_limit_bytes=VMEM_LIMIT)


def _resident(shape):
    return pl.BlockSpec(shape, lambda *_: (0,) * len(shape), pipeline_mode=pl.Buffered(1))


def _rmsnorm(x, g):
    ms = jnp.mean(x * x, axis=-1, keepdims=True)
    return x * lax.rsqrt(ms + EPS) * g


def _head_select(parts, lane_head):
    out = jnp.where(lane_head == 0, parts[0], 0.0)
    for h in range(1, HEADS_PER_GROUP):
        out = jnp.where(lane_head == h, parts[h], out)
    return out


def _ffn_kernel(x_ref, g_ref, wup_ref, wdn_ref, o_ref, act_ref, *, d_ff, chunk):
    x = x_ref[...]
    h = _rmsnorm(x, g_ref[...]).astype(BF16)
    for c in range(d_ff // chunk):
        lo = c * chunk
        gate = jnp.dot(h, wup_ref[:, lo:lo + chunk], preferred_element_type=F32)
        up = jnp.dot(h, wup_ref[:, d_ff + lo:d_ff + lo + chunk], preferred_element_type=F32)
        act_ref[:, lo:lo + chunk] = (gate * jax.nn.sigmoid(gate) * up).astype(BF16)
    o_ref[...] = x + 0.5 * jnp.dot(act_ref[...], wdn_ref[...], preferred_element_type=F32)


def _ffn(x, gain, w_up, w_down, tm):
    m, d = x.shape
    d_ff = w_down.shape[0]
    chunk = 256
    assert m % tm == 0 and d_ff % chunk == 0
    return pl.pallas_call(
        functools.partial(_ffn_kernel, d_ff=d_ff, chunk=chunk),
        out_shape=jax.ShapeDtypeStruct((m, d), F32),
        grid=(m // tm,),
        in_specs=[pl.BlockSpec((tm, d), lambda i: (i, 0)),
                  _resident((1, d)), _resident((d, 2 * d_ff)), _resident((d_ff, d))],
        out_specs=pl.BlockSpec((tm, d), lambda i: (i, 0)),
        scratch_shapes=[pltpu.VMEM((tm, d_ff), BF16)],
        compiler_params=_params(),
        name="ffn",
    )(x, gain, w_up, w_down)


def _proj_kernel(x_ref, g_ref, win_ref, qn_ref, kn_ref, seg_ref,
                 q_ref, k_ref, v_ref, u_ref, ga_ref, gb_ref, kv0_ref, kv1_ref, kv2_ref, *pool_ref,
                 kv_rows, pool_width, d_model):
    tm = x_ref.shape[0]
    h = _rmsnorm(x_ref[...], g_ref[...]).astype(BF16)

    def proj(lo, width):
        return jnp.dot(h, win_ref[:, lo:lo + width], preferred_element_type=F32)

    def head_norm(y, gain):
        ms = jnp.dot((y * y).astype(BF16), seg_ref[...], preferred_element_type=F32)
        return y * lax.rsqrt(ms + EPS) * gain

    for g, kv_ref in enumerate((kv0_ref, kv1_ref, kv2_ref)):
        lo = g * GROUP_WIDTH
        cols = slice(lo, lo + GROUP_WIDTH)
        qg = head_norm(proj(lo, GROUP_WIDTH), qn_ref[:, cols]) * (HEAD_DIM ** -0.5)
        kg = head_norm(proj(ATTN_WIDTH + lo, GROUP_WIDTH), kn_ref[:, cols])
        vg = proj(2 * ATTN_WIDTH + lo, GROUP_WIDTH)
        q_ref[:, cols] = qg.astype(BF16)
        k_ref[:, cols] = kg.astype(BF16)
        v_ref[:, cols] = vg.astype(BF16)
        r = kv_rows[g]
        kv_ref[0, :, 0:GROUP_WIDTH] = kg[tm - r:, :]
        kv_ref[0, :, GROUP_WIDTH:2 * GROUP_WIDTH] = vg[tm - r:, :]
    u = proj(3 * ATTN_WIDTH, pool_width)
    u_ref[...] = u
    if pool_ref:
        pool_ref[0][0] = u[tm - POOL_HALO:, :]
    ga_ref[...] = jax.nn.sigmoid(proj(3 * ATTN_WIDTH + pool_width, d_model)).astype(BF16)
    gb_ref[...] = jax.nn.sigmoid(proj(3 * ATTN_WIDTH + pool_width + d_model, d_model)).astype(BF16)


def _proj(x, gain, w_in, q_gain, k_gain, seg, *, tm, n_seq, keep_rows, with_pool_state):
    m, d = x.shape
    pool_width = w_in.shape[1] - 3 * ATTN_WIDTH - 2 * d
    seq_len = m // n_seq
    assert seq_len % tm == 0
    blocks_per_seq = seq_len // tm
    kv_rows = tuple(min(r, tm) for r in keep_rows)

    def kv_spec(g):
        n_blocks = keep_rows[g] // kv_rows[g]
        first = blocks_per_seq - n_blocks

        def index(i):
            return (i // blocks_per_seq, jnp.maximum(i % blocks_per_seq - first, 0), 0)
        return pl.BlockSpec((1, kv_rows[g], 2 * GROUP_WIDTH), index)

    row = lambda w: pl.BlockSpec((tm, w), lambda i: (i, 0))
    out_shape = [jax.ShapeDtypeStruct((m, ATTN_WIDTH), BF16)] * 3 + [
        jax.ShapeDtypeStruct((m, pool_width), F32),
        jax.ShapeDtypeStruct((m, d), BF16), jax.ShapeDtypeStruct((m, d), BF16)] + [
        jax.ShapeDtypeStruct((n_seq, keep_rows[g], 2 * GROUP_WIDTH), F32) for g in range(N_GROUPS)]
    out_specs = [row(ATTN_WIDTH)] * 3 + [row(pool_width), row(d), row(d)] + [
        kv_spec(g) for g in range(N_GROUPS)]
    if with_pool_state:
        out_shape.append(jax.ShapeDtypeStruct((n_seq, POOL_HALO, pool_width), F32))
        out_specs.append(pl.BlockSpec((1, POOL_HALO, pool_width),
                                      lambda i: (i // blocks_per_seq, 0, 0)))
    return pl.pallas_call(
        functools.partial(_proj_kernel, kv_rows=kv_rows, pool_width=pool_width, d_model=d),
        out_shape=out_shape,
        grid=(m // tm,),
        in_specs=[row(d), _resident((1, d)), _resident(w_in.shape),
                  _resident((1, ATTN_WIDTH)), _resident((1, ATTN_WIDTH)),
                  _resident((GROUP_WIDTH, GROUP_WIDTH))],
        out_specs=out_specs,
        compiler_params=_params(),
        name="proj",
    )(x, gain, w_in, q_gain, k_gain, seg)


def _attn_kernel(q_ref, k_ref, v_ref, bias_ref, o_ref, lse_ref, *, n_blocks):
    lane_head = lax.broadcasted_iota(jnp.int32, (QBLK, GROUP_WIDTH), 1) // HEAD_DIM
    nt = (((1,), (1,)), ((), ()))

    def stack_heads(qb):
        qf = qb.astype(F32)
        return jnp.concatenate([jnp.where(lane_head == h, qf, 0.0)
                                for h in range(HEADS_PER_GROUP)], axis=0).astype(BF16)

    def finish(s, v_span, row0):
        m = jnp.max(s, axis=-1, keepdims=True)
        p = jnp.exp(s - m)
        l = jnp.sum(p, axis=-1, keepdims=True)
        pv = jnp.dot(p.astype(BF16), v_span, preferred_element_type=F32) / l
        lse = jnp.broadcast_to(m + jnp.log(l), pv.shape)
        rows = lambda a: [a[h * QBLK:(h + 1) * QBLK] for h in range(HEADS_PER_GROUP)]
        o_ref[0, pl.ds(row0, QBLK), :] = _head_select(rows(pv), lane_head)
        lse_ref[0, pl.ds(row0, QBLK), :] = _head_select(rows(lse), lane_head)

    s0 = lax.dot_general(stack_heads(q_ref[0, 0:QBLK, :]), k_ref[0, 0:QBLK, :], nt,
                         preferred_element_type=F32) + bias_ref[:, QBLK:2 * QBLK]
    finish(s0, v_ref[0, 0:QBLK, :], 0)

    def body(i, carry):
        row0 = pl.multiple_of(i * QBLK, QBLK)
        prev = pl.multiple_of(i * QBLK - QBLK, QBLK)
        s = lax.dot_general(stack_heads(q_ref[0, pl.ds(row0, QBLK), :]),
                            k_ref[0, pl.ds(prev, 2 * QBLK), :], nt,
                            preferred_element_type=F32) + bias_ref[...]
        finish(s, v_ref[0, pl.ds(prev, 2 * QBLK), :], row0)
        return carry

    lax.fori_loop(1, n_blocks, body, 0)


def _prompt_bias(table, g):
    win, dil = ATTN_GROUPS[g]
    delta = np.arange(QBLK)[:, None] + QBLK - np.arange(2 * QBLK)[None, :]
    valid = (delta >= 0) & (delta <= win // dil)
    buckets = _t5_buckets(dil * np.clip(delta, 0, win // dil))
    heads = table[:, g * HEADS_PER_GROUP:(g + 1) * HEADS_PER_GROUP].astype(F32)
    bias = jnp.transpose(heads[buckets], (2, 0, 1))
    bias = jnp.where(valid[None], bias, NEG)
    return bias.reshape(HEADS_PER_GROUP * QBLK, 2 * QBLK)


def _prompt_attention(q, k, v, table, g, batch):
    _, dil = ATTN_GROUPS[g]
    m = q.shape[0]
    t = m // batch
    sub = t // dil
    assert sub % QBLK == 0
    view = lambda a: a.reshape(batch, sub, dil * ATTN_WIDTH)
    in_spec = pl.BlockSpec((1, sub, GROUP_WIDTH), lambda b, r: (b, 0, r * N_GROUPS + g))
    out_spec = pl.BlockSpec((1, sub, GROUP_WIDTH), lambda b, r: (b, 0, r))
    out_sds = jax.ShapeDtypeStruct((batch, sub, dil * GROUP_WIDTH), F32)
    o, lse = pl.pallas_call(
        functools.partial(_attn_kernel, n_blocks=sub // QBLK),
        out_shape=[out_sds, out_sds],
        grid=(batch, dil),
        in_specs=[in_spec, in_spec, in_spec, _resident((HEADS_PER_GROUP * QBLK, 2 * QBLK))],
        out_specs=[out_spec, out_spec],
        compiler_params=_params(2),
        name=f"attn_g{g}",
    )(view(q), view(k), view(v), _prompt_bias(table, g))
    return o.reshape(m, GROUP_WIDTH), lse.reshape(m, GROUP_WIDTH)


def _sample_attn_kernel(q_ref, n0_ref, n1_ref, n2_ref, c0_ref, c1_ref, c2_ref, b0_ref, b1_ref, b2_ref,
                        o0_ref, o1_ref, o2_ref, l0_ref, l1_ref, l2_ref, w0_ref, w1_ref, w2_ref,
                        ks0, vs0, ks1, vs1, ks2, vs2):
    t_new = q_ref.shape[1]
    lane_head = lax.broadcasted_iota(jnp.int32, (t_new, GROUP_WIDTH), 1) // HEAD_DIM
    nt = (((1,), (1,)), ((), ()))
    groups = ((n0_ref, c0_ref, b0_ref, o0_ref, l0_ref, w0_ref, ks0, vs0),
              (n1_ref, c1_ref, b1_ref, o1_ref, l1_ref, w1_ref, ks1, vs1),
              (n2_ref, c2_ref, b2_ref, o2_ref, l2_ref, w2_ref, ks2, vs2))
    for g, (new_ref, cache_ref, bias_ref, o_ref, lse_ref, win_ref, ks, vs) in enumerate(groups):
        past = cache_ref.shape[1]
        new = new_ref[0]
        win_ref[0, 0:past - t_new, :] = cache_ref[0, t_new:past, :]
        win_ref[0, past - t_new:past, :] = new
        pad = jnp.zeros((PAD_KEYS - t_new, GROUP_WIDTH), F32)
        ks[0:past, :] = cache_ref[0, :, 0:GROUP_WIDTH].astype(BF16)
        vs[0:past, :] = cache_ref[0, :, GROUP_WIDTH:2 * GROUP_WIDTH].astype(BF16)
        ks[past:past + PAD_KEYS, :] = jnp.concatenate([new[:, 0:GROUP_WIDTH], pad], axis=0).astype(BF16)
        vs[past:past + PAD_KEYS, :] = jnp.concatenate([new[:, GROUP_WIDTH:], pad], axis=0).astype(BF16)
        qg = q_ref[0, :, g * GROUP_WIDTH:(g + 1) * GROUP_WIDTH].astype(F32)
        qs = jnp.concatenate([jnp.where(lane_head == h, qg, 0.0)
                              for h in range(HEADS_PER_GROUP)], axis=0).astype(BF16)
        s = lax.dot_general(qs, ks[...], nt, preferred_element_type=F32) + bias_ref[...]
        m = jnp.max(s, axis=-1, keepdims=True)
        p = jnp.exp(s - m)
        l = jnp.sum(p, axis=-1, keepdims=True)
        pv = jnp.dot(p.astype(BF16), vs[...], preferred_element_type=F32) / l
        lse = jnp.broadcast_to(m + jnp.log(l), pv.shape)
        rows = lambda a: [a[h * t_new:(h + 1) * t_new] for h in range(HEADS_PER_GROUP)]
        o_ref[0] = _head_select(rows(pv), lane_head)
        lse_ref[0] = _head_select(rows(lse), lane_head)


def _sample_bias(table, g, past, t_new):
    win, dil = ATTN_GROUPS[g]
    idx = np.arange(past + PAD_KEYS)[None, :]
    delta = past + np.arange(t_new)[:, None] - idx
    valid = (idx < past + t_new) & (delta >= 0) & (delta % dil == 0) & (delta // dil <= win // dil)
    buckets = _t5_buckets(np.clip(delta, 0, win))
    heads = table[:, g * HEADS_PER_GROUP:(g + 1) * HEADS_PER_GROUP].astype(F32)
    bias = jnp.transpose(heads[buckets], (2, 0, 1))
    bias = jnp.where(valid[None], bias, NEG)
    return bias.reshape(HEADS_PER_GROUP * t_new, past + PAD_KEYS)


def _sample_attention(q, kv_new, caches, table):
    batch, t_new, _ = q.shape
    pasts = [c.shape[1] for c in caches]
    biases = [_sample_bias(table, g, pasts[g], t_new) for g in range(N_GROUPS)]
    per_seq = lambda rows, w: pl.BlockSpec((1, rows, w), lambda b: (b, 0, 0))
    small = jax.ShapeDtypeStruct((batch, t_new, GROUP_WIDTH), F32)
    outs = pl.pallas_call(
        _sample_attn_kernel,
        out_shape=[small] * 6 + [jax.ShapeDtypeStruct(c.shape, F32) for c in caches],
        grid=(batch,),
        in_specs=[per_seq(t_new, ATTN_WIDTH)] + [per_seq(t_new, 2 * GROUP_WIDTH)] * 3
        + [per_seq(p, 2 * GROUP_WIDTH) for p in pasts] + [_resident(b.shape) for b in biases],
        out_specs=[per_seq(t_new, GROUP_WIDTH)] * 6 + [per_seq(p, 2 * GROUP_WIDTH) for p in pasts],
        scratch_shapes=[pltpu.VMEM((p + PAD_KEYS, GROUP_WIDTH), BF16) for p in pasts for _ in range(2)],
        compiler_params=_params(),
        name="sample_attn",
    )(q, *kv_new, *caches, *biases)
    flat = lambda a: a.reshape(batch * t_new, GROUP_WIDTH)
    return [flat(a) for a in outs[0:3]], [flat(a) for a in outs[3:6]], outs[6:9]


def _merge_kernel(x_ref, o0_ref, o1_ref, o2_ref, l0_ref, l1_ref, l2_ref, u_ref, halo_ref, ga_ref, gb_ref,
                  wab_ref, wpg_ref, pscale_ref, wpb_ref, wout_ref, out_ref, ue_ref,
                  *, pos_base, blocks_per_seq):
    tm = x_ref.shape[0]
    n_seq = halo_ref.shape[0]
    t = tm // n_seq
    pool_width = u_ref.shape[1]
    gw = pool_width // len(POOL_WINDOWS)

    lses = (l0_ref[...], l1_ref[...], l2_ref[...])
    top = jnp.maximum(jnp.maximum(lses[0], lses[1]), lses[2])
    num = jnp.zeros_like(top)
    den = jnp.zeros_like(top)
    for o_ref, lse in zip((o0_ref, o1_ref, o2_ref), lses):
        e = jnp.exp(lse - top)
        num = num + e * o_ref[...]
        den = den + e
    attn = (num / den).astype(BF16)
    branch_a = jnp.dot(attn, wab_ref[...], preferred_element_type=F32)

    start = pos_base + (pl.program_id(0) % blocks_per_seq) * t
    u = u_ref[...]
    ue_ref[:, 0:POOL_HALO, :] = jnp.where(start > 0, halo_ref[...], 0.0)
    ue_ref[:, POOL_HALO:, :] = u.reshape(n_seq, t, pool_width)
    pos = start + lax.broadcasted_iota(jnp.int32, (1, t, gw), 1)
    pooled = []
    for gi, win in enumerate(POOL_WINDOWS):
        cols = slice(gi * gw, (gi + 1) * gw)
        s = ue_ref[:, POOL_HALO:, cols]
        for back in range(1, win):
            s = s + ue_ref[:, POOL_HALO - back:POOL_HALO - back + t, cols]
        cnt = jnp.minimum(pos + 1, win).astype(F32)
        d = (s / cnt - ue_ref[:, POOL_HALO:, cols]).reshape(tm, gw)
        pooled.append(jnp.dot(d.astype(BF16), wpg_ref[gi], preferred_element_type=F32))
    pooled = (jnp.concatenate(pooled, axis=-1) * pscale_ref[...]).astype(BF16)
    branch_b = jnp.dot(pooled, wpb_ref[...], preferred_element_type=F32)

    merged = (ga_ref[...].astype(F32) * branch_a + gb_ref[...].astype(F32) * branch_b).astype(BF16)
    out_ref[...] = x_ref[...] + jnp.dot(merged, wout_ref[...], preferred_element_type=F32)


def _merge(x, o, lse, u, halo, ga, gb, w, *, tm, halo_spec, pos_base, blocks_per_seq):
    m, d = x.shape
    pool_width = u.shape[1]
    n_seq = halo_spec.block_shape[0]
    row = lambda wd: pl.BlockSpec((tm, wd), lambda i: (i, 0))
    return pl.pallas_call(
        functools.partial(_merge_kernel, pos_base=pos_base, blocks_per_seq=blocks_per_seq),
        out_shape=jax.ShapeDtypeStruct((m, d), F32),
        grid=(m // tm,),
        in_specs=[row(d)] + [row(GROUP_WIDTH)] * 6 + [row(pool_width), halo_spec, row(d), row(d)]
        + [_resident(a.shape) for a in w],
        out_specs=row(d),
        scratch_shapes=[pltpu.VMEM((n_seq, POOL_HALO + tm // n_seq, pool_width), F32)],
        compiler_params=_params(),
        name="merge",
    )(x, *o, *lse, u, halo, ga, gb, *w)


def _window_out(kv, n_seq):
    return kv.reshape(n_seq, kv.shape[1], 2, HEADS_PER_GROUP, HEAD_DIM)


def kernel(x_prompt, x_sample, cache_kv_w128, cache_kv_w512, cache_kv_w2048, state_pool, rel_bias_table,
           norm_ffn1, ffn1_w_up, ffn1_w_down, norm_mix, w_in, q_norm, k_norm, pool_w_group, pool_scale,
           w_attn_branch, w_pool_branch, w_out, norm_ffn2, ffn2_w_up, ffn2_w_down):
    batch, seq, d_model = x_prompt.shape
    dec_batch, dec_seq, _ = x_sample.shape
    depth = norm_ffn1.shape[0]
    pool_width = state_pool.shape[-1]
    tm = 512
    ms = dec_batch * dec_seq
    seg = jnp.asarray(np.kron(np.eye(HEADS_PER_GROUP), np.full((HEAD_DIM, HEAD_DIM), 1.0 / HEAD_DIM)), BF16)
    caches_in = (cache_kv_w128, cache_kv_w512, cache_kv_w2048)
    keep_prompt = tuple(min(win, seq) for win, _ in ATTN_GROUPS)

    xp = x_prompt.reshape(batch * seq, d_model)
    xs = x_sample.reshape(ms, d_model)
    kv_p, kv_s, pool_p, pool_s = ([], [], []), ([], [], []), [], []
    for layer in range(depth):
        gain = lambda a: a[layer].reshape(1, -1).astype(F32)
        wup1, wdn1 = ffn1_w_up[layer].astype(BF16), ffn1_w_down[layer].astype(BF16)
        wup2, wdn2 = ffn2_w_up[layer].astype(BF16), ffn2_w_down[layer].astype(BF16)
        win = w_in[layer].astype(BF16)
        merge_w = (w_attn_branch[layer].astype(BF16), pool_w_group[layer].astype(BF16), gain(pool_scale),
                   w_pool_branch[layer].astype(BF16), w_out[layer].astype(BF16))
        qg, kg = gain(q_norm), gain(k_norm)

        xp = _ffn(xp, gain(norm_ffn1), wup1, wdn1, tm)
        q, k, v, u, ga, gb, kv0, kv1, kv2, pstate = _proj(
            xp, gain(norm_mix), win, qg, kg, seg, tm=tm, n_seq=batch, keep_rows=keep_prompt,
            with_pool_state=True)
        o, lse = zip(*[_prompt_attention(q, k, v, rel_bias_table, g, batch) for g in range(N_GROUPS)])
        blocks_per_seq = seq // tm
        halo_spec = pl.BlockSpec(
            (1, POOL_HALO, pool_width), lambda i: (jnp.maximum(i * (tm // POOL_HALO) - 1, 0), 0, 0))
        xp = _merge(xp, o, lse, u, u.reshape(-1, POOL_HALO, pool_width), ga, gb, merge_w, tm=tm,
                    halo_spec=halo_spec, pos_base=0, blocks_per_seq=blocks_per_seq)
        xp = _ffn(xp, gain(norm_ffn2), wup2, wdn2, tm)
        for g, kv in enumerate((kv0, kv1, kv2)):
            kv_p[g].append(_window_out(kv, batch))
        pool_p.append(pstate[:, POOL_HALO - POOL_STATE:])

        xs = _ffn(xs, gain(norm_ffn1), wup1, wdn1, ms)
        q, k, v, u, ga, gb, kv0, kv1, kv2 = _proj(
            xs, gain(norm_mix), win, qg, kg, seg, tm=ms, n_seq=1, keep_rows=(ms,) * N_GROUPS,
            with_pool_state=False)
        caches = [c[layer].reshape(dec_batch, c.shape[2], 2 * GROUP_WIDTH) for c in caches_in]
        kv_new = [a.reshape(dec_batch, dec_seq, 2 * GROUP_WIDTH) for a in (kv0, kv1, kv2)]
        o, lse, windows = _sample_attention(q.reshape(dec_batch, dec_seq, ATTN_WIDTH), kv_new, caches,
                                            rel_bias_table)
        history = jnp.concatenate(
            [jnp.zeros((dec_batch, POOL_HALO - POOL_STATE, pool_width), F32), state_pool[layer]], axis=1)
        halo_spec = pl.BlockSpec((dec_batch, POOL_HALO, pool_width), lambda i: (0, 0, 0))
        xs = _merge(xs, o, lse, u, history, ga, gb, merge_w, tm=ms, halo_spec=halo_spec,
                    pos_base=PAST_LEN, blocks_per_seq=1)
        xs = _ffn(xs, gain(norm_ffn2), wup2, wdn2, ms)
        for g in range(N_GROUPS):
            kv_s[g].append(_window_out(windows[g], dec_batch))
        ue = jnp.concatenate([state_pool[layer], u.reshape(dec_batch, dec_seq, pool_width)], axis=1)
        pool_s.append(ue[:, ue.shape[1] - POOL_STATE:])

    stack = lambda xs_: jnp.stack(xs_, axis=0)
    return (xp.reshape(batch, seq, d_model), xs.reshape(dec_batch, dec_seq, d_model),
            stack(kv_p[0]), stack(kv_p[1]), stack(kv_p[2]), stack(pool_p),
            stack(kv_s[0]), stack(kv_s[1]), stack(kv_s[2]), stack(pool_s))
```

```python
import functools
import math

import numpy as np
import jax
import jax.numpy as jnp
from jax import lax
from jax.experimental import pallas as pl
from jax.experimental.pallas import tpu as pltpu

HEAD_DIM = 64
HEADS_PER_GROUP = 4
GROUP_WIDTH = HEADS_PER_GROUP * HEAD_DIM
ATTN_GROUPS = ((128, 1), (512, 4), (2048, 16))
N_GROUPS = len(ATTN_GROUPS)
ATTN_WIDTH = N_GROUPS * GROUP_WIDTH
N_BUCKETS = 32
MAX_DISTANCE = 2048
POOL_WINDOWS = (2, 4, 8, 16)
POOL_STATE = max(POOL_WINDOWS) - 1
POOL_HALO = 16
EPS = 1e-6
PAST_LEN = 8192
NEG = -1e30
LANES = 128
SLABS = GROUP_WIDTH // LANES
QBLK = 128
VMEM_LIMIT = 50 * 1024 * 1024

F32 = jnp.float32
BF16 = jnp.bfloat16
NT_DIMS = (((1,), (1,)), ((), ()))


def _t5_buckets(distance):
    max_exact = N_BUCKETS // 2
    d = np.asarray(distance, dtype=np.int32)
    ratio = np.log(np.maximum(d, 1).astype(np.float32) / np.float32(max_exact))
    large = max_exact + (ratio / np.float32(math.log(MAX_DISTANCE / max_exact))
                         * (N_BUCKETS - max_exact)).astype(np.int32)
    large = np.minimum(large, N_BUCKETS - 1)
    return np.where(d < max_exact, d, large).astype(np.int32)


def _params(n_grid_dims=1):
    return pltpu.CompilerParams(dimension_semantics=("arbitrary",) * n_grid_dims,
                                vmem_limit_bytes=VMEM_LIMIT)


def _resident(shape):
    return pl.BlockSpec(shape, lambda *_: (0,) * len(shape), pipeline_mode=pl.Buffered(1))


_SMEM = pl.BlockSpec(memory_space=pltpu.SMEM)


def _rmsnorm(x, g):
    ms = jnp.mean(x * x, axis=-1, keepdims=True)
    return x * lax.rsqrt(ms + EPS) * g


def _head_select(parts, lane_head):
    out = jnp.where(lane_head == 0, parts[0], 0.0)
    for h in range(1, HEADS_PER_GROUP):
        out = jnp.where(lane_head == h, parts[h], out)
    return out


def _stack_heads(q, lane_head):
    return jnp.concatenate([jnp.where(lane_head == h, q, 0.0)
                            for h in range(HEADS_PER_GROUP)], axis=0).astype(BF16)


def _build_bias(bias_ref, bucket_ref, table_ref, g):
    buckets = bucket_ref[...]
    rows = buckets.shape[0]
    for h in range(HEADS_PER_GROUP):
        acc = jnp.full(buckets.shape, NEG, F32)
        for b in range(N_BUCKETS):
            acc = jnp.where(buckets == b, table_ref[b, g * HEADS_PER_GROUP + h], acc)
        bias_ref[h * rows:(h + 1) * rows, :] = acc


def _ffn_kernel(x_ref, g_ref, wup_ref, wdn_ref, o_ref, act_ref, *, d_ff, chunk):
    x = x_ref[...]
    h = _rmsnorm(x, g_ref[...]).astype(BF16)
    for c in range(d_ff // chunk):
        lo = c * chunk
        gate = jnp.dot(h, wup_ref[:, lo:lo + chunk], preferred_element_type=F32)
        up = jnp.dot(h, wup_ref[:, d_ff + lo:d_ff + lo + chunk], preferred_element_type=F32)
        act_ref[:, lo:lo + chunk] = (gate * jax.nn.sigmoid(gate) * up).astype(BF16)
    o_ref[...] = x + 0.5 * jnp.dot(act_ref[...], wdn_ref[...], preferred_element_type=F32)


def _ffn(x, gain, w_up, w_down, tm):
    m, d = x.shape
    d_ff = w_down.shape[0]
    chunk = 256
    assert m % tm == 0 and d_ff % chunk == 0
    return pl.pallas_call(
        functools.partial(_ffn_kernel, d_ff=d_ff, chunk=chunk),
        out_shape=jax.ShapeDtypeStruct((m, d), F32),
        grid=(m // tm,),
        in_specs=[pl.BlockSpec((tm, d), lambda i: (i, 0)),
                  _resident((1, d)), _resident((d, 2 * d_ff)), _resident((d_ff, d))],
        out_specs=pl.BlockSpec((tm, d), lambda i: (i, 0)),
        scratch_shapes=[pltpu.VMEM((tm, d_ff), BF16)],
        compiler_params=_params(),
        name="ffn",
    )(x, gain, w_up, w_down)


def _proj_kernel(x_ref, g_ref, win_ref, qn_ref, kn_ref, seg_ref, *refs,
                 dils, kv_rows, kv_first, blocks_per_seq, with_state, pool_width, d_model):
    qkv_refs = refs[0:3 * N_GROUPS]
    u_ref, ga_ref, gb_ref = refs[3 * N_GROUPS:3 * N_GROUPS + 3]
    state_refs = refs[3 * N_GROUPS + 3:-1]
    slab_ref = refs[-1]
    tm = x_ref.shape[0]
    h = _rmsnorm(x_ref[...], g_ref[...]).astype(BF16)
    block_in_seq = pl.program_id(0) % blocks_per_seq

    def proj(lo, width):
        return jnp.dot(h, win_ref[:, lo:lo + width], preferred_element_type=F32)

    def head_norm(y, gain):
        ms = jnp.dot((y * y).astype(BF16), seg_ref[...], preferred_element_type=F32)
        return y * lax.rsqrt(ms + EPS) * gain

    def emit(dst_ref, slot, val, dil):
        if dil == 1:
            dst_ref[0, 0] = val.astype(dst_ref.dtype)
            return
        for s in range(SLABS):
            slab_ref[slot, s] = val[:, s * LANES:(s + 1) * LANES]
        n = tm // dil
        for r in range(dil):
            parts = [slab_ref[slot, s, pl.ds(r, n, stride=dil), :] for s in range(SLABS)]
            dst_ref[0, r] = jnp.concatenate(parts, axis=-1).astype(dst_ref.dtype)

    for g in range(N_GROUPS):
        lo = g * GROUP_WIDTH
        cols = slice(lo, lo + GROUP_WIDTH)
        qg = head_norm(proj(lo, GROUP_WIDTH), qn_ref[:, cols]) * (HEAD_DIM ** -0.5)
        kg = head_norm(proj(ATTN_WIDTH + lo, GROUP_WIDTH), kn_ref[:, cols])
        vg = proj(2 * ATTN_WIDTH + lo, GROUP_WIDTH)
        for which, val in enumerate((qg, kg, vg)):
            emit(qkv_refs[3 * g + which], 3 * g + which, val, dils[g])
        if with_state:
            r = kv_rows[g]

            @pl.when(block_in_seq >= kv_first[g])
            def _(kg=kg, vg=vg, kv_ref=state_refs[g], r=r):
                kv_ref[0, 0] = kg.T[:, tm - r:]
                kv_ref[0, 1] = vg.T[:, tm - r:]
    u = proj(3 * ATTN_WIDTH, pool_width)
    u_ref[...] = u
    if with_state:
        state_refs[N_GROUPS][0] = u[tm - POOL_HALO:, :]
    ga_ref[...] = jax.nn.sigmoid(proj(3 * ATTN_WIDTH + pool_width, d_model)).astype(BF16)
    gb_ref[...] = jax.nn.sigmoid(proj(3 * ATTN_WIDTH + pool_width + d_model, d_model)).astype(BF16)


def _proj(x, gain, w_in, q_gain, k_gain, seg, *, tm, n_seq, dils, qkv_dtype, keep_rows):
    m, d = x.shape
    pool_width = w_in.shape[1] - 3 * ATTN_WIDTH - 2 * d
    seq_len = m // n_seq
    assert seq_len % tm == 0 and all(tm % dil == 0 for dil in dils)
    blocks_per_seq = seq_len // tm
    with_state = keep_rows is not None
    seq_block = lambda i: (i // blocks_per_seq, i % blocks_per_seq)

    row = lambda w: pl.BlockSpec((tm, w), lambda i: (i, 0))
    out_shape, out_specs = [], []
    for g in range(N_GROUPS):
        sds = jax.ShapeDtypeStruct((n_seq, dils[g], seq_len // dils[g], GROUP_WIDTH), qkv_dtype)
        spec = pl.BlockSpec((1, dils[g], tm // dils[g], GROUP_WIDTH),
                            lambda i: (seq_block(i)[0], 0, seq_block(i)[1], 0))
        out_shape += [sds] * 3
        out_specs += [spec] * 3
    out_shape += [jax.ShapeDtypeStruct((m, pool_width), F32),
                  jax.ShapeDtypeStruct((m, d), BF16), jax.ShapeDtypeStruct((m, d), BF16)]
    out_specs += [row(pool_width), row(d), row(d)]
    kv_rows = kv_first = ()
    if with_state:
        kv_rows = tuple(min(r, tm) for r in keep_rows)
        kv_first = tuple(blocks_per_seq - keep_rows[g] // kv_rows[g] for g in range(N_GROUPS))
        for g in range(N_GROUPS):
            out_shape.append(jax.ShapeDtypeStruct((n_seq, 2, GROUP_WIDTH, keep_rows[g]), F32))
            out_specs.append(pl.BlockSpec(
                (1, 2, GROUP_WIDTH, kv_rows[g]),
                lambda i, first=kv_first[g]: (seq_block(i)[0], 0, 0, jnp.maximum(seq_block(i)[1] - first, 0))))
        out_shape.append(jax.ShapeDtypeStruct((n_seq, POOL_HALO, pool_width), F32))
        out_specs.append(pl.BlockSpec((1, POOL_HALO, pool_width), lambda i: (seq_block(i)[0], 0, 0)))
    return pl.pallas_call(
        functools.partial(_proj_kernel, dils=dils, kv_rows=kv_rows, kv_first=kv_first,
                          blocks_per_seq=blocks_per_seq, with_state=with_state,
                          pool_width=pool_width, d_model=d),
        out_shape=out_shape,
        grid=(m // tm,),
        in_specs=[row(d), _resident((1, d)), _resident(w_in.shape),
                  _resident((1, ATTN_WIDTH)), _resident((1, ATTN_WIDTH)),
                  _resident((GROUP_WIDTH, GROUP_WIDTH))],
        out_specs=out_specs,
        scratch_shapes=[pltpu.VMEM((3 * N_GROUPS, SLABS, tm, LANES), F32)],
        compiler_params=_params(),
        name="proj",
    )(x, gain, w_in, q_gain, k_gain, seg)


def _attn_kernel(q_ref, k_ref, v_ref, bucket_ref, table_ref, o_ref, lse_ref, bias_ref, *, g, dil, n_blocks):
    res = pl.program_id(1)

    @pl.when((pl.program_id(0) == 0) & (res == 0))
    def _():
        _build_bias(bias_ref, bucket_ref, table_ref, g)

    lane_head = lax.broadcasted_iota(jnp.int32, (QBLK, GROUP_WIDTH), 1) // HEAD_DIM

    def finish(s, v_span, blk):
        m = jnp.max(s, axis=-1, keepdims=True)
        p = jnp.exp(s - m)
        l = jnp.sum(p, axis=-1, keepdims=True)
        pv = jnp.dot(p.astype(BF16), v_span, preferred_element_type=F32) / l
        lse = jnp.broadcast_to(m + jnp.log(l), pv.shape)
        rows = lambda a: [a[h * QBLK:(h + 1) * QBLK] for h in range(HEADS_PER_GROUP)]
        token0 = res + dil * QBLK * blk
        dst = pl.ds(token0, QBLK) if dil == 1 else pl.ds(token0, QBLK, stride=dil)
        for out_ref, val in ((o_ref, _head_select(rows(pv), lane_head)),
                             (lse_ref, _head_select(rows(lse), lane_head))):
            for s_ in range(SLABS):
                out_ref[0, s_, dst, :] = val[:, s_ * LANES:(s_ + 1) * LANES]

    def stacked_q(row0):
        return _stack_heads(q_ref[0, 0, pl.ds(row0, QBLK), :].astype(F32), lane_head)

    s0 = lax.dot_general(stacked_q(0), k_ref[0, 0, 0:QBLK, :], NT_DIMS,
                         preferred_element_type=F32) + bias_ref[:, QBLK:2 * QBLK]
    finish(s0, v_ref[0, 0, 0:QBLK, :], 0)

    def body(i, carry):
        row0 = pl.multiple_of(i * QBLK, QBLK)
        prev = pl.multiple_of(i * QBLK - QBLK, QBLK)
        s = lax.dot_general(stacked_q(row0), k_ref[0, 0, pl.ds(prev, 2 * QBLK), :], NT_DIMS,
                            preferred_element_type=F32) + bias_ref[...]
        finish(s, v_ref[0, 0, pl.ds(prev, 2 * QBLK), :], i)
        return carry

    lax.fori_loop(1, n_blocks, body, 0)


def _prompt_buckets(g):
    win, dil = ATTN_GROUPS[g]
    delta = np.arange(QBLK)[:, None] + QBLK - np.arange(2 * QBLK)[None, :]
    valid = (delta >= 0) & (delta <= win // dil)
    return np.where(valid, _t5_buckets(dil * np.clip(delta, 0, win // dil)), -1).astype(np.int32)


def _prompt_attention(q, k, v, table, g):
    _, dil = ATTN_GROUPS[g]
    batch, _, sub, _ = q.shape
    assert sub % QBLK == 0
    in_spec = pl.BlockSpec((1, 1, sub, GROUP_WIDTH), lambda b, r: (b, r, 0, 0))
    out_spec = pl.BlockSpec((1, SLABS, sub * dil, LANES), lambda b, r: (b, 0, 0, 0))
    out_sds = jax.ShapeDtypeStruct((batch, SLABS, sub * dil, LANES), F32)
    return pl.pallas_call(
        functools.partial(_attn_kernel, g=g, dil=dil, n_blocks=sub // QBLK),
        out_shape=[out_sds, out_sds],
        grid=(batch, dil),
        in_specs=[in_spec, in_spec, in_spec, _resident((QBLK, 2 * QBLK)), _SMEM],
        out_specs=[out_spec, out_spec],
        scratch_shapes=[pltpu.VMEM((HEADS_PER_GROUP * QBLK, 2 * QBLK), F32)],
        compiler_params=_params(2),
        name=f"attn_g{g}",
    )(q, k, v, jnp.asarray(_prompt_buckets(g)), table)


def _sample_attn_kernel(*refs):
    qkv_refs = refs[0:9]
    cache_refs = refs[9:12]
    bucket_refs = refs[12:15]
    table_ref = refs[15]
    o_refs, lse_refs, win_refs = refs[16:19], refs[19:22], refs[22:25]
    bias_refs = refs[25:28]
    t_new = qkv_refs[0].shape[1]
    lane_head = lax.broadcasted_iota(jnp.int32, (t_new, GROUP_WIDTH), 1) // HEAD_DIM
    new_lane = lax.broadcasted_iota(jnp.int32, (GROUP_WIDTH, LANES), 1) >= LANES - t_new
    pad = jnp.zeros((LANES - t_new, GROUP_WIDTH), F32)

    for g in range(N_GROUPS):
        @pl.when(pl.program_id(0) == 0)
        def _(g=g):
            _build_bias(bias_refs[g], bucket_refs[g], table_ref, g)

        q_ref, k_ref, v_ref = qkv_refs[3 * g:3 * g + 3]
        cache_ref, win_ref, bias_ref = cache_refs[g], win_refs[g], bias_refs[g]
        past = cache_ref.shape[3]
        new_rows = [jnp.concatenate([pad, r[0]], axis=0) for r in (k_ref, v_ref)]
        for c in range(2):
            shifted = pltpu.roll(cache_ref[0, c], past - t_new, 1)
            if past > LANES:
                win_ref[0, c, :, 0:past - LANES] = shifted[:, 0:past - LANES]
            win_ref[0, c, :, past - LANES:past] = jnp.where(new_lane, new_rows[c].T, shifted[:, past - LANES:])

        qs = _stack_heads(q_ref[0], lane_head)
        s_old = jnp.dot(qs, cache_ref[0, 0].astype(BF16), preferred_element_type=F32) + bias_ref[:, 0:past]
        s_new = lax.dot_general(qs, new_rows[0].astype(BF16), NT_DIMS,
                                preferred_element_type=F32) + bias_ref[:, past:past + LANES]
        m = jnp.maximum(jnp.max(s_old, axis=-1, keepdims=True), jnp.max(s_new, axis=-1, keepdims=True))
        p_old = jnp.exp(s_old - m)
        p_new = jnp.exp(s_new - m)
        l = jnp.sum(p_old, axis=-1, keepdims=True) + jnp.sum(p_new, axis=-1, keepdims=True)
        pv = (lax.dot_general(p_old.astype(BF16), cache_ref[0, 1].astype(BF16), NT_DIMS,
                              preferred_element_type=F32)
              + jnp.dot(p_new.astype(BF16), new_rows[1].astype(BF16), preferred_element_type=F32)) / l
        lse = jnp.broadcast_to(m + jnp.log(l), pv.shape)
        rows = lambda a: [a[h * t_new:(h + 1) * t_new] for h in range(HEADS_PER_GROUP)]
        for out_ref, val in ((o_refs[g], _head_select(rows(pv), lane_head)),
                             (lse_refs[g], _head_select(rows(lse), lane_head))):
            for s_ in range(SLABS):
                out_ref[0, s_] = val[:, s_ * LANES:(s_ + 1) * LANES]


def _sample_buckets(g, past, t_new):
    win, dil = ATTN_GROUPS[g]
    lane = np.arange(past + LANES)[None, :]
    key_pos = np.where(lane < past, lane, lane - (LANES - t_new))
    delta = past + np.arange(t_new)[:, None] - key_pos
    valid = ((lane < past) | (lane >= past + LANES - t_new)) & (delta >= 0) & (delta % dil == 0) & (delta <= win)
    return np.where(valid, _t5_buckets(np.clip(delta, 0, win)), -1).astype(np.int32)


def _sample_attention(qkv, caches, table):
    batch, t_new, _ = qkv[0].shape
    pasts = [c.shape[3] for c in caches]
    buckets = [jnp.asarray(_sample_buckets(g, pasts[g], t_new)) for g in range(N_GROUPS)]
    cache_spec = lambda p: pl.BlockSpec((1, 2, GROUP_WIDTH, p), lambda b: (b, 0, 0, 0))
    new_spec = pl.BlockSpec((1, t_new, GROUP_WIDTH), lambda b: (b, 0, 0))
    out_spec = pl.BlockSpec((1, SLABS, t_new, LANES), lambda b: (0, 0, b, 0))
    out_sds = jax.ShapeDtypeStruct((1, SLABS, batch * t_new, LANES), F32)
    outs = pl.pallas_call(
        _sample_attn_kernel,
        out_shape=[out_sds] * 6 + [jax.ShapeDtypeStruct(c.shape, F32) for c in caches],
        grid=(batch,),
        in_specs=[new_spec] * 9 + [cache_spec(p) for p in pasts] + [_resident(b.shape) for b in buckets]
        + [_SMEM],
        out_specs=[out_spec] * 6 + [cache_spec(p) for p in pasts],
        scratch_shapes=[pltpu.VMEM((HEADS_PER_GROUP * t_new, p + LANES), F32) for p in pasts],
        compiler_params=_params(),
        name="sample_attn",
    )(*qkv, *caches, *buckets, table)
    return outs[0:3], outs[3:6], outs[6:9]


def _merge_kernel(x_ref, o0_ref, o1_ref, o2_ref, l0_ref, l1_ref, l2_ref, u_ref, halo_ref, ga_ref, gb_ref,
                  wab_ref, wpg_ref, pscale_ref, wpb_ref, wout_ref, out_ref, ue_ref,
                  *, pos_base, blocks_per_seq):
    tm = x_ref.shape[0]
    n_seq = halo_ref.shape[0]
    t = tm // n_seq
    pool_width = u_ref.shape[1]
    gw = pool_width // len(POOL_WINDOWS)
    wide = lambda ref: jnp.concatenate([ref[0, s] for s in range(SLABS)], axis=-1)

    lses = [wide(r) for r in (l0_ref, l1_ref, l2_ref)]
    top = jnp.maximum(jnp.maximum(lses[0], lses[1]), lses[2])
    num = jnp.zeros_like(top)
    den = jnp.zeros_like(top)
    for o_ref, lse in zip((o0_ref, o1_ref, o2_ref), lses):
        e = jnp.exp(lse - top)
        num = num + e * wide(o_ref)
        den = den + e
    attn = (num / den).astype(BF16)
    branch_a = jnp.dot(attn, wab_ref[...], preferred_element_type=F32)

    start = pos_base + (pl.program_id(0) % blocks_per_seq) * t
    u = u_ref[...]
    ue_ref[:, 0:POOL_HALO, :] = jnp.where(start > 0, halo_ref[...], 0.0)
    ue_ref[:, POOL_HALO:, :] = u.reshape(n_seq, t, pool_width)
    pos = start + lax.broadcasted_iota(jnp.int32, (1, t, gw), 1)
    pooled = []
    for gi, win in enumerate(POOL_WINDOWS):
        cols = slice(gi * gw, (gi + 1) * gw)
        s = ue_ref[:, POOL_HALO:, cols]
        for back in range(1, win):
            s = s + ue_ref[:, POOL_HALO - back:POOL_HALO - back + t, cols]
        cnt = jnp.minimum(pos + 1, win).astype(F32)
        d = (s / cnt - ue_ref[:, POOL_HALO:, cols]).reshape(tm, gw)
        pooled.append(jnp.dot(d.astype(BF16), wpg_ref[gi], preferred_element_type=F32))
    pooled = (jnp.concatenate(pooled, axis=-1) * pscale_ref[...]).astype(BF16)
    branch_b = jnp.dot(pooled, wpb_ref[...], preferred_element_type=F32)

    merged = (ga_ref[...].astype(F32) * branch_a + gb_ref[...].astype(F32) * branch_b).astype(BF16)
    out_ref[...] = x_ref[...] + jnp.dot(merged, wout_ref[...], preferred_element_type=F32)


def _merge(x, o, lse, u, halo, ga, gb, w, *, tm, halo_spec, pos_base, blocks_per_seq):
    m, d = x.shape
    pool_width = u.shape[1]
    n_seq = halo_spec.block_shape[0]
    row = lambda wd: pl.BlockSpec((tm, wd), lambda i: (i, 0))
    slab = pl.BlockSpec((1, SLABS, tm, LANES), lambda i: (i // blocks_per_seq, 0, i % blocks_per_seq, 0))
    return pl.pallas_call(
        functools.partial(_merge_kernel, pos_base=pos_base, blocks_per_seq=blocks_per_seq),
        out_shape=jax.ShapeDtypeStruct((m, d), F32),
        grid=(m // tm,),
        in_specs=[row(d)] + [slab] * 6 + [row(pool_width), halo_spec, row(d), row(d)]
        + [_resident(a.shape) for a in w],
        out_specs=row(d),
        scratch_shapes=[pltpu.VMEM((n_seq, POOL_HALO + tm // n_seq, pool_width), F32)],
        compiler_params=_params(),
        name="merge",
    )(x, *o, *lse, u, halo, ga, gb, *w)


def _window_in(cache):
    n_seq, rows = cache.shape[0:2]
    return jnp.transpose(cache, (0, 2, 3, 4, 1)).reshape(n_seq, 2, GROUP_WIDTH, rows)


def _window_out(kv):
    n_seq, _, _, rows = kv.shape
    return jnp.transpose(kv.reshape(n_seq, 2, HEADS_PER_GROUP, HEAD_DIM, rows), (0, 4, 1, 2, 3))


def kernel(x_prompt, x_sample, cache_kv_w128, cache_kv_w512, cache_kv_w2048, state_pool, rel_bias_table,
           norm_ffn1, ffn1_w_up, ffn1_w_down, norm_mix, w_in, q_norm, k_norm, pool_w_group, pool_scale,
           w_attn_branch, w_pool_branch, w_out, norm_ffn2, ffn2_w_up, ffn2_w_down):
    batch, seq, d_model = x_prompt.shape
    dec_batch, dec_seq, _ = x_sample.shape
    depth = norm_ffn1.shape[0]
    pool_width = state_pool.shape[-1]
    tm = 512
    ms = dec_batch * dec_seq
    seg = jnp.asarray(np.kron(np.eye(HEADS_PER_GROUP), np.full((HEAD_DIM, HEAD_DIM), 1.0 / HEAD_DIM)), BF16)
    table = rel_bias_table.astype(F32)
    caches_in = (cache_kv_w128, cache_kv_w512, cache_kv_w2048)
    keep_prompt = tuple(min(win, seq) for win, _ in ATTN_GROUPS)
    dils = tuple(dil for _, dil in ATTN_GROUPS)

    xp = x_prompt.reshape(batch * seq, d_model)
    xs = x_sample.reshape(ms, d_model)
    kv_p, kv_s, pool_p, pool_s = ([], [], []), ([], [], []), [], []
    for layer in range(depth):
        gain = lambda a: a[layer].reshape(1, -1).astype(F32)
        wup1, wdn1 = ffn1_w_up[layer].astype(BF16), ffn1_w_down[layer].astype(BF16)
        wup2, wdn2 = ffn2_w_up[layer].astype(BF16), ffn2_w_down[layer].astype(BF16)
        win = w_in[layer].astype(BF16)
        merge_w = (w_attn_branch[layer].astype(BF16), pool_w_group[layer].astype(BF16), gain(pool_scale),
                   w_pool_branch[layer].astype(BF16), w_out[layer].astype(BF16))
        qg, kg = gain(q_norm), gain(k_norm)

        xp = _ffn(xp, gain(norm_ffn1), wup1, wdn1, tm)
        outs = _proj(xp, gain(norm_mix), win, qg, kg, seg, tm=tm, n_seq=batch, dils=dils,
                     qkv_dtype=BF16, keep_rows=keep_prompt)
        qkv, (u, ga, gb), windows, pstate = outs[0:9], outs[9:12], outs[12:15], outs[15]
        o, lse = zip(*[_prompt_attention(*qkv[3 * g:3 * g + 3], table, g) for g in range(N_GROUPS)])
        blocks_per_seq = seq // tm
        halo_spec = pl.BlockSpec(
            (1, POOL_HALO, pool_width), lambda i: (jnp.maximum(i * (tm // POOL_HALO) - 1, 0), 0, 0))
        xp = _merge(xp, o, lse, u, u.reshape(-1, POOL_HALO, pool_width), ga, gb, merge_w, tm=tm,
                    halo_spec=halo_spec, pos_base=0, blocks_per_seq=blocks_per_seq)
        xp = _ffn(xp, gain(norm_ffn2), wup2, wdn2, tm)
        for g in range(N_GROUPS):
            kv_p[g].append(_window_out(windows[g]))
        pool_p.append(pstate[:, POOL_HALO - POOL_STATE:])

        xs = _ffn(xs, gain(norm_ffn1), wup1, wdn1, ms)
        outs = _proj(xs, gain(norm_mix), win, qg, kg, seg, tm=ms, n_seq=1, dils=(1,) * N_GROUPS,
                     qkv_dtype=F32, keep_rows=None)
        qkv, (u, ga, gb) = outs[0:9], outs[9:12]
        o, lse, windows = _sample_attention(
            [a.reshape(dec_batch, dec_seq, GROUP_WIDTH) for a in qkv],
            [_window_in(c[layer]) for c in caches_in], table)
        history = jnp.concatenate(
            [jnp.zeros((dec_batch, POOL_HALO - POOL_STATE, pool_width), F32), state_pool[layer]], axis=1)
        halo_spec = pl.BlockSpec((dec_batch, POOL_HALO, pool_width), lambda i: (0, 0, 0))
        xs = _merge(xs, o, lse, u, history, ga, gb, merge_w, tm=ms, halo_spec=halo_spec,
                    pos_base=PAST_LEN, blocks_per_seq=1)
        xs = _ffn(xs, gain(norm_ffn2), wup2, wdn2, ms)
        for g in range(N_GROUPS):
            kv_s[g].append(_window_out(windows[g]))
        ue = jnp.concatenate([state_pool[layer], u.reshape(dec_batch, dec_seq, pool_width)], axis=1)
        pool_s.append(ue[:, ue.shape[1] - POOL_STATE:])

    stack = lambda xs_: jnp.stack(xs_, axis=0)
    return (xp.reshape(batch, seq, d_model), xs.reshape(dec_batch, dec_seq, d_model),
            stack(kv_p[0]), stack(kv_p[1]), stack(kv_p[2]), stack(pool_p),
            stack(kv_s[0]), stack(kv_s[1]), stack(kv_s[2]), stack(pool_s))
```

```python
import functools
import math

import numpy as np
import jax
import jax.numpy as jnp
from jax import lax
from jax.experimental import pallas as pl
from jax.experimental.pallas import tpu as pltpu

HEAD_DIM = 64
HEADS_PER_GROUP = 4
GROUP_WIDTH = HEADS_PER_GROUP * HEAD_DIM
ATTN_GROUPS = ((128, 1), (512, 4), (2048, 16))
N_GROUPS = len(ATTN_GROUPS)
ATTN_WIDTH = N_GROUPS * GROUP_WIDTH
N_BUCKETS = 32
MAX_DISTANCE = 2048
POOL_WINDOWS = (2, 4, 8, 16)
POOL_STATE = max(POOL_WINDOWS) - 1
POOL_HALO = 16
EPS = 1e-6
PAST_LEN = 8192
NEG = -1e30
LANES = 128
SLABS = GROUP_WIDTH // LANES
QBLK = 128
ATTN_UNROLL = 2
VMEM_LIMIT = 50 * 1024 * 1024

F32 = jnp.float32
BF16 = jnp.bfloat16
NT_DIMS = (((1,), (1,)), ((), ()))


def _t5_buckets(distance):
    max_exact = N_BUCKETS // 2
    d = np.asarray(distance, dtype=np.int32)
    ratio = np.log(np.maximum(d, 1).astype(np.float32) / np.float32(max_exact))
    large = max_exact + (ratio / np.float32(math.log(MAX_DISTANCE / max_exact))
                         * (N_BUCKETS - max_exact)).astype(np.int32)
    large = np.minimum(large, N_BUCKETS - 1)
    return np.where(d < max_exact, d, large).astype(np.int32)


def _params(n_grid_dims=1):
    return pltpu.CompilerParams(dimension_semantics=("arbitrary",) * n_grid_dims,
                                vmem_limit_bytes=VMEM_LIMIT)


def _resident(shape):
    return pl.BlockSpec(shape, lambda *_: (0,) * len(shape), pipeline_mode=pl.Buffered(1))


_SMEM = pl.BlockSpec(memory_space=pltpu.SMEM)


def _rmsnorm(x, g):
    ms = jnp.mean(x * x, axis=-1, keepdims=True)
    return x * lax.rsqrt(ms + EPS) * g


def _head_select(parts, lane_head):
    out = jnp.where(lane_head == 0, parts[0], 0.0)
    for h in range(1, HEADS_PER_GROUP):
        out = jnp.where(lane_head == h, parts[h], out)
    return out


def _stack_heads(q, lane_head):
    return jnp.concatenate([jnp.where(lane_head == h, q, 0.0)
                            for h in range(HEADS_PER_GROUP)], axis=0).astype(BF16)


def _build_bias(bias_ref, bucket_ref, table_ref, g):
    buckets = bucket_ref[...]
    rows = buckets.shape[0]
    for h in range(HEADS_PER_GROUP):
        acc = jnp.full(buckets.shape, NEG, F32)
        for b in range(N_BUCKETS):
            acc = jnp.where(buckets == b, table_ref[b, g * HEADS_PER_GROUP + h], acc)
        bias_ref[h * rows:(h + 1) * rows, :] = acc


def _ffn_kernel(x_ref, g_ref, wup_ref, wdn_ref, o_ref, act_ref, *, d_ff, chunk):
    x = x_ref[...]
    h = _rmsnorm(x, g_ref[...]).astype(BF16)
    for c in range(d_ff // chunk):
        lo = c * chunk
        gate = jnp.dot(h, wup_ref[:, lo:lo + chunk], preferred_element_type=F32)
        up = jnp.dot(h, wup_ref[:, d_ff + lo:d_ff + lo + chunk], preferred_element_type=F32)
        act_ref[:, lo:lo + chunk] = (gate * jax.nn.sigmoid(gate) * up).astype(BF16)
    o_ref[...] = x + 0.5 * jnp.dot(act_ref[...], wdn_ref[...], preferred_element_type=F32)


def _ffn(x, gain, w_up, w_down, tm):
    m, d = x.shape
    d_ff = w_down.shape[0]
    chunk = 256
    assert m % tm == 0 and d_ff % chunk == 0
    return pl.pallas_call(
        functools.partial(_ffn_kernel, d_ff=d_ff, chunk=chunk),
        out_shape=jax.ShapeDtypeStruct((m, d), F32),
        grid=(m // tm,),
        in_specs=[pl.BlockSpec((tm, d), lambda i: (i, 0)),
                  _resident((1, d)), _resident((d, 2 * d_ff)), _resident((d_ff, d))],
        out_specs=pl.BlockSpec((tm, d), lambda i: (i, 0)),
        scratch_shapes=[pltpu.VMEM((tm, d_ff), BF16)],
        compiler_params=_params(),
        name="ffn",
    )(x, gain, w_up, w_down)


def _proj_kernel(x_ref, g_ref, win_ref, qn_ref, kn_ref, seg_ref, *refs,
                 dils, kv_rows, with_state, pool_width, d_model):
    qkv_refs = refs[0:3 * N_GROUPS]
    u_ref, ga_ref, gb_ref = refs[3 * N_GROUPS:3 * N_GROUPS + 3]
    state_refs = refs[3 * N_GROUPS + 3:-1]
    slab_ref = refs[-1]
    tm = x_ref.shape[0]
    h = _rmsnorm(x_ref[...], g_ref[...]).astype(BF16)

    def proj(lo, width):
        return jnp.dot(h, win_ref[:, lo:lo + width], preferred_element_type=F32)

    def head_norm(y, gain):
        ms = jnp.dot((y * y).astype(BF16), seg_ref[...], preferred_element_type=F32)
        return y * lax.rsqrt(ms + EPS) * gain

    def emit(dst_ref, slot, val, dil):
        if dil == 1:
            dst_ref[0, 0] = val.astype(dst_ref.dtype)
            return
        for s in range(SLABS):
            slab_ref[slot, s] = val[:, s * LANES:(s + 1) * LANES]
        n = tm // dil
        for r in range(dil):
            parts = [slab_ref[slot, s, pl.ds(r, n, stride=dil), :] for s in range(SLABS)]
            dst_ref[0, r] = jnp.concatenate(parts, axis=-1).astype(dst_ref.dtype)

    for g in range(N_GROUPS):
        lo = g * GROUP_WIDTH
        cols = slice(lo, lo + GROUP_WIDTH)
        qg = head_norm(proj(lo, GROUP_WIDTH), qn_ref[:, cols]) * (HEAD_DIM ** -0.5)
        kg = head_norm(proj(ATTN_WIDTH + lo, GROUP_WIDTH), kn_ref[:, cols])
        vg = proj(2 * ATTN_WIDTH + lo, GROUP_WIDTH)
        for which, val in enumerate((qg, kg, vg)):
            emit(qkv_refs[3 * g + which], 3 * g + which, val, dils[g])
        if with_state:
            r = kv_rows[g]

            state_refs[g][0, 0] = kg.T[:, tm - r:]
            state_refs[g][0, 1] = vg.T[:, tm - r:]
    u = proj(3 * ATTN_WIDTH, pool_width)
    u_ref[...] = u
    if with_state:
        state_refs[N_GROUPS][0] = u[tm - POOL_HALO:, :]
    ga_ref[...] = jax.nn.sigmoid(proj(3 * ATTN_WIDTH + pool_width, d_model)).astype(BF16)
    gb_ref[...] = jax.nn.sigmoid(proj(3 * ATTN_WIDTH + pool_width + d_model, d_model)).astype(BF16)


def _proj(x, gain, w_in, q_gain, k_gain, seg, *, tm, n_seq, dils, qkv_dtype, keep_rows):
    m, d = x.shape
    pool_width = w_in.shape[1] - 3 * ATTN_WIDTH - 2 * d
    seq_len = m // n_seq
    assert seq_len % tm == 0 and all(tm % dil == 0 for dil in dils)
    blocks_per_seq = seq_len // tm
    with_state = keep_rows is not None
    seq_block = lambda i: (i // blocks_per_seq, i % blocks_per_seq)

    row = lambda w: pl.BlockSpec((tm, w), lambda i: (i, 0))
    out_shape, out_specs = [], []
    for g in range(N_GROUPS):
        sds = jax.ShapeDtypeStruct((n_seq, dils[g], seq_len // dils[g], GROUP_WIDTH), qkv_dtype)
        spec = pl.BlockSpec((1, dils[g], tm // dils[g], GROUP_WIDTH),
                            lambda i: (seq_block(i)[0], 0, seq_block(i)[1], 0))
        out_shape += [sds] * 3
        out_specs += [spec] * 3
    out_shape += [jax.ShapeDtypeStruct((m, pool_width), F32),
                  jax.ShapeDtypeStruct((m, d), BF16), jax.ShapeDtypeStruct((m, d), BF16)]
    out_specs += [row(pool_width), row(d), row(d)]
    kv_rows = kv_first = ()
    if with_state:
        kv_rows = tuple(min(r, tm) for r in keep_rows)
        kv_first = tuple(blocks_per_seq - keep_rows[g] // kv_rows[g] for g in range(N_GROUPS))
        for g in range(N_GROUPS):
            out_shape.append(jax.ShapeDtypeStruct((n_seq, 2, GROUP_WIDTH, keep_rows[g]), F32))
            out_specs.append(pl.BlockSpec(
                (1, 2, GROUP_WIDTH, kv_rows[g]),
                lambda i, first=kv_first[g]: (seq_block(i)[0], 0, 0, jnp.maximum(seq_block(i)[1] - first, 0))))
        out_shape.append(jax.ShapeDtypeStruct((n_seq, POOL_HALO, pool_width), F32))
        out_specs.append(pl.BlockSpec((1, POOL_HALO, pool_width), lambda i: (seq_block(i)[0], 0, 0)))
    return pl.pallas_call(
        functools.partial(_proj_kernel, dils=dils, kv_rows=kv_rows, with_state=with_state,
                          pool_width=pool_width, d_model=d),
        out_shape=out_shape,
        grid=(m // tm,),
        in_specs=[row(d), _resident((1, d)), _resident(w_in.shape),
                  _resident((1, ATTN_WIDTH)), _resident((1, ATTN_WIDTH)),
                  _resident((GROUP_WIDTH, GROUP_WIDTH))],
        out_specs=out_specs,
        scratch_shapes=[pltpu.VMEM((3 * N_GROUPS, SLABS, tm, LANES), F32)],
        compiler_params=_params(),
        name="proj",
    )(x, gain, w_in, q_gain, k_gain, seg)


def _attn_kernel(q_ref, k_ref, v_ref, bucket_ref, table_ref, o_ref, lse_ref, bias_ref,
                 *, g, dil, n_blocks, unroll):
    res = pl.program_id(1)

    @pl.when((pl.program_id(0) == 0) & (res == 0))
    def _():
        _build_bias(bias_ref.at[1], bucket_ref, table_ref, g)
        bias_ref[0, :, 0:QBLK] = bias_ref[1, :, QBLK:2 * QBLK]
        bias_ref[0, :, QBLK:2 * QBLK] = jnp.full((HEADS_PER_GROUP * QBLK, QBLK), NEG, F32)

    lane_head = lax.broadcasted_iota(jnp.int32, (QBLK, GROUP_WIDTH), 1) // HEAD_DIM

    def body(i, carry):
        row0 = pl.multiple_of(i * QBLK, QBLK)
        span0 = pl.multiple_of(jnp.maximum(i - 1, 0) * QBLK, QBLK)
        qs = _stack_heads(q_ref[0, 0, pl.ds(row0, QBLK), :].astype(F32), lane_head)
        s = lax.dot_general(qs, k_ref[0, 0, pl.ds(span0, 2 * QBLK), :], NT_DIMS,
                            preferred_element_type=F32) + bias_ref[jnp.minimum(i, 1)]
        m = jnp.max(s, axis=-1, keepdims=True)
        p = jnp.exp(s - m)
        l = jnp.sum(p, axis=-1, keepdims=True)
        pv = jnp.dot(p.astype(BF16), v_ref[0, 0, pl.ds(span0, 2 * QBLK), :],
                     preferred_element_type=F32) / l
        lse = jnp.broadcast_to(m + jnp.log(l), pv.shape)
        rows = lambda a: [a[h * QBLK:(h + 1) * QBLK] for h in range(HEADS_PER_GROUP)]
        token0 = res + dil * row0
        dst = pl.ds(token0, QBLK) if dil == 1 else pl.ds(token0, QBLK, stride=dil)
        for out_ref, val in ((o_ref, _head_select(rows(pv), lane_head)),
                             (lse_ref, _head_select(rows(lse), lane_head))):
            for s_ in range(SLABS):
                out_ref[0, s_, dst, :] = val[:, s_ * LANES:(s_ + 1) * LANES]
        return carry

    lax.fori_loop(0, n_blocks, body, 0, unroll=unroll)


def _prompt_buckets(g):
    win, dil = ATTN_GROUPS[g]
    delta = np.arange(QBLK)[:, None] + QBLK - np.arange(2 * QBLK)[None, :]
    valid = (delta >= 0) & (delta <= win // dil)
    return np.where(valid, _t5_buckets(dil * np.clip(delta, 0, win // dil)), -1).astype(np.int32)


def _prompt_attention(q, k, v, table, g):
    _, dil = ATTN_GROUPS[g]
    batch, _, sub, _ = q.shape
    n_blocks = sub // QBLK
    assert sub % QBLK == 0 and n_blocks % ATTN_UNROLL == 0
    in_spec = pl.BlockSpec((1, 1, sub, GROUP_WIDTH), lambda b, r: (b, r, 0, 0))
    out_spec = pl.BlockSpec((1, SLABS, sub * dil, LANES), lambda b, r: (b, 0, 0, 0))
    out_sds = jax.ShapeDtypeStruct((batch, SLABS, sub * dil, LANES), F32)
    return pl.pallas_call(
        functools.partial(_attn_kernel, g=g, dil=dil, n_blocks=n_blocks, unroll=ATTN_UNROLL),
        out_shape=[out_sds, out_sds],
        grid=(batch, dil),
        in_specs=[in_spec, in_spec, in_spec, _resident((QBLK, 2 * QBLK)), _SMEM],
        out_specs=[out_spec, out_spec],
        scratch_shapes=[pltpu.VMEM((2, HEADS_PER_GROUP * QBLK, 2 * QBLK), F32)],
        compiler_params=_params(2),
        name=f"attn_g{g}",
    )(q, k, v, jnp.asarray(_prompt_buckets(g)), table)


def _sample_attn_kernel(*refs):
    qkv_refs = refs[0:9]
    cache_refs = refs[9:12]
    bucket_refs = refs[12:15]
    table_ref = refs[15]
    o_refs, lse_refs, win_refs = refs[16:19], refs[19:22], refs[22:25]
    bias_refs = refs[25:28]
    t_new = qkv_refs[0].shape[1]
    lane_head = lax.broadcasted_iota(jnp.int32, (t_new, GROUP_WIDTH), 1) // HEAD_DIM
    new_lane = lax.broadcasted_iota(jnp.int32, (GROUP_WIDTH, LANES), 1) >= LANES - t_new
    pad = jnp.zeros((LANES - t_new, GROUP_WIDTH), F32)

    for g in range(N_GROUPS):
        @pl.when(pl.program_id(0) == 0)
        def _(g=g):
            _build_bias(bias_refs[g], bucket_refs[g], table_ref, g)

        q_ref, k_ref, v_ref = qkv_refs[3 * g:3 * g + 3]
        cache_ref, win_ref, bias_ref = cache_refs[g], win_refs[g], bias_refs[g]
        past = cache_ref.shape[3]
        new_rows = [jnp.concatenate([pad, r[0]], axis=0) for r in (k_ref, v_ref)]
        for c in range(2):
            shifted = pltpu.roll(cache_ref[0, c], past - t_new, 1)
            if past > LANES:
                win_ref[0, c, :, 0:past - LANES] = shifted[:, 0:past - LANES]
            win_ref[0, c, :, past - LANES:past] = jnp.where(new_lane, new_rows[c].T, shifted[:, past - LANES:])

        qs = _stack_heads(q_ref[0], lane_head)
        s_old = jnp.dot(qs, cache_ref[0, 0].astype(BF16), preferred_element_type=F32) + bias_ref[:, 0:past]
        s_new = lax.dot_general(qs, new_rows[0].astype(BF16), NT_DIMS,
                                preferred_element_type=F32) + bias_ref[:, past:past + LANES]
        m = jnp.maximum(jnp.max(s_old, axis=-1, keepdims=True), jnp.max(s_new, axis=-1, keepdims=True))
        p_old = jnp.exp(s_old - m)
        p_new = jnp.exp(s_new - m)
        l = jnp.sum(p_old, axis=-1, keepdims=True) + jnp.sum(p_new, axis=-1, keepdims=True)
        pv = (lax.dot_general(p_old.astype(BF16), cache_ref[0, 1].astype(BF16), NT_DIMS,
                              preferred_element_type=F32)
              + jnp.dot(p_new.astype(BF16), new_rows[1].astype(BF16), preferred_element_type=F32)) / l
        lse = jnp.broadcast_to(m + jnp.log(l), pv.shape)
        rows = lambda a: [a[h * t_new:(h + 1) * t_new] for h in range(HEADS_PER_GROUP)]
        for out_ref, val in ((o_refs[g], _head_select(rows(pv), lane_head)),
                             (lse_refs[g], _head_select(rows(lse), lane_head))):
            for s_ in range(SLABS):
                out_ref[0, s_] = val[:, s_ * LANES:(s_ + 1) * LANES]


def _sample_buckets(g, past, t_new):
    win, dil = ATTN_GROUPS[g]
    lane = np.arange(past + LANES)[None, :]
    key_pos = np.where(lane < past, lane, lane - (LANES - t_new))
    delta = past + np.arange(t_new)[:, None] - key_pos
    valid = ((lane < past) | (lane >= past + LANES - t_new)) & (delta >= 0) & (delta % dil == 0) & (delta <= win)
    return np.where(valid, _t5_buckets(np.clip(delta, 0, win)), -1).astype(np.int32)


def _sample_attention(qkv, caches, table):
    batch, t_new, _ = qkv[0].shape
    pasts = [c.shape[3] for c in caches]
    buckets = [jnp.asarray(_sample_buckets(g, pasts[g], t_new)) for g in range(N_GROUPS)]
    cache_spec = lambda p: pl.BlockSpec((1, 2, GROUP_WIDTH, p), lambda b: (b, 0, 0, 0))
    new_spec = pl.BlockSpec((1, t_new, GROUP_WIDTH), lambda b: (b, 0, 0))
    out_spec = pl.BlockSpec((1, SLABS, t_new, LANES), lambda b: (0, 0, b, 0))
    out_sds = jax.ShapeDtypeStruct((1, SLABS, batch * t_new, LANES), F32)
    outs = pl.pallas_call(
        _sample_attn_kernel,
        out_shape=[out_sds] * 6 + [jax.ShapeDtypeStruct(c.shape, F32) for c in caches],
        grid=(batch,),
        in_specs=[new_spec] * 9 + [cache_spec(p) for p in pasts] + [_resident(b.shape) for b in buckets]
        + [_SMEM],
        out_specs=[out_spec] * 6 + [cache_spec(p) for p in pasts],
        scratch_shapes=[pltpu.VMEM((HEADS_PER_GROUP * t_new, p + LANES), F32) for p in pasts],
        compiler_params=_params(),
        name="sample_attn",
    )(*qkv, *caches, *buckets, table)
    return outs[0:3], outs[3:6], outs[6:9]


def _merge_kernel(x_ref, o0_ref, o1_ref, o2_ref, l0_ref, l1_ref, l2_ref, u_ref, halo_ref, ga_ref, gb_ref,
                  wab_ref, wpg_ref, pscale_ref, wpb_ref, wout_ref, out_ref, ue_ref,
                  *, pos_base, blocks_per_seq):
    tm = x_ref.shape[0]
    n_seq = halo_ref.shape[0]
    t = tm // n_seq
    pool_width = u_ref.shape[1]
    gw = pool_width // len(POOL_WINDOWS)
    wide = lambda ref: jnp.concatenate([ref[0, s] for s in range(SLABS)], axis=-1)

    lses = [wide(r) for r in (l0_ref, l1_ref, l2_ref)]
    top = jnp.maximum(jnp.maximum(lses[0], lses[1]), lses[2])
    num = jnp.zeros_like(top)
    den = jnp.zeros_like(top)
    for o_ref, lse in zip((o0_ref, o1_ref, o2_ref), lses):
        e = jnp.exp(lse - top)
        num = num + e * wide(o_ref)
        den = den + e
    attn = (num / den).astype(BF16)
    branch_a = jnp.dot(attn, wab_ref[...], preferred_element_type=F32)

    start = pos_base + (pl.program_id(0) % blocks_per_seq) * t
    u = u_ref[...]
    ue_ref[:, 0:POOL_HALO, :] = jnp.where(start > 0, halo_ref[...], 0.0)
    ue_ref[:, POOL_HALO:, :] = u.reshape(n_seq, t, pool_width)
    pos = start + lax.broadcasted_iota(jnp.int32, (1, t, gw), 1)
    pooled = []
    for gi, win in enumerate(POOL_WINDOWS):
        cols = slice(gi * gw, (gi + 1) * gw)
        s = ue_ref[:, POOL_HALO:, cols]
        for back in range(1, win):
            s = s + ue_ref[:, POOL_HALO - back:POOL_HALO - back + t, cols]
        cnt = jnp.minimum(pos + 1, win).astype(F32)
        d = (s / cnt - ue_ref[:, POOL_HALO:, cols]).reshape(tm, gw)
        pooled.append(jnp.dot(d.astype(BF16), wpg_ref[gi], preferred_element_type=F32))
    pooled = (jnp.concatenate(pooled, axis=-1) * pscale_ref[...]).astype(BF16)
    branch_b = jnp.dot(pooled, wpb_ref[...], preferred_element_type=F32)

    merged = (ga_ref[...].astype(F32) * branch_a + gb_ref[...].astype(F32) * branch_b).astype(BF16)
    out_ref[...] = x_ref[...] + jnp.dot(merged, wout_ref[...], preferred_element_type=F32)


def _merge(x, o, lse, u, halo, ga, gb, w, *, tm, halo_spec, pos_base, blocks_per_seq):
    m, d = x.shape
    pool_width = u.shape[1]
    n_seq = halo_spec.block_shape[0]
    row = lambda wd: pl.BlockSpec((tm, wd), lambda i: (i, 0))
    slab = pl.BlockSpec((1, SLABS, tm, LANES), lambda i: (i // blocks_per_seq, 0, i % blocks_per_seq, 0))
    return pl.pallas_call(
        functools.partial(_merge_kernel, pos_base=pos_base, blocks_per_seq=blocks_per_seq),
        out_shape=jax.ShapeDtypeStruct((m, d), F32),
        grid=(m // tm,),
        in_specs=[row(d)] + [slab] * 6 + [row(pool_width), halo_spec, row(d), row(d)]
        + [_resident(a.shape) for a in w],
        out_specs=row(d),
        scratch_shapes=[pltpu.VMEM((n_seq, POOL_HALO + tm // n_seq, pool_width), F32)],
        compiler_params=_params(),
        name="merge",
    )(x, *o, *lse, u, halo, ga, gb, *w)


def _window_in(cache):
    n_seq, rows = cache.shape[0:2]
    return jnp.transpose(cache, (0, 2, 3, 4, 1)).reshape(n_seq, 2, GROUP_WIDTH, rows)


def _window_out(kv):
    n_seq, _, _, rows = kv.shape
    return jnp.transpose(kv.reshape(n_seq, 2, HEADS_PER_GROUP, HEAD_DIM, rows), (0, 4, 1, 2, 3))


def kernel(x_prompt, x_sample, cache_kv_w128, cache_kv_w512, cache_kv_w2048, state_pool, rel_bias_table,
           norm_ffn1, ffn1_w_up, ffn1_w_down, norm_mix, w_in, q_norm, k_norm, pool_w_group, pool_scale,
           w_attn_branch, w_pool_branch, w_out, norm_ffn2, ffn2_w_up, ffn2_w_down):
    batch, seq, d_model = x_prompt.shape
    dec_batch, dec_seq, _ = x_sample.shape
    depth = norm_ffn1.shape[0]
    pool_width = state_pool.shape[-1]
    tm = 512
    ms = dec_batch * dec_seq
    seg = jnp.asarray(np.kron(np.eye(HEADS_PER_GROUP), np.full((HEAD_DIM, HEAD_DIM), 1.0 / HEAD_DIM)), BF16)
    table = rel_bias_table.astype(F32)
    caches_in = (cache_kv_w128, cache_kv_w512, cache_kv_w2048)
    keep_prompt = tuple(min(win, seq) for win, _ in ATTN_GROUPS)
    dils = tuple(dil for _, dil in ATTN_GROUPS)

    xp = x_prompt.reshape(batch * seq, d_model)
    xs = x_sample.reshape(ms, d_model)
    kv_p, kv_s, pool_p, pool_s = ([], [], []), ([], [], []), [], []
    for layer in range(depth):
        gain = lambda a: a[layer].reshape(1, -1).astype(F32)
        wup1, wdn1 = ffn1_w_up[layer].astype(BF16), ffn1_w_down[layer].astype(BF16)
        wup2, wdn2 = ffn2_w_up[layer].astype(BF16), ffn2_w_down[layer].astype(BF16)
        win = w_in[layer].astype(BF16)
        merge_w = (w_attn_branch[layer].astype(BF16), pool_w_group[layer].astype(BF16), gain(pool_scale),
                   w_pool_branch[layer].astype(BF16), w_out[layer].astype(BF16))
        qg, kg = gain(q_norm), gain(k_norm)

        xp = _ffn(xp, gain(norm_ffn1), wup1, wdn1, tm)
        outs = _proj(xp, gain(norm_mix), win, qg, kg, seg, tm=tm, n_seq=batch, dils=dils,
                     qkv_dtype=BF16, keep_rows=keep_prompt)
        qkv, (u, ga, gb), windows, pstate = outs[0:9], outs[9:12], outs[12:15], outs[15]
        o, lse = zip(*[_prompt_attention(*qkv[3 * g:3 * g + 3], table, g) for g in range(N_GROUPS)])
        blocks_per_seq = seq // tm
        halo_spec = pl.BlockSpec(
            (1, POOL_HALO, pool_width), lambda i: (jnp.maximum(i * (tm // POOL_HALO) - 1, 0), 0, 0))
        xp = _merge(xp, o, lse, u, u.reshape(-1, POOL_HALO, pool_width), ga, gb, merge_w, tm=tm,
                    halo_spec=halo_spec, pos_base=0, blocks_per_seq=blocks_per_seq)
        xp = _ffn(xp, gain(norm_ffn2), wup2, wdn2, tm)
        for g in range(N_GROUPS):
            kv_p[g].append(_window_out(windows[g]))
        pool_p.append(pstate[:, POOL_HALO - POOL_STATE:])

        xs = _ffn(xs, gain(norm_ffn1), wup1, wdn1, ms)
        outs = _proj(xs, gain(norm_mix), win, qg, kg, seg, tm=ms, n_seq=1, dils=(1,) * N_GROUPS,
                     qkv_dtype=F32, keep_rows=None)
        qkv, (u, ga, gb) = outs[0:9], outs[9:12]
        o, lse, windows = _sample_attention(
            [a.reshape(dec_batch, dec_seq, GROUP_WIDTH) for a in qkv],
            [_window_in(c[layer]) for c in caches_in], table)
        history = jnp.concatenate(
            [jnp.zeros((dec_batch, POOL_HALO - POOL_STATE, pool_width), F32), state_pool[layer]], axis=1)
        halo_spec = pl.BlockSpec((dec_batch, POOL_HALO, pool_width), lambda i: (0, 0, 0))
        xs = _merge(xs, o, lse, u, history, ga, gb, merge_w, tm=ms, halo_spec=halo_spec,
                    pos_base=PAST_LEN, blocks_per_seq=1)
        xs = _ffn(xs, gain(norm_ffn2), wup2, wdn2, ms)
        for g in range(N_GROUPS):
            kv_s[g].append(_window_out(windows[g]))
        ue = jnp.concatenate([state_pool[layer], u.reshape(dec_batch, dec_seq, pool_width)], axis=1)
        pool_s.append(ue[:, ue.shape[1] - POOL_STATE:])

    stack = lambda xs_: jnp.stack(xs_, axis=0)
    return (xp.reshape(batch, seq, d_model), xs.reshape(dec_batch, dec_seq, d_model),
            stack(kv_p[0]), stack(kv_p[1]), stack(kv_p[2]), stack(pool_p),
            stack(kv_s[0]), stack(kv_s[1]), stack(kv_s[2]), stack(pool_s))
```

```python
import functools
import math

import numpy as np
import jax
import jax.numpy as jnp
from jax import lax
from jax.experimental import pallas as pl
from jax.experimental.pallas import tpu as pltpu

HEAD_DIM = 64
HEADS_PER_GROUP = 4
GROUP_WIDTH = HEADS_PER_GROUP * HEAD_DIM
ATTN_GROUPS = ((128, 1), (512, 4), (2048, 16))
N_GROUPS = len(ATTN_GROUPS)
ATTN_WIDTH = N_GROUPS * GROUP_WIDTH
N_BUCKETS = 32
MAX_DISTANCE = 2048
POOL_WINDOWS = (2, 4, 8, 16)
POOL_STATE = max(POOL_WINDOWS) - 1
POOL_HALO = 16
EPS = 1e-6
PAST_LEN = 8192
NEG = -1e30
LOG2E = math.log2(math.e)
LANES = 128
SLABS = GROUP_WIDTH // LANES
QBLK = 128
ATTN_UNROLL = 8
VMEM_LIMIT = 50 * 1024 * 1024

F32 = jnp.float32
BF16 = jnp.bfloat16
NT_DIMS = (((1,), (1,)), ((), ()))


def _t5_buckets(distance):
    max_exact = N_BUCKETS // 2
    d = np.asarray(distance, dtype=np.int32)
    ratio = np.log(np.maximum(d, 1).astype(np.float32) / np.float32(max_exact))
    large = max_exact + (ratio / np.float32(math.log(MAX_DISTANCE / max_exact))
                         * (N_BUCKETS - max_exact)).astype(np.int32)
    large = np.minimum(large, N_BUCKETS - 1)
    return np.where(d < max_exact, d, large).astype(np.int32)


def _params(n_grid_dims=1):
    return pltpu.CompilerParams(dimension_semantics=("arbitrary",) * n_grid_dims,
                                vmem_limit_bytes=VMEM_LIMIT)


def _resident(shape):
    return pl.BlockSpec(shape, lambda *_: (0,) * len(shape), pipeline_mode=pl.Buffered(1))


_SMEM = pl.BlockSpec(memory_space=pltpu.SMEM)


def _rmsnorm(x, g):
    ms = jnp.mean(x * x, axis=-1, keepdims=True)
    return x * lax.rsqrt(ms + EPS) * g


def _head_select(parts, lane_head):
    out = jnp.where(lane_head == 0, parts[0], 0.0)
    for h in range(1, HEADS_PER_GROUP):
        out = jnp.where(lane_head == h, parts[h], out)
    return out


def _stack_heads(q, lane_head):
    return jnp.concatenate([jnp.where(lane_head == h, q, 0.0)
                            for h in range(HEADS_PER_GROUP)], axis=0).astype(BF16)


def _build_bias(bias_ref, bucket_ref, table_ref, g):
    buckets = bucket_ref[...]
    rows = buckets.shape[0]
    for h in range(HEADS_PER_GROUP):
        acc = jnp.full(buckets.shape, NEG, F32)
        for b in range(N_BUCKETS):
            acc = jnp.where(buckets == b, table_ref[b, g * HEADS_PER_GROUP + h] * LOG2E, acc)
        bias_ref[h * rows:(h + 1) * rows, :] = acc


def _ffn_kernel(x_ref, g_ref, wup_ref, wdn_ref, o_ref, act_ref, *, d_ff, chunk):
    x = x_ref[...]
    h = _rmsnorm(x, g_ref[...]).astype(BF16)
    for c in range(d_ff // chunk):
        lo = c * chunk
        gate = jnp.dot(h, wup_ref[:, lo:lo + chunk], preferred_element_type=F32)
        up = jnp.dot(h, wup_ref[:, d_ff + lo:d_ff + lo + chunk], preferred_element_type=F32)
        act_ref[:, lo:lo + chunk] = (gate * jax.nn.sigmoid(gate) * up).astype(BF16)
    o_ref[...] = x + 0.5 * jnp.dot(act_ref[...], wdn_ref[...], preferred_element_type=F32)


def _ffn(x, gain, w_up, w_down, tm):
    m, d = x.shape
    d_ff = w_down.shape[0]
    chunk = 256
    assert m % tm == 0 and d_ff % chunk == 0
    return pl.pallas_call(
        functools.partial(_ffn_kernel, d_ff=d_ff, chunk=chunk),
        out_shape=jax.ShapeDtypeStruct((m, d), F32),
        grid=(m // tm,),
        in_specs=[pl.BlockSpec((tm, d), lambda i: (i, 0)),
                  _resident((1, d)), _resident((d, 2 * d_ff)), _resident((d_ff, d))],
        out_specs=pl.BlockSpec((tm, d), lambda i: (i, 0)),
        scratch_shapes=[pltpu.VMEM((tm, d_ff), BF16)],
        compiler_params=_params(),
        name="ffn",
    )(x, gain, w_up, w_down)


def _proj_kernel(x_ref, g_ref, win_ref, qn_ref, kn_ref, seg_ref, *refs,
                 dils, kv_rows, with_state, pool_width, d_model):
    qkv_refs = refs[0:3 * N_GROUPS]
    u_ref, ga_ref, gb_ref = refs[3 * N_GROUPS:3 * N_GROUPS + 3]
    state_refs = refs[3 * N_GROUPS + 3:-1]
    slab_ref = refs[-1]
    tm = x_ref.shape[0]
    h = _rmsnorm(x_ref[...], g_ref[...]).astype(BF16)

    def proj(lo, width):
        return jnp.dot(h, win_ref[:, lo:lo + width], preferred_element_type=F32)

    def head_norm(y, gain):
        ms = jnp.dot((y * y).astype(BF16), seg_ref[...], preferred_element_type=F32)
        return y * lax.rsqrt(ms + EPS) * gain

    def emit(dst_ref, slot, val, dil):
        if dil == 1:
            dst_ref[0, 0] = val.astype(dst_ref.dtype)
            return
        for s in range(SLABS):
            slab_ref[slot, s] = val[:, s * LANES:(s + 1) * LANES]
        n = tm // dil
        for r in range(dil):
            parts = [slab_ref[slot, s, pl.ds(r, n, stride=dil), :] for s in range(SLABS)]
            dst_ref[0, r] = jnp.concatenate(parts, axis=-1).astype(dst_ref.dtype)

    for g in range(N_GROUPS):
        lo = g * GROUP_WIDTH
        cols = slice(lo, lo + GROUP_WIDTH)
        qg = head_norm(proj(lo, GROUP_WIDTH), qn_ref[:, cols]) * (HEAD_DIM ** -0.5 * LOG2E)
        kg = head_norm(proj(ATTN_WIDTH + lo, GROUP_WIDTH), kn_ref[:, cols])
        vg = proj(2 * ATTN_WIDTH + lo, GROUP_WIDTH)
        for which, val in enumerate((qg, kg, vg)):
            emit(qkv_refs[3 * g + which], 3 * g + which, val, dils[g])
        if with_state:
            r = kv_rows[g]

            state_refs[g][0, 0] = kg.T[:, tm - r:]
            state_refs[g][0, 1] = vg.T[:, tm - r:]
    u = proj(3 * ATTN_WIDTH, pool_width)
    u_ref[...] = u
    if with_state:
        state_refs[N_GROUPS][0] = u[tm - POOL_HALO:, :]
    ga_ref[...] = jax.nn.sigmoid(proj(3 * ATTN_WIDTH + pool_width, d_model)).astype(BF16)
    gb_ref[...] = jax.nn.sigmoid(proj(3 * ATTN_WIDTH + pool_width + d_model, d_model)).astype(BF16)


def _proj(x, gain, w_in, q_gain, k_gain, seg, *, tm, n_seq, dils, qkv_dtype, keep_rows):
    m, d = x.shape
    pool_width = w_in.shape[1] - 3 * ATTN_WIDTH - 2 * d
    seq_len = m // n_seq
    assert seq_len % tm == 0 and all(tm % dil == 0 for dil in dils)
    blocks_per_seq = seq_len // tm
    with_state = keep_rows is not None
    seq_block = lambda i: (i // blocks_per_seq, i % blocks_per_seq)

    row = lambda w: pl.BlockSpec((tm, w), lambda i: (i, 0))
    out_shape, out_specs = [], []
    for g in range(N_GROUPS):
        sds = jax.ShapeDtypeStruct((n_seq, dils[g], seq_len // dils[g], GROUP_WIDTH), qkv_dtype)
        spec = pl.BlockSpec((1, dils[g], tm // dils[g], GROUP_WIDTH),
                            lambda i: (seq_block(i)[0], 0, seq_block(i)[1], 0))
        out_shape += [sds] * 3
        out_specs += [spec] * 3
    out_shape += [jax.ShapeDtypeStruct((m, pool_width), F32),
                  jax.ShapeDtypeStruct((m, d), BF16), jax.ShapeDtypeStruct((m, d), BF16)]
    out_specs += [row(pool_width), row(d), row(d)]
    kv_rows = kv_first = ()
    if with_state:
        kv_rows = tuple(min(r, tm) for r in keep_rows)
        kv_first = tuple(blocks_per_seq - keep_rows[g] // kv_rows[g] for g in range(N_GROUPS))
        for g in range(N_GROUPS):
            out_shape.append(jax.ShapeDtypeStruct((n_seq, 2, GROUP_WIDTH, keep_rows[g]), F32))
            out_specs.append(pl.BlockSpec(
                (1, 2, GROUP_WIDTH, kv_rows[g]),
                lambda i, first=kv_first[g]: (seq_block(i)[0], 0, 0, jnp.maximum(seq_block(i)[1] - first, 0))))
        out_shape.append(jax.ShapeDtypeStruct((n_seq, POOL_HALO, pool_width), F32))
        out_specs.append(pl.BlockSpec((1, POOL_HALO, pool_width), lambda i: (seq_block(i)[0], 0, 0)))
    return pl.pallas_call(
        functools.partial(_proj_kernel, dils=dils, kv_rows=kv_rows, with_state=with_state,
                          pool_width=pool_width, d_model=d),
        out_shape=out_shape,
        grid=(m // tm,),
        in_specs=[row(d), _resident((1, d)), _resident(w_in.shape),
                  _resident((1, ATTN_WIDTH)), _resident((1, ATTN_WIDTH)),
                  _resident((GROUP_WIDTH, GROUP_WIDTH))],
        out_specs=out_specs,
        scratch_shapes=[pltpu.VMEM((3 * N_GROUPS, SLABS, tm, LANES), F32)],
        compiler_params=_params(),
        name="proj",
    )(x, gain, w_in, q_gain, k_gain, seg)


def _attn_kernel(q_ref, k_ref, v_ref, bucket_ref, hmask_ref, table_ref, o_ref, lse_ref, bias_ref,
                 *, g, dil, n_blocks, unroll):
    @pl.when(pl.program_id(0) == 0)
    def _():
        _build_bias(bias_ref.at[1], bucket_ref, table_ref, g)
        bias_ref[0, :, 0:QBLK] = bias_ref[1, :, QBLK:2 * QBLK]
        bias_ref[0, :, QBLK:2 * QBLK] = jnp.full((HEADS_PER_GROUP * QBLK, QBLK), NEG, F32)

    first_head = lax.broadcasted_iota(jnp.int32, (QBLK, LANES), 1) < HEAD_DIM
    heads = range(HEADS_PER_GROUP)

    def per_slab(col):
        part = lambda h: jnp.broadcast_to(col[h * QBLK:(h + 1) * QBLK], (QBLK, LANES))
        return [jnp.where(first_head, part(2 * s), part(2 * s + 1)) for s in range(SLABS)]

    def body(n, carry):
        res = n // n_blocks
        i = n % n_blocks
        row0 = pl.multiple_of(i * QBLK, QBLK)
        span0 = pl.multiple_of(jnp.maximum(i - 1, 0) * QBLK, QBLK)
        q = q_ref[0, res, pl.ds(row0, QBLK), :]
        qs = jnp.concatenate([q * hmask_ref[h, 0:QBLK, :] for h in heads], axis=0)
        s = lax.dot_general(qs, k_ref[0, res, pl.ds(span0, 2 * QBLK), :], NT_DIMS,
                            preferred_element_type=F32) + bias_ref[jnp.minimum(i, 1)]
        m = jnp.max(s, axis=-1, keepdims=True)
        p = jnp.exp2(s - m)
        l = jnp.sum(p, axis=-1, keepdims=True)
        p = p.astype(BF16)
        v = v_ref[0, res, pl.ds(span0, 2 * QBLK), :]
        p_wide = jnp.concatenate([p[h * QBLK:(h + 1) * QBLK] for h in heads], axis=1)
        v_tall = jnp.concatenate([v * hmask_ref[h] for h in heads], axis=0)
        pv = jnp.dot(p_wide, v_tall, preferred_element_type=F32)
        token0 = res + dil * row0
        dst = pl.ds(token0, QBLK) if dil == 1 else pl.ds(token0, QBLK, stride=dil)
        for s_, (m_s, l_s) in enumerate(zip(per_slab(m), per_slab(l))):
            o_ref[0, s_, dst, :] = pv[:, s_ * LANES:(s_ + 1) * LANES] / l_s
            lse_ref[0, s_, dst, :] = m_s + jnp.log2(l_s)
        return carry

    lax.fori_loop(0, dil * n_blocks, body, 0, unroll=unroll)


def _prompt_buckets(g):
    win, dil = ATTN_GROUPS[g]
    delta = np.arange(QBLK)[:, None] + QBLK - np.arange(2 * QBLK)[None, :]
    valid = (delta >= 0) & (delta <= win // dil)
    return np.where(valid, _t5_buckets(dil * np.clip(delta, 0, win // dil)), -1).astype(np.int32)


def _prompt_attention(q, k, v, table, g):
    _, dil = ATTN_GROUPS[g]
    batch, _, sub, _ = q.shape
    n_blocks = sub // QBLK
    assert sub % QBLK == 0 and n_blocks >= 2 and (dil * n_blocks) % ATTN_UNROLL == 0
    head_of_lane = np.arange(GROUP_WIDTH) // HEAD_DIM
    hmask = np.broadcast_to(head_of_lane[None, None, :] == np.arange(HEADS_PER_GROUP)[:, None, None],
                            (HEADS_PER_GROUP, 2 * QBLK, GROUP_WIDTH))
    in_spec = pl.BlockSpec((1, dil, sub, GROUP_WIDTH), lambda b: (b, 0, 0, 0))
    out_spec = pl.BlockSpec((1, SLABS, sub * dil, LANES), lambda b: (b, 0, 0, 0))
    out_sds = jax.ShapeDtypeStruct((batch, SLABS, sub * dil, LANES), F32)
    return pl.pallas_call(
        functools.partial(_attn_kernel, g=g, dil=dil, n_blocks=n_blocks, unroll=ATTN_UNROLL),
        out_shape=[out_sds, out_sds],
        grid=(batch,),
        in_specs=[in_spec, in_spec, in_spec, _resident((QBLK, 2 * QBLK)), _resident(hmask.shape), _SMEM],
        out_specs=[out_spec, out_spec],
        scratch_shapes=[pltpu.VMEM((2, HEADS_PER_GROUP * QBLK, 2 * QBLK), F32)],
        compiler_params=_params(),
        name=f"attn_g{g}",
    )(q, k, v, jnp.asarray(_prompt_buckets(g)), jnp.asarray(hmask, BF16), table)


def _sample_attn_kernel(*refs):
    qkv_refs = refs[0:9]
    cache_refs = refs[9:12]
    bucket_refs = refs[12:15]
    table_ref = refs[15]
    o_refs, lse_refs, win_refs = refs[16:19], refs[19:22], refs[22:25]
    bias_refs = refs[25:28]
    t_new = qkv_refs[0].shape[1]
    lane_head = lax.broadcasted_iota(jnp.int32, (t_new, GROUP_WIDTH), 1) // HEAD_DIM
    new_lane = lax.broadcasted_iota(jnp.int32, (GROUP_WIDTH, LANES), 1) >= LANES - t_new
    pad = jnp.zeros((LANES - t_new, GROUP_WIDTH), F32)

    for g in range(N_GROUPS):
        @pl.when(pl.program_id(0) == 0)
        def _(g=g):
            _build_bias(bias_refs[g], bucket_refs[g], table_ref, g)

        q_ref, k_ref, v_ref = qkv_refs[3 * g:3 * g + 3]
        cache_ref, win_ref, bias_ref = cache_refs[g], win_refs[g], bias_refs[g]
        past = cache_ref.shape[3]
        new_rows = [jnp.concatenate([pad, r[0]], axis=0) for r in (k_ref, v_ref)]
        for c in range(2):
            shifted = pltpu.roll(cache_ref[0, c], past - t_new, 1)
            if past > LANES:
                win_ref[0, c, :, 0:past - LANES] = shifted[:, 0:past - LANES]
            win_ref[0, c, :, past - LANES:past] = jnp.where(new_lane, new_rows[c].T, shifted[:, past - LANES:])

        qs = _stack_heads(q_ref[0], lane_head)
        s_old = jnp.dot(qs, cache_ref[0, 0].astype(BF16), preferred_element_type=F32) + bias_ref[:, 0:past]
        s_new = lax.dot_general(qs, new_rows[0].astype(BF16), NT_DIMS,
                                preferred_element_type=F32) + bias_ref[:, past:past + LANES]
        m = jnp.maximum(jnp.max(s_old, axis=-1, keepdims=True), jnp.max(s_new, axis=-1, keepdims=True))
        p_old = jnp.exp2(s_old - m)
        p_new = jnp.exp2(s_new - m)
        l = jnp.sum(p_old, axis=-1, keepdims=True) + jnp.sum(p_new, axis=-1, keepdims=True)
        pv = (lax.dot_general(p_old.astype(BF16), cache_ref[0, 1].astype(BF16), NT_DIMS,
                              preferred_element_type=F32)
              + jnp.dot(p_new.astype(BF16), new_rows[1].astype(BF16), preferred_element_type=F32)) / l
        lse = jnp.broadcast_to(m + jnp.log2(l), pv.shape)
        rows = lambda a: [a[h * t_new:(h + 1) * t_new] for h in range(HEADS_PER_GROUP)]
        for out_ref, val in ((o_refs[g], _head_select(rows(pv), lane_head)),
                             (lse_refs[g], _head_select(rows(lse), lane_head))):
            for s_ in range(SLABS):
                out_ref[0, s_] = val[:, s_ * LANES:(s_ + 1) * LANES]


def _sample_buckets(g, past, t_new):
    win, dil = ATTN_GROUPS[g]
    lane = np.arange(past + LANES)[None, :]
    key_pos = np.where(lane < past, lane, lane - (LANES - t_new))
    delta = past + np.arange(t_new)[:, None] - key_pos
    valid = ((lane < past) | (lane >= past + LANES - t_new)) & (delta >= 0) & (delta % dil == 0) & (delta <= win)
    return np.where(valid, _t5_buckets(np.clip(delta, 0, win)), -1).astype(np.int32)


def _sample_attention(qkv, caches, table):
    batch, t_new, _ = qkv[0].shape
    pasts = [c.shape[3] for c in caches]
    buckets = [jnp.asarray(_sample_buckets(g, pasts[g], t_new)) for g in range(N_GROUPS)]
    cache_spec = lambda p: pl.BlockSpec((1, 2, GROUP_WIDTH, p), lambda b: (b, 0, 0, 0))
    new_spec = pl.BlockSpec((1, t_new, GROUP_WIDTH), lambda b: (b, 0, 0))
    out_spec = pl.BlockSpec((1, SLABS, t_new, LANES), lambda b: (0, 0, b, 0))
    out_sds = jax.ShapeDtypeStruct((1, SLABS, batch * t_new, LANES), F32)
    outs = pl.pallas_call(
        _sample_attn_kernel,
        out_shape=[out_sds] * 6 + [jax.ShapeDtypeStruct(c.shape, F32) for c in caches],
        grid=(batch,),
        in_specs=[new_spec] * 9 + [cache_spec(p) for p in pasts] + [_resident(b.shape) for b in buckets]
        + [_SMEM],
        out_specs=[out_spec] * 6 + [cache_spec(p) for p in pasts],
        scratch_shapes=[pltpu.VMEM((HEADS_PER_GROUP * t_new, p + LANES), F32) for p in pasts],
        compiler_params=_params(),
        name="sample_attn",
    )(*qkv, *caches, *buckets, table)
    return outs[0:3], outs[3:6], outs[6:9]


def _merge_kernel(x_ref, o0_ref, o1_ref, o2_ref, l0_ref, l1_ref, l2_ref, u_ref, halo_ref, ga_ref, gb_ref,
                  wab_ref, wpg_ref, pscale_ref, wpb_ref, wout_ref, out_ref, ue_ref,
                  *, pos_base, blocks_per_seq):
    tm = x_ref.shape[0]
    n_seq = halo_ref.shape[0]
    t = tm // n_seq
    pool_width = u_ref.shape[1]
    gw = pool_width // len(POOL_WINDOWS)
    wide = lambda ref: jnp.concatenate([ref[0, s] for s in range(SLABS)], axis=-1)

    lses = [wide(r) for r in (l0_ref, l1_ref, l2_ref)]
    top = jnp.maximum(jnp.maximum(lses[0], lses[1]), lses[2])
    num = jnp.zeros_like(top)
    den = jnp.zeros_like(top)
    for o_ref, lse in zip((o0_ref, o1_ref, o2_ref), lses):
        e = jnp.exp2(lse - top)
        num = num + e * wide(o_ref)
        den = den + e
    attn = (num / den).astype(BF16)
    branch_a = jnp.dot(attn, wab_ref[...], preferred_element_type=F32)

    start = pos_base + (pl.program_id(0) % blocks_per_seq) * t
    u = u_ref[...]
    ue_ref[:, 0:POOL_HALO, :] = jnp.where(start > 0, halo_ref[...], 0.0)
    ue_ref[:, POOL_HALO:, :] = u.reshape(n_seq, t, pool_width)
    pos = start + lax.broadcasted_iota(jnp.int32, (1, t, gw), 1)
    pooled = []
    for gi, win in enumerate(POOL_WINDOWS):
        cols = slice(gi * gw, (gi + 1) * gw)
        s = ue_ref[:, POOL_HALO:, cols]
        for back in range(1, win):
            s = s + ue_ref[:, POOL_HALO - back:POOL_HALO - back + t, cols]
        cnt = jnp.minimum(pos + 1, win).astype(F32)
        d = (s / cnt - ue_ref[:, POOL_HALO:, cols]).reshape(tm, gw)
        pooled.append(jnp.dot(d.astype(BF16), wpg_ref[gi], preferred_element_type=F32))
    pooled = (jnp.concatenate(pooled, axis=-1) * pscale_ref[...]).astype(BF16)
    branch_b = jnp.dot(pooled, wpb_ref[...], preferred_element_type=F32)

    merged = (ga_ref[...].astype(F32) * branch_a + gb_ref[...].astype(F32) * branch_b).astype(BF16)
    out_ref[...] = x_ref[...] + jnp.dot(merged, wout_ref[...], preferred_element_type=F32)


def _merge(x, o, lse, u, halo, ga, gb, w, *, tm, halo_spec, pos_base, blocks_per_seq):
    m, d = x.shape
    pool_width = u.shape[1]
    n_seq = halo_spec.block_shape[0]
    row = lambda wd: pl.BlockSpec((tm, wd), lambda i: (i, 0))
    slab = pl.BlockSpec((1, SLABS, tm, LANES), lambda i: (i // blocks_per_seq, 0, i % blocks_per_seq, 0))
    return pl.pallas_call(
        functools.partial(_merge_kernel, pos_base=pos_base, blocks_per_seq=blocks_per_seq),
        out_shape=jax.ShapeDtypeStruct((m, d), F32),
        grid=(m // tm,),
        in_specs=[row(d)] + [slab] * 6 + [row(pool_width), halo_spec, row(d), row(d)]
        + [_resident(a.shape) for a in w],
        out_specs=row(d),
        scratch_shapes=[pltpu.VMEM((n_seq, POOL_HALO + tm // n_seq, pool_width), F32)],
        compiler_params=_params(),
        name="merge",
    )(x, *o, *lse, u, halo, ga, gb, *w)


def _window_in(cache):
    n_seq, rows = cache.shape[0:2]
    return jnp.transpose(cache, (0, 2, 3, 4, 1)).reshape(n_seq, 2, GROUP_WIDTH, rows)


def _window_out(kv):
    n_seq, _, _, rows = kv.shape
    return jnp.transpose(kv.reshape(n_seq, 2, HEADS_PER_GROUP, HEAD_DIM, rows), (0, 4, 1, 2, 3))


def kernel(x_prompt, x_sample, cache_kv_w128, cache_kv_w512, cache_kv_w2048, state_pool, rel_bias_table,
           norm_ffn1, ffn1_w_up, ffn1_w_down, norm_mix, w_in, q_norm, k_norm, pool_w_group, pool_scale,
           w_attn_branch, w_pool_branch, w_out, norm_ffn2, ffn2_w_up, ffn2_w_down):
    batch, seq, d_model = x_prompt.shape
    dec_batch, dec_seq, _ = x_sample.shape
    depth = norm_ffn1.shape[0]
    pool_width = state_pool.shape[-1]
    tm = 512
    ms = dec_batch * dec_seq
    seg = jnp.asarray(np.kron(np.eye(HEADS_PER_GROUP), np.full((HEAD_DIM, HEAD_DIM), 1.0 / HEAD_DIM)), BF16)
    table = rel_bias_table.astype(F32)
    caches_in = (cache_kv_w128, cache_kv_w512, cache_kv_w2048)
    keep_prompt = tuple(min(win, seq) for win, _ in ATTN_GROUPS)
    dils = tuple(dil for _, dil in ATTN_GROUPS)

    xp = x_prompt.reshape(batch * seq, d_model)
    xs = x_sample.reshape(ms, d_model)
    kv_p, kv_s, pool_p, pool_s = ([], [], []), ([], [], []), [], []
    for layer in range(depth):
        gain = lambda a: a[layer].reshape(1, -1).astype(F32)
        wup1, wdn1 = ffn1_w_up[layer].astype(BF16), ffn1_w_down[layer].astype(BF16)
        wup2, wdn2 = ffn2_w_up[layer].astype(BF16), ffn2_w_down[layer].astype(BF16)
        win = w_in[layer].astype(BF16)
        merge_w = (w_attn_branch[layer].astype(BF16), pool_w_group[layer].astype(BF16), gain(pool_scale),
                   w_pool_branch[layer].astype(BF16), w_out[layer].astype(BF16))
        qg, kg = gain(q_norm), gain(k_norm)

        xp = _ffn(xp, gain(norm_ffn1), wup1, wdn1, tm)
        outs = _proj(xp, gain(norm_mix), win, qg, kg, seg, tm=tm, n_seq=batch, dils=dils,
                     qkv_dtype=BF16, keep_rows=keep_prompt)
        qkv, (u, ga, gb), windows, pstate = outs[0:9], outs[9:12], outs[12:15], outs[15]
        o, lse = zip(*[_prompt_attention(*qkv[3 * g:3 * g + 3], table, g) for g in range(N_GROUPS)])
        blocks_per_seq = seq // tm
        halo_spec = pl.BlockSpec(
            (1, POOL_HALO, pool_width), lambda i: (jnp.maximum(i * (tm // POOL_HALO) - 1, 0), 0, 0))
        xp = _merge(xp, o, lse, u, u.reshape(-1, POOL_HALO, pool_width), ga, gb, merge_w, tm=tm,
                    halo_spec=halo_spec, pos_base=0, blocks_per_seq=blocks_per_seq)
        xp = _ffn(xp, gain(norm_ffn2), wup2, wdn2, tm)
        for g in range(N_GROUPS):
            kv_p[g].append(_window_out(windows[g]))
        pool_p.append(pstate[:, POOL_HALO - POOL_STATE:])

        xs = _ffn(xs, gain(norm_ffn1), wup1, wdn1, ms)
        outs = _proj(xs, gain(norm_mix), win, qg, kg, seg, tm=ms, n_seq=1, dils=(1,) * N_GROUPS,
                     qkv_dtype=F32, keep_rows=None)
        qkv, (u, ga, gb) = outs[0:9], outs[9:12]
        o, lse, windows = _sample_attention(
            [a.reshape(dec_batch, dec_seq, GROUP_WIDTH) for a in qkv],
            [_window_in(c[layer]) for c in caches_in], table)
        history = jnp.concatenate(
            [jnp.zeros((dec_batch, POOL_HALO - POOL_STATE, pool_width), F32), state_pool[layer]], axis=1)
        halo_spec = pl.BlockSpec((dec_batch, POOL_HALO, pool_width), lambda i: (0, 0, 0))
        xs = _merge(xs, o, lse, u, history, ga, gb, merge_w, tm=ms, halo_spec=halo_spec,
                    pos_base=PAST_LEN, blocks_per_seq=1)
        xs = _ffn(xs, gain(norm_ffn2), wup2, wdn2, ms)
        for g in range(N_GROUPS):
            kv_s[g].append(_window_out(windows[g]))
        ue = jnp.concatenate([state_pool[layer], u.reshape(dec_batch, dec_seq, pool_width)], axis=1)
        pool_s.append(ue[:, ue.shape[1] - POOL_STATE:])

    stack = lambda xs_: jnp.stack(xs_, axis=0)
    return (xp.reshape(batch, seq, d_model), xs.reshape(dec_batch, dec_seq, d_model),
            stack(kv_p[0]), stack(kv_p[1]), stack(kv_p[2]), stack(pool_p),
            stack(kv_s[0]), stack(kv_s[1]), stack(kv_s[2]), stack(pool_s))
```

```python
import functools
import math

import numpy as np
import jax
import jax.numpy as jnp
from jax import lax
from jax.experimental import pallas as pl
from jax.experimental.pallas import tpu as pltpu

HEAD_DIM = 64
HEADS_PER_GROUP = 4
GROUP_WIDTH = HEADS_PER_GROUP * HEAD_DIM
ATTN_GROUPS = ((128, 1), (512, 4), (2048, 16))
N_GROUPS = len(ATTN_GROUPS)
ATTN_WIDTH = N_GROUPS * GROUP_WIDTH
N_BUCKETS = 32
MAX_DISTANCE = 2048
POOL_WINDOWS = (2, 4, 8, 16)
POOL_STATE = max(POOL_WINDOWS) - 1
POOL_HALO = 16
EPS = 1e-6
PAST_LEN = 8192
NEG = -1e30
LOG2E = math.log2(math.e)
LANES = 128
MXU_TILE = 256
SLABS = GROUP_WIDTH // LANES
QBLK = 128
ATTN_UNROLL = 8
VMEM_LIMIT = 58 * 1024 * 1024

F32 = jnp.float32
BF16 = jnp.bfloat16
NT_DIMS = (((1,), (1,)), ((), ()))


def _t5_buckets(distance):
    max_exact = N_BUCKETS // 2
    d = np.asarray(distance, dtype=np.int32)
    ratio = np.log(np.maximum(d, 1).astype(np.float32) / np.float32(max_exact))
    large = max_exact + (ratio / np.float32(math.log(MAX_DISTANCE / max_exact))
                         * (N_BUCKETS - max_exact)).astype(np.int32)
    large = np.minimum(large, N_BUCKETS - 1)
    return np.where(d < max_exact, d, large).astype(np.int32)


def _params(n_grid_dims=1):
    return pltpu.CompilerParams(dimension_semantics=("arbitrary",) * n_grid_dims,
                                vmem_limit_bytes=VMEM_LIMIT)


def _resident(shape):
    return pl.BlockSpec(shape, lambda *_: (0,) * len(shape), pipeline_mode=pl.Buffered(1))


_SMEM = pl.BlockSpec(memory_space=pltpu.SMEM)


def _rmsnorm(x, g):
    ms = jnp.mean(x * x, axis=-1, keepdims=True)
    return x * lax.rsqrt(ms + EPS) * g


def _head_select(parts, lane_head):
    out = jnp.where(lane_head == 0, parts[0], 0.0)
    for h in range(1, HEADS_PER_GROUP):
        out = jnp.where(lane_head == h, parts[h], out)
    return out


def _stack_heads(q, lane_head):
    return jnp.concatenate([jnp.where(lane_head == h, q, 0.0)
                            for h in range(HEADS_PER_GROUP)], axis=0).astype(BF16)


def _build_bias(bias_ref, bucket_ref, table_ref, g):
    buckets = bucket_ref[...]
    rows = buckets.shape[0]
    for h in range(HEADS_PER_GROUP):
        acc = jnp.full(buckets.shape, NEG, F32)
        for b in range(N_BUCKETS):
            acc = jnp.where(buckets == b, table_ref[b, g * HEADS_PER_GROUP + h] * LOG2E, acc)
        bias_ref[h * rows:(h + 1) * rows, :] = acc


def _interleave(*stages):
    stages = list(stages)
    while stages:
        for s in list(stages):
            if next(s, StopIteration) is StopIteration:
                stages.remove(s)


def _ffn_stages(x_ref, g_ref, wup_ref, wdn_ref, o_ref, act_ref):
    d_ff, d = wdn_ref.shape
    h = _rmsnorm(x_ref[...], g_ref[...]).astype(BF16)
    yield
    for lo in range(0, d_ff, MXU_TILE):
        gate = jnp.dot(h, wup_ref[:, lo:lo + MXU_TILE], preferred_element_type=F32)
        up = jnp.dot(h, wup_ref[:, d_ff + lo:d_ff + lo + MXU_TILE], preferred_element_type=F32)
        act_ref[:, lo:lo + MXU_TILE] = (gate * jax.nn.sigmoid(gate) * up).astype(BF16)
        yield
    for lo in range(0, d, MXU_TILE):
        cols = slice(lo, lo + MXU_TILE)
        o_ref[:, cols] = x_ref[:, cols] + 0.5 * jnp.dot(act_ref[...], wdn_ref[:, cols],
                                                        preferred_element_type=F32)
        yield


def _ffn_kernel(x_ref, g_ref, wup_ref, wdn_ref, o_ref, act_ref):
    _interleave(_ffn_stages(x_ref, g_ref, wup_ref, wdn_ref, o_ref, act_ref))


def _ffn(x, gain, w_up, w_down, tm):
    m, d = x.shape
    d_ff = w_down.shape[0]
    assert m % tm == 0 and d_ff % MXU_TILE == 0 and d % MXU_TILE == 0
    return pl.pallas_call(
        _ffn_kernel,
        out_shape=jax.ShapeDtypeStruct((m, d), F32),
        grid=(m // tm,),
        in_specs=[pl.BlockSpec((tm, d), lambda i: (i, 0)),
                  _resident((1, d)), _resident((d, 2 * d_ff)), _resident((d_ff, d))],
        out_specs=pl.BlockSpec((tm, d), lambda i: (i, 0)),
        scratch_shapes=[pltpu.VMEM((tm, d_ff), BF16)],
        compiler_params=_params(),
        name="ffn",
    )(x, gain, w_up, w_down)


def _proj_kernel(x_ref, g_ref, win_ref, qn_ref, kn_ref, seg_ref, *refs,
                 dils, kv_rows, with_state, pool_width, d_model):
    qkv_refs = refs[0:3 * N_GROUPS]
    u_ref, ga_ref, gb_ref = refs[3 * N_GROUPS:3 * N_GROUPS + 3]
    state_refs = refs[3 * N_GROUPS + 3:-1]
    slab_ref = refs[-1]
    tm = x_ref.shape[0]
    h = _rmsnorm(x_ref[...], g_ref[...]).astype(BF16)

    def proj(lo, width):
        return jnp.dot(h, win_ref[:, lo:lo + width], preferred_element_type=F32)

    def head_norm(y, gain):
        ms = jnp.dot((y * y).astype(BF16), seg_ref[...], preferred_element_type=F32)
        return y * lax.rsqrt(ms + EPS) * gain

    def emit(dst_ref, slot, val, dil):
        if dil == 1:
            dst_ref[0, 0] = val.astype(dst_ref.dtype)
            return
        for s in range(SLABS):
            slab_ref[slot, s] = val[:, s * LANES:(s + 1) * LANES]
        n = tm // dil
        for r in range(dil):
            parts = [slab_ref[slot, s, pl.ds(r, n, stride=dil), :] for s in range(SLABS)]
            dst_ref[0, r] = jnp.concatenate(parts, axis=-1).astype(dst_ref.dtype)

    for g in range(N_GROUPS):
        lo = g * GROUP_WIDTH
        cols = slice(lo, lo + GROUP_WIDTH)
        qg = head_norm(proj(lo, GROUP_WIDTH), qn_ref[:, cols]) * (HEAD_DIM ** -0.5 * LOG2E)
        kg = head_norm(proj(ATTN_WIDTH + lo, GROUP_WIDTH), kn_ref[:, cols])
        vg = proj(2 * ATTN_WIDTH + lo, GROUP_WIDTH)
        for which, val in enumerate((qg, kg, vg)):
            emit(qkv_refs[3 * g + which], 3 * g + which, val, dils[g])
        if with_state:
            r = kv_rows[g]

            state_refs[g][0, 0] = kg.T[:, tm - r:]
            state_refs[g][0, 1] = vg.T[:, tm - r:]
    u = proj(3 * ATTN_WIDTH, pool_width)
    u_ref[...] = u
    if with_state:
        state_refs[N_GROUPS][0] = u[tm - POOL_HALO:, :]
    for gate_ref, base in ((ga_ref, 3 * ATTN_WIDTH + pool_width), (gb_ref, 3 * ATTN_WIDTH + pool_width + d_model)):
        for lo in range(0, d_model, MXU_TILE):
            gate_ref[:, lo:lo + MXU_TILE] = jax.nn.sigmoid(proj(base + lo, MXU_TILE)).astype(BF16)


def _proj(x, gain, w_in, q_gain, k_gain, seg, *, tm, n_seq, dils, qkv_dtype, keep_rows):
    m, d = x.shape
    pool_width = w_in.shape[1] - 3 * ATTN_WIDTH - 2 * d
    seq_len = m // n_seq
    assert seq_len % tm == 0 and all(tm % dil == 0 for dil in dils)
    blocks_per_seq = seq_len // tm
    with_state = keep_rows is not None
    seq_block = lambda i: (i // blocks_per_seq, i % blocks_per_seq)

    row = lambda w: pl.BlockSpec((tm, w), lambda i: (i, 0))
    out_shape, out_specs = [], []
    for g in range(N_GROUPS):
        sds = jax.ShapeDtypeStruct((n_seq, dils[g], seq_len // dils[g], GROUP_WIDTH), qkv_dtype)
        spec = pl.BlockSpec((1, dils[g], tm // dils[g], GROUP_WIDTH),
                            lambda i: (seq_block(i)[0], 0, seq_block(i)[1], 0))
        out_shape += [sds] * 3
        out_specs += [spec] * 3
    out_shape += [jax.ShapeDtypeStruct((m, pool_width), F32),
                  jax.ShapeDtypeStruct((m, d), BF16), jax.ShapeDtypeStruct((m, d), BF16)]
    out_specs += [row(pool_width), row(d), row(d)]
    kv_rows = kv_first = ()
    if with_state:
        kv_rows = tuple(min(r, tm) for r in keep_rows)
        kv_first = tuple(blocks_per_seq - keep_rows[g] // kv_rows[g] for g in range(N_GROUPS))
        for g in range(N_GROUPS):
            out_shape.append(jax.ShapeDtypeStruct((n_seq, 2, GROUP_WIDTH, keep_rows[g]), F32))
            out_specs.append(pl.BlockSpec(
                (1, 2, GROUP_WIDTH, kv_rows[g]),
                lambda i, first=kv_first[g]: (seq_block(i)[0], 0, 0, jnp.maximum(seq_block(i)[1] - first, 0))))
        out_shape.append(jax.ShapeDtypeStruct((n_seq, POOL_HALO, pool_width), F32))
        out_specs.append(pl.BlockSpec((1, POOL_HALO, pool_width), lambda i: (seq_block(i)[0], 0, 0)))
    return pl.pallas_call(
        functools.partial(_proj_kernel, dils=dils, kv_rows=kv_rows, with_state=with_state,
                          pool_width=pool_width, d_model=d),
        out_shape=out_shape,
        grid=(m // tm,),
        in_specs=[row(d), _resident((1, d)), _resident(w_in.shape),
                  _resident((1, ATTN_WIDTH)), _resident((1, ATTN_WIDTH)),
                  _resident((GROUP_WIDTH, GROUP_WIDTH))],
        out_specs=out_specs,
        scratch_shapes=[pltpu.VMEM((3 * N_GROUPS, SLABS, tm, LANES), F32)],
        compiler_params=_params(),
        name="proj",
    )(x, gain, w_in, q_gain, k_gain, seg)


def _attn_kernel(q_ref, k_ref, v_ref, bucket_ref, hmask_ref, table_ref, o_ref, lse_ref, bias_ref,
                 *, g, dil, n_blocks, unroll):
    @pl.when(pl.program_id(0) == 0)
    def _():
        _build_bias(bias_ref.at[1], bucket_ref, table_ref, g)
        bias_ref[0, :, 0:QBLK] = bias_ref[1, :, QBLK:2 * QBLK]
        bias_ref[0, :, QBLK:2 * QBLK] = jnp.full((HEADS_PER_GROUP * QBLK, QBLK), NEG, F32)

    first_head = lax.broadcasted_iota(jnp.int32, (QBLK, LANES), 1) < HEAD_DIM
    heads = range(HEADS_PER_GROUP)

    def per_slab(col):
        part = lambda h: jnp.broadcast_to(col[h * QBLK:(h + 1) * QBLK], (QBLK, LANES))
        return [jnp.where(first_head, part(2 * s), part(2 * s + 1)) for s in range(SLABS)]

    def body(n, carry):
        res = n // n_blocks
        i = n % n_blocks
        row0 = pl.multiple_of(i * QBLK, QBLK)
        span0 = pl.multiple_of(jnp.maximum(i - 1, 0) * QBLK, QBLK)
        q = q_ref[0, res, pl.ds(row0, QBLK), :]
        qs = jnp.concatenate([q * hmask_ref[h, 0:QBLK, :] for h in heads], axis=0)
        s = lax.dot_general(qs, k_ref[0, res, pl.ds(span0, 2 * QBLK), :], NT_DIMS,
                            preferred_element_type=F32) + bias_ref[jnp.minimum(i, 1)]
        m = jnp.max(s, axis=-1, keepdims=True)
        p = jnp.exp2(s - m)
        l = jnp.sum(p, axis=-1, keepdims=True)
        p = p.astype(BF16)
        v = v_ref[0, res, pl.ds(span0, 2 * QBLK), :]
        p_wide = jnp.concatenate([p[h * QBLK:(h + 1) * QBLK] for h in heads], axis=1)
        v_tall = jnp.concatenate([v * hmask_ref[h] for h in heads], axis=0)
        pv = jnp.dot(p_wide, v_tall, preferred_element_type=F32)
        token0 = res + dil * row0
        dst = pl.ds(token0, QBLK) if dil == 1 else pl.ds(token0, QBLK, stride=dil)
        for s_, (m_s, l_s) in enumerate(zip(per_slab(m), per_slab(l))):
            o_ref[0, s_, dst, :] = pv[:, s_ * LANES:(s_ + 1) * LANES] / l_s
            lse_ref[0, s_, dst, :] = m_s + jnp.log2(l_s)
        return carry

    lax.fori_loop(0, dil * n_blocks, body, 0, unroll=unroll)


def _prompt_buckets(g):
    win, dil = ATTN_GROUPS[g]
    delta = np.arange(QBLK)[:, None] + QBLK - np.arange(2 * QBLK)[None, :]
    valid = (delta >= 0) & (delta <= win // dil)
    return np.where(valid, _t5_buckets(dil * np.clip(delta, 0, win // dil)), -1).astype(np.int32)


def _prompt_attention(q, k, v, table, g):
    _, dil = ATTN_GROUPS[g]
    batch, _, sub, _ = q.shape
    n_blocks = sub // QBLK
    assert sub % QBLK == 0 and n_blocks >= 2 and (dil * n_blocks) % ATTN_UNROLL == 0
    head_of_lane = np.arange(GROUP_WIDTH) // HEAD_DIM
    hmask = np.broadcast_to(head_of_lane[None, None, :] == np.arange(HEADS_PER_GROUP)[:, None, None],
                            (HEADS_PER_GROUP, 2 * QBLK, GROUP_WIDTH))
    in_spec = pl.BlockSpec((1, dil, sub, GROUP_WIDTH), lambda b: (b, 0, 0, 0))
    out_spec = pl.BlockSpec((1, SLABS, sub * dil, LANES), lambda b: (b, 0, 0, 0))
    out_sds = jax.ShapeDtypeStruct((batch, SLABS, sub * dil, LANES), F32)
    return pl.pallas_call(
        functools.partial(_attn_kernel, g=g, dil=dil, n_blocks=n_blocks, unroll=ATTN_UNROLL),
        out_shape=[out_sds, out_sds],
        grid=(batch,),
        in_specs=[in_spec, in_spec, in_spec, _resident((QBLK, 2 * QBLK)), _resident(hmask.shape), _SMEM],
        out_specs=[out_spec, out_spec],
        scratch_shapes=[pltpu.VMEM((2, HEADS_PER_GROUP * QBLK, 2 * QBLK), F32)],
        compiler_params=_params(),
        name=f"attn_g{g}",
    )(q, k, v, jnp.asarray(_prompt_buckets(g)), jnp.asarray(hmask, BF16), table)


def _sample_attn_kernel(*refs):
    qkv_refs = refs[0:9]
    cache_refs = refs[9:12]
    bucket_refs = refs[12:15]
    table_ref = refs[15]
    o_refs, lse_refs, win_refs = refs[16:19], refs[19:22], refs[22:25]
    bias_refs = refs[25:28]
    t_new = qkv_refs[0].shape[1]
    lane_head = lax.broadcasted_iota(jnp.int32, (t_new, GROUP_WIDTH), 1) // HEAD_DIM
    new_lane = lax.broadcasted_iota(jnp.int32, (GROUP_WIDTH, LANES), 1) >= LANES - t_new
    pad = jnp.zeros((LANES - t_new, GROUP_WIDTH), F32)

    for g in range(N_GROUPS):
        @pl.when(pl.program_id(0) == 0)
        def _(g=g):
            _build_bias(bias_refs[g], bucket_refs[g], table_ref, g)

        q_ref, k_ref, v_ref = qkv_refs[3 * g:3 * g + 3]
        cache_ref, win_ref, bias_ref = cache_refs[g], win_refs[g], bias_refs[g]
        past = cache_ref.shape[3]
        new_rows = [jnp.concatenate([pad, r[0]], axis=0) for r in (k_ref, v_ref)]
        for c in range(2):
            shifted = pltpu.roll(cache_ref[0, c], past - t_new, 1)
            if past > LANES:
                win_ref[0, c, :, 0:past - LANES] = shifted[:, 0:past - LANES]
            win_ref[0, c, :, past - LANES:past] = jnp.where(new_lane, new_rows[c].T, shifted[:, past - LANES:])

        qs = _stack_heads(q_ref[0], lane_head)
        s_old = jnp.dot(qs, cache_ref[0, 0].astype(BF16), preferred_element_type=F32) + bias_ref[:, 0:past]
        s_new = lax.dot_general(qs, new_rows[0].astype(BF16), NT_DIMS,
                                preferred_element_type=F32) + bias_ref[:, past:past + LANES]
        m = jnp.maximum(jnp.max(s_old, axis=-1, keepdims=True), jnp.max(s_new, axis=-1, keepdims=True))
        p_old = jnp.exp2(s_old - m)
        p_new = jnp.exp2(s_new - m)
        l = jnp.sum(p_old, axis=-1, keepdims=True) + jnp.sum(p_new, axis=-1, keepdims=True)
        pv = (lax.dot_general(p_old.astype(BF16), cache_ref[0, 1].astype(BF16), NT_DIMS,
                              preferred_element_type=F32)
              + jnp.dot(p_new.astype(BF16), new_rows[1].astype(BF16), preferred_element_type=F32)) / l
        lse = jnp.broadcast_to(m + jnp.log2(l), pv.shape)
        rows = lambda a: [a[h * t_new:(h + 1) * t_new] for h in range(HEADS_PER_GROUP)]
        for out_ref, val in ((o_refs[g], _head_select(rows(pv), lane_head)),
                             (lse_refs[g], _head_select(rows(lse), lane_head))):
            for s_ in range(SLABS):
                out_ref[0, s_] = val[:, s_ * LANES:(s_ + 1) * LANES]


def _sample_buckets(g, past, t_new):
    win, dil = ATTN_GROUPS[g]
    lane = np.arange(past + LANES)[None, :]
    key_pos = np.where(lane < past, lane, lane - (LANES - t_new))
    delta = past + np.arange(t_new)[:, None] - key_pos
    valid = ((lane < past) | (lane >= past + LANES - t_new)) & (delta >= 0) & (delta % dil == 0) & (delta <= win)
    return np.where(valid, _t5_buckets(np.clip(delta, 0, win)), -1).astype(np.int32)


def _sample_attention(qkv, caches, table):
    batch, t_new, _ = qkv[0].shape
    pasts = [c.shape[3] for c in caches]
    buckets = [jnp.asarray(_sample_buckets(g, pasts[g], t_new)) for g in range(N_GROUPS)]
    cache_spec = lambda p: pl.BlockSpec((1, 2, GROUP_WIDTH, p), lambda b: (b, 0, 0, 0))
    new_spec = pl.BlockSpec((1, t_new, GROUP_WIDTH), lambda b: (b, 0, 0))
    out_spec = pl.BlockSpec((1, SLABS, t_new, LANES), lambda b: (0, 0, b, 0))
    out_sds = jax.ShapeDtypeStruct((1, SLABS, batch * t_new, LANES), F32)
    outs = pl.pallas_call(
        _sample_attn_kernel,
        out_shape=[out_sds] * 6 + [jax.ShapeDtypeStruct(c.shape, F32) for c in caches],
        grid=(batch,),
        in_specs=[new_spec] * 9 + [cache_spec(p) for p in pasts] + [_resident(b.shape) for b in buckets]
        + [_SMEM],
        out_specs=[out_spec] * 6 + [cache_spec(p) for p in pasts],
        scratch_shapes=[pltpu.VMEM((HEADS_PER_GROUP * t_new, p + LANES), F32) for p in pasts],
        compiler_params=_params(),
        name="sample_attn",
    )(*qkv, *caches, *buckets, table)
    return outs[0:3], outs[3:6], outs[6:9]


def _back_kernel(x_ref, o0_ref, o1_ref, o2_ref, l0_ref, l1_ref, l2_ref, u_ref, halo_ref, ga_ref, gb_ref,
                 wab_ref, wpg_ref, pscale_ref, wpb_ref, wout_ref, g2_ref, wup_ref, wdn_ref,
                 out_ref, ue_ref, mixed_ref, act_ref, attn_ref, pooled_ref, merged_ref,
                 *, pos_base, blocks_per_seq):
    block = pl.program_id(0)
    tm, d_model = x_ref.shape
    n_seq = halo_ref.shape[0]
    t = tm // n_seq
    pool_width = u_ref.shape[1]
    gw = pool_width // len(POOL_WINDOWS)

    def merge_stages():
        for s in range(SLABS):
            lses = [r[0, s] for r in (l0_ref, l1_ref, l2_ref)]
            top = jnp.maximum(jnp.maximum(lses[0], lses[1]), lses[2])
            num = jnp.zeros_like(top)
            den = jnp.zeros_like(top)
            for o_ref, lse in zip((o0_ref, o1_ref, o2_ref), lses):
                e = jnp.exp2(lse - top)
                num = num + e * o_ref[0, s]
                den = den + e
            attn_ref[:, s * LANES:(s + 1) * LANES] = (num / den).astype(BF16)
            yield

        start = pos_base + (block % blocks_per_seq) * t
        ue_ref[:, 0:POOL_HALO, :] = jnp.where(start > 0, halo_ref[...], 0.0)
        ue_ref[:, POOL_HALO:, :] = u_ref[...].reshape(n_seq, t, pool_width)
        pos = start + lax.broadcasted_iota(jnp.int32, (1, t, gw), 1)
        for gi, win in enumerate(POOL_WINDOWS):
            cols = slice(gi * gw, (gi + 1) * gw)
            s = ue_ref[:, POOL_HALO:, cols]
            for back in range(1, win):
                s = s + ue_ref[:, POOL_HALO - back:POOL_HALO - back + t, cols]
            cnt = jnp.minimum(pos + 1, win).astype(F32)
            d = (s / cnt - ue_ref[:, POOL_HALO:, cols]).reshape(tm, gw)
            y = jnp.dot(d.astype(BF16), wpg_ref[gi], preferred_element_type=F32)
            pooled_ref[:, cols] = (y * pscale_ref[:, cols]).astype(BF16)
            yield

        for lo in range(0, d_model, MXU_TILE):
            cols = slice(lo, lo + MXU_TILE)
            branch_a = jnp.dot(attn_ref[...], wab_ref[:, cols], preferred_element_type=F32)
            branch_b = jnp.dot(pooled_ref[...], wpb_ref[:, cols], preferred_element_type=F32)
            merged_ref[:, cols] = (ga_ref[:, cols].astype(F32) * branch_a
                                   + gb_ref[:, cols].astype(F32) * branch_b).astype(BF16)
            yield
        for lo in range(0, d_model, MXU_TILE):
            cols = slice(lo, lo + MXU_TILE)
            mixed_ref[:, cols] = x_ref[:, cols] + jnp.dot(merged_ref[...], wout_ref[:, cols],
                                                          preferred_element_type=F32)
            yield

    _interleave(merge_stages())
    _interleave(_ffn_stages(mixed_ref, g2_ref, wup_ref, wdn_ref, out_ref, act_ref))


def _back(x, o, lse, u, halo, ga, gb, merge_w, ffn_w, *, tm, halo_block, halo_index, pos_base, blocks_per_seq):
    m, d = x.shape
    pool_width = u.shape[1]
    n_seq = halo_block[0]
    d_ff = ffn_w[2].shape[0]
    row = lambda wd: pl.BlockSpec((tm, wd), lambda i: (i, 0))
    slab = pl.BlockSpec((1, SLABS, tm, LANES), lambda i: (i // blocks_per_seq, 0, i % blocks_per_seq, 0))
    return pl.pallas_call(
        functools.partial(_back_kernel, pos_base=pos_base, blocks_per_seq=blocks_per_seq),
        out_shape=jax.ShapeDtypeStruct((m, d), F32),
        grid=(m // tm,),
        in_specs=[row(d)] + [slab] * 6 + [row(pool_width), pl.BlockSpec(halo_block, halo_index), row(d), row(d)]
        + [_resident(a.shape) for a in (*merge_w, *ffn_w)],
        out_specs=row(d),
        scratch_shapes=[pltpu.VMEM((n_seq, POOL_HALO + tm // n_seq, pool_width), F32),
                        pltpu.VMEM((tm, d), F32), pltpu.VMEM((tm, d_ff), BF16),
                        pltpu.VMEM((tm, GROUP_WIDTH), BF16), pltpu.VMEM((tm, pool_width), BF16),
                        pltpu.VMEM((tm, d), BF16)],
        compiler_params=_params(),
        name="back",
    )(x, *o, *lse, u, halo, ga, gb, *merge_w, *ffn_w)


def _window_in(cache):
    n_seq, rows = cache.shape[0:2]
    return jnp.transpose(cache, (0, 2, 3, 4, 1)).reshape(n_seq, 2, GROUP_WIDTH, rows)


def _window_out(kv):
    n_seq, _, _, rows = kv.shape
    return jnp.transpose(kv.reshape(n_seq, 2, HEADS_PER_GROUP, HEAD_DIM, rows), (0, 4, 1, 2, 3))


def kernel(x_prompt, x_sample, cache_kv_w128, cache_kv_w512, cache_kv_w2048, state_pool, rel_bias_table,
           norm_ffn1, ffn1_w_up, ffn1_w_down, norm_mix, w_in, q_norm, k_norm, pool_w_group, pool_scale,
           w_attn_branch, w_pool_branch, w_out, norm_ffn2, ffn2_w_up, ffn2_w_down):
    batch, seq, d_model = x_prompt.shape
    dec_batch, dec_seq, _ = x_sample.shape
    depth = norm_ffn1.shape[0]
    pool_width = state_pool.shape[-1]
    tm = 512
    ms = dec_batch * dec_seq
    seg = jnp.asarray(np.kron(np.eye(HEADS_PER_GROUP), np.full((HEAD_DIM, HEAD_DIM), 1.0 / HEAD_DIM)), BF16)
    table = rel_bias_table.astype(F32)
    caches_in = (cache_kv_w128, cache_kv_w512, cache_kv_w2048)
    keep_prompt = tuple(min(win, seq) for win, _ in ATTN_GROUPS)
    dils = tuple(dil for _, dil in ATTN_GROUPS)

    xp = x_prompt.reshape(batch * seq, d_model)
    xs = x_sample.reshape(ms, d_model)
    kv_p, kv_s, pool_p, pool_s = ([], [], []), ([], [], []), [], []
    for layer in range(depth):
        gain = lambda a: a[layer].reshape(1, -1).astype(F32)
        wup1, wdn1 = ffn1_w_up[layer].astype(BF16), ffn1_w_down[layer].astype(BF16)
        ffn2_w = (gain(norm_ffn2), ffn2_w_up[layer].astype(BF16), ffn2_w_down[layer].astype(BF16))
        win = w_in[layer].astype(BF16)
        merge_w = (w_attn_branch[layer].astype(BF16), pool_w_group[layer].astype(BF16), gain(pool_scale),
                   w_pool_branch[layer].astype(BF16), w_out[layer].astype(BF16))
        qg, kg = gain(q_norm), gain(k_norm)

        xp = _ffn(xp, gain(norm_ffn1), wup1, wdn1, 2 * tm)
        outs = _proj(xp, gain(norm_mix), win, qg, kg, seg, tm=tm, n_seq=batch, dils=dils,
                     qkv_dtype=BF16, keep_rows=keep_prompt)
        qkv, (u, ga, gb), windows, pstate = outs[0:9], outs[9:12], outs[12:15], outs[15]
        o, lse = zip(*[_prompt_attention(*qkv[3 * g:3 * g + 3], table, g) for g in range(N_GROUPS)])
        xp = _back(xp, o, lse, u, u.reshape(-1, POOL_HALO, pool_width), ga, gb, merge_w, ffn2_w, tm=tm,
                   halo_block=(1, POOL_HALO, pool_width),
                   halo_index=lambda blk: (jnp.maximum(blk * (tm // POOL_HALO) - 1, 0), 0, 0),
                   pos_base=0, blocks_per_seq=seq // tm)
        for g in range(N_GROUPS):
            kv_p[g].append(_window_out(windows[g]))
        pool_p.append(pstate[:, POOL_HALO - POOL_STATE:])

        xs = _ffn(xs, gain(norm_ffn1), wup1, wdn1, ms)
        outs = _proj(xs, gain(norm_mix), win, qg, kg, seg, tm=ms, n_seq=1, dils=(1,) * N_GROUPS,
                     qkv_dtype=F32, keep_rows=None)
        qkv, (u, ga, gb) = outs[0:9], outs[9:12]
        o, lse, windows = _sample_attention(
            [a.reshape(dec_batch, dec_seq, GROUP_WIDTH) for a in qkv],
            [_window_in(c[layer]) for c in caches_in], table)
        history = jnp.concatenate(
            [jnp.zeros((dec_batch, POOL_HALO - POOL_STATE, pool_width), F32), state_pool[layer]], axis=1)
        xs = _back(xs, o, lse, u, history, ga, gb, merge_w, ffn2_w, tm=ms,
                   halo_block=(dec_batch, POOL_HALO, pool_width), halo_index=lambda blk: (0, 0, 0),
                   pos_base=PAST_LEN, blocks_per_seq=1)
        for g in range(N_GROUPS):
            kv_s[g].append(_window_out(windows[g]))
        ue = jnp.concatenate([state_pool[layer], u.reshape(dec_batch, dec_seq, pool_width)], axis=1)
        pool_s.append(ue[:, ue.shape[1] - POOL_STATE:])

    stack = lambda xs_: jnp.stack(xs_, axis=0)
    return (xp.reshape(batch, seq, d_model), xs.reshape(dec_batch, dec_seq, d_model),
            stack(kv_p[0]), stack(kv_p[1]), stack(kv_p[2]), stack(pool_p),
            stack(kv_s[0]), stack(kv_s[1]), stack(kv_s[2]), stack(pool_s))
```

```python
import functools
import math

import numpy as np
import jax
import jax.numpy as jnp
from jax import lax
from jax.experimental import pallas as pl
from jax.experimental.pallas import tpu as pltpu

HEAD_DIM = 64
HEADS_PER_GROUP = 4
GROUP_WIDTH = HEADS_PER_GROUP * HEAD_DIM
ATTN_GROUPS = ((128, 1), (512, 4), (2048, 16))
N_GROUPS = len(ATTN_GROUPS)
ATTN_WIDTH = N_GROUPS * GROUP_WIDTH
N_BUCKETS = 32
MAX_DISTANCE = 2048
POOL_WINDOWS = (2, 4, 8, 16)
POOL_STATE = max(POOL_WINDOWS) - 1
POOL_HALO = 16
EPS = 1e-6
PAST_LEN = 8192
NEG = -1e30
LOG2E = math.log2(math.e)
LANES = 128
MXU_TILE = 256
SLABS = GROUP_WIDTH // LANES
QBLK = 128
ATTN_UNROLL = 8
VMEM_LIMIT = 58 * 1024 * 1024

F32 = jnp.float32
BF16 = jnp.bfloat16
NT_DIMS = (((1,), (1,)), ((), ()))


def _t5_buckets(distance):
    max_exact = N_BUCKETS // 2
    d = np.asarray(distance, dtype=np.int32)
    ratio = np.log(np.maximum(d, 1).astype(np.float32) / np.float32(max_exact))
    large = max_exact + (ratio / np.float32(math.log(MAX_DISTANCE / max_exact))
                         * (N_BUCKETS - max_exact)).astype(np.int32)
    large = np.minimum(large, N_BUCKETS - 1)
    return np.where(d < max_exact, d, large).astype(np.int32)


def _params(n_grid_dims=1):
    return pltpu.CompilerParams(dimension_semantics=("arbitrary",) * n_grid_dims,
                                vmem_limit_bytes=VMEM_LIMIT)


def _resident(shape):
    return pl.BlockSpec(shape, lambda *_: (0,) * len(shape), pipeline_mode=pl.Buffered(1))


_SMEM = pl.BlockSpec(memory_space=pltpu.SMEM)


def _rmsnorm(x, g):
    ms = jnp.mean(x * x, axis=-1, keepdims=True)
    return x * lax.rsqrt(ms + EPS) * g


def _head_select(parts, lane_head):
    out = jnp.where(lane_head == 0, parts[0], 0.0)
    for h in range(1, HEADS_PER_GROUP):
        out = jnp.where(lane_head == h, parts[h], out)
    return out


def _stack_heads(q, lane_head):
    return jnp.concatenate([jnp.where(lane_head == h, q, 0.0)
                            for h in range(HEADS_PER_GROUP)], axis=0).astype(BF16)


def _build_bias(bias_ref, bucket_ref, table_ref, g):
    buckets = bucket_ref[...]
    rows = buckets.shape[0]
    for h in range(HEADS_PER_GROUP):
        acc = jnp.full(buckets.shape, NEG, F32)
        for b in range(N_BUCKETS):
            acc = jnp.where(buckets == b, table_ref[b, g * HEADS_PER_GROUP + h] * LOG2E, acc)
        bias_ref[h * rows:(h + 1) * rows, :] = acc


def _interleave(*stages):
    stages = list(stages)
    while stages:
        for s in list(stages):
            if next(s, StopIteration) is StopIteration:
                stages.remove(s)


def _ffn_stages(x_ref, g_ref, wup_ref, wdn_ref, o_ref, act_ref):
    d_ff, d = wdn_ref.shape
    h = _rmsnorm(x_ref[...], g_ref[...]).astype(BF16)
    yield
    for lo in range(0, d_ff, MXU_TILE):
        gate = jnp.dot(h, wup_ref[:, lo:lo + MXU_TILE], preferred_element_type=F32)
        up = jnp.dot(h, wup_ref[:, d_ff + lo:d_ff + lo + MXU_TILE], preferred_element_type=F32)
        act_ref[:, lo:lo + MXU_TILE] = (gate * jax.nn.sigmoid(gate) * up).astype(BF16)
        yield
    for lo in range(0, d, MXU_TILE):
        cols = slice(lo, lo + MXU_TILE)
        o_ref[:, cols] = x_ref[:, cols] + 0.5 * jnp.dot(act_ref[...], wdn_ref[:, cols],
                                                        preferred_element_type=F32)
        yield


def _ffn_kernel(*refs, rider_counts):
    n_in, n_out = rider_counts
    x_ref, g_ref, wup_ref, wdn_ref = refs[0:4]
    o_ref = refs[4 + n_in]
    act_ref = refs[5 + n_in + n_out]
    stages = [_ffn_stages(x_ref, g_ref, wup_ref, wdn_ref, o_ref, act_ref)]
    if n_in:
        rider = (*refs[4:4 + n_in], *refs[5 + n_in:5 + n_in + n_out], *refs[6 + n_in + n_out:])
        _sample_attn_init(rider)
        stages.append(_sample_attn_stages(rider))
    _interleave(*stages)


def _ffn(x, gain, w_up, w_down, tm, rider=None):
    m, d = x.shape
    d_ff = w_down.shape[0]
    assert m % tm == 0 and d_ff % MXU_TILE == 0 and d % MXU_TILE == 0
    rider = rider or dict(args=[], in_specs=[], out_shape=[], out_specs=[], scratch_shapes=[], batch=m // tm)
    assert rider["batch"] == m // tm
    outs = pl.pallas_call(
        functools.partial(_ffn_kernel, rider_counts=(len(rider["args"]), len(rider["out_shape"]))),
        out_shape=[jax.ShapeDtypeStruct((m, d), F32)] + rider["out_shape"],
        grid=(m // tm,),
        in_specs=[pl.BlockSpec((tm, d), lambda i: (i, 0)),
                  _resident((1, d)), _resident((d, 2 * d_ff)), _resident((d_ff, d))] + rider["in_specs"],
        out_specs=[pl.BlockSpec((tm, d), lambda i: (i, 0))] + rider["out_specs"],
        scratch_shapes=[pltpu.VMEM((tm, d_ff), BF16)] + rider["scratch_shapes"],
        compiler_params=_params(),
        name="ffn",
    )(x, gain, w_up, w_down, *rider["args"])
    return outs[0], outs[1:]


def _proj_kernel(x_ref, g_ref, win_ref, qn_ref, kn_ref, seg_ref, *refs,
                 dils, kv_rows, with_state, pool_width, d_model):
    qkv_refs = refs[0:3 * N_GROUPS]
    u_ref, ga_ref, gb_ref = refs[3 * N_GROUPS:3 * N_GROUPS + 3]
    state_refs = refs[3 * N_GROUPS + 3:-1]
    slab_ref = refs[-1]
    tm = x_ref.shape[0]
    h = _rmsnorm(x_ref[...], g_ref[...]).astype(BF16)

    def proj(lo, width):
        return jnp.dot(h, win_ref[:, lo:lo + width], preferred_element_type=F32)

    def head_norm(y, gain):
        ms = jnp.dot((y * y).astype(BF16), seg_ref[...], preferred_element_type=F32)
        return y * lax.rsqrt(ms + EPS) * gain

    def emit(dst_ref, slot, val, dil):
        if dil == 1:
            dst_ref[0, 0] = val.astype(dst_ref.dtype)
            return
        for s in range(SLABS):
            slab_ref[slot, s] = val[:, s * LANES:(s + 1) * LANES]
        n = tm // dil
        for r in range(dil):
            parts = [slab_ref[slot, s, pl.ds(r, n, stride=dil), :] for s in range(SLABS)]
            dst_ref[0, r] = jnp.concatenate(parts, axis=-1).astype(dst_ref.dtype)

    for g in range(N_GROUPS):
        lo = g * GROUP_WIDTH
        cols = slice(lo, lo + GROUP_WIDTH)
        qg = head_norm(proj(lo, GROUP_WIDTH), qn_ref[:, cols]) * (HEAD_DIM ** -0.5 * LOG2E)
        kg = head_norm(proj(ATTN_WIDTH + lo, GROUP_WIDTH), kn_ref[:, cols])
        vg = proj(2 * ATTN_WIDTH + lo, GROUP_WIDTH)
        for which, val in enumerate((qg, kg, vg)):
            emit(qkv_refs[3 * g + which], 3 * g + which, val, dils[g])
        if with_state:
            r = kv_rows[g]

            state_refs[g][0, 0] = kg.T[:, tm - r:]
            state_refs[g][0, 1] = vg.T[:, tm - r:]
    u = proj(3 * ATTN_WIDTH, pool_width)
    u_ref[...] = u
    if with_state:
        state_refs[N_GROUPS][0] = u[tm - POOL_HALO:, :]
    for gate_ref, base in ((ga_ref, 3 * ATTN_WIDTH + pool_width), (gb_ref, 3 * ATTN_WIDTH + pool_width + d_model)):
        for lo in range(0, d_model, MXU_TILE):
            gate_ref[:, lo:lo + MXU_TILE] = jax.nn.sigmoid(proj(base + lo, MXU_TILE)).astype(BF16)


def _proj(x, gain, w_in, q_gain, k_gain, seg, *, tm, n_seq, dils, qkv_dtype, keep_rows):
    m, d = x.shape
    pool_width = w_in.shape[1] - 3 * ATTN_WIDTH - 2 * d
    seq_len = m // n_seq
    assert seq_len % tm == 0 and all(tm % dil == 0 for dil in dils)
    blocks_per_seq = seq_len // tm
    with_state = keep_rows is not None
    seq_block = lambda i: (i // blocks_per_seq, i % blocks_per_seq)

    row = lambda w: pl.BlockSpec((tm, w), lambda i: (i, 0))
    out_shape, out_specs = [], []
    for g in range(N_GROUPS):
        sds = jax.ShapeDtypeStruct((n_seq, dils[g], seq_len // dils[g], GROUP_WIDTH), qkv_dtype)
        spec = pl.BlockSpec((1, dils[g], tm // dils[g], GROUP_WIDTH),
                            lambda i: (seq_block(i)[0], 0, seq_block(i)[1], 0))
        out_shape += [sds] * 3
        out_specs += [spec] * 3
    out_shape += [jax.ShapeDtypeStruct((m, pool_width), F32),
                  jax.ShapeDtypeStruct((m, d), BF16), jax.ShapeDtypeStruct((m, d), BF16)]
    out_specs += [row(pool_width), row(d), row(d)]
    kv_rows = kv_first = ()
    if with_state:
        kv_rows = tuple(min(r, tm) for r in keep_rows)
        kv_first = tuple(blocks_per_seq - keep_rows[g] // kv_rows[g] for g in range(N_GROUPS))
        for g in range(N_GROUPS):
            out_shape.append(jax.ShapeDtypeStruct((n_seq, 2, GROUP_WIDTH, keep_rows[g]), F32))
            out_specs.append(pl.BlockSpec(
                (1, 2, GROUP_WIDTH, kv_rows[g]),
                lambda i, first=kv_first[g]: (seq_block(i)[0], 0, 0, jnp.maximum(seq_block(i)[1] - first, 0))))
        out_shape.append(jax.ShapeDtypeStruct((n_seq, POOL_HALO, pool_width), F32))
        out_specs.append(pl.BlockSpec((1, POOL_HALO, pool_width), lambda i: (seq_block(i)[0], 0, 0)))
    return pl.pallas_call(
        functools.partial(_proj_kernel, dils=dils, kv_rows=kv_rows, with_state=with_state,
                          pool_width=pool_width, d_model=d),
        out_shape=out_shape,
        grid=(m // tm,),
        in_specs=[row(d), _resident((1, d)), _resident(w_in.shape),
                  _resident((1, ATTN_WIDTH)), _resident((1, ATTN_WIDTH)),
                  _resident((GROUP_WIDTH, GROUP_WIDTH))],
        out_specs=out_specs,
        scratch_shapes=[pltpu.VMEM((3 * N_GROUPS, SLABS, tm, LANES), F32)],
        compiler_params=_params(),
        name="proj",
    )(x, gain, w_in, q_gain, k_gain, seg)


def _attn_kernel(q_ref, k_ref, v_ref, bucket_ref, hmask_ref, table_ref, o_ref, lse_ref, bias_ref,
                 *, g, dil, n_blocks, unroll):
    @pl.when(pl.program_id(0) == 0)
    def _():
        _build_bias(bias_ref.at[1], bucket_ref, table_ref, g)
        bias_ref[0, :, 0:QBLK] = bias_ref[1, :, QBLK:2 * QBLK]
        bias_ref[0, :, QBLK:2 * QBLK] = jnp.full((HEADS_PER_GROUP * QBLK, QBLK), NEG, F32)

    first_head = lax.broadcasted_iota(jnp.int32, (QBLK, LANES), 1) < HEAD_DIM
    heads = range(HEADS_PER_GROUP)

    def per_slab(col):
        part = lambda h: jnp.broadcast_to(col[h * QBLK:(h + 1) * QBLK], (QBLK, LANES))
        return [jnp.where(first_head, part(2 * s), part(2 * s + 1)) for s in range(SLABS)]

    def body(n, carry):
        res = n // n_blocks
        i = n % n_blocks
        row0 = pl.multiple_of(i * QBLK, QBLK)
        span0 = pl.multiple_of(jnp.maximum(i - 1, 0) * QBLK, QBLK)
        q = q_ref[0, res, pl.ds(row0, QBLK), :]
        qs = jnp.concatenate([q * hmask_ref[h, 0:QBLK, :] for h in heads], axis=0)
        s = lax.dot_general(qs, k_ref[0, res, pl.ds(span0, 2 * QBLK), :], NT_DIMS,
                            preferred_element_type=F32) + bias_ref[jnp.minimum(i, 1)]
        m = jnp.max(s, axis=-1, keepdims=True)
        p = jnp.exp2(s - m)
        l = jnp.sum(p, axis=-1, keepdims=True)
        p = p.astype(BF16)
        v = v_ref[0, res, pl.ds(span0, 2 * QBLK), :]
        p_wide = jnp.concatenate([p[h * QBLK:(h + 1) * QBLK] for h in heads], axis=1)
        v_tall = jnp.concatenate([v * hmask_ref[h] for h in heads], axis=0)
        pv = jnp.dot(p_wide, v_tall, preferred_element_type=F32)
        token0 = res + dil * row0
        dst = pl.ds(token0, QBLK) if dil == 1 else pl.ds(token0, QBLK, stride=dil)
        for s_, (m_s, l_s) in enumerate(zip(per_slab(m), per_slab(l))):
            o_ref[0, s_, dst, :] = pv[:, s_ * LANES:(s_ + 1) * LANES] / l_s
            lse_ref[0, s_, dst, :] = m_s + jnp.log2(l_s)
        return carry

    lax.fori_loop(0, dil * n_blocks, body, 0, unroll=unroll)


def _prompt_buckets(g):
    win, dil = ATTN_GROUPS[g]
    delta = np.arange(QBLK)[:, None] + QBLK - np.arange(2 * QBLK)[None, :]
    valid = (delta >= 0) & (delta <= win // dil)
    return np.where(valid, _t5_buckets(dil * np.clip(delta, 0, win // dil)), -1).astype(np.int32)


def _prompt_attention(q, k, v, table, g):
    _, dil = ATTN_GROUPS[g]
    batch, _, sub, _ = q.shape
    n_blocks = sub // QBLK
    assert sub % QBLK == 0 and n_blocks >= 2 and (dil * n_blocks) % ATTN_UNROLL == 0
    head_of_lane = np.arange(GROUP_WIDTH) // HEAD_DIM
    hmask = np.broadcast_to(head_of_lane[None, None, :] == np.arange(HEADS_PER_GROUP)[:, None, None],
                            (HEADS_PER_GROUP, 2 * QBLK, GROUP_WIDTH))
    in_spec = pl.BlockSpec((1, dil, sub, GROUP_WIDTH), lambda b: (b, 0, 0, 0))
    out_spec = pl.BlockSpec((1, SLABS, sub * dil, LANES), lambda b: (b, 0, 0, 0))
    out_sds = jax.ShapeDtypeStruct((batch, SLABS, sub * dil, LANES), F32)
    return pl.pallas_call(
        functools.partial(_attn_kernel, g=g, dil=dil, n_blocks=n_blocks, unroll=ATTN_UNROLL),
        out_shape=[out_sds, out_sds],
        grid=(batch,),
        in_specs=[in_spec, in_spec, in_spec, _resident((QBLK, 2 * QBLK)), _resident(hmask.shape), _SMEM],
        out_specs=[out_spec, out_spec],
        scratch_shapes=[pltpu.VMEM((2, HEADS_PER_GROUP * QBLK, 2 * QBLK), F32)],
        compiler_params=_params(),
        name=f"attn_g{g}",
    )(q, k, v, jnp.asarray(_prompt_buckets(g)), jnp.asarray(hmask, BF16), table)


def _sample_attn_init(refs):
    bucket_refs, table_ref, bias_refs = refs[12:15], refs[15], refs[25:28]

    @pl.when(pl.program_id(0) == 0)
    def _():
        for g in range(N_GROUPS):
            _build_bias(bias_refs[g], bucket_refs[g], table_ref, g)


def _sample_attn_stages(refs):
    qkv_refs = refs[0:9]
    cache_refs = refs[9:12]
    o_refs, lse_refs, win_refs = refs[16:19], refs[19:22], refs[22:25]
    bias_refs = refs[25:28]
    t_new = qkv_refs[0].shape[1]
    lane_head = lax.broadcasted_iota(jnp.int32, (t_new, GROUP_WIDTH), 1) // HEAD_DIM
    new_lane = lax.broadcasted_iota(jnp.int32, (GROUP_WIDTH, LANES), 1) >= LANES - t_new
    pad = jnp.zeros((LANES - t_new, GROUP_WIDTH), F32)

    for g in range(N_GROUPS):
        q_ref, k_ref, v_ref = qkv_refs[3 * g:3 * g + 3]
        cache_ref, win_ref, bias_ref = cache_refs[g], win_refs[g], bias_refs[g]
        past = cache_ref.shape[3]
        new_rows = [jnp.concatenate([pad, r[0]], axis=0) for r in (k_ref, v_ref)]
        for c in range(2):
            shifted = pltpu.roll(cache_ref[0, c], past - t_new, 1)
            if past > LANES:
                win_ref[0, c, :, 0:past - LANES] = shifted[:, 0:past - LANES]
            win_ref[0, c, :, past - LANES:past] = jnp.where(new_lane, new_rows[c].T, shifted[:, past - LANES:])
            yield

        qs = _stack_heads(q_ref[0], lane_head)
        s_old = jnp.dot(qs, cache_ref[0, 0].astype(BF16), preferred_element_type=F32) + bias_ref[:, 0:past]
        s_new = lax.dot_general(qs, new_rows[0].astype(BF16), NT_DIMS,
                                preferred_element_type=F32) + bias_ref[:, past:past + LANES]
        m = jnp.maximum(jnp.max(s_old, axis=-1, keepdims=True), jnp.max(s_new, axis=-1, keepdims=True))
        p_old = jnp.exp2(s_old - m)
        p_new = jnp.exp2(s_new - m)
        l = jnp.sum(p_old, axis=-1, keepdims=True) + jnp.sum(p_new, axis=-1, keepdims=True)
        pv = (lax.dot_general(p_old.astype(BF16), cache_ref[0, 1].astype(BF16), NT_DIMS,
                              preferred_element_type=F32)
              + jnp.dot(p_new.astype(BF16), new_rows[1].astype(BF16), preferred_element_type=F32)) / l
        lse = jnp.broadcast_to(m + jnp.log2(l), pv.shape)
        rows = lambda a: [a[h * t_new:(h + 1) * t_new] for h in range(HEADS_PER_GROUP)]
        for out_ref, val in ((o_refs[g], _head_select(rows(pv), lane_head)),
                             (lse_refs[g], _head_select(rows(lse), lane_head))):
            for s_ in range(SLABS):
                out_ref[0, s_] = val[:, s_ * LANES:(s_ + 1) * LANES]
        yield


def _sample_attn_kernel(*refs):
    _sample_attn_init(refs)
    _interleave(_sample_attn_stages(refs))


def _sample_buckets(g, past, t_new):
    win, dil = ATTN_GROUPS[g]
    lane = np.arange(past + LANES)[None, :]
    key_pos = np.where(lane < past, lane, lane - (LANES - t_new))
    delta = past + np.arange(t_new)[:, None] - key_pos
    valid = ((lane < past) | (lane >= past + LANES - t_new)) & (delta >= 0) & (delta % dil == 0) & (delta <= win)
    return np.where(valid, _t5_buckets(np.clip(delta, 0, win)), -1).astype(np.int32)


def _sample_attention_parts(qkv, caches, table):
    batch, t_new, _ = qkv[0].shape
    pasts = [c.shape[3] for c in caches]
    buckets = [jnp.asarray(_sample_buckets(g, pasts[g], t_new)) for g in range(N_GROUPS)]
    cache_spec = lambda p: pl.BlockSpec((1, 2, GROUP_WIDTH, p), lambda b: (b, 0, 0, 0))
    new_spec = pl.BlockSpec((1, t_new, GROUP_WIDTH), lambda b: (b, 0, 0))
    out_spec = pl.BlockSpec((1, SLABS, t_new, LANES), lambda b: (0, 0, b, 0))
    out_sds = jax.ShapeDtypeStruct((1, SLABS, batch * t_new, LANES), F32)
    return dict(
        batch=batch,
        args=[*qkv, *caches, *buckets, table],
        out_shape=[out_sds] * 6 + [jax.ShapeDtypeStruct(c.shape, F32) for c in caches],
        in_specs=[new_spec] * 9 + [cache_spec(p) for p in pasts] + [_resident(b.shape) for b in buckets]
        + [_SMEM],
        out_specs=[out_spec] * 6 + [cache_spec(p) for p in pasts],
        scratch_shapes=[pltpu.VMEM((HEADS_PER_GROUP * t_new, p + LANES), F32) for p in pasts])


def _sample_attention(parts):
    return pl.pallas_call(
        _sample_attn_kernel, grid=(parts["batch"],), out_shape=parts["out_shape"],
        in_specs=parts["in_specs"], out_specs=parts["out_specs"], scratch_shapes=parts["scratch_shapes"],
        compiler_params=_params(), name="sample_attn")(*parts["args"])


def _back_kernel(x_ref, o0_ref, o1_ref, o2_ref, l0_ref, l1_ref, l2_ref, u_ref, halo_ref, ga_ref, gb_ref,
                 wab_ref, wpg_ref, pscale_ref, wpb_ref, wout_ref, g2_ref, wup_ref, wdn_ref,
                 out_ref, ue_ref, mixed_ref, act_ref, attn_ref, pooled_ref, merged_ref,
                 *, pos_base, blocks_per_seq):
    block = pl.program_id(0)
    tm, d_model = x_ref.shape
    n_seq = halo_ref.shape[0]
    t = tm // n_seq
    pool_width = u_ref.shape[1]
    gw = pool_width // len(POOL_WINDOWS)

    def merge_stages():
        for s in range(SLABS):
            lses = [r[0, s] for r in (l0_ref, l1_ref, l2_ref)]
            top = jnp.maximum(jnp.maximum(lses[0], lses[1]), lses[2])
            num = jnp.zeros_like(top)
            den = jnp.zeros_like(top)
            for o_ref, lse in zip((o0_ref, o1_ref, o2_ref), lses):
                e = jnp.exp2(lse - top)
                num = num + e * o_ref[0, s]
                den = den + e
            attn_ref[:, s * LANES:(s + 1) * LANES] = (num / den).astype(BF16)
            yield

        start = pos_base + (block % blocks_per_seq) * t
        ue_ref[:, 0:POOL_HALO, :] = jnp.where(start > 0, halo_ref[...], 0.0)
        ue_ref[:, POOL_HALO:, :] = u_ref[...].reshape(n_seq, t, pool_width)
        pos = start + lax.broadcasted_iota(jnp.int32, (1, t, gw), 1)
        for gi, win in enumerate(POOL_WINDOWS):
            cols = slice(gi * gw, (gi + 1) * gw)
            s = ue_ref[:, POOL_HALO:, cols]
            for back in range(1, win):
                s = s + ue_ref[:, POOL_HALO - back:POOL_HALO - back + t, cols]
            cnt = jnp.minimum(pos + 1, win).astype(F32)
            d = (s / cnt - ue_ref[:, POOL_HALO:, cols]).reshape(tm, gw)
            y = jnp.dot(d.astype(BF16), wpg_ref[gi], preferred_element_type=F32)
            pooled_ref[:, cols] = (y * pscale_ref[:, cols]).astype(BF16)
            yield

        for lo in range(0, d_model, MXU_TILE):
            cols = slice(lo, lo + MXU_TILE)
            branch_a = jnp.dot(attn_ref[...], wab_ref[:, cols], preferred_element_type=F32)
            branch_b = jnp.dot(pooled_ref[...], wpb_ref[:, cols], preferred_element_type=F32)
            merged_ref[:, cols] = (ga_ref[:, cols].astype(F32) * branch_a
                                   + gb_ref[:, cols].astype(F32) * branch_b).astype(BF16)
            yield
        for lo in range(0, d_model, MXU_TILE):
            cols = slice(lo, lo + MXU_TILE)
            mixed_ref[:, cols] = x_ref[:, cols] + jnp.dot(merged_ref[...], wout_ref[:, cols],
                                                          preferred_element_type=F32)
            yield

    _interleave(merge_stages())
    _interleave(_ffn_stages(mixed_ref, g2_ref, wup_ref, wdn_ref, out_ref, act_ref))


def _back(x, o, lse, u, halo, ga, gb, merge_w, ffn_w, *, tm, halo_block, halo_index, pos_base, blocks_per_seq):
    m, d = x.shape
    pool_width = u.shape[1]
    n_seq = halo_block[0]
    d_ff = ffn_w[2].shape[0]
    row = lambda wd: pl.BlockSpec((tm, wd), lambda i: (i, 0))
    slab = pl.BlockSpec((1, SLABS, tm, LANES), lambda i: (i // blocks_per_seq, 0, i % blocks_per_seq, 0))
    return pl.pallas_call(
        functools.partial(_back_kernel, pos_base=pos_base, blocks_per_seq=blocks_per_seq),
        out_shape=jax.ShapeDtypeStruct((m, d), F32),
        grid=(m // tm,),
        in_specs=[row(d)] + [slab] * 6 + [row(pool_width), pl.BlockSpec(halo_block, halo_index), row(d), row(d)]
        + [_resident(a.shape) for a in (*merge_w, *ffn_w)],
        out_specs=row(d),
        scratch_shapes=[pltpu.VMEM((n_seq, POOL_HALO + tm // n_seq, pool_width), F32),
                        pltpu.VMEM((tm, d), F32), pltpu.VMEM((tm, d_ff), BF16),
                        pltpu.VMEM((tm, GROUP_WIDTH), BF16), pltpu.VMEM((tm, pool_width), BF16),
                        pltpu.VMEM((tm, d), BF16)],
        compiler_params=_params(),
        name="back",
    )(x, *o, *lse, u, halo, ga, gb, *merge_w, *ffn_w)


def _window_in(cache):
    n_seq, rows = cache.shape[0:2]
    return jnp.transpose(cache, (0, 2, 3, 4, 1)).reshape(n_seq, 2, GROUP_WIDTH, rows)


def _window_out(kv):
    n_seq, _, _, rows = kv.shape
    return jnp.transpose(kv.reshape(n_seq, 2, HEADS_PER_GROUP, HEAD_DIM, rows), (0, 4, 1, 2, 3))


def kernel(x_prompt, x_sample, cache_kv_w128, cache_kv_w512, cache_kv_w2048, state_pool, rel_bias_table,
           norm_ffn1, ffn1_w_up, ffn1_w_down, norm_mix, w_in, q_norm, k_norm, pool_w_group, pool_scale,
           w_attn_branch, w_pool_branch, w_out, norm_ffn2, ffn2_w_up, ffn2_w_down):
    batch, seq, d_model = x_prompt.shape
    dec_batch, dec_seq, _ = x_sample.shape
    depth = norm_ffn1.shape[0]
    pool_width = state_pool.shape[-1]
    tm = 512
    ms = dec_batch * dec_seq
    seg = jnp.asarray(np.kron(np.eye(HEADS_PER_GROUP), np.full((HEAD_DIM, HEAD_DIM), 1.0 / HEAD_DIM)), BF16)
    table = rel_bias_table.astype(F32)
    caches_in = (cache_kv_w128, cache_kv_w512, cache_kv_w2048)
    keep_prompt = tuple(min(win, seq) for win, _ in ATTN_GROUPS)
    dils = tuple(dil for _, dil in ATTN_GROUPS)

    xp = x_prompt.reshape(batch * seq, d_model)
    xs = x_sample.reshape(ms, d_model)
    kv_p, kv_s, pool_p, pool_s = ([], [], []), ([], [], []), [], []
    for layer in range(depth):
        gain = lambda a: a[layer].reshape(1, -1).astype(F32)
        wup1, wdn1 = ffn1_w_up[layer].astype(BF16), ffn1_w_down[layer].astype(BF16)
        ffn2_w = (gain(norm_ffn2), ffn2_w_up[layer].astype(BF16), ffn2_w_down[layer].astype(BF16))
        win = w_in[layer].astype(BF16)
        merge_w = (w_attn_branch[layer].astype(BF16), pool_w_group[layer].astype(BF16), gain(pool_scale),
                   w_pool_branch[layer].astype(BF16), w_out[layer].astype(BF16))
        qg, kg = gain(q_norm), gain(k_norm)

        xs, _ = _ffn(xs, gain(norm_ffn1), wup1, wdn1, ms)
        outs = _proj(xs, gain(norm_mix), win, qg, kg, seg, tm=ms, n_seq=1, dils=(1,) * N_GROUPS,
                     qkv_dtype=F32, keep_rows=None)
        qkv_s, (u_s, ga_s, gb_s) = outs[0:9], outs[9:12]
        sample_parts = _sample_attention_parts(
            [a.reshape(dec_batch, dec_seq, GROUP_WIDTH) for a in qkv_s],
            [_window_in(c[layer]) for c in caches_in], table)

        if (batch * seq) // tm == dec_batch:
            xp, sample_outs = _ffn(xp, gain(norm_ffn1), wup1, wdn1, tm, rider=sample_parts)
        else:
            xp, _ = _ffn(xp, gain(norm_ffn1), wup1, wdn1, tm)
            sample_outs = _sample_attention(sample_parts)
        outs = _proj(xp, gain(norm_mix), win, qg, kg, seg, tm=tm, n_seq=batch, dils=dils,
                     qkv_dtype=BF16, keep_rows=keep_prompt)
        qkv, (u, ga, gb), windows, pstate = outs[0:9], outs[9:12], outs[12:15], outs[15]
        o, lse = zip(*[_prompt_attention(*qkv[3 * g:3 * g + 3], table, g) for g in range(N_GROUPS)])
        xp = _back(xp, o, lse, u, u.reshape(-1, POOL_HALO, pool_width), ga, gb, merge_w, ffn2_w, tm=tm,
                   halo_block=(1, POOL_HALO, pool_width),
                   halo_index=lambda blk: (jnp.maximum(blk * (tm // POOL_HALO) - 1, 0), 0, 0),
                   pos_base=0, blocks_per_seq=seq // tm)
        for g in range(N_GROUPS):
            kv_p[g].append(_window_out(windows[g]))
        pool_p.append(pstate[:, POOL_HALO - POOL_STATE:])

        o, lse, windows = sample_outs[0:3], sample_outs[3:6], sample_outs[6:9]
        history = jnp.concatenate(
            [jnp.zeros((dec_batch, POOL_HALO - POOL_STATE, pool_width), F32), state_pool[layer]], axis=1)
        xs = _back(xs, o, lse, u_s, history, ga_s, gb_s, merge_w, ffn2_w, tm=ms,
                   halo_block=(dec_batch, POOL_HALO, pool_width), halo_index=lambda blk: (0, 0, 0),
                   pos_base=PAST_LEN, blocks_per_seq=1)
        for g in range(N_GROUPS):
            kv_s[g].append(_window_out(windows[g]))
        ue = jnp.concatenate([state_pool[layer], u_s.reshape(dec_batch, dec_seq, pool_width)], axis=1)
        pool_s.append(ue[:, ue.shape[1] - POOL_STATE:])

    stack = lambda xs_: jnp.stack(xs_, axis=0)
    return (xp.reshape(batch, seq, d_model), xs.reshape(dec_batch, dec_seq, d_model),
            stack(kv_p[0]), stack(kv_p[1]), stack(kv_p[2]), stack(pool_p),
            stack(kv_s[0]), stack(kv_s[1]), stack(kv_s[2]), stack(pool_s))
```

```python
import functools
import math

import numpy as np
import jax
import jax.numpy as jnp
from jax import lax
from jax.experimental import pallas as pl
from jax.experimental.pallas import tpu as pltpu

HEAD_DIM = 64
HEADS_PER_GROUP = 4
GROUP_WIDTH = HEADS_PER_GROUP * HEAD_DIM
ATTN_GROUPS = ((128, 1), (512, 4), (2048, 16))
N_GROUPS = len(ATTN_GROUPS)
ATTN_WIDTH = N_GROUPS * GROUP_WIDTH
N_BUCKETS = 32
MAX_DISTANCE = 2048
POOL_WINDOWS = (2, 4, 8, 16)
POOL_STATE = max(POOL_WINDOWS) - 1
POOL_HALO = 16
EPS = 1e-6
PAST_LEN = 8192
NEG = -1e30
LOG2E = math.log2(math.e)
LANES = 128
MXU_TILE = 256
SLABS = GROUP_WIDTH // LANES
QBLK = 128
ATTN_UNROLL = 8
VMEM_LIMIT = 58 * 1024 * 1024

F32 = jnp.float32
BF16 = jnp.bfloat16
NT_DIMS = (((1,), (1,)), ((), ()))


def _t5_buckets(distance):
    max_exact = N_BUCKETS // 2
    d = np.asarray(distance, dtype=np.int32)
    ratio = np.log(np.maximum(d, 1).astype(np.float32) / np.float32(max_exact))
    large = max_exact + (ratio / np.float32(math.log(MAX_DISTANCE / max_exact))
                         * (N_BUCKETS - max_exact)).astype(np.int32)
    large = np.minimum(large, N_BUCKETS - 1)
    return np.where(d < max_exact, d, large).astype(np.int32)


def _params(n_grid_dims=1):
    return pltpu.CompilerParams(dimension_semantics=("arbitrary",) * n_grid_dims,
                                vmem_limit_bytes=VMEM_LIMIT)


def _resident(shape):
    return pl.BlockSpec(shape, lambda *_: (0,) * len(shape), pipeline_mode=pl.Buffered(1))


_SMEM = pl.BlockSpec(memory_space=pltpu.SMEM)


def _rmsnorm(x, g):
    ms = jnp.mean(x * x, axis=-1, keepdims=True)
    return x * lax.rsqrt(ms + EPS) * g


def _head_select(parts, lane_head):
    out = jnp.where(lane_head == 0, parts[0], 0.0)
    for h in range(1, HEADS_PER_GROUP):
        out = jnp.where(lane_head == h, parts[h], out)
    return out


def _stack_heads(q, lane_head):
    return jnp.concatenate([jnp.where(lane_head == h, q, 0.0)
                            for h in range(HEADS_PER_GROUP)], axis=0).astype(BF16)


def _build_bias(bias_ref, bucket_ref, table_ref, g):
    buckets = bucket_ref[...]
    rows = buckets.shape[0]
    for h in range(HEADS_PER_GROUP):
        acc = jnp.full(buckets.shape, NEG, F32)
        for b in range(N_BUCKETS):
            acc = jnp.where(buckets == b, table_ref[b, g * HEADS_PER_GROUP + h] * LOG2E, acc)
        bias_ref[h * rows:(h + 1) * rows, :] = acc


def _ffn_block(x_ref, g_ref, wup_ref, wdn_ref, o_ref, act_ref):
    d_ff, d = wdn_ref.shape
    h = _rmsnorm(x_ref[...], g_ref[...]).astype(BF16)
    for lo in range(0, d_ff, MXU_TILE):
        gate = jnp.dot(h, wup_ref[:, lo:lo + MXU_TILE], preferred_element_type=F32)
        up = jnp.dot(h, wup_ref[:, d_ff + lo:d_ff + lo + MXU_TILE], preferred_element_type=F32)
        act_ref[:, lo:lo + MXU_TILE] = (gate * jax.nn.sigmoid(gate) * up).astype(BF16)
    for lo in range(0, d, MXU_TILE):
        cols = slice(lo, lo + MXU_TILE)
        o_ref[:, cols] = x_ref[:, cols] + 0.5 * jnp.dot(act_ref[...], wdn_ref[:, cols],
                                                        preferred_element_type=F32)


def _ffn_kernel(*refs, rider_counts):
    n_in, n_out = rider_counts
    x_ref, g_ref, wup_ref, wdn_ref = refs[0:4]
    o_ref = refs[4 + n_in]
    act_ref = refs[5 + n_in + n_out]
    if n_in:
        _sample_attn_block((*refs[4:4 + n_in], *refs[5 + n_in:5 + n_in + n_out], *refs[6 + n_in + n_out:]))
    _ffn_block(x_ref, g_ref, wup_ref, wdn_ref, o_ref, act_ref)


def _ffn(x, gain, w_up, w_down, tm, rider=None):
    m, d = x.shape
    d_ff = w_down.shape[0]
    assert m % tm == 0 and d_ff % MXU_TILE == 0 and d % MXU_TILE == 0
    rider = rider or dict(args=[], in_specs=[], out_shape=[], out_specs=[], scratch_shapes=[], batch=m // tm)
    assert rider["batch"] == m // tm
    outs = pl.pallas_call(
        functools.partial(_ffn_kernel, rider_counts=(len(rider["args"]), len(rider["out_shape"]))),
        out_shape=[jax.ShapeDtypeStruct((m, d), F32)] + rider["out_shape"],
        grid=(m // tm,),
        in_specs=[pl.BlockSpec((tm, d), lambda i: (i, 0)),
                  _resident((1, d)), _resident((d, 2 * d_ff)), _resident((d_ff, d))] + rider["in_specs"],
        out_specs=[pl.BlockSpec((tm, d), lambda i: (i, 0))] + rider["out_specs"],
        scratch_shapes=[pltpu.VMEM((tm, d_ff), BF16)] + rider["scratch_shapes"],
        compiler_params=_params(),
        name="ffn",
    )(x, gain, w_up, w_down, *rider["args"])
    return outs[0], outs[1:]


def _proj_kernel(x_ref, g_ref, win_ref, qn_ref, kn_ref, seg_ref, *refs,
                 dils, kv_rows, with_state, pool_width, d_model):
    qkv_refs = refs[0:3 * N_GROUPS]
    u_ref, ga_ref, gb_ref = refs[3 * N_GROUPS:3 * N_GROUPS + 3]
    state_refs = refs[3 * N_GROUPS + 3:-1]
    slab_ref = refs[-1]
    tm = x_ref.shape[0]
    h = _rmsnorm(x_ref[...], g_ref[...]).astype(BF16)

    def proj(lo, width):
        return jnp.dot(h, win_ref[:, lo:lo + width], preferred_element_type=F32)

    def head_norm(y, gain):
        ms = jnp.dot((y * y).astype(BF16), seg_ref[...], preferred_element_type=F32)
        return y * lax.rsqrt(ms + EPS) * gain

    def emit(dst_ref, slot, val, dil):
        if dil == 1:
            dst_ref[0, 0] = val.astype(dst_ref.dtype)
            return
        for s in range(SLABS):
            slab_ref[slot, s] = val[:, s * LANES:(s + 1) * LANES]
        n = tm // dil
        for r in range(dil):
            parts = [slab_ref[slot, s, pl.ds(r, n, stride=dil), :] for s in range(SLABS)]
            dst_ref[0, r] = jnp.concatenate(parts, axis=-1).astype(dst_ref.dtype)

    for g in range(N_GROUPS):
        lo = g * GROUP_WIDTH
        cols = slice(lo, lo + GROUP_WIDTH)
        qg = head_norm(proj(lo, GROUP_WIDTH), qn_ref[:, cols]) * (HEAD_DIM ** -0.5 * LOG2E)
        kg = head_norm(proj(ATTN_WIDTH + lo, GROUP_WIDTH), kn_ref[:, cols])
        vg = proj(2 * ATTN_WIDTH + lo, GROUP_WIDTH)
        for which, val in enumerate((qg, kg, vg)):
            emit(qkv_refs[3 * g + which], 3 * g + which, val, dils[g])
        if with_state:
            r = kv_rows[g]
            state_refs[g][0, 0] = kg.T[:, tm - r:]
            state_refs[g][0, 1] = vg.T[:, tm - r:]
    u = proj(3 * ATTN_WIDTH, pool_width)
    u_ref[...] = u
    if with_state:
        state_refs[N_GROUPS][0] = u[tm - POOL_HALO:, :]
    for gate_ref, base in ((ga_ref, 3 * ATTN_WIDTH + pool_width), (gb_ref, 3 * ATTN_WIDTH + pool_width + d_model)):
        for lo in range(0, d_model, MXU_TILE):
            gate_ref[:, lo:lo + MXU_TILE] = jax.nn.sigmoid(proj(base + lo, MXU_TILE)).astype(BF16)


def _proj(x, gain, w_in, q_gain, k_gain, seg, *, tm, n_seq, dils, qkv_dtype, keep_rows):
    m, d = x.shape
    pool_width = w_in.shape[1] - 3 * ATTN_WIDTH - 2 * d
    seq_len = m // n_seq
    assert seq_len % tm == 0 and all(tm % dil == 0 for dil in dils)
    blocks_per_seq = seq_len // tm
    with_state = keep_rows is not None
    seq_block = lambda i: (i // blocks_per_seq, i % blocks_per_seq)

    row = lambda w: pl.BlockSpec((tm, w), lambda i: (i, 0))
    out_shape, out_specs = [], []
    for g in range(N_GROUPS):
        sds = jax.ShapeDtypeStruct((n_seq, dils[g], seq_len // dils[g], GROUP_WIDTH), qkv_dtype)
        spec = pl.BlockSpec((1, dils[g], tm // dils[g], GROUP_WIDTH),
                            lambda i: (seq_block(i)[0], 0, seq_block(i)[1], 0))
        out_shape += [sds] * 3
        out_specs += [spec] * 3
    out_shape += [jax.ShapeDtypeStruct((m, pool_width), F32),
                  jax.ShapeDtypeStruct((m, d), BF16), jax.ShapeDtypeStruct((m, d), BF16)]
    out_specs += [row(pool_width), row(d), row(d)]
    kv_rows = kv_first = ()
    if with_state:
        kv_rows = tuple(min(r, tm) for r in keep_rows)
        kv_first = tuple(blocks_per_seq - keep_rows[g] // kv_rows[g] for g in range(N_GROUPS))
        for g in range(N_GROUPS):
            out_shape.append(jax.ShapeDtypeStruct((n_seq, 2, GROUP_WIDTH, keep_rows[g]), F32))
            out_specs.append(pl.BlockSpec(
                (1, 2, GROUP_WIDTH, kv_rows[g]),
                lambda i, first=kv_first[g]: (seq_block(i)[0], 0, 0, jnp.maximum(seq_block(i)[1] - first, 0))))
        out_shape.append(jax.ShapeDtypeStruct((n_seq, POOL_HALO, pool_width), F32))
        out_specs.append(pl.BlockSpec((1, POOL_HALO, pool_width), lambda i: (seq_block(i)[0], 0, 0)))
    return pl.pallas_call(
        functools.partial(_proj_kernel, dils=dils, kv_rows=kv_rows, with_state=with_state,
                          pool_width=pool_width, d_model=d),
        out_shape=out_shape,
        grid=(m // tm,),
        in_specs=[row(d), _resident((1, d)), _resident(w_in.shape),
                  _resident((1, ATTN_WIDTH)), _resident((1, ATTN_WIDTH)),
                  _resident((GROUP_WIDTH, GROUP_WIDTH))],
        out_specs=out_specs,
        scratch_shapes=[pltpu.VMEM((3 * N_GROUPS, SLABS, tm, LANES), F32)],
        compiler_params=_params(),
        name="proj",
    )(x, gain, w_in, q_gain, k_gain, seg)


def _attn_kernel(*refs, g, dil, n_blocks, unroll, n_cast):
    q_ref, k_ref, v_ref, bucket_ref, hmask_ref, table_ref = refs[0:6]
    wide_refs = refs[6:6 + n_cast]
    o_ref, lse_ref = refs[6 + n_cast:8 + n_cast]
    narrow_refs = refs[8 + n_cast:8 + 2 * n_cast]
    bias_ref = refs[8 + 2 * n_cast]
    for wide_ref, narrow_ref in zip(wide_refs, narrow_refs):
        narrow_ref[...] = wide_ref[...].astype(BF16)

    @pl.when(pl.program_id(0) == 0)
    def _():
        _build_bias(bias_ref.at[1], bucket_ref, table_ref, g)
        bias_ref[0, :, 0:QBLK] = bias_ref[1, :, QBLK:2 * QBLK]
        bias_ref[0, :, QBLK:2 * QBLK] = jnp.full((HEADS_PER_GROUP * QBLK, QBLK), NEG, F32)

    first_head = lax.broadcasted_iota(jnp.int32, (QBLK, LANES), 1) < HEAD_DIM
    heads = range(HEADS_PER_GROUP)

    def per_slab(col):
        part = lambda h: jnp.broadcast_to(col[h * QBLK:(h + 1) * QBLK], (QBLK, LANES))
        return [jnp.where(first_head, part(2 * s), part(2 * s + 1)) for s in range(SLABS)]

    def body(n, carry):
        res = n // n_blocks
        i = n % n_blocks
        row0 = pl.multiple_of(i * QBLK, QBLK)
        span0 = pl.multiple_of(jnp.maximum(i - 1, 0) * QBLK, QBLK)
        q = q_ref[0, res, pl.ds(row0, QBLK), :]
        qs = jnp.concatenate([q * hmask_ref[h, 0:QBLK, :] for h in heads], axis=0)
        s = lax.dot_general(qs, k_ref[0, res, pl.ds(span0, 2 * QBLK), :], NT_DIMS,
                            preferred_element_type=F32) + bias_ref[jnp.minimum(i, 1)]
        m = jnp.max(s, axis=-1, keepdims=True)
        p = jnp.exp2(s - m)
        l = jnp.sum(p, axis=-1, keepdims=True)
        p = p.astype(BF16)
        v = v_ref[0, res, pl.ds(span0, 2 * QBLK), :]
        p_wide = jnp.concatenate([p[h * QBLK:(h + 1) * QBLK] for h in heads], axis=1)
        v_tall = jnp.concatenate([v * hmask_ref[h] for h in heads], axis=0)
        pv = jnp.dot(p_wide, v_tall, preferred_element_type=F32)
        token0 = res + dil * row0
        dst = pl.ds(token0, QBLK) if dil == 1 else pl.ds(token0, QBLK, stride=dil)
        for s_, (m_s, l_s) in enumerate(zip(per_slab(m), per_slab(l))):
            o_ref[0, s_, dst, :] = pv[:, s_ * LANES:(s_ + 1) * LANES] / l_s
            lse_ref[0, s_, dst, :] = m_s + jnp.log2(l_s)
        return carry

    lax.fori_loop(0, dil * n_blocks, body, 0, unroll=unroll)


def _prompt_buckets(g):
    win, dil = ATTN_GROUPS[g]
    delta = np.arange(QBLK)[:, None] + QBLK - np.arange(2 * QBLK)[None, :]
    valid = (delta >= 0) & (delta <= win // dil)
    return np.where(valid, _t5_buckets(dil * np.clip(delta, 0, win // dil)), -1).astype(np.int32)


def _prompt_attention(q, k, v, table, g, to_narrow=()):
    _, dil = ATTN_GROUPS[g]
    batch, _, sub, _ = q.shape
    n_blocks = sub // QBLK
    assert sub % QBLK == 0 and n_blocks >= 2 and (dil * n_blocks) % ATTN_UNROLL == 0
    assert all(w.shape[0] % (16 * batch) == 0 for w in to_narrow)
    cast_specs = [pl.BlockSpec((w.shape[0] // batch, w.shape[1]), lambda b: (b, 0)) for w in to_narrow]
    head_of_lane = np.arange(GROUP_WIDTH) // HEAD_DIM
    hmask = np.broadcast_to(head_of_lane[None, None, :] == np.arange(HEADS_PER_GROUP)[:, None, None],
                            (HEADS_PER_GROUP, 2 * QBLK, GROUP_WIDTH))
    in_spec = pl.BlockSpec((1, dil, sub, GROUP_WIDTH), lambda b: (b, 0, 0, 0))
    out_spec = pl.BlockSpec((1, SLABS, sub * dil, LANES), lambda b: (b, 0, 0, 0))
    out_sds = jax.ShapeDtypeStruct((batch, SLABS, sub * dil, LANES), F32)
    outs = pl.pallas_call(
        functools.partial(_attn_kernel, g=g, dil=dil, n_blocks=n_blocks, unroll=ATTN_UNROLL,
                          n_cast=len(to_narrow)),
        out_shape=[out_sds, out_sds] + [jax.ShapeDtypeStruct(w.shape, BF16) for w in to_narrow],
        grid=(batch,),
        in_specs=[in_spec, in_spec, in_spec, _resident((QBLK, 2 * QBLK)), _resident(hmask.shape), _SMEM]
        + cast_specs,
        out_specs=[out_spec, out_spec] + cast_specs,
        scratch_shapes=[pltpu.VMEM((2, HEADS_PER_GROUP * QBLK, 2 * QBLK), F32)],
        compiler_params=_params(),
        name=f"attn_g{g}",
    )(q, k, v, jnp.asarray(_prompt_buckets(g)), jnp.asarray(hmask, BF16), table, *to_narrow)
    return outs[0], outs[1], outs[2:]


def _sample_attn_block(refs):
    qkv_refs = refs[0:9]
    cache_refs = refs[9:12]
    bucket_refs, table_ref = refs[12:15], refs[15]
    o_refs, lse_refs, win_refs = refs[16:19], refs[19:22], refs[22:25]
    bias_refs = refs[25:28]

    @pl.when(pl.program_id(0) == 0)
    def _():
        for g in range(N_GROUPS):
            _build_bias(bias_refs[g], bucket_refs[g], table_ref, g)

    t_new = qkv_refs[0].shape[1]
    lane_head = lax.broadcasted_iota(jnp.int32, (t_new, GROUP_WIDTH), 1) // HEAD_DIM
    new_lane = lax.broadcasted_iota(jnp.int32, (GROUP_WIDTH, LANES), 1) >= LANES - t_new
    pad = jnp.zeros((LANES - t_new, GROUP_WIDTH), F32)

    for g in range(N_GROUPS):
        q_ref, k_ref, v_ref = qkv_refs[3 * g:3 * g + 3]
        cache_ref, win_ref, bias_ref = cache_refs[g], win_refs[g], bias_refs[g]
        past = cache_ref.shape[3]
        new_rows = [jnp.concatenate([pad, r[0]], axis=0) for r in (k_ref, v_ref)]
        for c in range(2):
            shifted = pltpu.roll(cache_ref[0, c], past - t_new, 1)
            if past > LANES:
                win_ref[0, c, :, 0:past - LANES] = shifted[:, 0:past - LANES]
            win_ref[0, c, :, past - LANES:past] = jnp.where(new_lane, new_rows[c].T, shifted[:, past - LANES:])

        qs = _stack_heads(q_ref[0], lane_head)
        s_old = jnp.dot(qs, cache_ref[0, 0].astype(BF16), preferred_element_type=F32) + bias_ref[:, 0:past]
        s_new = lax.dot_general(qs, new_rows[0].astype(BF16), NT_DIMS,
                                preferred_element_type=F32) + bias_ref[:, past:past + LANES]
        m = jnp.maximum(jnp.max(s_old, axis=-1, keepdims=True), jnp.max(s_new, axis=-1, keepdims=True))
        p_old = jnp.exp2(s_old - m)
        p_new = jnp.exp2(s_new - m)
        l = jnp.sum(p_old, axis=-1, keepdims=True) + jnp.sum(p_new, axis=-1, keepdims=True)
        pv = (lax.dot_general(p_old.astype(BF16), cache_ref[0, 1].astype(BF16), NT_DIMS,
                              preferred_element_type=F32)
              + jnp.dot(p_new.astype(BF16), new_rows[1].astype(BF16), preferred_element_type=F32)) / l
        lse = jnp.broadcast_to(m + jnp.log2(l), pv.shape)
        rows = lambda a: [a[h * t_new:(h + 1) * t_new] for h in range(HEADS_PER_GROUP)]
        for out_ref, val in ((o_refs[g], _head_select(rows(pv), lane_head)),
                             (lse_refs[g], _head_select(rows(lse), lane_head))):
            for s_ in range(SLABS):
                out_ref[0, s_] = val[:, s_ * LANES:(s_ + 1) * LANES]


def _sample_attn_kernel(*refs):
    _sample_attn_block(refs)


def _sample_buckets(g, past, t_new):
    win, dil = ATTN_GROUPS[g]
    lane = np.arange(past + LANES)[None, :]
    key_pos = np.where(lane < past, lane, lane - (LANES - t_new))
    delta = past + np.arange(t_new)[:, None] - key_pos
    valid = ((lane < past) | (lane >= past + LANES - t_new)) & (delta >= 0) & (delta % dil == 0) & (delta <= win)
    return np.where(valid, _t5_buckets(np.clip(delta, 0, win)), -1).astype(np.int32)


def _sample_attention_parts(qkv, caches, table):
    batch, t_new, _ = qkv[0].shape
    pasts = [c.shape[3] for c in caches]
    buckets = [jnp.asarray(_sample_buckets(g, pasts[g], t_new)) for g in range(N_GROUPS)]
    cache_spec = lambda p: pl.BlockSpec((1, 2, GROUP_WIDTH, p), lambda b: (b, 0, 0, 0))
    new_spec = pl.BlockSpec((1, t_new, GROUP_WIDTH), lambda b: (b, 0, 0))
    out_spec = pl.BlockSpec((1, SLABS, t_new, LANES), lambda b: (0, 0, b, 0))
    out_sds = jax.ShapeDtypeStruct((1, SLABS, batch * t_new, LANES), F32)
    return dict(
        batch=batch,
        args=[*qkv, *caches, *buckets, table],
        out_shape=[out_sds] * 6 + [jax.ShapeDtypeStruct(c.shape, F32) for c in caches],
        in_specs=[new_spec] * 9 + [cache_spec(p) for p in pasts] + [_resident(b.shape) for b in buckets]
        + [_SMEM],
        out_specs=[out_spec] * 6 + [cache_spec(p) for p in pasts],
        scratch_shapes=[pltpu.VMEM((HEADS_PER_GROUP * t_new, p + LANES), F32) for p in pasts])


def _sample_attention(parts):
    return pl.pallas_call(
        _sample_attn_kernel, grid=(parts["batch"],), out_shape=parts["out_shape"],
        in_specs=parts["in_specs"], out_specs=parts["out_specs"], scratch_shapes=parts["scratch_shapes"],
        compiler_params=_params(), name="sample_attn")(*parts["args"])


def _back_kernel(x_ref, o0_ref, o1_ref, o2_ref, l0_ref, l1_ref, l2_ref, u_ref, halo_ref, ga_ref, gb_ref,
                 wab_ref, wpg_ref, pscale_ref, wpb_ref, wout_ref, g2_ref, wup_ref, wdn_ref,
                 out_ref, ue_ref, mixed_ref, act_ref, attn_ref, pooled_ref, merged_ref,
                 *, pos_base, blocks_per_seq):
    block = pl.program_id(0)
    tm, d_model = x_ref.shape
    n_seq = halo_ref.shape[0]
    t = tm // n_seq
    pool_width = u_ref.shape[1]
    gw = pool_width // len(POOL_WINDOWS)

    for s in range(SLABS):
        lses = [r[0, s] for r in (l0_ref, l1_ref, l2_ref)]
        top = jnp.maximum(jnp.maximum(lses[0], lses[1]), lses[2])
        num = jnp.zeros_like(top)
        den = jnp.zeros_like(top)
        for o_ref, lse in zip((o0_ref, o1_ref, o2_ref), lses):
            e = jnp.exp2(lse - top)
            num = num + e * o_ref[0, s]
            den = den + e
        attn_ref[:, s * LANES:(s + 1) * LANES] = (num / den).astype(BF16)

    start = pos_base + (block % blocks_per_seq) * t
    ue_ref[:, 0:POOL_HALO, :] = jnp.where(start > 0, halo_ref[...], 0.0)
    ue_ref[:, POOL_HALO:, :] = u_ref[...].reshape(n_seq, t, pool_width)
    pos = start + lax.broadcasted_iota(jnp.int32, (1, t, gw), 1)
    for gi, win in enumerate(POOL_WINDOWS):
        cols = slice(gi * gw, (gi + 1) * gw)
        s = ue_ref[:, POOL_HALO:, cols]
        for back in range(1, win):
            s = s + ue_ref[:, POOL_HALO - back:POOL_HALO - back + t, cols]
        cnt = jnp.minimum(pos + 1, win).astype(F32)
        d = (s / cnt - ue_ref[:, POOL_HALO:, cols]).reshape(tm, gw)
        y = jnp.dot(d.astype(BF16), wpg_ref[gi], preferred_element_type=F32)
        pooled_ref[:, cols] = (y * pscale_ref[:, cols]).astype(BF16)

    for lo in range(0, d_model, MXU_TILE):
        cols = slice(lo, lo + MXU_TILE)
        branch_a = jnp.dot(attn_ref[...], wab_ref[:, cols], preferred_element_type=F32)
        branch_b = jnp.dot(pooled_ref[...], wpb_ref[:, cols], preferred_element_type=F32)
        merged_ref[:, cols] = (ga_ref[:, cols].astype(F32) * branch_a
                               + gb_ref[:, cols].astype(F32) * branch_b).astype(BF16)
    for lo in range(0, d_model, MXU_TILE):
        cols = slice(lo, lo + MXU_TILE)
        mixed_ref[:, cols] = x_ref[:, cols] + jnp.dot(merged_ref[...], wout_ref[:, cols],
                                                      preferred_element_type=F32)

    _ffn_block(mixed_ref, g2_ref, wup_ref, wdn_ref, out_ref, act_ref)


def _back(x, o, lse, u, halo, ga, gb, merge_w, ffn_w, *, tm, halo_block, halo_index, pos_base, blocks_per_seq):
    m, d = x.shape
    pool_width = u.shape[1]
    n_seq = halo_block[0]
    d_ff = ffn_w[2].shape[0]
    row = lambda wd: pl.BlockSpec((tm, wd), lambda i: (i, 0))
    slab = pl.BlockSpec((1, SLABS, tm, LANES), lambda i: (i // blocks_per_seq, 0, i % blocks_per_seq, 0))
    return pl.pallas_call(
        functools.partial(_back_kernel, pos_base=pos_base, blocks_per_seq=blocks_per_seq),
        out_shape=jax.ShapeDtypeStruct((m, d), F32),
        grid=(m // tm,),
        in_specs=[row(d)] + [slab] * 6 + [row(pool_width), pl.BlockSpec(halo_block, halo_index), row(d), row(d)]
        + [_resident(a.shape) for a in (*merge_w, *ffn_w)],
        out_specs=row(d),
        scratch_shapes=[pltpu.VMEM((n_seq, POOL_HALO + tm // n_seq, pool_width), F32),
                        pltpu.VMEM((tm, d), F32), pltpu.VMEM((tm, d_ff), BF16),
                        pltpu.VMEM((tm, GROUP_WIDTH), BF16), pltpu.VMEM((tm, pool_width), BF16),
                        pltpu.VMEM((tm, d), BF16)],
        compiler_params=_params(),
        name="back",
    )(x, *o, *lse, u, halo, ga, gb, *merge_w, *ffn_w)


def _window_in(cache):
    n_seq, rows = cache.shape[0:2]
    return jnp.transpose(cache, (0, 2, 3, 4, 1)).reshape(n_seq, 2, GROUP_WIDTH, rows)


def _window_out(kv):
    n_seq, _, _, rows = kv.shape
    return jnp.transpose(kv.reshape(n_seq, 2, HEADS_PER_GROUP, HEAD_DIM, rows), (0, 4, 1, 2, 3))


def kernel(x_prompt, x_sample, cache_kv_w128, cache_kv_w512, cache_kv_w2048, state_pool, rel_bias_table,
           norm_ffn1, ffn1_w_up, ffn1_w_down, norm_mix, w_in, q_norm, k_norm, pool_w_group, pool_scale,
           w_attn_branch, w_pool_branch, w_out, norm_ffn2, ffn2_w_up, ffn2_w_down):
    batch, seq, d_model = x_prompt.shape
    dec_batch, dec_seq, _ = x_sample.shape
    depth = norm_ffn1.shape[0]
    pool_width = state_pool.shape[-1]
    tm = 512
    ms = dec_batch * dec_seq
    seg = jnp.asarray(np.kron(np.eye(HEADS_PER_GROUP), np.full((HEAD_DIM, HEAD_DIM), 1.0 / HEAD_DIM)), BF16)
    table = rel_bias_table.astype(F32)
    caches_in = (cache_kv_w128, cache_kv_w512, cache_kv_w2048)
    keep_prompt = tuple(min(win, seq) for win, _ in ATTN_GROUPS)
    dils = tuple(dil for _, dil in ATTN_GROUPS)

    xp = x_prompt.reshape(batch * seq, d_model)
    xs = x_sample.reshape(ms, d_model)
    kv_p, kv_s, pool_p, pool_s = ([], [], []), ([], [], []), [], []
    for layer in range(depth):
        gain = lambda a: a[layer].reshape(1, -1).astype(F32)
        wup1, wdn1 = ffn1_w_up[layer].astype(BF16), ffn1_w_down[layer].astype(BF16)
        win = w_in[layer].astype(BF16)
        qg, kg = gain(q_norm), gain(k_norm)
        pool_groups = pool_w_group[layer]
        late_w = (ffn2_w_down[layer], w_attn_branch[layer], pool_groups.reshape(-1, pool_groups.shape[-1]),
                  w_pool_branch[layer], w_out[layer])

        xs, _ = _ffn(xs, gain(norm_ffn1), wup1, wdn1, ms)
        outs = _proj(xs, gain(norm_mix), win, qg, kg, seg, tm=ms, n_seq=1, dils=(1,) * N_GROUPS,
                     qkv_dtype=F32, keep_rows=None)
        qkv_s, (u_s, ga_s, gb_s) = outs[0:9], outs[9:12]
        sample_parts = _sample_attention_parts(
            [a.reshape(dec_batch, dec_seq, GROUP_WIDTH) for a in qkv_s],
            [_window_in(c[layer]) for c in caches_in], table)

        if (batch * seq) // tm == dec_batch:
            xp, sample_outs = _ffn(xp, gain(norm_ffn1), wup1, wdn1, tm, rider=sample_parts)
        else:
            xp, _ = _ffn(xp, gain(norm_ffn1), wup1, wdn1, tm)
            sample_outs = _sample_attention(sample_parts)
        outs = _proj(xp, gain(norm_mix), win, qg, kg, seg, tm=tm, n_seq=batch, dils=dils,
                     qkv_dtype=BF16, keep_rows=keep_prompt)
        qkv, (u, ga, gb), windows, pstate = outs[0:9], outs[9:12], outs[12:15], outs[15]
        narrowing = ((ffn2_w_up[layer],), late_w) + ((),) * (N_GROUPS - 2)
        o, lse, narrowed = zip(*[_prompt_attention(*qkv[3 * g:3 * g + 3], table, g, to_narrow=narrowing[g])
                                 for g in range(N_GROUPS)])
        (wup2,), (wdn2, wab, wpg, wpb, wout) = narrowed[0:2]
        ffn2_w = (gain(norm_ffn2), wup2, wdn2)
        merge_w = (wab, wpg.reshape(pool_groups.shape), gain(pool_scale), wpb, wout)
        xp = _back(xp, o, lse, u, u.reshape(-1, POOL_HALO, pool_width), ga, gb, merge_w, ffn2_w, tm=tm,
                   halo_block=(1, POOL_HALO, pool_width),
                   halo_index=lambda blk: (jnp.maximum(blk * (tm // POOL_HALO) - 1, 0), 0, 0),
                   pos_base=0, blocks_per_seq=seq // tm)
        for g in range(N_GROUPS):
            kv_p[g].append(_window_out(windows[g]))
        pool_p.append(pstate[:, POOL_HALO - POOL_STATE:])

        o, lse, windows = sample_outs[0:3], sample_outs[3:6], sample_outs[6:9]
        history = jnp.concatenate(
            [jnp.zeros((dec_batch, POOL_HALO - POOL_STATE, pool_width), F32), state_pool[layer]], axis=1)
        xs = _back(xs, o, lse, u_s, history, ga_s, gb_s, merge_w, ffn2_w, tm=ms,
                   halo_block=(dec_batch, POOL_HALO, pool_width), halo_index=lambda blk: (0, 0, 0),
                   pos_base=PAST_LEN, blocks_per_seq=1)
        for g in range(N_GROUPS):
            kv_s[g].append(_window_out(windows[g]))
        ue = jnp.concatenate([state_pool[layer], u_s.reshape(dec_batch, dec_seq, pool_width)], axis=1)
        pool_s.append(ue[:, ue.shape[1] - POOL_STATE:])

    stack = lambda xs_: jnp.stack(xs_, axis=0)
    return (xp.reshape(batch, seq, d_model), xs.reshape(dec_batch, dec_seq, d_model),
            stack(kv_p[0]), stack(kv_p[1]), stack(kv_p[2]), stack(pool_p),
            stack(kv_s[0]), stack(kv_s[1]), stack(kv_s[2]), stack(pool_s))
```

```python
import functools
import math

import numpy as np
import jax
import jax.numpy as jnp
from jax import lax
from jax.experimental import pallas as pl
from jax.experimental.pallas import tpu as pltpu

HEAD_DIM = 64
HEADS_PER_GROUP = 4
GROUP_WIDTH = HEADS_PER_GROUP * HEAD_DIM
ATTN_GROUPS = ((128, 1), (512, 4), (2048, 16))
N_GROUPS = len(ATTN_GROUPS)
ATTN_WIDTH = N_GROUPS * GROUP_WIDTH
N_BUCKETS = 32
MAX_DISTANCE = 2048
POOL_WINDOWS = (2, 4, 8, 16)
POOL_STATE = max(POOL_WINDOWS) - 1
POOL_HALO = 16
EPS = 1e-6
PAST_LEN = 8192
NEG = -1e30
LOG2E = math.log2(math.e)
LANES = 128
MXU_TILE = 256
SLABS = GROUP_WIDTH // LANES
SINGLE_OP_STRIDE = 4
QBLK = 128
ATTN_UNROLL = 8
VMEM_LIMIT = 58 * 1024 * 1024

F32 = jnp.float32
BF16 = jnp.bfloat16
NT_DIMS = (((1,), (1,)), ((), ()))


def _t5_buckets(distance):
    max_exact = N_BUCKETS // 2
    d = np.asarray(distance, dtype=np.int32)
    ratio = np.log(np.maximum(d, 1).astype(np.float32) / np.float32(max_exact))
    large = max_exact + (ratio / np.float32(math.log(MAX_DISTANCE / max_exact))
                         * (N_BUCKETS - max_exact)).astype(np.int32)
    large = np.minimum(large, N_BUCKETS - 1)
    return np.where(d < max_exact, d, large).astype(np.int32)


def _params(n_grid_dims=1):
    return pltpu.CompilerParams(dimension_semantics=("arbitrary",) * n_grid_dims,
                                vmem_limit_bytes=VMEM_LIMIT)


def _resident(shape):
    return pl.BlockSpec(shape, lambda *_: (0,) * len(shape), pipeline_mode=pl.Buffered(1))


_SMEM = pl.BlockSpec(memory_space=pltpu.SMEM)


def _rmsnorm(x, g):
    ms = jnp.mean(x * x, axis=-1, keepdims=True)
    return x * lax.rsqrt(ms + EPS) * g


def _head_select(parts, lane_head):
    out = jnp.where(lane_head == 0, parts[0], 0.0)
    for h in range(1, HEADS_PER_GROUP):
        out = jnp.where(lane_head == h, parts[h], out)
    return out


def _stack_heads(q, lane_head):
    return jnp.concatenate([jnp.where(lane_head == h, q, 0.0)
                            for h in range(HEADS_PER_GROUP)], axis=0).astype(BF16)


def _build_bias(bias_ref, bucket_ref, table_ref, g):
    buckets = bucket_ref[...]
    rows = buckets.shape[0]
    for h in range(HEADS_PER_GROUP):
        acc = jnp.full(buckets.shape, NEG, F32)
        for b in range(N_BUCKETS):
            acc = jnp.where(buckets == b, table_ref[b, g * HEADS_PER_GROUP + h] * LOG2E, acc)
        bias_ref[h * rows:(h + 1) * rows, :] = acc


def _ffn_block(x_ref, g_ref, wup_ref, wdn_ref, o_ref, act_ref):
    d_ff, d = wdn_ref.shape
    h = _rmsnorm(x_ref[...], g_ref[...]).astype(BF16)
    for lo in range(0, d_ff, MXU_TILE):
        gate = jnp.dot(h, wup_ref[:, lo:lo + MXU_TILE], preferred_element_type=F32)
        up = jnp.dot(h, wup_ref[:, d_ff + lo:d_ff + lo + MXU_TILE], preferred_element_type=F32)
        act_ref[:, lo:lo + MXU_TILE] = (gate * jax.nn.sigmoid(gate) * up).astype(BF16)
    for lo in range(0, d, MXU_TILE):
        cols = slice(lo, lo + MXU_TILE)
        o_ref[:, cols] = x_ref[:, cols] + 0.5 * jnp.dot(act_ref[...], wdn_ref[:, cols],
                                                        preferred_element_type=F32)


def _ffn_kernel(*refs, rider_counts):
    n_in, n_out = rider_counts
    x_ref, g_ref, wup_ref, wdn_ref = refs[0:4]
    o_ref = refs[4 + n_in]
    act_ref = refs[5 + n_in + n_out]
    if n_in:
        _sample_attn_block((*refs[4:4 + n_in], *refs[5 + n_in:5 + n_in + n_out], *refs[6 + n_in + n_out:]))
    _ffn_block(x_ref, g_ref, wup_ref, wdn_ref, o_ref, act_ref)


def _ffn(x, gain, w_up, w_down, tm, rider=None):
    m, d = x.shape
    d_ff = w_down.shape[0]
    assert m % tm == 0 and d_ff % MXU_TILE == 0 and d % MXU_TILE == 0
    rider = rider or dict(args=[], in_specs=[], out_shape=[], out_specs=[], scratch_shapes=[], batch=m // tm)
    assert rider["batch"] == m // tm
    outs = pl.pallas_call(
        functools.partial(_ffn_kernel, rider_counts=(len(rider["args"]), len(rider["out_shape"]))),
        out_shape=[jax.ShapeDtypeStruct((m, d), F32)] + rider["out_shape"],
        grid=(m // tm,),
        in_specs=[pl.BlockSpec((tm, d), lambda i: (i, 0)),
                  _resident((1, d)), _resident((d, 2 * d_ff)), _resident((d_ff, d))] + rider["in_specs"],
        out_specs=[pl.BlockSpec((tm, d), lambda i: (i, 0))] + rider["out_specs"],
        scratch_shapes=[pltpu.VMEM((tm, d_ff), BF16)] + rider["scratch_shapes"],
        compiler_params=_params(),
        name="ffn",
    )(x, gain, w_up, w_down, *rider["args"])
    return outs[0], outs[1:]


def _proj_kernel(x_ref, g_ref, win_ref, qn_ref, kn_ref, seg_ref, *refs,
                 dils, kv_rows, with_state, pool_width, d_model):
    qkv_refs = refs[0:3 * N_GROUPS]
    u_ref, ga_ref, gb_ref = refs[3 * N_GROUPS:3 * N_GROUPS + 3]
    state_refs = refs[3 * N_GROUPS + 3:-2]
    slab_ref, regroup_ref = refs[-2:]
    tm = x_ref.shape[0]
    h = _rmsnorm(x_ref[...], g_ref[...]).astype(BF16)

    def proj(lo, width):
        return jnp.dot(h, win_ref[:, lo:lo + width], preferred_element_type=F32)

    def head_norm(y, gain):
        ms = jnp.dot((y * y).astype(BF16), seg_ref[...], preferred_element_type=F32)
        return y * lax.rsqrt(ms + EPS) * gain

    def emit(dst_ref, slot, val, dil):
        if dil == 1:
            dst_ref[0, 0] = val.astype(dst_ref.dtype)
            return
        for s in range(SLABS):
            slab_ref[slot, s] = val[:, s * LANES:(s + 1) * LANES]
        n = tm // dil
        if dil <= SINGLE_OP_STRIDE:
            for r in range(dil):
                parts = [slab_ref[slot, s, pl.ds(r, n, stride=dil), :] for s in range(SLABS)]
                dst_ref[0, r] = jnp.concatenate(parts, axis=-1).astype(dst_ref.dtype)
            return
        outer = SINGLE_OP_STRIDE
        inner = dil // outer
        assert inner <= SINGLE_OP_STRIDE
        quarter = tm // outer
        for a in range(outer):
            for s in range(SLABS):
                regroup_ref[s, a * quarter:(a + 1) * quarter, :] = slab_ref[slot, s, pl.ds(a, quarter, stride=outer), :]
        for r in range(dil):
            start = (r % outer) * quarter + r // outer
            parts = [regroup_ref[s, pl.ds(start, n, stride=inner), :] for s in range(SLABS)]
            dst_ref[0, r] = jnp.concatenate(parts, axis=-1).astype(dst_ref.dtype)

    def finish(g, which, y):
        cols = slice(g * GROUP_WIDTH, (g + 1) * GROUP_WIDTH)
        if which == 0:
            y = head_norm(y, qn_ref[:, cols]) * (HEAD_DIM ** -0.5 * LOG2E)
        elif which == 1:
            y = head_norm(y, kn_ref[:, cols])
        emit(qkv_refs[3 * g + which], 3 * sum(d > 1 for d in dils[:g]) + which, y, dils[g])
        if with_state and which > 0:
            state_refs[g][0, which - 1] = y.T[:, tm - kv_rows[g]:]

    pending = None
    for g in range(N_GROUPS):
        for which in range(3):
            y = proj(which * ATTN_WIDTH + g * GROUP_WIDTH, GROUP_WIDTH)
            if pending is not None:
                finish(*pending)
            pending = (g, which, y)
    u = proj(3 * ATTN_WIDTH, pool_width)
    finish(*pending)
    u_ref[...] = u
    if with_state:
        state_refs[N_GROUPS][0] = u[tm - POOL_HALO:, :]
    for gate_ref, base in ((ga_ref, 3 * ATTN_WIDTH + pool_width), (gb_ref, 3 * ATTN_WIDTH + pool_width + d_model)):
        for lo in range(0, d_model, MXU_TILE):
            gate_ref[:, lo:lo + MXU_TILE] = jax.nn.sigmoid(proj(base + lo, MXU_TILE)).astype(BF16)


def _proj(x, gain, w_in, q_gain, k_gain, seg, *, tm, n_seq, dils, qkv_dtype, keep_rows):
    m, d = x.shape
    pool_width = w_in.shape[1] - 3 * ATTN_WIDTH - 2 * d
    seq_len = m // n_seq
    assert seq_len % tm == 0 and all(tm % dil == 0 for dil in dils)
    blocks_per_seq = seq_len // tm
    with_state = keep_rows is not None
    seq_block = lambda i: (i // blocks_per_seq, i % blocks_per_seq)

    row = lambda w: pl.BlockSpec((tm, w), lambda i: (i, 0))
    out_shape, out_specs = [], []
    for g in range(N_GROUPS):
        sds = jax.ShapeDtypeStruct((n_seq, dils[g], seq_len // dils[g], GROUP_WIDTH), qkv_dtype)
        spec = pl.BlockSpec((1, dils[g], tm // dils[g], GROUP_WIDTH),
                            lambda i: (seq_block(i)[0], 0, seq_block(i)[1], 0))
        out_shape += [sds] * 3
        out_specs += [spec] * 3
    out_shape += [jax.ShapeDtypeStruct((m, pool_width), F32),
                  jax.ShapeDtypeStruct((m, d), BF16), jax.ShapeDtypeStruct((m, d), BF16)]
    out_specs += [row(pool_width), row(d), row(d)]
    kv_rows = kv_first = ()
    if with_state:
        kv_rows = tuple(min(r, tm) for r in keep_rows)
        kv_first = tuple(blocks_per_seq - keep_rows[g] // kv_rows[g] for g in range(N_GROUPS))
        for g in range(N_GROUPS):
            out_shape.append(jax.ShapeDtypeStruct((n_seq, 2, GROUP_WIDTH, keep_rows[g]), F32))
            out_specs.append(pl.BlockSpec(
                (1, 2, GROUP_WIDTH, kv_rows[g]),
                lambda i, first=kv_first[g]: (seq_block(i)[0], 0, 0, jnp.maximum(seq_block(i)[1] - first, 0))))
        out_shape.append(jax.ShapeDtypeStruct((n_seq, POOL_HALO, pool_width), F32))
        out_specs.append(pl.BlockSpec((1, POOL_HALO, pool_width), lambda i: (seq_block(i)[0], 0, 0)))
    return pl.pallas_call(
        functools.partial(_proj_kernel, dils=dils, kv_rows=kv_rows, with_state=with_state,
                          pool_width=pool_width, d_model=d),
        out_shape=out_shape,
        grid=(m // tm,),
        in_specs=[row(d), _resident((1, d)), _resident(w_in.shape),
                  _resident((1, ATTN_WIDTH)), _resident((1, ATTN_WIDTH)),
                  _resident((GROUP_WIDTH, GROUP_WIDTH))],
        out_specs=out_specs,
        scratch_shapes=[pltpu.VMEM((max(1, 3 * sum(dil > 1 for dil in dils)), SLABS, tm, LANES), F32),
                        pltpu.VMEM((SLABS, tm, LANES), F32)],
        compiler_params=_params(),
        name="proj",
    )(x, gain, w_in, q_gain, k_gain, seg)


def _attn_kernel(*refs, g, dil, n_blocks, unroll, n_cast):
    q_ref, k_ref, v_ref, bucket_ref, hmask_ref, table_ref = refs[0:6]
    wide_refs = refs[6:6 + n_cast]
    o_ref, lse_ref = refs[6 + n_cast:8 + n_cast]
    narrow_refs = refs[8 + n_cast:8 + 2 * n_cast]
    bias_ref = refs[8 + 2 * n_cast]
    for wide_ref, narrow_ref in zip(wide_refs, narrow_refs):
        narrow_ref[...] = wide_ref[...].astype(BF16)

    @pl.when(pl.program_id(0) == 0)
    def _():
        _build_bias(bias_ref.at[1], bucket_ref, table_ref, g)
        bias_ref[0, :, 0:QBLK] = bias_ref[1, :, QBLK:2 * QBLK]
        bias_ref[0, :, QBLK:2 * QBLK] = jnp.full((HEADS_PER_GROUP * QBLK, QBLK), NEG, F32)

    first_head = lax.broadcasted_iota(jnp.int32, (QBLK, LANES), 1) < HEAD_DIM
    heads = range(HEADS_PER_GROUP)

    def per_slab(col):
        part = lambda h: jnp.broadcast_to(col[h * QBLK:(h + 1) * QBLK], (QBLK, LANES))
        return [jnp.where(first_head, part(2 * s), part(2 * s + 1)) for s in range(SLABS)]

    def body(n, carry):
        res = n // n_blocks
        i = n % n_blocks
        row0 = pl.multiple_of(i * QBLK, QBLK)
        span0 = pl.multiple_of(jnp.maximum(i - 1, 0) * QBLK, QBLK)
        q = q_ref[0, res, pl.ds(row0, QBLK), :]
        qs = jnp.concatenate([q * hmask_ref[h, 0:QBLK, :] for h in heads], axis=0)
        s = lax.dot_general(qs, k_ref[0, res, pl.ds(span0, 2 * QBLK), :], NT_DIMS,
                            preferred_element_type=F32) + bias_ref[jnp.minimum(i, 1)]
        m = jnp.max(s, axis=-1, keepdims=True)
        p = jnp.exp2(s - m)
        l = jnp.sum(p, axis=-1, keepdims=True)
        p = p.astype(BF16)
        v = v_ref[0, res, pl.ds(span0, 2 * QBLK), :]
        p_wide = jnp.concatenate([p[h * QBLK:(h + 1) * QBLK] for h in heads], axis=1)
        v_tall = jnp.concatenate([v * hmask_ref[h] for h in heads], axis=0)
        pv = jnp.dot(p_wide, v_tall, preferred_element_type=F32)
        token0 = res + dil * row0
        dst = pl.ds(token0, QBLK) if dil == 1 else pl.ds(token0, QBLK, stride=dil)
        for s_, (m_s, l_s) in enumerate(zip(per_slab(m), per_slab(l))):
            o_ref[0, s_, dst, :] = pv[:, s_ * LANES:(s_ + 1) * LANES] / l_s
            lse_ref[0, s_, dst, :] = m_s + jnp.log2(l_s)
        return carry

    lax.fori_loop(0, dil * n_blocks, body, 0, unroll=unroll)


def _prompt_buckets(g):
    win, dil = ATTN_GROUPS[g]
    delta = np.arange(QBLK)[:, None] + QBLK - np.arange(2 * QBLK)[None, :]
    valid = (delta >= 0) & (delta <= win // dil)
    return np.where(valid, _t5_buckets(dil * np.clip(delta, 0, win // dil)), -1).astype(np.int32)


def _prompt_attention(q, k, v, table, g, to_narrow=()):
    _, dil = ATTN_GROUPS[g]
    batch, _, sub, _ = q.shape
    n_blocks = sub // QBLK
    assert sub % QBLK == 0 and n_blocks >= 2 and (dil * n_blocks) % ATTN_UNROLL == 0
    assert all(w.shape[0] % (16 * batch) == 0 for w in to_narrow)
    cast_specs = [pl.BlockSpec((w.shape[0] // batch, w.shape[1]), lambda b: (b, 0)) for w in to_narrow]
    head_of_lane = np.arange(GROUP_WIDTH) // HEAD_DIM
    hmask = np.broadcast_to(head_of_lane[None, None, :] == np.arange(HEADS_PER_GROUP)[:, None, None],
                            (HEADS_PER_GROUP, 2 * QBLK, GROUP_WIDTH))
    in_spec = pl.BlockSpec((1, dil, sub, GROUP_WIDTH), lambda b: (b, 0, 0, 0))
    out_spec = pl.BlockSpec((1, SLABS, sub * dil, LANES), lambda b: (b, 0, 0, 0))
    out_sds = jax.ShapeDtypeStruct((batch, SLABS, sub * dil, LANES), F32)
    outs = pl.pallas_call(
        functools.partial(_attn_kernel, g=g, dil=dil, n_blocks=n_blocks, unroll=ATTN_UNROLL,
                          n_cast=len(to_narrow)),
        out_shape=[out_sds, out_sds] + [jax.ShapeDtypeStruct(w.shape, BF16) for w in to_narrow],
        grid=(batch,),
        in_specs=[in_spec, in_spec, in_spec, _resident((QBLK, 2 * QBLK)), _resident(hmask.shape), _SMEM]
        + cast_specs,
        out_specs=[out_spec, out_spec] + cast_specs,
        scratch_shapes=[pltpu.VMEM((2, HEADS_PER_GROUP * QBLK, 2 * QBLK), F32)],
        compiler_params=_params(),
        name=f"attn_g{g}",
    )(q, k, v, jnp.asarray(_prompt_buckets(g)), jnp.asarray(hmask, BF16), table, *to_narrow)
    return outs[0], outs[1], outs[2:]


def _sample_attn_block(refs):
    qkv_refs = refs[0:9]
    cache_refs = refs[9:12]
    bucket_refs, table_ref = refs[12:15], refs[15]
    o_refs, lse_refs, win_refs = refs[16:19], refs[19:22], refs[22:25]
    bias_refs = refs[25:28]

    @pl.when(pl.program_id(0) == 0)
    def _():
        for g in range(N_GROUPS):
            _build_bias(bias_refs[g], bucket_refs[g], table_ref, g)

    t_new = qkv_refs[0].shape[1]
    lane_head = lax.broadcasted_iota(jnp.int32, (t_new, GROUP_WIDTH), 1) // HEAD_DIM
    new_lane = lax.broadcasted_iota(jnp.int32, (GROUP_WIDTH, LANES), 1) >= LANES - t_new
    pad = jnp.zeros((LANES - t_new, GROUP_WIDTH), F32)

    for g in range(N_GROUPS):
        q_ref, k_ref, v_ref = qkv_refs[3 * g:3 * g + 3]
        cache_ref, win_ref, bias_ref = cache_refs[g], win_refs[g], bias_refs[g]
        past = cache_ref.shape[3]
        new_rows = [jnp.concatenate([pad, r[0]], axis=0) for r in (k_ref, v_ref)]
        for c in range(2):
            shifted = pltpu.roll(cache_ref[0, c], past - t_new, 1)
            if past > LANES:
                win_ref[0, c, :, 0:past - LANES] = shifted[:, 0:past - LANES]
            win_ref[0, c, :, past - LANES:past] = jnp.where(new_lane, new_rows[c].T, shifted[:, past - LANES:])

        qs = _stack_heads(q_ref[0], lane_head)
        s_old = jnp.dot(qs, cache_ref[0, 0].astype(BF16), preferred_element_type=F32) + bias_ref[:, 0:past]
        s_new = lax.dot_general(qs, new_rows[0].astype(BF16), NT_DIMS,
                                preferred_element_type=F32) + bias_ref[:, past:past + LANES]
        m = jnp.maximum(jnp.max(s_old, axis=-1, keepdims=True), jnp.max(s_new, axis=-1, keepdims=True))
        p_old = jnp.exp2(s_old - m)
        p_new = jnp.exp2(s_new - m)
        l = jnp.sum(p_old, axis=-1, keepdims=True) + jnp.sum(p_new, axis=-1, keepdims=True)
        pv = (lax.dot_general(p_old.astype(BF16), cache_ref[0, 1].astype(BF16), NT_DIMS,
                              preferred_element_type=F32)
              + jnp.dot(p_new.astype(BF16), new_rows[1].astype(BF16), preferred_element_type=F32)) / l
        lse = jnp.broadcast_to(m + jnp.log2(l), pv.shape)
        rows = lambda a: [a[h * t_new:(h + 1) * t_new] for h in range(HEADS_PER_GROUP)]
        for out_ref, val in ((o_refs[g], _head_select(rows(pv), lane_head)),
                             (lse_refs[g], _head_select(rows(lse), lane_head))):
            for s_ in range(SLABS):
                out_ref[0, s_] = val[:, s_ * LANES:(s_ + 1) * LANES]


def _sample_attn_kernel(*refs):
    _sample_attn_block(refs)


def _sample_buckets(g, past, t_new):
    win, dil = ATTN_GROUPS[g]
    lane = np.arange(past + LANES)[None, :]
    key_pos = np.where(lane < past, lane, lane - (LANES - t_new))
    delta = past + np.arange(t_new)[:, None] - key_pos
    valid = ((lane < past) | (lane >= past + LANES - t_new)) & (delta >= 0) & (delta % dil == 0) & (delta <= win)
    return np.where(valid, _t5_buckets(np.clip(delta, 0, win)), -1).astype(np.int32)


def _sample_attention_parts(qkv, caches, table):
    batch, t_new, _ = qkv[0].shape
    pasts = [c.shape[3] for c in caches]
    buckets = [jnp.asarray(_sample_buckets(g, pasts[g], t_new)) for g in range(N_GROUPS)]
    cache_spec = lambda p: pl.BlockSpec((1, 2, GROUP_WIDTH, p), lambda b: (b, 0, 0, 0))
    new_spec = pl.BlockSpec((1, t_new, GROUP_WIDTH), lambda b: (b, 0, 0))
    out_spec = pl.BlockSpec((1, SLABS, t_new, LANES), lambda b: (0, 0, b, 0))
    out_sds = jax.ShapeDtypeStruct((1, SLABS, batch * t_new, LANES), F32)
    return dict(
        batch=batch,
        args=[*qkv, *caches, *buckets, table],
        out_shape=[out_sds] * 6 + [jax.ShapeDtypeStruct(c.shape, F32) for c in caches],
        in_specs=[new_spec] * 9 + [cache_spec(p) for p in pasts] + [_resident(b.shape) for b in buckets]
        + [_SMEM],
        out_specs=[out_spec] * 6 + [cache_spec(p) for p in pasts],
        scratch_shapes=[pltpu.VMEM((HEADS_PER_GROUP * t_new, p + LANES), F32) for p in pasts])


def _sample_attention(parts):
    return pl.pallas_call(
        _sample_attn_kernel, grid=(parts["batch"],), out_shape=parts["out_shape"],
        in_specs=parts["in_specs"], out_specs=parts["out_specs"], scratch_shapes=parts["scratch_shapes"],
        compiler_params=_params(), name="sample_attn")(*parts["args"])


def _back_kernel(x_ref, o0_ref, o1_ref, o2_ref, l0_ref, l1_ref, l2_ref, u_ref, halo_ref, ga_ref, gb_ref,
                 wab_ref, wpg_ref, pscale_ref, wpb_ref, wout_ref, g2_ref, wup_ref, wdn_ref,
                 out_ref, ue_ref, mixed_ref, act_ref, attn_ref, pooled_ref, merged_ref,
                 *, pos_base, blocks_per_seq):
    block = pl.program_id(0)
    tm, d_model = x_ref.shape
    n_seq = halo_ref.shape[0]
    t = tm // n_seq
    pool_width = u_ref.shape[1]
    gw = pool_width // len(POOL_WINDOWS)

    for s in range(SLABS):
        lses = [r[0, s] for r in (l0_ref, l1_ref, l2_ref)]
        top = jnp.maximum(jnp.maximum(lses[0], lses[1]), lses[2])
        num = jnp.zeros_like(top)
        den = jnp.zeros_like(top)
        for o_ref, lse in zip((o0_ref, o1_ref, o2_ref), lses):
            e = jnp.exp2(lse - top)
            num = num + e * o_ref[0, s]
            den = den + e
        attn_ref[:, s * LANES:(s + 1) * LANES] = (num / den).astype(BF16)

    start = pos_base + (block % blocks_per_seq) * t
    ue_ref[:, 0:POOL_HALO, :] = jnp.where(start > 0, halo_ref[...], 0.0)
    ue_ref[:, POOL_HALO:, :] = u_ref[...].reshape(n_seq, t, pool_width)
    pos = start + lax.broadcasted_iota(jnp.int32, (1, t, gw), 1)
    for gi, win in enumerate(POOL_WINDOWS):
        cols = slice(gi * gw, (gi + 1) * gw)
        s = ue_ref[:, POOL_HALO:, cols]
        for back in range(1, win):
            s = s + ue_ref[:, POOL_HALO - back:POOL_HALO - back + t, cols]
        cnt = jnp.minimum(pos + 1, win).astype(F32)
        d = (s / cnt - ue_ref[:, POOL_HALO:, cols]).reshape(tm, gw)
        y = jnp.dot(d.astype(BF16), wpg_ref[gi], preferred_element_type=F32)
        pooled_ref[:, cols] = (y * pscale_ref[:, cols]).astype(BF16)

    for lo in range(0, d_model, MXU_TILE):
        cols = slice(lo, lo + MXU_TILE)
        branch_a = jnp.dot(attn_ref[...], wab_ref[:, cols], preferred_element_type=F32)
        branch_b = jnp.dot(pooled_ref[...], wpb_ref[:, cols], preferred_element_type=F32)
        merged_ref[:, cols] = (ga_ref[:, cols].astype(F32) * branch_a
                               + gb_ref[:, cols].astype(F32) * branch_b).astype(BF16)
    for lo in range(0, d_model, MXU_TILE):
        cols = slice(lo, lo + MXU_TILE)
        mixed_ref[:, cols] = x_ref[:, cols] + jnp.dot(merged_ref[...], wout_ref[:, cols],
                                                      preferred_element_type=F32)

    _ffn_block(mixed_ref, g2_ref, wup_ref, wdn_ref, out_ref, act_ref)


def _back(x, o, lse, u, halo, ga, gb, merge_w, ffn_w, *, tm, halo_block, halo_index, pos_base, blocks_per_seq):
    m, d = x.shape
    pool_width = u.shape[1]
    n_seq = halo_block[0]
    d_ff = ffn_w[2].shape[0]
    row = lambda wd: pl.BlockSpec((tm, wd), lambda i: (i, 0))
    slab = pl.BlockSpec((1, SLABS, tm, LANES), lambda i: (i // blocks_per_seq, 0, i % blocks_per_seq, 0))
    return pl.pallas_call(
        functools.partial(_back_kernel, pos_base=pos_base, blocks_per_seq=blocks_per_seq),
        out_shape=jax.ShapeDtypeStruct((m, d), F32),
        grid=(m // tm,),
        in_specs=[row(d)] + [slab] * 6 + [row(pool_width), pl.BlockSpec(halo_block, halo_index), row(d), row(d)]
        + [_resident(a.shape) for a in (*merge_w, *ffn_w)],
        out_specs=row(d),
        scratch_shapes=[pltpu.VMEM((n_seq, POOL_HALO + tm // n_seq, pool_width), F32),
                        pltpu.VMEM((tm, d), F32), pltpu.VMEM((tm, d_ff), BF16),
                        pltpu.VMEM((tm, GROUP_WIDTH), BF16), pltpu.VMEM((tm, pool_width), BF16),
                        pltpu.VMEM((tm, d), BF16)],
        compiler_params=_params(),
        name="back",
    )(x, *o, *lse, u, halo, ga, gb, *merge_w, *ffn_w)


def _window_in(cache):
    n_seq, rows = cache.shape[0:2]
    return jnp.transpose(cache, (0, 2, 3, 4, 1)).reshape(n_seq, 2, GROUP_WIDTH, rows)


def _window_out(kv):
    n_seq, _, _, rows = kv.shape
    return jnp.transpose(kv.reshape(n_seq, 2, HEADS_PER_GROUP, HEAD_DIM, rows), (0, 4, 1, 2, 3))


def kernel(x_prompt, x_sample, cache_kv_w128, cache_kv_w512, cache_kv_w2048, state_pool, rel_bias_table,
           norm_ffn1, ffn1_w_up, ffn1_w_down, norm_mix, w_in, q_norm, k_norm, pool_w_group, pool_scale,
           w_attn_branch, w_pool_branch, w_out, norm_ffn2, ffn2_w_up, ffn2_w_down):
    batch, seq, d_model = x_prompt.shape
    dec_batch, dec_seq, _ = x_sample.shape
    depth = norm_ffn1.shape[0]
    pool_width = state_pool.shape[-1]
    tm = 512
    ms = dec_batch * dec_seq
    seg = jnp.asarray(np.kron(np.eye(HEADS_PER_GROUP), np.full((HEAD_DIM, HEAD_DIM), 1.0 / HEAD_DIM)), BF16)
    table = rel_bias_table.astype(F32)
    caches_in = (cache_kv_w128, cache_kv_w512, cache_kv_w2048)
    keep_prompt = tuple(min(win, seq) for win, _ in ATTN_GROUPS)
    dils = tuple(dil for _, dil in ATTN_GROUPS)

    xp = x_prompt.reshape(batch * seq, d_model)
    xs = x_sample.reshape(ms, d_model)
    kv_p, kv_s, pool_p, pool_s = ([], [], []), ([], [], []), [], []
    for layer in range(depth):
        gain = lambda a: a[layer].reshape(1, -1).astype(F32)
        wup1, wdn1 = ffn1_w_up[layer].astype(BF16), ffn1_w_down[layer].astype(BF16)
        win = w_in[layer].astype(BF16)
        qg, kg = gain(q_norm), gain(k_norm)
        pool_groups = pool_w_group[layer]
        late_w = (ffn2_w_down[layer], w_attn_branch[layer], pool_groups.reshape(-1, pool_groups.shape[-1]),
                  w_pool_branch[layer], w_out[layer])

        xs, _ = _ffn(xs, gain(norm_ffn1), wup1, wdn1, ms)
        outs = _proj(xs, gain(norm_mix), win, qg, kg, seg, tm=ms, n_seq=1, dils=(1,) * N_GROUPS,
                     qkv_dtype=F32, keep_rows=None)
        qkv_s, (u_s, ga_s, gb_s) = outs[0:9], outs[9:12]
        sample_parts = _sample_attention_parts(
            [a.reshape(dec_batch, dec_seq, GROUP_WIDTH) for a in qkv_s],
            [_window_in(c[layer]) for c in caches_in], table)

        if (batch * seq) // tm == dec_batch:
            xp, sample_outs = _ffn(xp, gain(norm_ffn1), wup1, wdn1, tm, rider=sample_parts)
        else:
            xp, _ = _ffn(xp, gain(norm_ffn1), wup1, wdn1, tm)
            sample_outs = _sample_attention(sample_parts)
        outs = _proj(xp, gain(norm_mix), win, qg, kg, seg, tm=2 * tm, n_seq=batch, dils=dils,
                     qkv_dtype=BF16, keep_rows=keep_prompt)
        qkv, (u, ga, gb), windows, pstate = outs[0:9], outs[9:12], outs[12:15], outs[15]
        narrowing = ((ffn2_w_up[layer],), late_w) + ((),) * (N_GROUPS - 2)
        o, lse, narrowed = zip(*[_prompt_attention(*qkv[3 * g:3 * g + 3], table, g, to_narrow=narrowing[g])
                                 for g in range(N_GROUPS)])
        (wup2,), (wdn2, wab, wpg, wpb, wout) = narrowed[0:2]
        ffn2_w = (gain(norm_ffn2), wup2, wdn2)
        merge_w = (wab, wpg.reshape(pool_groups.shape), gain(pool_scale), wpb, wout)
        xp = _back(xp, o, lse, u, u.reshape(-1, POOL_HALO, pool_width), ga, gb, merge_w, ffn2_w, tm=tm,
                   halo_block=(1, POOL_HALO, pool_width),
                   halo_index=lambda blk: (jnp.maximum(blk * (tm // POOL_HALO) - 1, 0), 0, 0),
                   pos_base=0, blocks_per_seq=seq // tm)
        for g in range(N_GROUPS):
            kv_p[g].append(_window_out(windows[g]))
        pool_p.append(pstate[:, POOL_HALO - POOL_STATE:])

        o, lse, windows = sample_outs[0:3], sample_outs[3:6], sample_outs[6:9]
        history = jnp.concatenate(
            [jnp.zeros((dec_batch, POOL_HALO - POOL_STATE, pool_width), F32), state_pool[layer]], axis=1)
        xs = _back(xs, o, lse, u_s, history, ga_s, gb_s, merge_w, ffn2_w, tm=ms,
                   halo_block=(dec_batch, POOL_HALO, pool_width), halo_index=lambda blk: (0, 0, 0),
                   pos_base=PAST_LEN, blocks_per_seq=1)
        for g in range(N_GROUPS):
            kv_s[g].append(_window_out(windows[g]))
        ue = jnp.concatenate([state_pool[layer], u_s.reshape(dec_batch, dec_seq, pool_width)], axis=1)
        pool_s.append(ue[:, ue.shape[1] - POOL_STATE:])

    stack = lambda xs_: jnp.stack(xs_, axis=0)
    return (xp.reshape(batch, seq, d_model), xs.reshape(dec_batch, dec_seq, d_model),
            stack(kv_p[0]), stack(kv_p[1]), stack(kv_p[2]), stack(pool_p),
            stack(kv_s[0]), stack(kv_s[1]), stack(kv_s[2]), stack(pool_s))
```

```python
import functools
import math

import numpy as np
import jax
import jax.numpy as jnp
from jax import lax
from jax.experimental import pallas as pl
from jax.experimental.pallas import tpu as pltpu

HEAD_DIM = 64
HEADS_PER_GROUP = 4
GROUP_WIDTH = HEADS_PER_GROUP * HEAD_DIM
ATTN_GROUPS = ((128, 1), (512, 4), (2048, 16))
N_GROUPS = len(ATTN_GROUPS)
ATTN_WIDTH = N_GROUPS * GROUP_WIDTH
N_BUCKETS = 32
MAX_DISTANCE = 2048
POOL_WINDOWS = (2, 4, 8, 16)
POOL_STATE = max(POOL_WINDOWS) - 1
POOL_HALO = 16
POOL_PAD = POOL_HALO + 8
EPS = 1e-6
PAST_LEN = 8192
NEG = -1e30
LOG2E = math.log2(math.e)
LANES = 128
MXU_TILE = 256
SLABS = GROUP_WIDTH // LANES
SINGLE_OP_STRIDE = 4
QBLK = 128
ATTN_UNROLL = 8
VMEM_LIMIT = 58 * 1024 * 1024

F32 = jnp.float32
BF16 = jnp.bfloat16
NT_DIMS = (((1,), (1,)), ((), ()))


def _t5_buckets(distance):
    max_exact = N_BUCKETS // 2
    d = np.asarray(distance, dtype=np.int32)
    ratio = np.log(np.maximum(d, 1).astype(np.float32) / np.float32(max_exact))
    large = max_exact + (ratio / np.float32(math.log(MAX_DISTANCE / max_exact))
                         * (N_BUCKETS - max_exact)).astype(np.int32)
    large = np.minimum(large, N_BUCKETS - 1)
    return np.where(d < max_exact, d, large).astype(np.int32)


def _params(n_grid_dims=1):
    return pltpu.CompilerParams(dimension_semantics=("arbitrary",) * n_grid_dims,
                                vmem_limit_bytes=VMEM_LIMIT)


def _resident(shape):
    return pl.BlockSpec(shape, lambda *_: (0,) * len(shape), pipeline_mode=pl.Buffered(1))


_SMEM = pl.BlockSpec(memory_space=pltpu.SMEM)


def _rmsnorm(x, g):
    ms = jnp.mean(x * x, axis=-1, keepdims=True)
    return x * lax.rsqrt(ms + EPS) * g


def _head_select(parts, lane_head):
    out = jnp.where(lane_head == 0, parts[0], 0.0)
    for h in range(1, HEADS_PER_GROUP):
        out = jnp.where(lane_head == h, parts[h], out)
    return out


def _stack_heads(q, lane_head):
    return jnp.concatenate([jnp.where(lane_head == h, q, 0.0)
                            for h in range(HEADS_PER_GROUP)], axis=0).astype(BF16)


def _build_bias(bias_ref, bucket_ref, table_ref, g):
    buckets = bucket_ref[...]
    rows = buckets.shape[0]
    for h in range(HEADS_PER_GROUP):
        acc = jnp.full(buckets.shape, NEG, F32)
        for b in range(N_BUCKETS):
            acc = jnp.where(buckets == b, table_ref[b, g * HEADS_PER_GROUP + h] * LOG2E, acc)
        bias_ref[h * rows:(h + 1) * rows, :] = acc


def _ffn_block(x_ref, g_ref, wup_ref, wdn_ref, o_ref, act_ref):
    d_ff, d = wdn_ref.shape
    h = _rmsnorm(x_ref[...], g_ref[...]).astype(BF16)
    for lo in range(0, d_ff, MXU_TILE):
        gate = jnp.dot(h, wup_ref[:, lo:lo + MXU_TILE], preferred_element_type=F32)
        up = jnp.dot(h, wup_ref[:, d_ff + lo:d_ff + lo + MXU_TILE], preferred_element_type=F32)
        act_ref[:, lo:lo + MXU_TILE] = (gate * jax.nn.sigmoid(gate) * up).astype(BF16)
    for lo in range(0, d, MXU_TILE):
        cols = slice(lo, lo + MXU_TILE)
        o_ref[:, cols] = x_ref[:, cols] + 0.5 * jnp.dot(act_ref[...], wdn_ref[:, cols],
                                                        preferred_element_type=F32)


def _ffn_kernel(*refs, rider_counts):
    n_in, n_out = rider_counts
    x_ref, g_ref, wup_ref, wdn_ref = refs[0:4]
    o_ref = refs[4 + n_in]
    act_ref = refs[5 + n_in + n_out]
    if n_in:
        _sample_attn_block((*refs[4:4 + n_in], *refs[5 + n_in:5 + n_in + n_out], *refs[6 + n_in + n_out:]))
    _ffn_block(x_ref, g_ref, wup_ref, wdn_ref, o_ref, act_ref)


def _ffn(x, gain, w_up, w_down, tm, rider=None):
    m, d = x.shape
    d_ff = w_down.shape[0]
    assert m % tm == 0 and d_ff % MXU_TILE == 0 and d % MXU_TILE == 0
    rider = rider or dict(args=[], in_specs=[], out_shape=[], out_specs=[], scratch_shapes=[], batch=m // tm)
    assert rider["batch"] == m // tm
    outs = pl.pallas_call(
        functools.partial(_ffn_kernel, rider_counts=(len(rider["args"]), len(rider["out_shape"]))),
        out_shape=[jax.ShapeDtypeStruct((m, d), F32)] + rider["out_shape"],
        grid=(m // tm,),
        in_specs=[pl.BlockSpec((tm, d), lambda i: (i, 0)),
                  _resident((1, d)), _resident((d, 2 * d_ff)), _resident((d_ff, d))] + rider["in_specs"],
        out_specs=[pl.BlockSpec((tm, d), lambda i: (i, 0))] + rider["out_specs"],
        scratch_shapes=[pltpu.VMEM((tm, d_ff), BF16)] + rider["scratch_shapes"],
        compiler_params=_params(),
        name="ffn",
    )(x, gain, w_up, w_down, *rider["args"])
    return outs[0], outs[1:]


def _proj_kernel(x_ref, g_ref, win_ref, qn_ref, kn_ref, seg_ref, *refs,
                 dils, kv_rows, with_state, pool_width, d_model):
    qkv_refs = refs[0:3 * N_GROUPS]
    u_ref, ga_ref, gb_ref = refs[3 * N_GROUPS:3 * N_GROUPS + 3]
    state_refs = refs[3 * N_GROUPS + 3:-2]
    slab_ref, regroup_ref = refs[-2:]
    tm = x_ref.shape[0]
    h = _rmsnorm(x_ref[...], g_ref[...]).astype(BF16)

    def proj(lo, width):
        return jnp.dot(h, win_ref[:, lo:lo + width], preferred_element_type=F32)

    def head_norm(y, gain):
        ms = jnp.dot((y * y).astype(BF16), seg_ref[...], preferred_element_type=F32)
        return y * lax.rsqrt(ms + EPS) * gain

    def emit(dst_ref, slot, val, dil):
        if dil == 1:
            dst_ref[0, 0] = val.astype(dst_ref.dtype)
            return
        for s in range(SLABS):
            slab_ref[slot, s] = val[:, s * LANES:(s + 1) * LANES]
        n = tm // dil
        if dil <= SINGLE_OP_STRIDE:
            for r in range(dil):
                parts = [slab_ref[slot, s, pl.ds(r, n, stride=dil), :] for s in range(SLABS)]
                dst_ref[0, r] = jnp.concatenate(parts, axis=-1).astype(dst_ref.dtype)
            return
        outer = SINGLE_OP_STRIDE
        inner = dil // outer
        assert inner <= SINGLE_OP_STRIDE
        quarter = tm // outer
        for a in range(outer):
            for s in range(SLABS):
                regroup_ref[s, a * quarter:(a + 1) * quarter, :] = slab_ref[slot, s, pl.ds(a, quarter, stride=outer), :]
        for r in range(dil):
            start = (r % outer) * quarter + r // outer
            parts = [regroup_ref[s, pl.ds(start, n, stride=inner), :] for s in range(SLABS)]
            dst_ref[0, r] = jnp.concatenate(parts, axis=-1).astype(dst_ref.dtype)

    def finish(g, which, y):
        cols = slice(g * GROUP_WIDTH, (g + 1) * GROUP_WIDTH)
        if which == 0:
            y = head_norm(y, qn_ref[:, cols]) * (HEAD_DIM ** -0.5 * LOG2E)
        elif which == 1:
            y = head_norm(y, kn_ref[:, cols])
        emit(qkv_refs[3 * g + which], 3 * sum(d > 1 for d in dils[:g]) + which, y, dils[g])
        if with_state and which > 0:
            state_refs[g][0, which - 1] = y.T[:, tm - kv_rows[g]:]

    pending = None
    for g in range(N_GROUPS):
        for which in range(3):
            y = proj(which * ATTN_WIDTH + g * GROUP_WIDTH, GROUP_WIDTH)
            if pending is not None:
                finish(*pending)
            pending = (g, which, y)
    u = proj(3 * ATTN_WIDTH, pool_width)
    finish(*pending)
    u_ref[...] = u
    if with_state:
        state_refs[N_GROUPS][0] = u[tm - POOL_HALO:, :]
    for gate_ref, base in ((ga_ref, 3 * ATTN_WIDTH + pool_width), (gb_ref, 3 * ATTN_WIDTH + pool_width + d_model)):
        for lo in range(0, d_model, MXU_TILE):
            gate_ref[:, lo:lo + MXU_TILE] = jax.nn.sigmoid(proj(base + lo, MXU_TILE)).astype(BF16)


def _proj(x, gain, w_in, q_gain, k_gain, seg, *, tm, n_seq, dils, qkv_dtype, keep_rows):
    m, d = x.shape
    pool_width = w_in.shape[1] - 3 * ATTN_WIDTH - 2 * d
    seq_len = m // n_seq
    assert seq_len % tm == 0 and all(tm % dil == 0 for dil in dils)
    blocks_per_seq = seq_len // tm
    with_state = keep_rows is not None
    seq_block = lambda i: (i // blocks_per_seq, i % blocks_per_seq)

    row = lambda w: pl.BlockSpec((tm, w), lambda i: (i, 0))
    out_shape, out_specs = [], []
    for g in range(N_GROUPS):
        sds = jax.ShapeDtypeStruct((n_seq, dils[g], seq_len // dils[g], GROUP_WIDTH), qkv_dtype)
        spec = pl.BlockSpec((1, dils[g], tm // dils[g], GROUP_WIDTH),
                            lambda i: (seq_block(i)[0], 0, seq_block(i)[1], 0))
        out_shape += [sds] * 3
        out_specs += [spec] * 3
    out_shape += [jax.ShapeDtypeStruct((m, pool_width), F32),
                  jax.ShapeDtypeStruct((m, d), BF16), jax.ShapeDtypeStruct((m, d), BF16)]
    out_specs += [row(pool_width), row(d), row(d)]
    kv_rows = kv_first = ()
    if with_state:
        kv_rows = tuple(min(r, tm) for r in keep_rows)
        kv_first = tuple(blocks_per_seq - keep_rows[g] // kv_rows[g] for g in range(N_GROUPS))
        for g in range(N_GROUPS):
            out_shape.append(jax.ShapeDtypeStruct((n_seq, 2, GROUP_WIDTH, keep_rows[g]), F32))
            out_specs.append(pl.BlockSpec(
                (1, 2, GROUP_WIDTH, kv_rows[g]),
                lambda i, first=kv_first[g]: (seq_block(i)[0], 0, 0, jnp.maximum(seq_block(i)[1] - first, 0))))
        out_shape.append(jax.ShapeDtypeStruct((n_seq, POOL_HALO, pool_width), F32))
        out_specs.append(pl.BlockSpec((1, POOL_HALO, pool_width), lambda i: (seq_block(i)[0], 0, 0)))
    return pl.pallas_call(
        functools.partial(_proj_kernel, dils=dils, kv_rows=kv_rows, with_state=with_state,
                          pool_width=pool_width, d_model=d),
        out_shape=out_shape,
        grid=(m // tm,),
        in_specs=[row(d), _resident((1, d)), _resident(w_in.shape),
                  _resident((1, ATTN_WIDTH)), _resident((1, ATTN_WIDTH)),
                  _resident((GROUP_WIDTH, GROUP_WIDTH))],
        out_specs=out_specs,
        scratch_shapes=[pltpu.VMEM((max(1, 3 * sum(dil > 1 for dil in dils)), SLABS, tm, LANES), F32),
                        pltpu.VMEM((SLABS, tm, LANES), F32)],
        compiler_params=_params(),
        name="proj",
    )(x, gain, w_in, q_gain, k_gain, seg)


def _attn_kernel(*refs, g, dil, n_blocks, unroll, n_cast):
    q_ref, k_ref, v_ref, bucket_ref, hmask_ref, table_ref = refs[0:6]
    wide_refs = refs[6:6 + n_cast]
    o_ref, lse_ref = refs[6 + n_cast:8 + n_cast]
    narrow_refs = refs[8 + n_cast:8 + 2 * n_cast]
    bias_ref = refs[8 + 2 * n_cast]
    for wide_ref, narrow_ref in zip(wide_refs, narrow_refs):
        narrow_ref[...] = wide_ref[...].astype(BF16)

    @pl.when(pl.program_id(0) == 0)
    def _():
        _build_bias(bias_ref.at[1], bucket_ref, table_ref, g)
        bias_ref[0, :, 0:QBLK] = bias_ref[1, :, QBLK:2 * QBLK]
        bias_ref[0, :, QBLK:2 * QBLK] = jnp.full((HEADS_PER_GROUP * QBLK, QBLK), NEG, F32)

    first_head = lax.broadcasted_iota(jnp.int32, (QBLK, LANES), 1) < HEAD_DIM
    heads = range(HEADS_PER_GROUP)

    def per_slab(col):
        part = lambda h: jnp.broadcast_to(col[h * QBLK:(h + 1) * QBLK], (QBLK, LANES))
        return [jnp.where(first_head, part(2 * s), part(2 * s + 1)) for s in range(SLABS)]

    def body(n, carry):
        res = n // n_blocks
        i = n % n_blocks
        row0 = pl.multiple_of(i * QBLK, QBLK)
        span0 = pl.multiple_of(jnp.maximum(i - 1, 0) * QBLK, QBLK)
        q = q_ref[0, res, pl.ds(row0, QBLK), :]
        qs = jnp.concatenate([q * hmask_ref[h, 0:QBLK, :] for h in heads], axis=0)
        s = lax.dot_general(qs, k_ref[0, res, pl.ds(span0, 2 * QBLK), :], NT_DIMS,
                            preferred_element_type=F32) + bias_ref[jnp.minimum(i, 1)]
        m = jnp.max(s, axis=-1, keepdims=True)
        p = jnp.exp2(s - m)
        l = jnp.sum(p, axis=-1, keepdims=True)
        p = p.astype(BF16)
        v = v_ref[0, res, pl.ds(span0, 2 * QBLK), :]
        p_wide = jnp.concatenate([p[h * QBLK:(h + 1) * QBLK] for h in heads], axis=1)
        v_tall = jnp.concatenate([v * hmask_ref[h] for h in heads], axis=0)
        pv = jnp.dot(p_wide, v_tall, preferred_element_type=F32)
        token0 = res + dil * row0
        dst = pl.ds(token0, QBLK) if dil == 1 else pl.ds(token0, QBLK, stride=dil)
        for s_, (m_s, l_s) in enumerate(zip(per_slab(m), per_slab(l))):
            o_ref[0, s_, dst, :] = pv[:, s_ * LANES:(s_ + 1) * LANES] / l_s
            lse_ref[0, s_, dst, :] = m_s + jnp.log2(l_s)
        return carry

    lax.fori_loop(0, dil * n_blocks, body, 0, unroll=unroll)


def _prompt_buckets(g):
    win, dil = ATTN_GROUPS[g]
    delta = np.arange(QBLK)[:, None] + QBLK - np.arange(2 * QBLK)[None, :]
    valid = (delta >= 0) & (delta <= win // dil)
    return np.where(valid, _t5_buckets(dil * np.clip(delta, 0, win // dil)), -1).astype(np.int32)


def _prompt_attention(q, k, v, table, g, to_narrow=()):
    _, dil = ATTN_GROUPS[g]
    batch, _, sub, _ = q.shape
    n_blocks = sub // QBLK
    assert sub % QBLK == 0 and n_blocks >= 2 and (dil * n_blocks) % ATTN_UNROLL == 0
    assert all(w.shape[0] % (16 * batch) == 0 for w in to_narrow)
    cast_specs = [pl.BlockSpec((w.shape[0] // batch, w.shape[1]), lambda b: (b, 0)) for w in to_narrow]
    head_of_lane = np.arange(GROUP_WIDTH) // HEAD_DIM
    hmask = np.broadcast_to(head_of_lane[None, None, :] == np.arange(HEADS_PER_GROUP)[:, None, None],
                            (HEADS_PER_GROUP, 2 * QBLK, GROUP_WIDTH))
    in_spec = pl.BlockSpec((1, dil, sub, GROUP_WIDTH), lambda b: (b, 0, 0, 0))
    out_spec = pl.BlockSpec((1, SLABS, sub * dil, LANES), lambda b: (b, 0, 0, 0))
    out_sds = jax.ShapeDtypeStruct((batch, SLABS, sub * dil, LANES), F32)
    outs = pl.pallas_call(
        functools.partial(_attn_kernel, g=g, dil=dil, n_blocks=n_blocks, unroll=ATTN_UNROLL,
                          n_cast=len(to_narrow)),
        out_shape=[out_sds, out_sds] + [jax.ShapeDtypeStruct(w.shape, BF16) for w in to_narrow],
        grid=(batch,),
        in_specs=[in_spec, in_spec, in_spec, _resident((QBLK, 2 * QBLK)), _resident(hmask.shape), _SMEM]
        + cast_specs,
        out_specs=[out_spec, out_spec] + cast_specs,
        scratch_shapes=[pltpu.VMEM((2, HEADS_PER_GROUP * QBLK, 2 * QBLK), F32)],
        compiler_params=_params(),
        name=f"attn_g{g}",
    )(q, k, v, jnp.asarray(_prompt_buckets(g)), jnp.asarray(hmask, BF16), table, *to_narrow)
    return outs[0], outs[1], outs[2:]


def _sample_attn_block(refs):
    qkv_refs = refs[0:9]
    cache_refs = refs[9:12]
    bucket_refs, table_ref = refs[12:15], refs[15]
    o_refs, lse_refs, win_refs = refs[16:19], refs[19:22], refs[22:25]
    bias_refs = refs[25:28]

    @pl.when(pl.program_id(0) == 0)
    def _():
        for g in range(N_GROUPS):
            _build_bias(bias_refs[g], bucket_refs[g], table_ref, g)

    t_new = qkv_refs[0].shape[1]
    lane_head = lax.broadcasted_iota(jnp.int32, (t_new, GROUP_WIDTH), 1) // HEAD_DIM
    new_lane = lax.broadcasted_iota(jnp.int32, (GROUP_WIDTH, LANES), 1) >= LANES - t_new
    pad = jnp.zeros((LANES - t_new, GROUP_WIDTH), F32)

    for g in range(N_GROUPS):
        q_ref, k_ref, v_ref = qkv_refs[3 * g:3 * g + 3]
        cache_ref, win_ref, bias_ref = cache_refs[g], win_refs[g], bias_refs[g]
        past = cache_ref.shape[3]
        new_rows = [jnp.concatenate([pad, r[0]], axis=0) for r in (k_ref, v_ref)]
        for c in range(2):
            shifted = pltpu.roll(cache_ref[0, c], past - t_new, 1)
            if past > LANES:
                win_ref[0, c, :, 0:past - LANES] = shifted[:, 0:past - LANES]
            win_ref[0, c, :, past - LANES:past] = jnp.where(new_lane, new_rows[c].T, shifted[:, past - LANES:])

        qs = _stack_heads(q_ref[0], lane_head)
        s_old = jnp.dot(qs, cache_ref[0, 0].astype(BF16), preferred_element_type=F32) + bias_ref[:, 0:past]
        s_new = lax.dot_general(qs, new_rows[0].astype(BF16), NT_DIMS,
                                preferred_element_type=F32) + bias_ref[:, past:past + LANES]
        m = jnp.maximum(jnp.max(s_old, axis=-1, keepdims=True), jnp.max(s_new, axis=-1, keepdims=True))
        p_old = jnp.exp2(s_old - m)
        p_new = jnp.exp2(s_new - m)
        l = jnp.sum(p_old, axis=-1, keepdims=True) + jnp.sum(p_new, axis=-1, keepdims=True)
        pv = (lax.dot_general(p_old.astype(BF16), cache_ref[0, 1].astype(BF16), NT_DIMS,
                              preferred_element_type=F32)
              + jnp.dot(p_new.astype(BF16), new_rows[1].astype(BF16), preferred_element_type=F32)) / l
        lse = jnp.broadcast_to(m + jnp.log2(l), pv.shape)
        rows = lambda a: [a[h * t_new:(h + 1) * t_new] for h in range(HEADS_PER_GROUP)]
        for out_ref, val in ((o_refs[g], _head_select(rows(pv), lane_head)),
                             (lse_refs[g], _head_select(rows(lse), lane_head))):
            for s_ in range(SLABS):
                out_ref[0, s_] = val[:, s_ * LANES:(s_ + 1) * LANES]


def _sample_attn_kernel(*refs):
    _sample_attn_block(refs)


def _sample_buckets(g, past, t_new):
    win, dil = ATTN_GROUPS[g]
    lane = np.arange(past + LANES)[None, :]
    key_pos = np.where(lane < past, lane, lane - (LANES - t_new))
    delta = past + np.arange(t_new)[:, None] - key_pos
    valid = ((lane < past) | (lane >= past + LANES - t_new)) & (delta >= 0) & (delta % dil == 0) & (delta <= win)
    return np.where(valid, _t5_buckets(np.clip(delta, 0, win)), -1).astype(np.int32)


def _sample_attention_parts(qkv, caches, table):
    batch, t_new, _ = qkv[0].shape
    pasts = [c.shape[3] for c in caches]
    buckets = [jnp.asarray(_sample_buckets(g, pasts[g], t_new)) for g in range(N_GROUPS)]
    cache_spec = lambda p: pl.BlockSpec((1, 2, GROUP_WIDTH, p), lambda b: (b, 0, 0, 0))
    new_spec = pl.BlockSpec((1, t_new, GROUP_WIDTH), lambda b: (b, 0, 0))
    out_spec = pl.BlockSpec((1, SLABS, t_new, LANES), lambda b: (0, 0, b, 0))
    out_sds = jax.ShapeDtypeStruct((1, SLABS, batch * t_new, LANES), F32)
    return dict(
        batch=batch,
        args=[*qkv, *caches, *buckets, table],
        out_shape=[out_sds] * 6 + [jax.ShapeDtypeStruct(c.shape, F32) for c in caches],
        in_specs=[new_spec] * 9 + [cache_spec(p) for p in pasts] + [_resident(b.shape) for b in buckets]
        + [_SMEM],
        out_specs=[out_spec] * 6 + [cache_spec(p) for p in pasts],
        scratch_shapes=[pltpu.VMEM((HEADS_PER_GROUP * t_new, p + LANES), F32) for p in pasts])


def _sample_attention(parts):
    return pl.pallas_call(
        _sample_attn_kernel, grid=(parts["batch"],), out_shape=parts["out_shape"],
        in_specs=parts["in_specs"], out_specs=parts["out_specs"], scratch_shapes=parts["scratch_shapes"],
        compiler_params=_params(), name="sample_attn")(*parts["args"])


def _back_kernel(x_ref, o0_ref, o1_ref, o2_ref, l0_ref, l1_ref, l2_ref, u_ref, halo_ref, ga_ref, gb_ref,
                 wab_ref, wpg_ref, pscale_ref, wpb_ref, wout_ref, g2_ref, wup_ref, wdn_ref,
                 out_ref, ue_ref, level_ref, mixed_ref, act_ref, attn_ref, pooled_ref, merged_ref,
                 *, pos_base, blocks_per_seq):
    block = pl.program_id(0)
    tm, d_model = x_ref.shape
    n_seq = halo_ref.shape[0]
    t = tm // n_seq
    pool_width = u_ref.shape[1]
    gw = pool_width // len(POOL_WINDOWS)

    for s in range(SLABS):
        lses = [r[0, s] for r in (l0_ref, l1_ref, l2_ref)]
        top = jnp.maximum(jnp.maximum(lses[0], lses[1]), lses[2])
        num = jnp.zeros_like(top)
        den = jnp.zeros_like(top)
        for o_ref, lse in zip((o0_ref, o1_ref, o2_ref), lses):
            e = jnp.exp2(lse - top)
            num = num + e * o_ref[0, s]
            den = den + e
        attn_ref[:, s * LANES:(s + 1) * LANES] = (num / den).astype(BF16)

    start = pos_base + (block % blocks_per_seq) * t
    first, end = POOL_PAD - POOL_HALO, POOL_PAD + t
    ue_ref[:, 0:first, :] = jnp.zeros((n_seq, first, pool_width), F32)
    ue_ref[:, first:POOL_PAD, :] = jnp.where(start > 0, halo_ref[...], 0.0)
    ue_ref[:, POOL_PAD:, :] = u_ref[...].reshape(n_seq, t, pool_width)
    level_ref[:, :, 0:first, :] = jnp.zeros((2, n_seq, first, gw), F32)
    pos = start + lax.broadcasted_iota(jnp.int32, (1, t, gw), 1)
    for gi, win in enumerate(POOL_WINDOWS):
        cols = slice(gi * gw, (gi + 1) * gw)
        read = lambda lo, hi, cols=cols: ue_ref[:, lo:hi, cols]
        span = 1
        while 2 * span < win:
            buf = (span.bit_length() - 1) % 2
            level_ref[buf, :, first:, :] = read(first, end) + read(first - span, end - span)
            read = lambda lo, hi, buf=buf: level_ref[buf, :, lo:hi, :]
            span *= 2
        s = read(POOL_PAD, end) + read(POOL_PAD - span, end - span)
        cnt = jnp.minimum(pos + 1, win).astype(F32)
        d = (s / cnt - ue_ref[:, POOL_PAD:, cols]).reshape(tm, gw)
        y = jnp.dot(d.astype(BF16), wpg_ref[gi], preferred_element_type=F32)
        pooled_ref[:, cols] = (y * pscale_ref[:, cols]).astype(BF16)

    for lo in range(0, d_model, MXU_TILE):
        cols = slice(lo, lo + MXU_TILE)
        branch_a = jnp.dot(attn_ref[...], wab_ref[:, cols], preferred_element_type=F32)
        branch_b = jnp.dot(pooled_ref[...], wpb_ref[:, cols], preferred_element_type=F32)
        merged_ref[:, cols] = (ga_ref[:, cols].astype(F32) * branch_a
                               + gb_ref[:, cols].astype(F32) * branch_b).astype(BF16)
    for lo in range(0, d_model, MXU_TILE):
        cols = slice(lo, lo + MXU_TILE)
        mixed_ref[:, cols] = x_ref[:, cols] + jnp.dot(merged_ref[...], wout_ref[:, cols],
                                                      preferred_element_type=F32)

    _ffn_block(mixed_ref, g2_ref, wup_ref, wdn_ref, out_ref, act_ref)


def _back(x, o, lse, u, halo, ga, gb, merge_w, ffn_w, *, tm, halo_block, halo_index, pos_base, blocks_per_seq):
    m, d = x.shape
    pool_width = u.shape[1]
    n_seq = halo_block[0]
    d_ff = ffn_w[2].shape[0]
    row = lambda wd: pl.BlockSpec((tm, wd), lambda i: (i, 0))
    slab = pl.BlockSpec((1, SLABS, tm, LANES), lambda i: (i // blocks_per_seq, 0, i % blocks_per_seq, 0))
    return pl.pallas_call(
        functools.partial(_back_kernel, pos_base=pos_base, blocks_per_seq=blocks_per_seq),
        out_shape=jax.ShapeDtypeStruct((m, d), F32),
        grid=(m // tm,),
        in_specs=[row(d)] + [slab] * 6 + [row(pool_width), pl.BlockSpec(halo_block, halo_index), row(d), row(d)]
        + [_resident(a.shape) for a in (*merge_w, *ffn_w)],
        out_specs=row(d),
        scratch_shapes=[pltpu.VMEM((n_seq, POOL_PAD + tm // n_seq, pool_width), F32),
                        pltpu.VMEM((2, n_seq, POOL_PAD + tm // n_seq, pool_width // len(POOL_WINDOWS)), F32),
                        pltpu.VMEM((tm, d), F32), pltpu.VMEM((tm, d_ff), BF16),
                        pltpu.VMEM((tm, GROUP_WIDTH), BF16), pltpu.VMEM((tm, pool_width), BF16),
                        pltpu.VMEM((tm, d), BF16)],
        compiler_params=_params(),
        name="back",
    )(x, *o, *lse, u, halo, ga, gb, *merge_w, *ffn_w)


def _window_in(cache):
    n_seq, rows = cache.shape[0:2]
    return jnp.transpose(cache, (0, 2, 3, 4, 1)).reshape(n_seq, 2, GROUP_WIDTH, rows)


def _window_out(kv):
    n_seq, _, _, rows = kv.shape
    return jnp.transpose(kv.reshape(n_seq, 2, HEADS_PER_GROUP, HEAD_DIM, rows), (0, 4, 1, 2, 3))


def kernel(x_prompt, x_sample, cache_kv_w128, cache_kv_w512, cache_kv_w2048, state_pool, rel_bias_table,
           norm_ffn1, ffn1_w_up, ffn1_w_down, norm_mix, w_in, q_norm, k_norm, pool_w_group, pool_scale,
           w_attn_branch, w_pool_branch, w_out, norm_ffn2, ffn2_w_up, ffn2_w_down):
    batch, seq, d_model = x_prompt.shape
    dec_batch, dec_seq, _ = x_sample.shape
    depth = norm_ffn1.shape[0]
    pool_width = state_pool.shape[-1]
    tm = 512
    ms = dec_batch * dec_seq
    seg = jnp.asarray(np.kron(np.eye(HEADS_PER_GROUP), np.full((HEAD_DIM, HEAD_DIM), 1.0 / HEAD_DIM)), BF16)
    table = rel_bias_table.astype(F32)
    caches_in = (cache_kv_w128, cache_kv_w512, cache_kv_w2048)
    keep_prompt = tuple(min(win, seq) for win, _ in ATTN_GROUPS)
    dils = tuple(dil for _, dil in ATTN_GROUPS)

    xp = x_prompt.reshape(batch * seq, d_model)
    xs = x_sample.reshape(ms, d_model)
    kv_p, kv_s, pool_p, pool_s = ([], [], []), ([], [], []), [], []
    for layer in range(depth):
        gain = lambda a: a[layer].reshape(1, -1).astype(F32)
        wup1, wdn1 = ffn1_w_up[layer].astype(BF16), ffn1_w_down[layer].astype(BF16)
        win = w_in[layer].astype(BF16)
        qg, kg = gain(q_norm), gain(k_norm)
        pool_groups = pool_w_group[layer]
        late_w = (ffn2_w_down[layer], w_attn_branch[layer], pool_groups.reshape(-1, pool_groups.shape[-1]),
                  w_pool_branch[layer], w_out[layer])

        xs, _ = _ffn(xs, gain(norm_ffn1), wup1, wdn1, ms)
        outs = _proj(xs, gain(norm_mix), win, qg, kg, seg, tm=ms, n_seq=1, dils=(1,) * N_GROUPS,
                     qkv_dtype=F32, keep_rows=None)
        qkv_s, (u_s, ga_s, gb_s) = outs[0:9], outs[9:12]
        sample_parts = _sample_attention_parts(
            [a.reshape(dec_batch, dec_seq, GROUP_WIDTH) for a in qkv_s],
            [_window_in(c[layer]) for c in caches_in], table)

        if (batch * seq) // tm == dec_batch:
            xp, sample_outs = _ffn(xp, gain(norm_ffn1), wup1, wdn1, tm, rider=sample_parts)
        else:
            xp, _ = _ffn(xp, gain(norm_ffn1), wup1, wdn1, tm)
            sample_outs = _sample_attention(sample_parts)
        outs = _proj(xp, gain(norm_mix), win, qg, kg, seg, tm=2 * tm, n_seq=batch, dils=dils,
                     qkv_dtype=BF16, keep_rows=keep_prompt)
        qkv, (u, ga, gb), windows, pstate = outs[0:9], outs[9:12], outs[12:15], outs[15]
        narrowing = ((ffn2_w_up[layer],), late_w) + ((),) * (N_GROUPS - 2)
        o, lse, narrowed = zip(*[_prompt_attention(*qkv[3 * g:3 * g + 3], table, g, to_narrow=narrowing[g])
                                 for g in range(N_GROUPS)])
        (wup2,), (wdn2, wab, wpg, wpb, wout) = narrowed[0:2]
        ffn2_w = (gain(norm_ffn2), wup2, wdn2)
        merge_w = (wab, wpg.reshape(pool_groups.shape), gain(pool_scale), wpb, wout)
        xp = _back(xp, o, lse, u, u.reshape(-1, POOL_HALO, pool_width), ga, gb, merge_w, ffn2_w, tm=tm,
                   halo_block=(1, POOL_HALO, pool_width),
                   halo_index=lambda blk: (jnp.maximum(blk * (tm // POOL_HALO) - 1, 0), 0, 0),
                   pos_base=0, blocks_per_seq=seq // tm)
        for g in range(N_GROUPS):
            kv_p[g].append(_window_out(windows[g]))
        pool_p.append(pstate[:, POOL_HALO - POOL_STATE:])

        o, lse, windows = sample_outs[0:3], sample_outs[3:6], sample_outs[6:9]
        history = jnp.concatenate(
            [jnp.zeros((dec_batch, POOL_HALO - POOL_STATE, pool_width), F32), state_pool[layer]], axis=1)
        xs = _back(xs, o, lse, u_s, history, ga_s, gb_s, merge_w, ffn2_w, tm=ms,
                   halo_block=(dec_batch, POOL_HALO, pool_width), halo_index=lambda blk: (0, 0, 0),
                   pos_base=PAST_LEN, blocks_per_seq=1)
        for g in range(N_GROUPS):
            kv_s[g].append(_window_out(windows[g]))
        ue = jnp.concatenate([state_pool[layer], u_s.reshape(dec_batch, dec_seq, pool_width)], axis=1)
        pool_s.append(ue[:, ue.shape[1] - POOL_STATE:])

    stack = lambda xs_: jnp.stack(xs_, axis=0)
    return (xp.reshape(batch, seq, d_model), xs.reshape(dec_batch, dec_seq, d_model),
            stack(kv_p[0]), stack(kv_p[1]), stack(kv_p[2]), stack(pool_p),
            stack(kv_s[0]), stack(kv_s[1]), stack(kv_s[2]), stack(pool_s))
```

```python
import functools
import math

import numpy as np
import jax
import jax.numpy as jnp
from jax import lax
from jax.experimental import pallas as pl
from jax.experimental.pallas import tpu as pltpu

HEAD_DIM = 64
HEADS_PER_GROUP = 4
GROUP_WIDTH = HEADS_PER_GROUP * HEAD_DIM
ATTN_GROUPS = ((128, 1), (512, 4), (2048, 16))
N_GROUPS = len(ATTN_GROUPS)
ATTN_WIDTH = N_GROUPS * GROUP_WIDTH
N_BUCKETS = 32
MAX_DISTANCE = 2048
POOL_WINDOWS = (2, 4, 8, 16)
POOL_STATE = max(POOL_WINDOWS) - 1
POOL_HALO = 16
POOL_PAD = POOL_HALO + 8
EPS = 1e-6
PAST_LEN = 8192
NEG = -1e30
LOG2E = math.log2(math.e)
LANES = 128
MXU_TILE = 256
SLABS = GROUP_WIDTH // LANES
SINGLE_OP_STRIDE = 4
QBLK = 128
ATTN_UNROLL = 16
VMEM_LIMIT = 58 * 1024 * 1024

F32 = jnp.float32
BF16 = jnp.bfloat16
NT_DIMS = (((1,), (1,)), ((), ()))


def _t5_buckets(distance):
    max_exact = N_BUCKETS // 2
    d = np.asarray(distance, dtype=np.int32)
    ratio = np.log(np.maximum(d, 1).astype(np.float32) / np.float32(max_exact))
    large = max_exact + (ratio / np.float32(math.log(MAX_DISTANCE / max_exact))
                         * (N_BUCKETS - max_exact)).astype(np.int32)
    large = np.minimum(large, N_BUCKETS - 1)
    return np.where(d < max_exact, d, large).astype(np.int32)


def _params(n_grid_dims=1):
    return pltpu.CompilerParams(dimension_semantics=("arbitrary",) * n_grid_dims,
                                vmem_limit_bytes=VMEM_LIMIT)


def _resident(shape):
    return pl.BlockSpec(shape, lambda *_: (0,) * len(shape), pipeline_mode=pl.Buffered(1))


_SMEM = pl.BlockSpec(memory_space=pltpu.SMEM)


def _rmsnorm(x, g):
    ms = jnp.mean(x * x, axis=-1, keepdims=True)
    return x * lax.rsqrt(ms + EPS) * g


def _head_select(parts, lane_head):
    out = jnp.where(lane_head == 0, parts[0], 0.0)
    for h in range(1, HEADS_PER_GROUP):
        out = jnp.where(lane_head == h, parts[h], out)
    return out


def _stack_heads(q, lane_head):
    return jnp.concatenate([jnp.where(lane_head == h, q, 0.0)
                            for h in range(HEADS_PER_GROUP)], axis=0).astype(BF16)


def _build_bias(bias_ref, bucket_ref, table_ref, g):
    buckets = bucket_ref[...]
    rows = buckets.shape[0]
    for h in range(HEADS_PER_GROUP):
        acc = jnp.full(buckets.shape, NEG, F32)
        for b in range(N_BUCKETS):
            acc = jnp.where(buckets == b, table_ref[b, g * HEADS_PER_GROUP + h] * LOG2E, acc)
        bias_ref[h * rows:(h + 1) * rows, :] = acc


def _ffn_block(x_ref, g_ref, wup_ref, wdn_ref, o_ref, act_ref):
    d_ff, d = wdn_ref.shape
    h = _rmsnorm(x_ref[...], g_ref[...]).astype(BF16)
    for lo in range(0, d_ff, MXU_TILE):
        gate = jnp.dot(h, wup_ref[:, lo:lo + MXU_TILE], preferred_element_type=F32)
        up = jnp.dot(h, wup_ref[:, d_ff + lo:d_ff + lo + MXU_TILE], preferred_element_type=F32)
        act_ref[:, lo:lo + MXU_TILE] = (gate * jax.nn.sigmoid(gate) * up).astype(BF16)
    for lo in range(0, d, MXU_TILE):
        cols = slice(lo, lo + MXU_TILE)
        o_ref[:, cols] = x_ref[:, cols] + 0.5 * jnp.dot(act_ref[...], wdn_ref[:, cols],
                                                        preferred_element_type=F32)


def _ffn_kernel(*refs, rider_counts):
    n_in, n_out = rider_counts
    x_ref, g_ref, wup_ref, wdn_ref = refs[0:4]
    o_ref = refs[4 + n_in]
    act_ref = refs[5 + n_in + n_out]
    if n_in:
        _sample_attn_block((*refs[4:4 + n_in], *refs[5 + n_in:5 + n_in + n_out], *refs[6 + n_in + n_out:]))
    _ffn_block(x_ref, g_ref, wup_ref, wdn_ref, o_ref, act_ref)


def _ffn(x, gain, w_up, w_down, tm, rider=None):
    m, d = x.shape
    d_ff = w_down.shape[0]
    assert m % tm == 0 and d_ff % MXU_TILE == 0 and d % MXU_TILE == 0
    rider = rider or dict(args=[], in_specs=[], out_shape=[], out_specs=[], scratch_shapes=[], batch=m // tm)
    assert rider["batch"] == m // tm
    outs = pl.pallas_call(
        functools.partial(_ffn_kernel, rider_counts=(len(rider["args"]), len(rider["out_shape"]))),
        out_shape=[jax.ShapeDtypeStruct((m, d), F32)] + rider["out_shape"],
        grid=(m // tm,),
        in_specs=[pl.BlockSpec((tm, d), lambda i: (i, 0)),
                  _resident((1, d)), _resident((d, 2 * d_ff)), _resident((d_ff, d))] + rider["in_specs"],
        out_specs=[pl.BlockSpec((tm, d), lambda i: (i, 0))] + rider["out_specs"],
        scratch_shapes=[pltpu.VMEM((tm, d_ff), BF16)] + rider["scratch_shapes"],
        compiler_params=_params(),
        name="ffn",
    )(x, gain, w_up, w_down, *rider["args"])
    return outs[0], outs[1:]


def _proj_kernel(x_ref, g_ref, win_ref, qn_ref, kn_ref, seg_ref, *refs,
                 dils, kv_rows, with_state, pool_width, d_model):
    qkv_refs = refs[0:3 * N_GROUPS]
    u_ref, ga_ref, gb_ref = refs[3 * N_GROUPS:3 * N_GROUPS + 3]
    state_refs = refs[3 * N_GROUPS + 3:-2]
    slab_ref, regroup_ref = refs[-2:]
    tm = x_ref.shape[0]
    h = _rmsnorm(x_ref[...], g_ref[...]).astype(BF16)

    def proj(lo, width):
        return jnp.dot(h, win_ref[:, lo:lo + width], preferred_element_type=F32)

    def head_norm(y, gain):
        ms = jnp.dot((y * y).astype(BF16), seg_ref[...], preferred_element_type=F32)
        return y * lax.rsqrt(ms + EPS) * gain

    def emit(dst_ref, slot, val, dil):
        if dil == 1:
            dst_ref[0, 0] = val.astype(dst_ref.dtype)
            return
        for s in range(SLABS):
            slab_ref[slot, s] = val[:, s * LANES:(s + 1) * LANES]
        n = tm // dil
        if dil <= SINGLE_OP_STRIDE:
            for r in range(dil):
                parts = [slab_ref[slot, s, pl.ds(r, n, stride=dil), :] for s in range(SLABS)]
                dst_ref[0, r] = jnp.concatenate(parts, axis=-1).astype(dst_ref.dtype)
            return
        outer = SINGLE_OP_STRIDE
        inner = dil // outer
        assert inner <= SINGLE_OP_STRIDE
        quarter = tm // outer
        for a in range(outer):
            for s in range(SLABS):
                regroup_ref[s, a * quarter:(a + 1) * quarter, :] = slab_ref[slot, s, pl.ds(a, quarter, stride=outer), :]
        for r in range(dil):
            start = (r % outer) * quarter + r // outer
            parts = [regroup_ref[s, pl.ds(start, n, stride=inner), :] for s in range(SLABS)]
            dst_ref[0, r] = jnp.concatenate(parts, axis=-1).astype(dst_ref.dtype)

    def finish(g, which, y):
        cols = slice(g * GROUP_WIDTH, (g + 1) * GROUP_WIDTH)
        if which == 0:
            y = head_norm(y, qn_ref[:, cols]) * (HEAD_DIM ** -0.5 * LOG2E)
        elif which == 1:
            y = head_norm(y, kn_ref[:, cols])
        emit(qkv_refs[3 * g + which], 3 * sum(d > 1 for d in dils[:g]) + which, y, dils[g])
        if with_state and which > 0:
            state_refs[g][0, which - 1] = y.T[:, tm - kv_rows[g]:]

    pending = None
    for g in range(N_GROUPS):
        for which in range(3):
            y = proj(which * ATTN_WIDTH + g * GROUP_WIDTH, GROUP_WIDTH)
            if pending is not None:
                finish(*pending)
            pending = (g, which, y)
    u = proj(3 * ATTN_WIDTH, pool_width)
    finish(*pending)
    u_ref[...] = u
    if with_state:
        state_refs[N_GROUPS][0] = u[tm - POOL_HALO:, :]
    for gate_ref, base in ((ga_ref, 3 * ATTN_WIDTH + pool_width), (gb_ref, 3 * ATTN_WIDTH + pool_width + d_model)):
        for lo in range(0, d_model, MXU_TILE):
            gate_ref[:, lo:lo + MXU_TILE] = jax.nn.sigmoid(proj(base + lo, MXU_TILE)).astype(BF16)


def _proj(x, gain, w_in, q_gain, k_gain, seg, *, tm, n_seq, dils, qkv_dtype, keep_rows):
    m, d = x.shape
    pool_width = w_in.shape[1] - 3 * ATTN_WIDTH - 2 * d
    seq_len = m // n_seq
    assert seq_len % tm == 0 and all(tm % dil == 0 for dil in dils)
    blocks_per_seq = seq_len // tm
    with_state = keep_rows is not None
    seq_block = lambda i: (i // blocks_per_seq, i % blocks_per_seq)

    row = lambda w: pl.BlockSpec((tm, w), lambda i: (i, 0))
    out_shape, out_specs = [], []
    for g in range(N_GROUPS):
        sds = jax.ShapeDtypeStruct((n_seq, dils[g], seq_len // dils[g], GROUP_WIDTH), qkv_dtype)
        spec = pl.BlockSpec((1, dils[g], tm // dils[g], GROUP_WIDTH),
                            lambda i: (seq_block(i)[0], 0, seq_block(i)[1], 0))
        out_shape += [sds] * 3
        out_specs += [spec] * 3
    out_shape += [jax.ShapeDtypeStruct((m, pool_width), F32),
                  jax.ShapeDtypeStruct((m, d), BF16), jax.ShapeDtypeStruct((m, d), BF16)]
    out_specs += [row(pool_width), row(d), row(d)]
    kv_rows = kv_first = ()
    if with_state:
        kv_rows = tuple(min(r, tm) for r in keep_rows)
        kv_first = tuple(blocks_per_seq - keep_rows[g] // kv_rows[g] for g in range(N_GROUPS))
        for g in range(N_GROUPS):
            out_shape.append(jax.ShapeDtypeStruct((n_seq, 2, GROUP_WIDTH, keep_rows[g]), F32))
            out_specs.append(pl.BlockSpec(
                (1, 2, GROUP_WIDTH, kv_rows[g]),
                lambda i, first=kv_first[g]: (seq_block(i)[0], 0, 0, jnp.maximum(seq_block(i)[1] - first, 0))))
        out_shape.append(jax.ShapeDtypeStruct((n_seq, POOL_HALO, pool_width), F32))
        out_specs.append(pl.BlockSpec((1, POOL_HALO, pool_width), lambda i: (seq_block(i)[0], 0, 0)))
    return pl.pallas_call(
        functools.partial(_proj_kernel, dils=dils, kv_rows=kv_rows, with_state=with_state,
                          pool_width=pool_width, d_model=d),
        out_shape=out_shape,
        grid=(m // tm,),
        in_specs=[row(d), _resident((1, d)), _resident(w_in.shape),
                  _resident((1, ATTN_WIDTH)), _resident((1, ATTN_WIDTH)),
                  _resident((GROUP_WIDTH, GROUP_WIDTH))],
        out_specs=out_specs,
        scratch_shapes=[pltpu.VMEM((max(1, 3 * sum(dil > 1 for dil in dils)), SLABS, tm, LANES), F32),
                        pltpu.VMEM((SLABS, tm, LANES), F32)],
        compiler_params=_params(),
        name="proj",
    )(x, gain, w_in, q_gain, k_gain, seg)


def _attn_kernel(*refs, g, dil, n_blocks, unroll, n_cast):
    q_ref, k_ref, v_ref, bucket_ref, hmask_ref, table_ref = refs[0:6]
    wide_refs = refs[6:6 + n_cast]
    o_ref, lse_ref = refs[6 + n_cast:8 + n_cast]
    narrow_refs = refs[8 + n_cast:8 + 2 * n_cast]
    bias_ref = refs[8 + 2 * n_cast]
    for wide_ref, narrow_ref in zip(wide_refs, narrow_refs):
        narrow_ref[...] = wide_ref[...].astype(BF16)

    @pl.when(pl.program_id(0) == 0)
    def _():
        _build_bias(bias_ref.at[1], bucket_ref, table_ref, g)
        bias_ref[0, :, 0:QBLK] = bias_ref[1, :, QBLK:2 * QBLK]
        bias_ref[0, :, QBLK:2 * QBLK] = jnp.full((HEADS_PER_GROUP * QBLK, QBLK), NEG, F32)

    first_head = lax.broadcasted_iota(jnp.int32, (QBLK, LANES), 1) < HEAD_DIM
    heads = range(HEADS_PER_GROUP)

    def per_slab(col):
        part = lambda h: jnp.broadcast_to(col[h * QBLK:(h + 1) * QBLK], (QBLK, LANES))
        return [jnp.where(first_head, part(2 * s), part(2 * s + 1)) for s in range(SLABS)]

    def body(n, carry):
        res = n // n_blocks
        i = n % n_blocks
        row0 = pl.multiple_of(i * QBLK, QBLK)
        span0 = pl.multiple_of(jnp.maximum(i - 1, 0) * QBLK, QBLK)
        q = q_ref[0, res, pl.ds(row0, QBLK), :]
        qs = jnp.concatenate([q * hmask_ref[h] for h in heads], axis=0)
        s = lax.dot_general(qs, k_ref[0, res, pl.ds(span0, 2 * QBLK), :], NT_DIMS,
                            preferred_element_type=F32) + bias_ref[jnp.minimum(i, 1)]
        m = jnp.max(s, axis=-1, keepdims=True)
        p = jnp.exp2(s - m)
        l = jnp.sum(p, axis=-1, keepdims=True)
        p = p.astype(BF16)
        v = v_ref[0, res, pl.ds(span0, 2 * QBLK), :]
        p_wide = jnp.concatenate([p[h * QBLK:(h + 1) * QBLK] for h in heads], axis=1)
        v_tall = jnp.concatenate([v * hmask_ref[h] for h in heads], axis=0)
        pv = jnp.dot(p_wide, v_tall, preferred_element_type=F32)
        token0 = res + dil * row0
        dst = pl.ds(token0, QBLK) if dil == 1 else pl.ds(token0, QBLK, stride=dil)
        for s_, (m_s, l_s) in enumerate(zip(per_slab(m), per_slab(l))):
            o_ref[0, s_, dst, :] = pv[:, s_ * LANES:(s_ + 1) * LANES] / l_s
            lse_ref[0, s_, dst, :] = m_s + jnp.log2(l_s)
        return carry

    lax.fori_loop(0, dil * n_blocks, body, 0, unroll=unroll)


def _prompt_buckets(g):
    win, dil = ATTN_GROUPS[g]
    delta = np.arange(QBLK)[:, None] + QBLK - np.arange(2 * QBLK)[None, :]
    valid = (delta >= 0) & (delta <= win // dil)
    return np.where(valid, _t5_buckets(dil * np.clip(delta, 0, win // dil)), -1).astype(np.int32)


def _prompt_attention(q, k, v, table, g, to_narrow=()):
    _, dil = ATTN_GROUPS[g]
    batch, _, sub, _ = q.shape
    n_blocks = sub // QBLK
    assert sub % QBLK == 0 and n_blocks >= 2 and (dil * n_blocks) % ATTN_UNROLL == 0
    assert all(w.shape[0] % (16 * batch) == 0 for w in to_narrow)
    cast_specs = [pl.BlockSpec((w.shape[0] // batch, w.shape[1]), lambda b: (b, 0)) for w in to_narrow]
    head_of_lane = np.arange(GROUP_WIDTH) // HEAD_DIM
    hmask = head_of_lane[None, None, :] == np.arange(HEADS_PER_GROUP)[:, None, None]
    in_spec = pl.BlockSpec((1, dil, sub, GROUP_WIDTH), lambda b: (b, 0, 0, 0))
    out_spec = pl.BlockSpec((1, SLABS, sub * dil, LANES), lambda b: (b, 0, 0, 0))
    out_sds = jax.ShapeDtypeStruct((batch, SLABS, sub * dil, LANES), F32)
    outs = pl.pallas_call(
        functools.partial(_attn_kernel, g=g, dil=dil, n_blocks=n_blocks, unroll=ATTN_UNROLL,
                          n_cast=len(to_narrow)),
        out_shape=[out_sds, out_sds] + [jax.ShapeDtypeStruct(w.shape, BF16) for w in to_narrow],
        grid=(batch,),
        in_specs=[in_spec, in_spec, in_spec, _resident((QBLK, 2 * QBLK)), _resident(hmask.shape), _SMEM]
        + cast_specs,
        out_specs=[out_spec, out_spec] + cast_specs,
        scratch_shapes=[pltpu.VMEM((2, HEADS_PER_GROUP * QBLK, 2 * QBLK), F32)],
        compiler_params=_params(),
        name=f"attn_g{g}",
    )(q, k, v, jnp.asarray(_prompt_buckets(g)), jnp.asarray(hmask, BF16), table, *to_narrow)
    return outs[0], outs[1], outs[2:]


def _sample_attn_block(refs):
    qkv_refs = refs[0:9]
    cache_refs = refs[9:12]
    bucket_refs, table_ref = refs[12:15], refs[15]
    o_refs, lse_refs, win_refs = refs[16:19], refs[19:22], refs[22:25]
    bias_refs = refs[25:28]

    @pl.when(pl.program_id(0) == 0)
    def _():
        for g in range(N_GROUPS):
            _build_bias(bias_refs[g], bucket_refs[g], table_ref, g)

    t_new = qkv_refs[0].shape[1]
    lane_head = lax.broadcasted_iota(jnp.int32, (t_new, GROUP_WIDTH), 1) // HEAD_DIM
    new_lane = lax.broadcasted_iota(jnp.int32, (GROUP_WIDTH, LANES), 1) >= LANES - t_new
    pad = jnp.zeros((LANES - t_new, GROUP_WIDTH), F32)

    for g in range(N_GROUPS):
        q_ref, k_ref, v_ref = qkv_refs[3 * g:3 * g + 3]
        cache_ref, win_ref, bias_ref = cache_refs[g], win_refs[g], bias_refs[g]
        past = cache_ref.shape[3]
        new_rows = [jnp.concatenate([pad, r[0]], axis=0) for r in (k_ref, v_ref)]
        for c in range(2):
            shifted = pltpu.roll(cache_ref[0, c], past - t_new, 1)
            if past > LANES:
                win_ref[0, c, :, 0:past - LANES] = shifted[:, 0:past - LANES]
            win_ref[0, c, :, past - LANES:past] = jnp.where(new_lane, new_rows[c].T, shifted[:, past - LANES:])

        qs = _stack_heads(q_ref[0], lane_head)
        s_old = jnp.dot(qs, cache_ref[0, 0].astype(BF16), preferred_element_type=F32) + bias_ref[:, 0:past]
        s_new = lax.dot_general(qs, new_rows[0].astype(BF16), NT_DIMS,
                                preferred_element_type=F32) + bias_ref[:, past:past + LANES]
        m = jnp.maximum(jnp.max(s_old, axis=-1, keepdims=True), jnp.max(s_new, axis=-1, keepdims=True))
        p_old = jnp.exp2(s_old - m)
        p_new = jnp.exp2(s_new - m)
        l = jnp.sum(p_old, axis=-1, keepdims=True) + jnp.sum(p_new, axis=-1, keepdims=True)
        pv = (lax.dot_general(p_old.astype(BF16), cache_ref[0, 1].astype(BF16), NT_DIMS,
                              preferred_element_type=F32)
              + jnp.dot(p_new.astype(BF16), new_rows[1].astype(BF16), preferred_element_type=F32)) / l
        lse = jnp.broadcast_to(m + jnp.log2(l), pv.shape)
        rows = lambda a: [a[h * t_new:(h + 1) * t_new] for h in range(HEADS_PER_GROUP)]
        for out_ref, val in ((o_refs[g], _head_select(rows(pv), lane_head)),
                             (lse_refs[g], _head_select(rows(lse), lane_head))):
            for s_ in range(SLABS):
                out_ref[0, s_] = val[:, s_ * LANES:(s_ + 1) * LANES]


def _sample_attn_kernel(*refs):
    _sample_attn_block(refs)


def _sample_buckets(g, past, t_new):
    win, dil = ATTN_GROUPS[g]
    lane = np.arange(past + LANES)[None, :]
    key_pos = np.where(lane < past, lane, lane - (LANES - t_new))
    delta = past + np.arange(t_new)[:, None] - key_pos
    valid = ((lane < past) | (lane >= past + LANES - t_new)) & (delta >= 0) & (delta % dil == 0) & (delta <= win)
    return np.where(valid, _t5_buckets(np.clip(delta, 0, win)), -1).astype(np.int32)


def _sample_attention_parts(qkv, caches, table):
    batch, t_new, _ = qkv[0].shape
    pasts = [c.shape[3] for c in caches]
    buckets = [jnp.asarray(_sample_buckets(g, pasts[g], t_new)) for g in range(N_GROUPS)]
    cache_spec = lambda p: pl.BlockSpec((1, 2, GROUP_WIDTH, p), lambda b: (b, 0, 0, 0))
    new_spec = pl.BlockSpec((1, t_new, GROUP_WIDTH), lambda b: (b, 0, 0))
    out_spec = pl.BlockSpec((1, SLABS, t_new, LANES), lambda b: (0, 0, b, 0))
    out_sds = jax.ShapeDtypeStruct((1, SLABS, batch * t_new, LANES), F32)
    return dict(
        batch=batch,
        args=[*qkv, *caches, *buckets, table],
        out_shape=[out_sds] * 6 + [jax.ShapeDtypeStruct(c.shape, F32) for c in caches],
        in_specs=[new_spec] * 9 + [cache_spec(p) for p in pasts] + [_resident(b.shape) for b in buckets]
        + [_SMEM],
        out_specs=[out_spec] * 6 + [cache_spec(p) for p in pasts],
        scratch_shapes=[pltpu.VMEM((HEADS_PER_GROUP * t_new, p + LANES), F32) for p in pasts])


def _sample_attention(parts):
    return pl.pallas_call(
        _sample_attn_kernel, grid=(parts["batch"],), out_shape=parts["out_shape"],
        in_specs=parts["in_specs"], out_specs=parts["out_specs"], scratch_shapes=parts["scratch_shapes"],
        compiler_params=_params(), name="sample_attn")(*parts["args"])


def _back_kernel(x_ref, o0_ref, o1_ref, o2_ref, l0_ref, l1_ref, l2_ref, u_ref, halo_ref, ga_ref, gb_ref,
                 wab_ref, wpg_ref, pscale_ref, wpb_ref, wout_ref, g2_ref, wup_ref, wdn_ref,
                 out_ref, ue_ref, level_ref, mixed_ref, act_ref, attn_ref, pooled_ref, merged_ref,
                 *, pos_base, blocks_per_seq):
    block = pl.program_id(0)
    tm, d_model = x_ref.shape
    n_seq = halo_ref.shape[0]
    t = tm // n_seq
    pool_width = u_ref.shape[1]
    gw = pool_width // len(POOL_WINDOWS)

    for s in range(SLABS):
        lses = [r[0, s] for r in (l0_ref, l1_ref, l2_ref)]
        top = jnp.maximum(jnp.maximum(lses[0], lses[1]), lses[2])
        num = jnp.zeros_like(top)
        den = jnp.zeros_like(top)
        for o_ref, lse in zip((o0_ref, o1_ref, o2_ref), lses):
            e = jnp.exp2(lse - top)
            num = num + e * o_ref[0, s]
            den = den + e
        attn_ref[:, s * LANES:(s + 1) * LANES] = (num / den).astype(BF16)

    start = pos_base + (block % blocks_per_seq) * t
    first, end = POOL_PAD - POOL_HALO, POOL_PAD + t
    ue_ref[:, 0:first, :] = jnp.zeros((n_seq, first, pool_width), F32)
    ue_ref[:, first:POOL_PAD, :] = jnp.where(start > 0, halo_ref[...], 0.0)
    ue_ref[:, POOL_PAD:, :] = u_ref[...].reshape(n_seq, t, pool_width)
    level_ref[:, :, 0:first, :] = jnp.zeros((2, n_seq, first, gw), F32)
    pos = start + lax.broadcasted_iota(jnp.int32, (1, t, gw), 1)
    for gi, win in enumerate(POOL_WINDOWS):
        cols = slice(gi * gw, (gi + 1) * gw)
        read = lambda lo, hi, cols=cols: ue_ref[:, lo:hi, cols]
        span = 1
        while 2 * span < win:
            buf = (span.bit_length() - 1) % 2
            level_ref[buf, :, first:, :] = read(first, end) + read(first - span, end - span)
            read = lambda lo, hi, buf=buf: level_ref[buf, :, lo:hi, :]
            span *= 2
        s = read(POOL_PAD, end) + read(POOL_PAD - span, end - span)
        cnt = jnp.minimum(pos + 1, win).astype(F32)
        d = (s / cnt - ue_ref[:, POOL_PAD:, cols]).reshape(tm, gw)
        y = jnp.dot(d.astype(BF16), wpg_ref[gi], preferred_element_type=F32)
        pooled_ref[:, cols] = (y * pscale_ref[:, cols]).astype(BF16)

    for lo in range(0, d_model, MXU_TILE):
        cols = slice(lo, lo + MXU_TILE)
        branch_a = jnp.dot(attn_ref[...], wab_ref[:, cols], preferred_element_type=F32)
        branch_b = jnp.dot(pooled_ref[...], wpb_ref[:, cols], preferred_element_type=F32)
        merged_ref[:, cols] = (ga_ref[:, cols].astype(F32) * branch_a
                               + gb_ref[:, cols].astype(F32) * branch_b).astype(BF16)
    for lo in range(0, d_model, MXU_TILE):
        cols = slice(lo, lo + MXU_TILE)
        mixed_ref[:, cols] = x_ref[:, cols] + jnp.dot(merged_ref[...], wout_ref[:, cols],
                                                      preferred_element_type=F32)

    _ffn_block(mixed_ref, g2_ref, wup_ref, wdn_ref, out_ref, act_ref)


def _back(x, o, lse, u, halo, ga, gb, merge_w, ffn_w, *, tm, halo_block, halo_index, pos_base, blocks_per_seq):
    m, d = x.shape
    pool_width = u.shape[1]
    n_seq = halo_block[0]
    d_ff = ffn_w[2].shape[0]
    row = lambda wd: pl.BlockSpec((tm, wd), lambda i: (i, 0))
    slab = pl.BlockSpec((1, SLABS, tm, LANES), lambda i: (i // blocks_per_seq, 0, i % blocks_per_seq, 0))
    return pl.pallas_call(
        functools.partial(_back_kernel, pos_base=pos_base, blocks_per_seq=blocks_per_seq),
        out_shape=jax.ShapeDtypeStruct((m, d), F32),
        grid=(m // tm,),
        in_specs=[row(d)] + [slab] * 6 + [row(pool_width), pl.BlockSpec(halo_block, halo_index), row(d), row(d)]
        + [_resident(a.shape) for a in (*merge_w, *ffn_w)],
        out_specs=row(d),
        scratch_shapes=[pltpu.VMEM((n_seq, POOL_PAD + tm // n_seq, pool_width), F32),
                        pltpu.VMEM((2, n_seq, POOL_PAD + tm // n_seq, pool_width // len(POOL_WINDOWS)), F32),
                        pltpu.VMEM((tm, d), F32), pltpu.VMEM((tm, d_ff), BF16),
                        pltpu.VMEM((tm, GROUP_WIDTH), BF16), pltpu.VMEM((tm, pool_width), BF16),
                        pltpu.VMEM((tm, d), BF16)],
        compiler_params=_params(),
        name="back",
    )(x, *o, *lse, u, halo, ga, gb, *merge_w, *ffn_w)


def _window_in(cache):
    n_seq, rows = cache.shape[0:2]
    return jnp.transpose(cache, (0, 2, 3, 4, 1)).reshape(n_seq, 2, GROUP_WIDTH, rows)


def _window_out(kv):
    n_seq, _, _, rows = kv.shape
    return jnp.transpose(kv.reshape(n_seq, 2, HEADS_PER_GROUP, HEAD_DIM, rows), (0, 4, 1, 2, 3))


def kernel(x_prompt, x_sample, cache_kv_w128, cache_kv_w512, cache_kv_w2048, state_pool, rel_bias_table,
           norm_ffn1, ffn1_w_up, ffn1_w_down, norm_mix, w_in, q_norm, k_norm, pool_w_group, pool_scale,
           w_attn_branch, w_pool_branch, w_out, norm_ffn2, ffn2_w_up, ffn2_w_down):
    batch, seq, d_model = x_prompt.shape
    dec_batch, dec_seq, _ = x_sample.shape
    depth = norm_ffn1.shape[0]
    pool_width = state_pool.shape[-1]
    tm = 512
    ms = dec_batch * dec_seq
    seg = jnp.asarray(np.kron(np.eye(HEADS_PER_GROUP), np.full((HEAD_DIM, HEAD_DIM), 1.0 / HEAD_DIM)), BF16)
    table = rel_bias_table.astype(F32)
    caches_in = (cache_kv_w128, cache_kv_w512, cache_kv_w2048)
    keep_prompt = tuple(min(win, seq) for win, _ in ATTN_GROUPS)
    dils = tuple(dil for _, dil in ATTN_GROUPS)

    xp = x_prompt.reshape(batch * seq, d_model)
    xs = x_sample.reshape(ms, d_model)
    kv_p, kv_s, pool_p, pool_s = ([], [], []), ([], [], []), [], []
    for layer in range(depth):
        gain = lambda a: a[layer].reshape(1, -1).astype(F32)
        wup1, wdn1 = ffn1_w_up[layer].astype(BF16), ffn1_w_down[layer].astype(BF16)
        win = w_in[layer].astype(BF16)
        qg, kg = gain(q_norm), gain(k_norm)
        pool_groups = pool_w_group[layer]
        late_w = (ffn2_w_down[layer], w_attn_branch[layer], pool_groups.reshape(-1, pool_groups.shape[-1]),
                  w_pool_branch[layer], w_out[layer])

        xs, _ = _ffn(xs, gain(norm_ffn1), wup1, wdn1, ms)
        outs = _proj(xs, gain(norm_mix), win, qg, kg, seg, tm=ms, n_seq=1, dils=(1,) * N_GROUPS,
                     qkv_dtype=F32, keep_rows=None)
        qkv_s, (u_s, ga_s, gb_s) = outs[0:9], outs[9:12]
        sample_parts = _sample_attention_parts(
            [a.reshape(dec_batch, dec_seq, GROUP_WIDTH) for a in qkv_s],
            [_window_in(c[layer]) for c in caches_in], table)

        if (batch * seq) // tm == dec_batch:
            xp, sample_outs = _ffn(xp, gain(norm_ffn1), wup1, wdn1, tm, rider=sample_parts)
        else:
            xp, _ = _ffn(xp, gain(norm_ffn1), wup1, wdn1, tm)
            sample_outs = _sample_attention(sample_parts)
        outs = _proj(xp, gain(norm_mix), win, qg, kg, seg, tm=2 * tm, n_seq=batch, dils=dils,
                     qkv_dtype=BF16, keep_rows=keep_prompt)
        qkv, (u, ga, gb), windows, pstate = outs[0:9], outs[9:12], outs[12:15], outs[15]
        narrowing = ((ffn2_w_up[layer],), late_w) + ((),) * (N_GROUPS - 2)
        o, lse, narrowed = zip(*[_prompt_attention(*qkv[3 * g:3 * g + 3], table, g, to_narrow=narrowing[g])
                                 for g in range(N_GROUPS)])
        (wup2,), (wdn2, wab, wpg, wpb, wout) = narrowed[0:2]
        ffn2_w = (gain(norm_ffn2), wup2, wdn2)
        merge_w = (wab, wpg.reshape(pool_groups.shape), gain(pool_scale), wpb, wout)
        xp = _back(xp, o, lse, u, u.reshape(-1, POOL_HALO, pool_width), ga, gb, merge_w, ffn2_w, tm=tm,
                   halo_block=(1, POOL_HALO, pool_width),
                   halo_index=lambda blk: (jnp.maximum(blk * (tm // POOL_HALO) - 1, 0), 0, 0),
                   pos_base=0, blocks_per_seq=seq // tm)
        for g in range(N_GROUPS):
            kv_p[g].append(_window_out(windows[g]))
        pool_p.append(pstate[:, POOL_HALO - POOL_STATE:])

        o, lse, windows = sample_outs[0:3], sample_outs[3:6], sample_outs[6:9]
        history = jnp.concatenate(
            [jnp.zeros((dec_batch, POOL_HALO - POOL_STATE, pool_width), F32), state_pool[layer]], axis=1)
        xs = _back(xs, o, lse, u_s, history, ga_s, gb_s, merge_w, ffn2_w, tm=ms,
                   halo_block=(dec_batch, POOL_HALO, pool_width), halo_index=lambda blk: (0, 0, 0),
                   pos_base=PAST_LEN, blocks_per_seq=1)
        for g in range(N_GROUPS):
            kv_s[g].append(_window_out(windows[g]))
        ue = jnp.concatenate([state_pool[layer], u_s.reshape(dec_batch, dec_seq, pool_width)], axis=1)
        pool_s.append(ue[:, ue.shape[1] - POOL_STATE:])

    stack = lambda xs_: jnp.stack(xs_, axis=0)
    return (xp.reshape(batch, seq, d_model), xs.reshape(dec_batch, dec_seq, d_model),
            stack(kv_p[0]), stack(kv_p[1]), stack(kv_p[2]), stack(pool_p),
            stack(kv_s[0]), stack(kv_s[1]), stack(kv_s[2]), stack(pool_s))
```

```python
import functools
import math

import numpy as np
import jax
import jax.numpy as jnp
from jax import lax
from jax.experimental import pallas as pl
from jax.experimental.pallas import tpu as pltpu

HEAD_DIM = 64
HEADS_PER_GROUP = 4
GROUP_WIDTH = HEADS_PER_GROUP * HEAD_DIM
ATTN_GROUPS = ((128, 1), (512, 4), (2048, 16))
N_GROUPS = len(ATTN_GROUPS)
ATTN_WIDTH = N_GROUPS * GROUP_WIDTH
N_BUCKETS = 32
MAX_DISTANCE = 2048
POOL_WINDOWS = (2, 4, 8, 16)
POOL_STATE = max(POOL_WINDOWS) - 1
POOL_HALO = 16
POOL_PAD = POOL_HALO + 8
EPS = 1e-6
PAST_LEN = 8192
NEG = -1e30
LOG2E = math.log2(math.e)
LANES = 128
MXU_TILE = 256
SLABS = GROUP_WIDTH // LANES
SINGLE_OP_STRIDE = 4
QBLK = 128
ATTN_UNROLL = 16
VMEM_LIMIT = 58 * 1024 * 1024

F32 = jnp.float32
BF16 = jnp.bfloat16
NT_DIMS = (((1,), (1,)), ((), ()))


def _t5_buckets(distance):
    max_exact = N_BUCKETS // 2
    d = np.asarray(distance, dtype=np.int32)
    ratio = np.log(np.maximum(d, 1).astype(np.float32) / np.float32(max_exact))
    large = max_exact + (ratio / np.float32(math.log(MAX_DISTANCE / max_exact))
                         * (N_BUCKETS - max_exact)).astype(np.int32)
    large = np.minimum(large, N_BUCKETS - 1)
    return np.where(d < max_exact, d, large).astype(np.int32)


def _params(n_grid_dims=1):
    return pltpu.CompilerParams(dimension_semantics=("arbitrary",) * n_grid_dims,
                                vmem_limit_bytes=VMEM_LIMIT)


def _resident(shape):
    return pl.BlockSpec(shape, lambda *_: (0,) * len(shape), pipeline_mode=pl.Buffered(1))


_SMEM = pl.BlockSpec(memory_space=pltpu.SMEM)


def _rmsnorm(x, g):
    ms = jnp.mean(x * x, axis=-1, keepdims=True)
    return x * lax.rsqrt(ms + EPS) * g


def _head_select(parts, lane_head):
    out = jnp.where(lane_head == 0, parts[0], 0.0)
    for h in range(1, HEADS_PER_GROUP):
        out = jnp.where(lane_head == h, parts[h], out)
    return out


def _stack_heads(q, lane_head):
    return jnp.concatenate([jnp.where(lane_head == h, q, 0.0)
                            for h in range(HEADS_PER_GROUP)], axis=0).astype(BF16)


def _build_bias(bias_ref, bucket_ref, table_ref, g):
    buckets = bucket_ref[...]
    rows = buckets.shape[0]
    for h in range(HEADS_PER_GROUP):
        acc = jnp.full(buckets.shape, NEG, F32)
        for b in range(N_BUCKETS):
            acc = jnp.where(buckets == b, table_ref[b, g * HEADS_PER_GROUP + h] * LOG2E, acc)
        bias_ref[h * rows:(h + 1) * rows, :] = acc


def _matmul_weight(w_ref, narrow_ref, rows, cols):
    w = w_ref[rows, cols]
    if narrow_ref is not None:
        w = w.astype(BF16)
        narrow_ref[rows, cols] = w
    return w


def _ffn_block(x_ref, g_ref, wup_ref, wdn_ref, o_ref, act_ref, narrow_refs=(None, None)):
    d_ff, d = wdn_ref.shape
    every = slice(None)
    h = _rmsnorm(x_ref[...], g_ref[...]).astype(BF16)
    for lo in range(0, d_ff, MXU_TILE):
        w_gate = _matmul_weight(wup_ref, narrow_refs[0], every, slice(lo, lo + MXU_TILE))
        w_up = _matmul_weight(wup_ref, narrow_refs[0], every, slice(d_ff + lo, d_ff + lo + MXU_TILE))
        gate = jnp.dot(h, w_gate, preferred_element_type=F32)
        up = jnp.dot(h, w_up, preferred_element_type=F32)
        act_ref[:, lo:lo + MXU_TILE] = (gate * jax.nn.sigmoid(gate) * up).astype(BF16)
    for lo in range(0, d, MXU_TILE):
        cols = slice(lo, lo + MXU_TILE)
        w_down = _matmul_weight(wdn_ref, narrow_refs[1], every, cols)
        o_ref[:, cols] = x_ref[:, cols] + 0.5 * jnp.dot(act_ref[...], w_down, preferred_element_type=F32)


def _ffn_kernel(*refs, rider_counts, narrow):
    n_in, n_out = rider_counts
    n_narrow = 2 if narrow else 0
    x_ref, g_ref, wup_ref, wdn_ref = refs[0:4]
    o_ref = refs[4 + n_in]
    narrow_refs = refs[5 + n_in:5 + n_in + n_narrow] if narrow else (None, None)
    first_rider_out = 5 + n_in + n_narrow
    act_ref = refs[first_rider_out + n_out]
    if n_in:
        _sample_attn_block((*refs[4:4 + n_in], *refs[first_rider_out:first_rider_out + n_out],
                            *refs[first_rider_out + n_out + 1:]))
    _ffn_block(x_ref, g_ref, wup_ref, wdn_ref, o_ref, act_ref, narrow_refs)


def _ffn(x, gain, w_up, w_down, tm, rider=None, narrow=False):
    m, d = x.shape
    d_ff = w_down.shape[0]
    assert m % tm == 0 and d_ff % MXU_TILE == 0 and d % MXU_TILE == 0
    assert not narrow or m == tm, "each weight chunk is narrowed once only on a single-step grid"
    rider = rider or dict(args=[], in_specs=[], out_shape=[], out_specs=[], scratch_shapes=[], batch=m // tm)
    assert rider["batch"] == m // tm
    weights = (w_up, w_down)
    narrowed = [jax.ShapeDtypeStruct(w.shape, BF16) for w in weights] if narrow else []
    outs = pl.pallas_call(
        functools.partial(_ffn_kernel, rider_counts=(len(rider["args"]), len(rider["out_shape"])), narrow=narrow),
        out_shape=[jax.ShapeDtypeStruct((m, d), F32)] + narrowed + rider["out_shape"],
        grid=(m // tm,),
        in_specs=[pl.BlockSpec((tm, d), lambda i: (i, 0)),
                  _resident((1, d)), _resident((d, 2 * d_ff)), _resident((d_ff, d))] + rider["in_specs"],
        out_specs=[pl.BlockSpec((tm, d), lambda i: (i, 0))] + [_resident(w.shape) for w in narrowed]
        + rider["out_specs"],
        scratch_shapes=[pltpu.VMEM((tm, d_ff), BF16)] + rider["scratch_shapes"],
        compiler_params=_params(),
        name="ffn",
    )(x, gain, w_up, w_down, *rider["args"])
    return outs[0], outs[1:]


def _proj_kernel(x_ref, g_ref, win_ref, qn_ref, kn_ref, seg_ref, *refs,
                 dils, kv_rows, with_state, narrow, pool_width, d_model):
    qkv_refs = refs[0:3 * N_GROUPS]
    u_ref, ga_ref, gb_ref = refs[3 * N_GROUPS:3 * N_GROUPS + 3]
    state_refs = refs[3 * N_GROUPS + 3:]
    narrow_ref = refs[-3] if narrow else None
    slab_ref, regroup_ref = refs[-2:]
    tm = x_ref.shape[0]
    h = _rmsnorm(x_ref[...], g_ref[...]).astype(BF16)

    def proj(lo, width):
        w = _matmul_weight(win_ref, narrow_ref, slice(None), slice(lo, lo + width))
        return jnp.dot(h, w, preferred_element_type=F32)

    def head_norm(y, gain):
        ms = jnp.dot((y * y).astype(BF16), seg_ref[...], preferred_element_type=F32)
        return y * lax.rsqrt(ms + EPS) * gain

    def emit(dst_ref, slot, val, dil):
        if dil == 1:
            dst_ref[0, 0] = val.astype(dst_ref.dtype)
            return
        for s in range(SLABS):
            slab_ref[slot, s] = val[:, s * LANES:(s + 1) * LANES]
        n = tm // dil
        if dil <= SINGLE_OP_STRIDE:
            for r in range(dil):
                parts = [slab_ref[slot, s, pl.ds(r, n, stride=dil), :] for s in range(SLABS)]
                dst_ref[0, r] = jnp.concatenate(parts, axis=-1).astype(dst_ref.dtype)
            return
        outer = SINGLE_OP_STRIDE
        inner = dil // outer
        assert inner <= SINGLE_OP_STRIDE
        quarter = tm // outer
        for a in range(outer):
            for s in range(SLABS):
                regroup_ref[s, a * quarter:(a + 1) * quarter, :] = slab_ref[slot, s, pl.ds(a, quarter, stride=outer), :]
        for r in range(dil):
            start = (r % outer) * quarter + r // outer
            parts = [regroup_ref[s, pl.ds(start, n, stride=inner), :] for s in range(SLABS)]
            dst_ref[0, r] = jnp.concatenate(parts, axis=-1).astype(dst_ref.dtype)

    def finish(g, which, y):
        cols = slice(g * GROUP_WIDTH, (g + 1) * GROUP_WIDTH)
        if which == 0:
            y = head_norm(y, qn_ref[:, cols]) * (HEAD_DIM ** -0.5 * LOG2E)
        elif which == 1:
            y = head_norm(y, kn_ref[:, cols])
        emit(qkv_refs[3 * g + which], 3 * sum(d > 1 for d in dils[:g]) + which, y, dils[g])
        if with_state and which > 0:
            state_refs[g][0, which - 1] = y.T[:, tm - kv_rows[g]:]

    pending = None
    for g in range(N_GROUPS):
        for which in range(3):
            y = proj(which * ATTN_WIDTH + g * GROUP_WIDTH, GROUP_WIDTH)
            if pending is not None:
                finish(*pending)
            pending = (g, which, y)
    u = proj(3 * ATTN_WIDTH, pool_width)
    finish(*pending)
    u_ref[...] = u
    if with_state:
        state_refs[N_GROUPS][0] = u[tm - POOL_HALO:, :]
    for gate_ref, base in ((ga_ref, 3 * ATTN_WIDTH + pool_width), (gb_ref, 3 * ATTN_WIDTH + pool_width + d_model)):
        for lo in range(0, d_model, MXU_TILE):
            gate_ref[:, lo:lo + MXU_TILE] = jax.nn.sigmoid(proj(base + lo, MXU_TILE)).astype(BF16)


def _proj(x, gain, w_in, q_gain, k_gain, seg, *, tm, n_seq, dils, qkv_dtype, keep_rows, narrow=False):
    m, d = x.shape
    pool_width = w_in.shape[1] - 3 * ATTN_WIDTH - 2 * d
    seq_len = m // n_seq
    assert seq_len % tm == 0 and all(tm % dil == 0 for dil in dils)
    blocks_per_seq = seq_len // tm
    with_state = keep_rows is not None
    seq_block = lambda i: (i // blocks_per_seq, i % blocks_per_seq)

    row = lambda w: pl.BlockSpec((tm, w), lambda i: (i, 0))
    out_shape, out_specs = [], []
    for g in range(N_GROUPS):
        sds = jax.ShapeDtypeStruct((n_seq, dils[g], seq_len // dils[g], GROUP_WIDTH), qkv_dtype)
        spec = pl.BlockSpec((1, dils[g], tm // dils[g], GROUP_WIDTH),
                            lambda i: (seq_block(i)[0], 0, seq_block(i)[1], 0))
        out_shape += [sds] * 3
        out_specs += [spec] * 3
    out_shape += [jax.ShapeDtypeStruct((m, pool_width), F32),
                  jax.ShapeDtypeStruct((m, d), BF16), jax.ShapeDtypeStruct((m, d), BF16)]
    out_specs += [row(pool_width), row(d), row(d)]
    kv_rows = kv_first = ()
    if with_state:
        kv_rows = tuple(min(r, tm) for r in keep_rows)
        kv_first = tuple(blocks_per_seq - keep_rows[g] // kv_rows[g] for g in range(N_GROUPS))
        for g in range(N_GROUPS):
            out_shape.append(jax.ShapeDtypeStruct((n_seq, 2, GROUP_WIDTH, keep_rows[g]), F32))
            out_specs.append(pl.BlockSpec(
                (1, 2, GROUP_WIDTH, kv_rows[g]),
                lambda i, first=kv_first[g]: (seq_block(i)[0], 0, 0, jnp.maximum(seq_block(i)[1] - first, 0))))
        out_shape.append(jax.ShapeDtypeStruct((n_seq, POOL_HALO, pool_width), F32))
        out_specs.append(pl.BlockSpec((1, POOL_HALO, pool_width), lambda i: (seq_block(i)[0], 0, 0)))
    if narrow:
        assert m == tm, "each weight chunk is narrowed once only on a single-step grid"
        out_shape.append(jax.ShapeDtypeStruct(w_in.shape, BF16))
        out_specs.append(_resident(w_in.shape))
    return pl.pallas_call(
        functools.partial(_proj_kernel, dils=dils, kv_rows=kv_rows, with_state=with_state, narrow=narrow,
                          pool_width=pool_width, d_model=d),
        out_shape=out_shape,
        grid=(m // tm,),
        in_specs=[row(d), _resident((1, d)), _resident(w_in.shape),
                  _resident((1, ATTN_WIDTH)), _resident((1, ATTN_WIDTH)),
                  _resident((GROUP_WIDTH, GROUP_WIDTH))],
        out_specs=out_specs,
        scratch_shapes=[pltpu.VMEM((max(1, 3 * sum(dil > 1 for dil in dils)), SLABS, tm, LANES), F32),
                        pltpu.VMEM((SLABS, tm, LANES), F32)],
        compiler_params=_params(),
        name="proj",
    )(x, gain, w_in, q_gain, k_gain, seg)


def _attn_kernel(*refs, g, dil, n_blocks, unroll, n_cast):
    q_ref, k_ref, v_ref, bucket_ref, hmask_ref, table_ref = refs[0:6]
    wide_refs = refs[6:6 + n_cast]
    o_ref, lse_ref = refs[6 + n_cast:8 + n_cast]
    narrow_refs = refs[8 + n_cast:8 + 2 * n_cast]
    bias_ref = refs[8 + 2 * n_cast]
    for wide_ref, narrow_ref in zip(wide_refs, narrow_refs):
        narrow_ref[...] = wide_ref[...].astype(BF16)

    @pl.when(pl.program_id(0) == 0)
    def _():
        _build_bias(bias_ref.at[1], bucket_ref, table_ref, g)
        bias_ref[0, :, 0:QBLK] = bias_ref[1, :, QBLK:2 * QBLK]
        bias_ref[0, :, QBLK:2 * QBLK] = jnp.full((HEADS_PER_GROUP * QBLK, QBLK), NEG, F32)

    first_head = lax.broadcasted_iota(jnp.int32, (QBLK, LANES), 1) < HEAD_DIM
    heads = range(HEADS_PER_GROUP)

    def per_slab(col):
        part = lambda h: jnp.broadcast_to(col[h * QBLK:(h + 1) * QBLK], (QBLK, LANES))
        return [jnp.where(first_head, part(2 * s), part(2 * s + 1)) for s in range(SLABS)]

    def body(n, carry):
        res = n // n_blocks
        i = n % n_blocks
        row0 = pl.multiple_of(i * QBLK, QBLK)
        span0 = pl.multiple_of(jnp.maximum(i - 1, 0) * QBLK, QBLK)
        q = q_ref[0, res, pl.ds(row0, QBLK), :]
        qs = jnp.concatenate([q * hmask_ref[h] for h in heads], axis=0)
        s = lax.dot_general(qs, k_ref[0, res, pl.ds(span0, 2 * QBLK), :], NT_DIMS,
                            preferred_element_type=F32) + bias_ref[jnp.minimum(i, 1)]
        m = jnp.max(s, axis=-1, keepdims=True)
        p = jnp.exp2(s - m)
        l = jnp.sum(p, axis=-1, keepdims=True)
        p = p.astype(BF16)
        v = v_ref[0, res, pl.ds(span0, 2 * QBLK), :]
        p_wide = jnp.concatenate([p[h * QBLK:(h + 1) * QBLK] for h in heads], axis=1)
        v_tall = jnp.concatenate([v * hmask_ref[h] for h in heads], axis=0)
        pv = jnp.dot(p_wide, v_tall, preferred_element_type=F32)
        token0 = res + dil * row0
        dst = pl.ds(token0, QBLK) if dil == 1 else pl.ds(token0, QBLK, stride=dil)
        for s_, (m_s, l_s) in enumerate(zip(per_slab(m), per_slab(l))):
            o_ref[0, s_, dst, :] = pv[:, s_ * LANES:(s_ + 1) * LANES] / l_s
            lse_ref[0, s_, dst, :] = m_s + jnp.log2(l_s)
        return carry

    lax.fori_loop(0, dil * n_blocks, body, 0, unroll=unroll)


def _prompt_buckets(g):
    win, dil = ATTN_GROUPS[g]
    delta = np.arange(QBLK)[:, None] + QBLK - np.arange(2 * QBLK)[None, :]
    valid = (delta >= 0) & (delta <= win // dil)
    return np.where(valid, _t5_buckets(dil * np.clip(delta, 0, win // dil)), -1).astype(np.int32)


def _prompt_attention(q, k, v, table, g, to_narrow=()):
    _, dil = ATTN_GROUPS[g]
    batch, _, sub, _ = q.shape
    n_blocks = sub // QBLK
    assert sub % QBLK == 0 and n_blocks >= 2 and (dil * n_blocks) % ATTN_UNROLL == 0
    assert all(w.shape[0] % (16 * batch) == 0 for w in to_narrow)
    cast_specs = [pl.BlockSpec((w.shape[0] // batch, w.shape[1]), lambda b: (b, 0)) for w in to_narrow]
    head_of_lane = np.arange(GROUP_WIDTH) // HEAD_DIM
    hmask = head_of_lane[None, None, :] == np.arange(HEADS_PER_GROUP)[:, None, None]
    in_spec = pl.BlockSpec((1, dil, sub, GROUP_WIDTH), lambda b: (b, 0, 0, 0))
    out_spec = pl.BlockSpec((1, SLABS, sub * dil, LANES), lambda b: (b, 0, 0, 0))
    out_sds = jax.ShapeDtypeStruct((batch, SLABS, sub * dil, LANES), F32)
    outs = pl.pallas_call(
        functools.partial(_attn_kernel, g=g, dil=dil, n_blocks=n_blocks, unroll=ATTN_UNROLL,
                          n_cast=len(to_narrow)),
        out_shape=[out_sds, out_sds] + [jax.ShapeDtypeStruct(w.shape, BF16) for w in to_narrow],
        grid=(batch,),
        in_specs=[in_spec, in_spec, in_spec, _resident((QBLK, 2 * QBLK)), _resident(hmask.shape), _SMEM]
        + cast_specs,
        out_specs=[out_spec, out_spec] + cast_specs,
        scratch_shapes=[pltpu.VMEM((2, HEADS_PER_GROUP * QBLK, 2 * QBLK), F32)],
        compiler_params=_params(),
        name=f"attn_g{g}",
    )(q, k, v, jnp.asarray(_prompt_buckets(g)), jnp.asarray(hmask, BF16), table, *to_narrow)
    return outs[0], outs[1], outs[2:]


def _sample_attn_block(refs):
    qkv_refs = refs[0:9]
    cache_refs = refs[9:12]
    bucket_refs, table_ref = refs[12:15], refs[15]
    o_refs, lse_refs, win_refs = refs[16:19], refs[19:22], refs[22:25]
    bias_refs = refs[25:28]

    @pl.when(pl.program_id(0) == 0)
    def _():
        for g in range(N_GROUPS):
            _build_bias(bias_refs[g], bucket_refs[g], table_ref, g)

    t_new = qkv_refs[0].shape[1]
    lane_head = lax.broadcasted_iota(jnp.int32, (t_new, GROUP_WIDTH), 1) // HEAD_DIM
    new_lane = lax.broadcasted_iota(jnp.int32, (GROUP_WIDTH, LANES), 1) >= LANES - t_new
    pad = jnp.zeros((LANES - t_new, GROUP_WIDTH), F32)

    for g in range(N_GROUPS):
        q_ref, k_ref, v_ref = qkv_refs[3 * g:3 * g + 3]
        cache_ref, win_ref, bias_ref = cache_refs[g], win_refs[g], bias_refs[g]
        past = cache_ref.shape[3]
        new_rows = [jnp.concatenate([pad, r[0]], axis=0) for r in (k_ref, v_ref)]
        for c in range(2):
            shifted = pltpu.roll(cache_ref[0, c], past - t_new, 1)
            if past > LANES:
                win_ref[0, c, :, 0:past - LANES] = shifted[:, 0:past - LANES]
            win_ref[0, c, :, past - LANES:past] = jnp.where(new_lane, new_rows[c].T, shifted[:, past - LANES:])

        qs = _stack_heads(q_ref[0], lane_head)
        s_old = jnp.dot(qs, cache_ref[0, 0].astype(BF16), preferred_element_type=F32) + bias_ref[:, 0:past]
        s_new = lax.dot_general(qs, new_rows[0].astype(BF16), NT_DIMS,
                                preferred_element_type=F32) + bias_ref[:, past:past + LANES]
        m = jnp.maximum(jnp.max(s_old, axis=-1, keepdims=True), jnp.max(s_new, axis=-1, keepdims=True))
        p_old = jnp.exp2(s_old - m)
        p_new = jnp.exp2(s_new - m)
        l = jnp.sum(p_old, axis=-1, keepdims=True) + jnp.sum(p_new, axis=-1, keepdims=True)
        pv = (lax.dot_general(p_old.astype(BF16), cache_ref[0, 1].astype(BF16), NT_DIMS,
                              preferred_element_type=F32)
              + jnp.dot(p_new.astype(BF16), new_rows[1].astype(BF16), preferred_element_type=F32)) / l
        lse = jnp.broadcast_to(m + jnp.log2(l), pv.shape)
        rows = lambda a: [a[h * t_new:(h + 1) * t_new] for h in range(HEADS_PER_GROUP)]
        for out_ref, val in ((o_refs[g], _head_select(rows(pv), lane_head)),
                             (lse_refs[g], _head_select(rows(lse), lane_head))):
            for s_ in range(SLABS):
                out_ref[0, s_] = val[:, s_ * LANES:(s_ + 1) * LANES]


def _sample_attn_kernel(*refs):
    _sample_attn_block(refs)


def _sample_buckets(g, past, t_new):
    win, dil = ATTN_GROUPS[g]
    lane = np.arange(past + LANES)[None, :]
    key_pos = np.where(lane < past, lane, lane - (LANES - t_new))
    delta = past + np.arange(t_new)[:, None] - key_pos
    valid = ((lane < past) | (lane >= past + LANES - t_new)) & (delta >= 0) & (delta % dil == 0) & (delta <= win)
    return np.where(valid, _t5_buckets(np.clip(delta, 0, win)), -1).astype(np.int32)


def _sample_attention_parts(qkv, caches, table):
    batch, t_new, _ = qkv[0].shape
    pasts = [c.shape[3] for c in caches]
    buckets = [jnp.asarray(_sample_buckets(g, pasts[g], t_new)) for g in range(N_GROUPS)]
    cache_spec = lambda p: pl.BlockSpec((1, 2, GROUP_WIDTH, p), lambda b: (b, 0, 0, 0))
    new_spec = pl.BlockSpec((1, t_new, GROUP_WIDTH), lambda b: (b, 0, 0))
    out_spec = pl.BlockSpec((1, SLABS, t_new, LANES), lambda b: (0, 0, b, 0))
    out_sds = jax.ShapeDtypeStruct((1, SLABS, batch * t_new, LANES), F32)
    return dict(
        batch=batch,
        args=[*qkv, *caches, *buckets, table],
        out_shape=[out_sds] * 6 + [jax.ShapeDtypeStruct(c.shape, F32) for c in caches],
        in_specs=[new_spec] * 9 + [cache_spec(p) for p in pasts] + [_resident(b.shape) for b in buckets]
        + [_SMEM],
        out_specs=[out_spec] * 6 + [cache_spec(p) for p in pasts],
        scratch_shapes=[pltpu.VMEM((HEADS_PER_GROUP * t_new, p + LANES), F32) for p in pasts])


def _sample_attention(parts):
    return pl.pallas_call(
        _sample_attn_kernel, grid=(parts["batch"],), out_shape=parts["out_shape"],
        in_specs=parts["in_specs"], out_specs=parts["out_specs"], scratch_shapes=parts["scratch_shapes"],
        compiler_params=_params(), name="sample_attn")(*parts["args"])


def _back_kernel(x_ref, o0_ref, o1_ref, o2_ref, l0_ref, l1_ref, l2_ref, u_ref, halo_ref, ga_ref, gb_ref,
                 wab_ref, wpg_ref, pscale_ref, wpb_ref, wout_ref, g2_ref, wup_ref, wdn_ref,
                 out_ref, ue_ref, level_ref, mixed_ref, act_ref, attn_ref, pooled_ref, merged_ref,
                 *, pos_base, blocks_per_seq):
    block = pl.program_id(0)
    tm, d_model = x_ref.shape
    n_seq = halo_ref.shape[0]
    t = tm // n_seq
    pool_width = u_ref.shape[1]
    gw = pool_width // len(POOL_WINDOWS)

    for s in range(SLABS):
        lses = [r[0, s] for r in (l0_ref, l1_ref, l2_ref)]
        top = jnp.maximum(jnp.maximum(lses[0], lses[1]), lses[2])
        num = jnp.zeros_like(top)
        den = jnp.zeros_like(top)
        for o_ref, lse in zip((o0_ref, o1_ref, o2_ref), lses):
            e = jnp.exp2(lse - top)
            num = num + e * o_ref[0, s]
            den = den + e
        attn_ref[:, s * LANES:(s + 1) * LANES] = (num / den).astype(BF16)

    start = pos_base + (block % blocks_per_seq) * t
    first, end = POOL_PAD - POOL_HALO, POOL_PAD + t
    ue_ref[:, 0:first, :] = jnp.zeros((n_seq, first, pool_width), F32)
    ue_ref[:, first:POOL_PAD, :] = jnp.where(start > 0, halo_ref[...], 0.0)
    ue_ref[:, POOL_PAD:, :] = u_ref[...].reshape(n_seq, t, pool_width)
    level_ref[:, :, 0:first, :] = jnp.zeros((2, n_seq, first, gw), F32)
    pos = start + lax.broadcasted_iota(jnp.int32, (1, t, gw), 1)
    for gi, win in enumerate(POOL_WINDOWS):
        cols = slice(gi * gw, (gi + 1) * gw)
        read = lambda lo, hi, cols=cols: ue_ref[:, lo:hi, cols]
        span = 1
        while 2 * span < win:
            buf = (span.bit_length() - 1) % 2
            level_ref[buf, :, first:, :] = read(first, end) + read(first - span, end - span)
            read = lambda lo, hi, buf=buf: level_ref[buf, :, lo:hi, :]
            span *= 2
        s = read(POOL_PAD, end) + read(POOL_PAD - span, end - span)
        cnt = jnp.minimum(pos + 1, win).astype(F32)
        d = (s / cnt - ue_ref[:, POOL_PAD:, cols]).reshape(tm, gw)
        y = jnp.dot(d.astype(BF16), wpg_ref[gi], preferred_element_type=F32)
        pooled_ref[:, cols] = (y * pscale_ref[:, cols]).astype(BF16)

    for lo in range(0, d_model, MXU_TILE):
        cols = slice(lo, lo + MXU_TILE)
        branch_a = jnp.dot(attn_ref[...], wab_ref[:, cols], preferred_element_type=F32)
        branch_b = jnp.dot(pooled_ref[...], wpb_ref[:, cols], preferred_element_type=F32)
        merged_ref[:, cols] = (ga_ref[:, cols].astype(F32) * branch_a
                               + gb_ref[:, cols].astype(F32) * branch_b).astype(BF16)
    for lo in range(0, d_model, MXU_TILE):
        cols = slice(lo, lo + MXU_TILE)
        mixed_ref[:, cols] = x_ref[:, cols] + jnp.dot(merged_ref[...], wout_ref[:, cols],
                                                      preferred_element_type=F32)

    _ffn_block(mixed_ref, g2_ref, wup_ref, wdn_ref, out_ref, act_ref)


def _back(x, o, lse, u, halo, ga, gb, merge_w, ffn_w, *, tm, halo_block, halo_index, pos_base, blocks_per_seq):
    m, d = x.shape
    pool_width = u.shape[1]
    n_seq = halo_block[0]
    d_ff = ffn_w[2].shape[0]
    row = lambda wd: pl.BlockSpec((tm, wd), lambda i: (i, 0))
    slab = pl.BlockSpec((1, SLABS, tm, LANES), lambda i: (i // blocks_per_seq, 0, i % blocks_per_seq, 0))
    return pl.pallas_call(
        functools.partial(_back_kernel, pos_base=pos_base, blocks_per_seq=blocks_per_seq),
        out_shape=jax.ShapeDtypeStruct((m, d), F32),
        grid=(m // tm,),
        in_specs=[row(d)] + [slab] * 6 + [row(pool_width), pl.BlockSpec(halo_block, halo_index), row(d), row(d)]
        + [_resident(a.shape) for a in (*merge_w, *ffn_w)],
        out_specs=row(d),
        scratch_shapes=[pltpu.VMEM((n_seq, POOL_PAD + tm // n_seq, pool_width), F32),
                        pltpu.VMEM((2, n_seq, POOL_PAD + tm // n_seq, pool_width // len(POOL_WINDOWS)), F32),
                        pltpu.VMEM((tm, d), F32), pltpu.VMEM((tm, d_ff), BF16),
                        pltpu.VMEM((tm, GROUP_WIDTH), BF16), pltpu.VMEM((tm, pool_width), BF16),
                        pltpu.VMEM((tm, d), BF16)],
        compiler_params=_params(),
        name="back",
    )(x, *o, *lse, u, halo, ga, gb, *merge_w, *ffn_w)


def _window_in(cache):
    n_seq, rows = cache.shape[0:2]
    return jnp.transpose(cache, (0, 2, 3, 4, 1)).reshape(n_seq, 2, GROUP_WIDTH, rows)


def _window_out(kv):
    n_seq, _, _, rows = kv.shape
    return jnp.transpose(kv.reshape(n_seq, 2, HEADS_PER_GROUP, HEAD_DIM, rows), (0, 4, 1, 2, 3))


def kernel(x_prompt, x_sample, cache_kv_w128, cache_kv_w512, cache_kv_w2048, state_pool, rel_bias_table,
           norm_ffn1, ffn1_w_up, ffn1_w_down, norm_mix, w_in, q_norm, k_norm, pool_w_group, pool_scale,
           w_attn_branch, w_pool_branch, w_out, norm_ffn2, ffn2_w_up, ffn2_w_down):
    batch, seq, d_model = x_prompt.shape
    dec_batch, dec_seq, _ = x_sample.shape
    depth = norm_ffn1.shape[0]
    pool_width = state_pool.shape[-1]
    tm = 512
    ms = dec_batch * dec_seq
    seg = jnp.asarray(np.kron(np.eye(HEADS_PER_GROUP), np.full((HEAD_DIM, HEAD_DIM), 1.0 / HEAD_DIM)), BF16)
    table = rel_bias_table.astype(F32)
    caches_in = (cache_kv_w128, cache_kv_w512, cache_kv_w2048)
    keep_prompt = tuple(min(win, seq) for win, _ in ATTN_GROUPS)
    dils = tuple(dil for _, dil in ATTN_GROUPS)

    xp = x_prompt.reshape(batch * seq, d_model)
    xs = x_sample.reshape(ms, d_model)
    kv_p, kv_s, pool_p, pool_s = ([], [], []), ([], [], []), [], []
    for layer in range(depth):
        gain = lambda a: a[layer].reshape(1, -1).astype(F32)
        qg, kg = gain(q_norm), gain(k_norm)
        pool_groups = pool_w_group[layer]
        late_w = (ffn2_w_down[layer], w_attn_branch[layer], pool_groups.reshape(-1, pool_groups.shape[-1]),
                  w_pool_branch[layer], w_out[layer])

        xs, (wup1, wdn1) = _ffn(xs, gain(norm_ffn1), ffn1_w_up[layer], ffn1_w_down[layer], ms, narrow=True)
        outs = _proj(xs, gain(norm_mix), w_in[layer], qg, kg, seg, tm=ms, n_seq=1, dils=(1,) * N_GROUPS,
                     qkv_dtype=F32, keep_rows=None, narrow=True)
        qkv_s, (u_s, ga_s, gb_s), win = outs[0:9], outs[9:12], outs[12]
        sample_parts = _sample_attention_parts(
            [a.reshape(dec_batch, dec_seq, GROUP_WIDTH) for a in qkv_s],
            [_window_in(c[layer]) for c in caches_in], table)

        if (batch * seq) // tm == dec_batch:
            xp, sample_outs = _ffn(xp, gain(norm_ffn1), wup1, wdn1, tm, rider=sample_parts)
        else:
            xp, _ = _ffn(xp, gain(norm_ffn1), wup1, wdn1, tm)
            sample_outs = _sample_attention(sample_parts)
        outs = _proj(xp, gain(norm_mix), win, qg, kg, seg, tm=2 * tm, n_seq=batch, dils=dils,
                     qkv_dtype=BF16, keep_rows=keep_prompt)
        qkv, (u, ga, gb), windows, pstate = outs[0:9], outs[9:12], outs[12:15], outs[15]
        narrowing = ((ffn2_w_up[layer],), late_w) + ((),) * (N_GROUPS - 2)
        o, lse, narrowed = zip(*[_prompt_attention(*qkv[3 * g:3 * g + 3], table, g, to_narrow=narrowing[g])
                                 for g in range(N_GROUPS)])
        (wup2,), (wdn2, wab, wpg, wpb, wout) = narrowed[0:2]
        ffn2_w = (gain(norm_ffn2), wup2, wdn2)
        merge_w = (wab, wpg.reshape(pool_groups.shape), gain(pool_scale), wpb, wout)
        xp = _back(xp, o, lse, u, u.reshape(-1, POOL_HALO, pool_width), ga, gb, merge_w, ffn2_w, tm=tm,
                   halo_block=(1, POOL_HALO, pool_width),
                   halo_index=lambda blk: (jnp.maximum(blk * (tm // POOL_HALO) - 1, 0), 0, 0),
                   pos_base=0, blocks_per_seq=seq // tm)
        for g in range(N_GROUPS):
            kv_p[g].append(_window_out(windows[g]))
        pool_p.append(pstate[:, POOL_HALO - POOL_STATE:])

        o, lse, windows = sample_outs[0:3], sample_outs[3:6], sample_outs[6:9]
        history = jnp.concatenate(
            [jnp.zeros((dec_batch, POOL_HALO - POOL_STATE, pool_width), F32), state_pool[layer]], axis=1)
        xs = _back(xs, o, lse, u_s, history, ga_s, gb_s, merge_w, ffn2_w, tm=ms,
                   halo_block=(dec_batch, POOL_HALO, pool_width), halo_index=lambda blk: (0, 0, 0),
                   pos_base=PAST_LEN, blocks_per_seq=1)
        for g in range(N_GROUPS):
            kv_s[g].append(_window_out(windows[g]))
        ue = jnp.concatenate([state_pool[layer], u_s.reshape(dec_batch, dec_seq, pool_width)], axis=1)
        pool_s.append(ue[:, ue.shape[1] - POOL_STATE:])

    stack = lambda xs_: jnp.stack(xs_, axis=0)
    return (xp.reshape(batch, seq, d_model), xs.reshape(dec_batch, dec_seq, d_model),
            stack(kv_p[0]), stack(kv_p[1]), stack(kv_p[2]), stack(pool_p),
            stack(kv_s[0]), stack(kv_s[1]), stack(kv_s[2]), stack(pool_s))
```

```python
import functools
import math

import numpy as np
import jax
import jax.numpy as jnp
from jax import lax
from jax.experimental import pallas as pl
from jax.experimental.pallas import tpu as pltpu

HEAD_DIM = 64
HEADS_PER_GROUP = 4
GROUP_WIDTH = HEADS_PER_GROUP * HEAD_DIM
ATTN_GROUPS = ((128, 1), (512, 4), (2048, 16))
N_GROUPS = len(ATTN_GROUPS)
ATTN_WIDTH = N_GROUPS * GROUP_WIDTH
N_BUCKETS = 32
MAX_DISTANCE = 2048
POOL_WINDOWS = (2, 4, 8, 16)
POOL_STATE = max(POOL_WINDOWS) - 1
POOL_HALO = 16
POOL_PAD = POOL_HALO + 8
EPS = 1e-6
PAST_LEN = 8192
NEG = -1e30
LOG2E = math.log2(math.e)
LANES = 128
MXU_TILE = 256
SLABS = GROUP_WIDTH // LANES
SINGLE_OP_STRIDE = 4
ROW_BLOCK = 512
QBLK = 128
ATTN_UNROLL = 16
VMEM_LIMIT = 58 * 1024 * 1024

F32 = jnp.float32
BF16 = jnp.bfloat16
NT_DIMS = (((1,), (1,)), ((), ()))


def _t5_buckets(distance):
    max_exact = N_BUCKETS // 2
    d = np.asarray(distance, dtype=np.int32)
    ratio = np.log(np.maximum(d, 1).astype(np.float32) / np.float32(max_exact))
    large = max_exact + (ratio / np.float32(math.log(MAX_DISTANCE / max_exact))
                         * (N_BUCKETS - max_exact)).astype(np.int32)
    large = np.minimum(large, N_BUCKETS - 1)
    return np.where(d < max_exact, d, large).astype(np.int32)


def _params(n_grid_dims=1):
    return pltpu.CompilerParams(dimension_semantics=("arbitrary",) * n_grid_dims,
                                vmem_limit_bytes=VMEM_LIMIT)


def _resident(shape):
    return pl.BlockSpec(shape, lambda *_: (0,) * len(shape), pipeline_mode=pl.Buffered(1))


_SMEM = pl.BlockSpec(memory_space=pltpu.SMEM)


def _rmsnorm(x, g):
    ms = jnp.mean(x * x, axis=-1, keepdims=True)
    return x * lax.rsqrt(ms + EPS) * g


def _head_select(parts, lane_head):
    out = jnp.where(lane_head == 0, parts[0], 0.0)
    for h in range(1, HEADS_PER_GROUP):
        out = jnp.where(lane_head == h, parts[h], out)
    return out


def _stack_heads(q, lane_head):
    return jnp.concatenate([jnp.where(lane_head == h, q, 0.0)
                            for h in range(HEADS_PER_GROUP)], axis=0).astype(BF16)


def _build_bias(bias_ref, bucket_ref, table_ref, g):
    buckets = bucket_ref[...]
    rows = buckets.shape[0]
    for h in range(HEADS_PER_GROUP):
        acc = jnp.full(buckets.shape, NEG, F32)
        for b in range(N_BUCKETS):
            acc = jnp.where(buckets == b, table_ref[b, g * HEADS_PER_GROUP + h] * LOG2E, acc)
        bias_ref[h * rows:(h + 1) * rows, :] = acc


def _matmul_weight(w_ref, narrow_ref, rows, cols):
    w = w_ref[rows, cols]
    if narrow_ref is not None:
        w = w.astype(BF16)
        narrow_ref[rows, cols] = w
    return w


def _ffn_block(x_ref, g_ref, wup_ref, wdn_ref, o_ref, act_ref, narrow_refs=(None, None)):
    d_ff, d = wdn_ref.shape
    every = slice(None)
    h = _rmsnorm(x_ref[...], g_ref[...]).astype(BF16)
    for lo in range(0, d_ff, MXU_TILE):
        w_gate = _matmul_weight(wup_ref, narrow_refs[0], every, slice(lo, lo + MXU_TILE))
        w_up = _matmul_weight(wup_ref, narrow_refs[0], every, slice(d_ff + lo, d_ff + lo + MXU_TILE))
        gate = jnp.dot(h, w_gate, preferred_element_type=F32)
        up = jnp.dot(h, w_up, preferred_element_type=F32)
        act_ref[:, lo:lo + MXU_TILE] = (gate * jax.nn.sigmoid(gate) * up).astype(BF16)
    for lo in range(0, d, MXU_TILE):
        cols = slice(lo, lo + MXU_TILE)
        w_down = _matmul_weight(wdn_ref, narrow_refs[1], every, cols)
        o_ref[:, cols] = x_ref[:, cols] + 0.5 * jnp.dot(act_ref[...], w_down, preferred_element_type=F32)


def _ffn_kernel(*refs, rider_counts, narrow):
    n_in, n_out = rider_counts
    n_narrow = 2 if narrow else 0
    x_ref, g_ref, wup_ref, wdn_ref = refs[0:4]
    o_ref = refs[4 + n_in]
    narrow_refs = refs[5 + n_in:5 + n_in + n_narrow] if narrow else (None, None)
    first_rider_out = 5 + n_in + n_narrow
    act_ref = refs[first_rider_out + n_out]
    if n_in:
        _sample_attn_block((*refs[4:4 + n_in], *refs[first_rider_out:first_rider_out + n_out],
                            *refs[first_rider_out + n_out + 1:]))
    _ffn_block(x_ref, g_ref, wup_ref, wdn_ref, o_ref, act_ref, narrow_refs)


def _ffn(x, gain, w_up, w_down, tm, rider=None, narrow=False):
    m, d = x.shape
    d_ff = w_down.shape[0]
    assert m % tm == 0 and d_ff % MXU_TILE == 0 and d % MXU_TILE == 0
    assert not narrow or m == tm, "each weight chunk is narrowed once only on a single-step grid"
    rider = rider or dict(args=[], in_specs=[], out_shape=[], out_specs=[], scratch_shapes=[], batch=m // tm)
    assert rider["batch"] == m // tm
    weights = (w_up, w_down)
    narrowed = [jax.ShapeDtypeStruct(w.shape, BF16) for w in weights] if narrow else []
    outs = pl.pallas_call(
        functools.partial(_ffn_kernel, rider_counts=(len(rider["args"]), len(rider["out_shape"])), narrow=narrow),
        out_shape=[jax.ShapeDtypeStruct((m, d), F32)] + narrowed + rider["out_shape"],
        grid=(m // tm,),
        in_specs=[pl.BlockSpec((tm, d), lambda i: (i, 0)),
                  _resident((1, d)), _resident((d, 2 * d_ff)), _resident((d_ff, d))] + rider["in_specs"],
        out_specs=[pl.BlockSpec((tm, d), lambda i: (i, 0))] + [_resident(w.shape) for w in narrowed]
        + rider["out_specs"],
        scratch_shapes=[pltpu.VMEM((tm, d_ff), BF16)] + rider["scratch_shapes"],
        compiler_params=_params(),
        name="ffn",
    )(x, gain, w_up, w_down, *rider["args"])
    return outs[0], outs[1:]


def _proj_kernel(x_ref, g_ref, win_ref, qn_ref, kn_ref, seg_ref, *refs,
                 dils, kv_rows, with_state, narrow, pool_width, d_model):
    qkv_refs = refs[0:N_GROUPS]
    u_ref, gate_ref = refs[N_GROUPS:N_GROUPS + 2]
    state_refs = refs[N_GROUPS + 2:]
    narrow_ref = refs[-3] if narrow else None
    slab_ref, regroup_ref = refs[-2:]
    tm = x_ref.shape[0]
    h = _rmsnorm(x_ref[...], g_ref[...]).astype(BF16)

    def proj(lo, width):
        w = _matmul_weight(win_ref, narrow_ref, slice(None), slice(lo, lo + width))
        return jnp.dot(h, w, preferred_element_type=F32)

    def head_norm(y, gain):
        ms = jnp.dot((y * y).astype(BF16), seg_ref[...], preferred_element_type=F32)
        return y * lax.rsqrt(ms + EPS) * gain

    def emit(dst_ref, which, slot, val, dil):
        if dil == 1:
            dst_ref[0, 0, which] = val.astype(dst_ref.dtype)
            return
        for s in range(SLABS):
            slab_ref[slot, s] = val[:, s * LANES:(s + 1) * LANES]
        n = tm // dil
        if dil <= SINGLE_OP_STRIDE:
            for r in range(dil):
                parts = [slab_ref[slot, s, pl.ds(r, n, stride=dil), :] for s in range(SLABS)]
                dst_ref[0, r, which] = jnp.concatenate(parts, axis=-1).astype(dst_ref.dtype)
            return
        outer = SINGLE_OP_STRIDE
        inner = dil // outer
        assert inner <= SINGLE_OP_STRIDE
        quarter = tm // outer
        for a in range(outer):
            for s in range(SLABS):
                regroup_ref[s, a * quarter:(a + 1) * quarter, :] = slab_ref[slot, s, pl.ds(a, quarter, stride=outer), :]
        for r in range(dil):
            start = (r % outer) * quarter + r // outer
            parts = [regroup_ref[s, pl.ds(start, n, stride=inner), :] for s in range(SLABS)]
            dst_ref[0, r, which] = jnp.concatenate(parts, axis=-1).astype(dst_ref.dtype)

    def finish(g, which, y):
        cols = slice(g * GROUP_WIDTH, (g + 1) * GROUP_WIDTH)
        if which == 0:
            y = head_norm(y, qn_ref[:, cols]) * (HEAD_DIM ** -0.5 * LOG2E)
        elif which == 1:
            y = head_norm(y, kn_ref[:, cols])
        emit(qkv_refs[g], which, 3 * sum(d > 1 for d in dils[:g]) + which, y, dils[g])
        if with_state and which > 0:
            state_refs[g][0, which - 1] = y.T[:, tm - kv_rows[g]:]

    pending = None
    for g in range(N_GROUPS):
        for which in range(3):
            y = proj(which * ATTN_WIDTH + g * GROUP_WIDTH, GROUP_WIDTH)
            if pending is not None:
                finish(*pending)
            pending = (g, which, y)
    u = proj(3 * ATTN_WIDTH, pool_width)
    finish(*pending)
    u_ref[...] = u
    if with_state:
        state_refs[N_GROUPS][0] = u[tm - POOL_HALO:, :]
    gates_base = 3 * ATTN_WIDTH + pool_width
    for lo in range(0, 2 * d_model, MXU_TILE):
        gate_ref[:, lo:lo + MXU_TILE] = jax.nn.sigmoid(proj(gates_base + lo, MXU_TILE)).astype(BF16)


def _proj(x, gain, w_in, q_gain, k_gain, seg, *, tm, n_seq, dils, qkv_dtype, keep_rows, narrow=False):
    m, d = x.shape
    pool_width = w_in.shape[1] - 3 * ATTN_WIDTH - 2 * d
    seq_len = m // n_seq
    assert seq_len % tm == 0 and all(tm % dil == 0 for dil in dils)
    blocks_per_seq = seq_len // tm
    with_state = keep_rows is not None
    seq_block = lambda i: (i // blocks_per_seq, i % blocks_per_seq)

    row = lambda w: pl.BlockSpec((tm, w), lambda i: (i, 0))
    out_shape, out_specs = [], []
    for g in range(N_GROUPS):
        out_shape.append(jax.ShapeDtypeStruct((n_seq, dils[g], 3, seq_len // dils[g], GROUP_WIDTH), qkv_dtype))
        out_specs.append(pl.BlockSpec((1, dils[g], 3, tm // dils[g], GROUP_WIDTH),
                                      lambda i: (seq_block(i)[0], 0, 0, seq_block(i)[1], 0)))
    out_shape += [jax.ShapeDtypeStruct((m, pool_width), F32), jax.ShapeDtypeStruct((m, 2 * d), BF16)]
    out_specs += [row(pool_width), row(2 * d)]
    kv_rows = kv_first = ()
    if with_state:
        kv_rows = tuple(min(r, tm) for r in keep_rows)
        kv_first = tuple(blocks_per_seq - keep_rows[g] // kv_rows[g] for g in range(N_GROUPS))
        for g in range(N_GROUPS):
            out_shape.append(jax.ShapeDtypeStruct((n_seq, 2, GROUP_WIDTH, keep_rows[g]), F32))
            out_specs.append(pl.BlockSpec(
                (1, 2, GROUP_WIDTH, kv_rows[g]),
                lambda i, first=kv_first[g]: (seq_block(i)[0], 0, 0, jnp.maximum(seq_block(i)[1] - first, 0))))
        out_shape.append(jax.ShapeDtypeStruct((n_seq, POOL_HALO, pool_width), F32))
        out_specs.append(pl.BlockSpec((1, POOL_HALO, pool_width), lambda i: (seq_block(i)[0], 0, 0)))
    if narrow:
        assert m == tm, "each weight chunk is narrowed once only on a single-step grid"
        out_shape.append(jax.ShapeDtypeStruct(w_in.shape, BF16))
        out_specs.append(_resident(w_in.shape))
    return pl.pallas_call(
        functools.partial(_proj_kernel, dils=dils, kv_rows=kv_rows, with_state=with_state, narrow=narrow,
                          pool_width=pool_width, d_model=d),
        out_shape=out_shape,
        grid=(m // tm,),
        in_specs=[row(d), _resident((1, d)), _resident(w_in.shape),
                  _resident((1, ATTN_WIDTH)), _resident((1, ATTN_WIDTH)),
                  _resident((GROUP_WIDTH, GROUP_WIDTH))],
        out_specs=out_specs,
        scratch_shapes=[pltpu.VMEM((max(1, 3 * sum(dil > 1 for dil in dils)), SLABS, tm, LANES), F32),
                        pltpu.VMEM((SLABS, tm, LANES), F32)],
        compiler_params=_params(),
        name="proj",
    )(x, gain, w_in, q_gain, k_gain, seg)


def _attn_kernel(*refs, g, dil, n_blocks, unroll, n_cast):
    qkv_ref, bucket_ref, hmask_ref, table_ref = refs[0:4]
    wide_refs = refs[4:4 + n_cast]
    out_ref = refs[4 + n_cast]
    narrow_refs = refs[5 + n_cast:5 + 2 * n_cast]
    bias_ref = refs[5 + 2 * n_cast]
    for wide_ref, narrow_ref in zip(wide_refs, narrow_refs):
        narrow_ref[...] = wide_ref[...].astype(BF16)

    @pl.when(pl.program_id(0) == 0)
    def _():
        _build_bias(bias_ref.at[1], bucket_ref, table_ref, g)
        bias_ref[0, :, 0:QBLK] = bias_ref[1, :, QBLK:2 * QBLK]
        bias_ref[0, :, QBLK:2 * QBLK] = jnp.full((HEADS_PER_GROUP * QBLK, QBLK), NEG, F32)

    first_head = lax.broadcasted_iota(jnp.int32, (QBLK, LANES), 1) < HEAD_DIM
    heads = range(HEADS_PER_GROUP)

    def per_slab(col):
        part = lambda h: jnp.broadcast_to(col[h * QBLK:(h + 1) * QBLK], (QBLK, LANES))
        return [jnp.where(first_head, part(2 * s), part(2 * s + 1)) for s in range(SLABS)]

    def body(n, carry):
        res = n // n_blocks
        i = n % n_blocks
        row0 = pl.multiple_of(i * QBLK, QBLK)
        span0 = pl.multiple_of(jnp.maximum(i - 1, 0) * QBLK, QBLK)
        q = qkv_ref[0, res, 0, pl.ds(row0, QBLK), :]
        qs = jnp.concatenate([q * hmask_ref[h] for h in heads], axis=0)
        s = lax.dot_general(qs, qkv_ref[0, res, 1, pl.ds(span0, 2 * QBLK), :], NT_DIMS,
                            preferred_element_type=F32) + bias_ref[jnp.minimum(i, 1)]
        m = jnp.max(s, axis=-1, keepdims=True)
        p = jnp.exp2(s - m)
        l = jnp.sum(p, axis=-1, keepdims=True)
        p = p.astype(BF16)
        v = qkv_ref[0, res, 2, pl.ds(span0, 2 * QBLK), :]
        p_wide = jnp.concatenate([p[h * QBLK:(h + 1) * QBLK] for h in heads], axis=1)
        v_tall = jnp.concatenate([v * hmask_ref[h] for h in heads], axis=0)
        pv = jnp.dot(p_wide, v_tall, preferred_element_type=F32)
        token0 = res + dil * row0
        dst = pl.ds(token0, QBLK) if dil == 1 else pl.ds(token0, QBLK, stride=dil)
        for s_, (m_s, l_s) in enumerate(zip(per_slab(m), per_slab(l))):
            out_ref[0, 0, s_, dst, :] = pv[:, s_ * LANES:(s_ + 1) * LANES] / l_s
            out_ref[0, 1, s_, dst, :] = m_s + jnp.log2(l_s)
        return carry

    lax.fori_loop(0, dil * n_blocks, body, 0, unroll=unroll)


def _prompt_buckets(g):
    win, dil = ATTN_GROUPS[g]
    delta = np.arange(QBLK)[:, None] + QBLK - np.arange(2 * QBLK)[None, :]
    valid = (delta >= 0) & (delta <= win // dil)
    return np.where(valid, _t5_buckets(dil * np.clip(delta, 0, win // dil)), -1).astype(np.int32)


def _prompt_attention(qkv, table, g, to_narrow=()):
    _, dil = ATTN_GROUPS[g]
    batch, _, _, sub, _ = qkv.shape
    n_blocks = sub // QBLK
    assert sub % QBLK == 0 and n_blocks >= 2 and (dil * n_blocks) % ATTN_UNROLL == 0
    assert all(w.shape[0] % (16 * batch) == 0 for w in to_narrow)
    cast_specs = [pl.BlockSpec((w.shape[0] // batch, w.shape[1]), lambda b: (b, 0)) for w in to_narrow]
    head_of_lane = np.arange(GROUP_WIDTH) // HEAD_DIM
    hmask = head_of_lane[None, None, :] == np.arange(HEADS_PER_GROUP)[:, None, None]
    in_spec = pl.BlockSpec((1, dil, 3, sub, GROUP_WIDTH), lambda b: (b, 0, 0, 0, 0))
    out_spec = pl.BlockSpec((1, 2, SLABS, sub * dil, LANES), lambda b: (b, 0, 0, 0, 0))
    out_sds = jax.ShapeDtypeStruct((batch, 2, SLABS, sub * dil, LANES), F32)
    outs = pl.pallas_call(
        functools.partial(_attn_kernel, g=g, dil=dil, n_blocks=n_blocks, unroll=ATTN_UNROLL,
                          n_cast=len(to_narrow)),
        out_shape=[out_sds] + [jax.ShapeDtypeStruct(w.shape, BF16) for w in to_narrow],
        grid=(batch,),
        in_specs=[in_spec, _resident((QBLK, 2 * QBLK)), _resident(hmask.shape), _SMEM] + cast_specs,
        out_specs=[out_spec] + cast_specs,
        scratch_shapes=[pltpu.VMEM((2, HEADS_PER_GROUP * QBLK, 2 * QBLK), F32)],
        compiler_params=_params(),
        name=f"attn_g{g}",
    )(qkv, jnp.asarray(_prompt_buckets(g)), jnp.asarray(hmask, BF16), table, *to_narrow)
    return outs[0], outs[1:]


def _sample_attn_block(refs):
    n = N_GROUPS
    qkv_refs, cache_refs, bucket_refs, table_ref = refs[0:n], refs[n:2 * n], refs[2 * n:3 * n], refs[3 * n]
    out_refs, win_refs, bias_refs = refs[3 * n + 1:4 * n + 1], refs[4 * n + 1:5 * n + 1], refs[5 * n + 1:6 * n + 1]

    @pl.when(pl.program_id(0) == 0)
    def _():
        for g in range(N_GROUPS):
            _build_bias(bias_refs[g], bucket_refs[g], table_ref, g)

    t_new = qkv_refs[0].shape[3]
    lane_head = lax.broadcasted_iota(jnp.int32, (t_new, GROUP_WIDTH), 1) // HEAD_DIM
    new_lane = lax.broadcasted_iota(jnp.int32, (GROUP_WIDTH, LANES), 1) >= LANES - t_new
    pad = jnp.zeros((LANES - t_new, GROUP_WIDTH), F32)

    for g in range(N_GROUPS):
        q, k, v = (qkv_refs[g][0, 0, which] for which in range(3))
        cache_ref, win_ref, bias_ref = cache_refs[g], win_refs[g], bias_refs[g]
        past = cache_ref.shape[3]
        new_rows = [jnp.concatenate([pad, rows_], axis=0) for rows_ in (k, v)]
        for c in range(2):
            shifted = pltpu.roll(cache_ref[0, c], past - t_new, 1)
            if past > LANES:
                win_ref[0, c, :, 0:past - LANES] = shifted[:, 0:past - LANES]
            win_ref[0, c, :, past - LANES:past] = jnp.where(new_lane, new_rows[c].T, shifted[:, past - LANES:])

        qs = _stack_heads(q, lane_head)
        s_old = jnp.dot(qs, cache_ref[0, 0].astype(BF16), preferred_element_type=F32) + bias_ref[:, 0:past]
        s_new = lax.dot_general(qs, new_rows[0].astype(BF16), NT_DIMS,
                                preferred_element_type=F32) + bias_ref[:, past:past + LANES]
        m = jnp.maximum(jnp.max(s_old, axis=-1, keepdims=True), jnp.max(s_new, axis=-1, keepdims=True))
        p_old = jnp.exp2(s_old - m)
        p_new = jnp.exp2(s_new - m)
        l = jnp.sum(p_old, axis=-1, keepdims=True) + jnp.sum(p_new, axis=-1, keepdims=True)
        pv = (lax.dot_general(p_old.astype(BF16), cache_ref[0, 1].astype(BF16), NT_DIMS,
                              preferred_element_type=F32)
              + jnp.dot(p_new.astype(BF16), new_rows[1].astype(BF16), preferred_element_type=F32)) / l
        lse = jnp.broadcast_to(m + jnp.log2(l), pv.shape)
        rows = lambda a: [a[h * t_new:(h + 1) * t_new] for h in range(HEADS_PER_GROUP)]
        for which, val in enumerate((_head_select(rows(pv), lane_head), _head_select(rows(lse), lane_head))):
            for s_ in range(SLABS):
                out_refs[g][0, which, s_] = val[:, s_ * LANES:(s_ + 1) * LANES]


def _sample_attn_kernel(*refs):
    _sample_attn_block(refs)


def _sample_buckets(g, past, t_new):
    win, dil = ATTN_GROUPS[g]
    lane = np.arange(past + LANES)[None, :]
    key_pos = np.where(lane < past, lane, lane - (LANES - t_new))
    delta = past + np.arange(t_new)[:, None] - key_pos
    valid = ((lane < past) | (lane >= past + LANES - t_new)) & (delta >= 0) & (delta % dil == 0) & (delta <= win)
    return np.where(valid, _t5_buckets(np.clip(delta, 0, win)), -1).astype(np.int32)


def _sample_attention_parts(qkv, caches, table):
    batch = caches[0].shape[0]
    t_new = qkv[0].shape[3] // batch
    pasts = [c.shape[3] for c in caches]
    buckets = [jnp.asarray(_sample_buckets(g, pasts[g], t_new)) for g in range(N_GROUPS)]
    cache_spec = lambda p: pl.BlockSpec((1, 2, GROUP_WIDTH, p), lambda b: (b, 0, 0, 0))
    new_spec = pl.BlockSpec((1, 1, 3, t_new, GROUP_WIDTH), lambda b: (0, 0, 0, b, 0))
    out_spec = pl.BlockSpec((1, 2, SLABS, t_new, LANES), lambda b: (0, 0, 0, b, 0))
    out_sds = jax.ShapeDtypeStruct((1, 2, SLABS, batch * t_new, LANES), F32)
    return dict(
        batch=batch,
        args=[*qkv, *caches, *buckets, table],
        out_shape=[out_sds] * N_GROUPS + [jax.ShapeDtypeStruct(c.shape, F32) for c in caches],
        in_specs=[new_spec] * N_GROUPS + [cache_spec(p) for p in pasts] + [_resident(b.shape) for b in buckets]
        + [_SMEM],
        out_specs=[out_spec] * N_GROUPS + [cache_spec(p) for p in pasts],
        scratch_shapes=[pltpu.VMEM((HEADS_PER_GROUP * t_new, p + LANES), F32) for p in pasts])


def _sample_attention(parts):
    return pl.pallas_call(
        _sample_attn_kernel, grid=(parts["batch"],), out_shape=parts["out_shape"],
        in_specs=parts["in_specs"], out_specs=parts["out_specs"], scratch_shapes=parts["scratch_shapes"],
        compiler_params=_params(), name="sample_attn")(*parts["args"])


def _back_kernel(x_ref, ol0_ref, ol1_ref, ol2_ref, u_ref, halo_ref, gate_ref,
                 wab_ref, wpg_ref, pscale_ref, wpb_ref, wout_ref, g2_ref, wup_ref, wdn_ref,
                 out_ref, ue_ref, level_ref, mixed_ref, act_ref, attn_ref, pooled_ref, merged_ref,
                 *, pos_base, blocks_per_seq):
    block = pl.program_id(0)
    tm, d_model = x_ref.shape
    n_seq = halo_ref.shape[0]
    t = tm // n_seq
    pool_width = u_ref.shape[1]
    gw = pool_width // len(POOL_WINDOWS)

    for s in range(SLABS):
        ol_refs = (ol0_ref, ol1_ref, ol2_ref)
        lses = [r[0, 1, s] for r in ol_refs]
        top = jnp.maximum(jnp.maximum(lses[0], lses[1]), lses[2])
        num = jnp.zeros_like(top)
        den = jnp.zeros_like(top)
        for ol_ref, lse in zip(ol_refs, lses):
            e = jnp.exp2(lse - top)
            num = num + e * ol_ref[0, 0, s]
            den = den + e
        attn_ref[:, s * LANES:(s + 1) * LANES] = (num / den).astype(BF16)

    start = pos_base + (block % blocks_per_seq) * t
    first, end = POOL_PAD - POOL_HALO, POOL_PAD + t
    ue_ref[:, 0:first, :] = jnp.zeros((n_seq, first, pool_width), F32)
    ue_ref[:, first:POOL_PAD, :] = jnp.where(start > 0, halo_ref[...], 0.0)
    ue_ref[:, POOL_PAD:, :] = u_ref[...].reshape(n_seq, t, pool_width)
    level_ref[:, :, 0:first, :] = jnp.zeros((2, n_seq, first, gw), F32)
    pos = start + lax.broadcasted_iota(jnp.int32, (1, t, gw), 1)
    for gi, win in enumerate(POOL_WINDOWS):
        cols = slice(gi * gw, (gi + 1) * gw)
        read = lambda lo, hi, cols=cols: ue_ref[:, lo:hi, cols]
        span = 1
        while 2 * span < win:
            buf = (span.bit_length() - 1) % 2
            level_ref[buf, :, first:, :] = read(first, end) + read(first - span, end - span)
            read = lambda lo, hi, buf=buf: level_ref[buf, :, lo:hi, :]
            span *= 2
        s = read(POOL_PAD, end) + read(POOL_PAD - span, end - span)
        cnt = jnp.minimum(pos + 1, win).astype(F32)
        d = (s / cnt - ue_ref[:, POOL_PAD:, cols]).reshape(tm, gw)
        y = jnp.dot(d.astype(BF16), wpg_ref[gi], preferred_element_type=F32)
        pooled_ref[:, cols] = (y * pscale_ref[:, cols]).astype(BF16)

    for lo in range(0, d_model, MXU_TILE):
        cols = slice(lo, lo + MXU_TILE)
        branch_a = jnp.dot(attn_ref[...], wab_ref[:, cols], preferred_element_type=F32)
        branch_b = jnp.dot(pooled_ref[...], wpb_ref[:, cols], preferred_element_type=F32)
        gate_a = gate_ref[:, cols].astype(F32)
        gate_b = gate_ref[:, d_model + lo:d_model + lo + MXU_TILE].astype(F32)
        merged_ref[:, cols] = (gate_a * branch_a + gate_b * branch_b).astype(BF16)
    for lo in range(0, d_model, MXU_TILE):
        cols = slice(lo, lo + MXU_TILE)
        mixed_ref[:, cols] = x_ref[:, cols] + jnp.dot(merged_ref[...], wout_ref[:, cols],
                                                      preferred_element_type=F32)

    _ffn_block(mixed_ref, g2_ref, wup_ref, wdn_ref, out_ref, act_ref)


def _back(x, ol, u, halo, gates, merge_w, ffn_w, *, tm, halo_block, halo_index, pos_base, blocks_per_seq):
    m, d = x.shape
    pool_width = u.shape[1]
    n_seq = halo_block[0]
    d_ff = ffn_w[2].shape[0]
    row = lambda wd: pl.BlockSpec((tm, wd), lambda i: (i, 0))
    slab = pl.BlockSpec((1, 2, SLABS, tm, LANES),
                        lambda i: (i // blocks_per_seq, 0, 0, i % blocks_per_seq, 0))
    return pl.pallas_call(
        functools.partial(_back_kernel, pos_base=pos_base, blocks_per_seq=blocks_per_seq),
        out_shape=jax.ShapeDtypeStruct((m, d), F32),
        grid=(m // tm,),
        in_specs=[row(d)] + [slab] * N_GROUPS
        + [row(pool_width), pl.BlockSpec(halo_block, halo_index), row(2 * d)]
        + [_resident(a.shape) for a in (*merge_w, *ffn_w)],
        out_specs=row(d),
        scratch_shapes=[pltpu.VMEM((n_seq, POOL_PAD + tm // n_seq, pool_width), F32),
                        pltpu.VMEM((2, n_seq, POOL_PAD + tm // n_seq, pool_width // len(POOL_WINDOWS)), F32),
                        pltpu.VMEM((tm, d), F32), pltpu.VMEM((tm, d_ff), BF16),
                        pltpu.VMEM((tm, GROUP_WIDTH), BF16), pltpu.VMEM((tm, pool_width), BF16),
                        pltpu.VMEM((tm, d), BF16)],
        compiler_params=_params(),
        name="back",
    )(x, *ol, u, halo, gates, *merge_w, *ffn_w)


def _window_in(cache):
    n_seq, rows = cache.shape[0:2]
    return jnp.transpose(cache, (0, 2, 3, 4, 1)).reshape(n_seq, 2, GROUP_WIDTH, rows)


def _window_out(kv):
    n_seq, _, _, rows = kv.shape
    return jnp.transpose(kv.reshape(n_seq, 2, HEADS_PER_GROUP, HEAD_DIM, rows), (0, 4, 1, 2, 3))


def kernel(x_prompt, x_sample, cache_kv_w128, cache_kv_w512, cache_kv_w2048, state_pool, rel_bias_table,
           norm_ffn1, ffn1_w_up, ffn1_w_down, norm_mix, w_in, q_norm, k_norm, pool_w_group, pool_scale,
           w_attn_branch, w_pool_branch, w_out, norm_ffn2, ffn2_w_up, ffn2_w_down):
    batch, seq, d_model = x_prompt.shape
    dec_batch, dec_seq, _ = x_sample.shape
    depth = norm_ffn1.shape[0]
    pool_width = state_pool.shape[-1]
    tm = ROW_BLOCK
    ms = dec_batch * dec_seq
    seg = jnp.asarray(np.kron(np.eye(HEADS_PER_GROUP), np.full((HEAD_DIM, HEAD_DIM), 1.0 / HEAD_DIM)), BF16)
    table = rel_bias_table.astype(F32)
    caches_in = (cache_kv_w128, cache_kv_w512, cache_kv_w2048)
    keep_prompt = tuple(min(win, seq) for win, _ in ATTN_GROUPS)
    dils = tuple(dil for _, dil in ATTN_GROUPS)

    xp = x_prompt.reshape(batch * seq, d_model)
    xs = x_sample.reshape(ms, d_model)
    kv_p, kv_s, pool_p, pool_s = ([], [], []), ([], [], []), [], []
    for layer in range(depth):
        gain = lambda a: a[layer].reshape(1, -1).astype(F32)
        qg, kg = gain(q_norm), gain(k_norm)
        pool_groups = pool_w_group[layer]
        late_w = (ffn2_w_down[layer], w_attn_branch[layer], pool_groups.reshape(-1, pool_groups.shape[-1]),
                  w_pool_branch[layer], w_out[layer])

        xs, (wup1, wdn1) = _ffn(xs, gain(norm_ffn1), ffn1_w_up[layer], ffn1_w_down[layer], ms, narrow=True)
        outs = _proj(xs, gain(norm_mix), w_in[layer], qg, kg, seg, tm=ms, n_seq=1, dils=(1,) * N_GROUPS,
                     qkv_dtype=F32, keep_rows=None, narrow=True)
        qkv_s, u_s, gates_s, win = outs[0:N_GROUPS], outs[N_GROUPS], outs[N_GROUPS + 1], outs[N_GROUPS + 2]
        sample_parts = _sample_attention_parts(qkv_s, [_window_in(c[layer]) for c in caches_in], table)

        if (batch * seq) // tm == dec_batch:
            xp, sample_outs = _ffn(xp, gain(norm_ffn1), wup1, wdn1, tm, rider=sample_parts)
        else:
            xp, _ = _ffn(xp, gain(norm_ffn1), wup1, wdn1, tm)
            sample_outs = _sample_attention(sample_parts)
        outs = _proj(xp, gain(norm_mix), win, qg, kg, seg, tm=2 * tm, n_seq=batch, dils=dils,
                     qkv_dtype=BF16, keep_rows=keep_prompt)
        qkv, u, gates = outs[0:N_GROUPS], outs[N_GROUPS], outs[N_GROUPS + 1]
        windows, pstate = outs[N_GROUPS + 2:2 * N_GROUPS + 2], outs[2 * N_GROUPS + 2]
        narrowing = ((ffn2_w_up[layer],), late_w) + ((),) * (N_GROUPS - 2)
        ol, narrowed = zip(*[_prompt_attention(qkv[g], table, g, to_narrow=narrowing[g])
                             for g in range(N_GROUPS)])
        (wup2,), (wdn2, wab, wpg, wpb, wout) = narrowed[0:2]
        ffn2_w = (gain(norm_ffn2), wup2, wdn2)
        merge_w = (wab, wpg.reshape(pool_groups.shape), gain(pool_scale), wpb, wout)
        xp = _back(xp, ol, u, u.reshape(-1, POOL_HALO, pool_width), gates, merge_w, ffn2_w, tm=tm,
                   halo_block=(1, POOL_HALO, pool_width),
                   halo_index=lambda blk: (jnp.maximum(blk * (tm // POOL_HALO) - 1, 0), 0, 0),
                   pos_base=0, blocks_per_seq=seq // tm)
        for g in range(N_GROUPS):
            kv_p[g].append(_window_out(windows[g]))
        pool_p.append(pstate[:, POOL_HALO - POOL_STATE:])

        ol, windows = sample_outs[0:N_GROUPS], sample_outs[N_GROUPS:2 * N_GROUPS]
        history = jnp.concatenate(
            [jnp.zeros((dec_batch, POOL_HALO - POOL_STATE, pool_width), F32), state_pool[layer]], axis=1)
        xs = _back(xs, ol, u_s, history, gates_s, merge_w, ffn2_w, tm=ms,
                   halo_block=(dec_batch, POOL_HALO, pool_width), halo_index=lambda blk: (0, 0, 0),
                   pos_base=PAST_LEN, blocks_per_seq=1)
        for g in range(N_GROUPS):
            kv_s[g].append(_window_out(windows[g]))
        ue = jnp.concatenate([state_pool[layer], u_s.reshape(dec_batch, dec_seq, pool_width)], axis=1)
        pool_s.append(ue[:, ue.shape[1] - POOL_STATE:])

    stack = lambda xs_: jnp.stack(xs_, axis=0)
    return (xp.reshape(batch, seq, d_model), xs.reshape(dec_batch, dec_seq, d_model),
            stack(kv_p[0]), stack(kv_p[1]), stack(kv_p[2]), stack(pool_p),
            stack(kv_s[0]), stack(kv_s[1]), stack(kv_s[2]), stack(pool_s))
```

```python
import functools
import math

import numpy as np
import jax
import jax.numpy as jnp
from jax import lax
from jax.experimental import pallas as pl
from jax.experimental.pallas import tpu as pltpu

HEAD_DIM = 64
HEADS_PER_GROUP = 4
GROUP_WIDTH = HEADS_PER_GROUP * HEAD_DIM
ATTN_GROUPS = ((128, 1), (512, 4), (2048, 16))
N_GROUPS = len(ATTN_GROUPS)
ATTN_WIDTH = N_GROUPS * GROUP_WIDTH
N_BUCKETS = 32
MAX_DISTANCE = 2048
POOL_WINDOWS = (2, 4, 8, 16)
POOL_STATE = max(POOL_WINDOWS) - 1
POOL_HALO = 16
POOL_PAD = POOL_HALO + 8
EPS = 1e-6
PAST_LEN = 8192
NEG = -1e30
LOG2E = math.log2(math.e)
LANES = 128
MXU_TILE = 256
SLABS = GROUP_WIDTH // LANES
SINGLE_OP_STRIDE = 4
RIDER_PHASE_GAP = 2
ROW_BLOCK = 512
QBLK = 128
ATTN_UNROLL = 16
VMEM_LIMIT = 58 * 1024 * 1024

F32 = jnp.float32
BF16 = jnp.bfloat16
NT_DIMS = (((1,), (1,)), ((), ()))


def _t5_buckets(distance):
    max_exact = N_BUCKETS // 2
    d = np.asarray(distance, dtype=np.int32)
    ratio = np.log(np.maximum(d, 1).astype(np.float32) / np.float32(max_exact))
    large = max_exact + (ratio / np.float32(math.log(MAX_DISTANCE / max_exact))
                         * (N_BUCKETS - max_exact)).astype(np.int32)
    large = np.minimum(large, N_BUCKETS - 1)
    return np.where(d < max_exact, d, large).astype(np.int32)


def _params(n_grid_dims=1):
    return pltpu.CompilerParams(dimension_semantics=("arbitrary",) * n_grid_dims,
                                vmem_limit_bytes=VMEM_LIMIT)


def _resident(shape):
    return pl.BlockSpec(shape, lambda *_: (0,) * len(shape), pipeline_mode=pl.Buffered(1))


_SMEM = pl.BlockSpec(memory_space=pltpu.SMEM)


def _rmsnorm(x, g):
    ms = jnp.mean(x * x, axis=-1, keepdims=True)
    return x * lax.rsqrt(ms + EPS) * g


def _head_select(parts, lane_head):
    out = jnp.where(lane_head == 0, parts[0], 0.0)
    for h in range(1, HEADS_PER_GROUP):
        out = jnp.where(lane_head == h, parts[h], out)
    return out


def _stack_heads(q, lane_head):
    return jnp.concatenate([jnp.where(lane_head == h, q, 0.0)
                            for h in range(HEADS_PER_GROUP)], axis=0).astype(BF16)


def _build_bias(bias_ref, bucket_ref, table_ref, g):
    buckets = bucket_ref[...]
    rows = buckets.shape[0]
    for h in range(HEADS_PER_GROUP):
        acc = jnp.full(buckets.shape, NEG, F32)
        for b in range(N_BUCKETS):
            acc = jnp.where(buckets == b, table_ref[b, g * HEADS_PER_GROUP + h] * LOG2E, acc)
        bias_ref[h * rows:(h + 1) * rows, :] = acc


def _matmul_weight(w_ref, narrow_ref, rows, cols):
    w = w_ref[rows, cols]
    if narrow_ref is not None:
        w = w.astype(BF16)
        narrow_ref[rows, cols] = w
    return w


def _ffn_block(x_ref, g_ref, wup_ref, wdn_ref, o_ref, act_ref, narrow_refs=(None, None), rider_phases=()):
    d_ff, d = wdn_ref.shape
    every = slice(None)
    rider_phases = iter(rider_phases)
    next(rider_phases, None)
    h = _rmsnorm(x_ref[...], g_ref[...]).astype(BF16)
    for chunk, lo in enumerate(range(0, d_ff, MXU_TILE)):
        w_gate = _matmul_weight(wup_ref, narrow_refs[0], every, slice(lo, lo + MXU_TILE))
        w_up = _matmul_weight(wup_ref, narrow_refs[0], every, slice(d_ff + lo, d_ff + lo + MXU_TILE))
        gate = jnp.dot(h, w_gate, preferred_element_type=F32)
        up = jnp.dot(h, w_up, preferred_element_type=F32)
        if chunk % RIDER_PHASE_GAP == RIDER_PHASE_GAP - 1:
            next(rider_phases, None)
        act_ref[:, lo:lo + MXU_TILE] = (gate * jax.nn.sigmoid(gate) * up).astype(BF16)
    for lo in range(0, d, MXU_TILE):
        cols = slice(lo, lo + MXU_TILE)
        w_down = _matmul_weight(wdn_ref, narrow_refs[1], every, cols)
        o_ref[:, cols] = x_ref[:, cols] + 0.5 * jnp.dot(act_ref[...], w_down, preferred_element_type=F32)


def _ffn_kernel(*refs, rider_counts, narrow):
    n_in, n_out = rider_counts
    n_narrow = 2 if narrow else 0
    x_ref, g_ref, wup_ref, wdn_ref = refs[0:4]
    o_ref = refs[4 + n_in]
    narrow_refs = refs[5 + n_in:5 + n_in + n_narrow] if narrow else (None, None)
    first_rider_out = 5 + n_in + n_narrow
    act_ref = refs[first_rider_out + n_out]
    rider_phases = ()
    if n_in:
        rider_phases = _sample_attn_phases((*refs[4:4 + n_in], *refs[first_rider_out:first_rider_out + n_out],
                                            *refs[first_rider_out + n_out + 1:]))
    _ffn_block(x_ref, g_ref, wup_ref, wdn_ref, o_ref, act_ref, narrow_refs, rider_phases)


def _ffn(x, gain, w_up, w_down, tm, rider=None, narrow=False):
    m, d = x.shape
    d_ff = w_down.shape[0]
    assert m % tm == 0 and d_ff % MXU_TILE == 0 and d % MXU_TILE == 0
    assert not narrow or m == tm, "each weight chunk is narrowed once only on a single-step grid"
    rider = rider or dict(args=[], in_specs=[], out_shape=[], out_specs=[], scratch_shapes=[], batch=m // tm)
    assert rider["batch"] == m // tm
    weights = (w_up, w_down)
    narrowed = [jax.ShapeDtypeStruct(w.shape, BF16) for w in weights] if narrow else []
    outs = pl.pallas_call(
        functools.partial(_ffn_kernel, rider_counts=(len(rider["args"]), len(rider["out_shape"])), narrow=narrow),
        out_shape=[jax.ShapeDtypeStruct((m, d), F32)] + narrowed + rider["out_shape"],
        grid=(m // tm,),
        in_specs=[pl.BlockSpec((tm, d), lambda i: (i, 0)),
                  _resident((1, d)), _resident((d, 2 * d_ff)), _resident((d_ff, d))] + rider["in_specs"],
        out_specs=[pl.BlockSpec((tm, d), lambda i: (i, 0))] + [_resident(w.shape) for w in narrowed]
        + rider["out_specs"],
        scratch_shapes=[pltpu.VMEM((tm, d_ff), BF16)] + rider["scratch_shapes"],
        compiler_params=_params(),
        name="ffn",
    )(x, gain, w_up, w_down, *rider["args"])
    return outs[0], outs[1:]


def _proj_kernel(x_ref, g_ref, win_ref, qn_ref, kn_ref, seg_ref, *refs,
                 dils, kv_rows, with_state, narrow, pool_width, d_model):
    qkv_refs = refs[0:N_GROUPS]
    u_ref, gate_ref = refs[N_GROUPS:N_GROUPS + 2]
    state_refs = refs[N_GROUPS + 2:]
    narrow_ref = refs[-3] if narrow else None
    slab_ref, regroup_ref = refs[-2:]
    tm = x_ref.shape[0]
    h = _rmsnorm(x_ref[...], g_ref[...]).astype(BF16)

    def proj(lo, width):
        w = _matmul_weight(win_ref, narrow_ref, slice(None), slice(lo, lo + width))
        return jnp.dot(h, w, preferred_element_type=F32)

    def head_norm(y, gain):
        ms = jnp.dot((y * y).astype(BF16), seg_ref[...], preferred_element_type=F32)
        return y * lax.rsqrt(ms + EPS) * gain

    def emit(dst_ref, which, slot, val, dil):
        if dil == 1:
            dst_ref[0, 0, which] = val.astype(dst_ref.dtype)
            return
        for s in range(SLABS):
            slab_ref[slot, s] = val[:, s * LANES:(s + 1) * LANES]
        n = tm // dil
        if dil <= SINGLE_OP_STRIDE:
            for r in range(dil):
                parts = [slab_ref[slot, s, pl.ds(r, n, stride=dil), :] for s in range(SLABS)]
                dst_ref[0, r, which] = jnp.concatenate(parts, axis=-1).astype(dst_ref.dtype)
            return
        outer = SINGLE_OP_STRIDE
        inner = dil // outer
        assert inner <= SINGLE_OP_STRIDE
        quarter = tm // outer
        for a in range(outer):
            for s in range(SLABS):
                regroup_ref[s, a * quarter:(a + 1) * quarter, :] = slab_ref[slot, s, pl.ds(a, quarter, stride=outer), :]
        for r in range(dil):
            start = (r % outer) * quarter + r // outer
            parts = [regroup_ref[s, pl.ds(start, n, stride=inner), :] for s in range(SLABS)]
            dst_ref[0, r, which] = jnp.concatenate(parts, axis=-1).astype(dst_ref.dtype)

    def finish(g, which, y):
        cols = slice(g * GROUP_WIDTH, (g + 1) * GROUP_WIDTH)
        if which == 0:
            y = head_norm(y, qn_ref[:, cols]) * (HEAD_DIM ** -0.5 * LOG2E)
        elif which == 1:
            y = head_norm(y, kn_ref[:, cols])
        emit(qkv_refs[g], which, 3 * sum(d > 1 for d in dils[:g]) + which, y, dils[g])
        if with_state and which > 0:
            state_refs[g][0, which - 1] = y.T[:, tm - kv_rows[g]:]

    pending = None
    for g in range(N_GROUPS):
        for which in range(3):
            y = proj(which * ATTN_WIDTH + g * GROUP_WIDTH, GROUP_WIDTH)
            if pending is not None:
                finish(*pending)
            pending = (g, which, y)
    u = proj(3 * ATTN_WIDTH, pool_width)
    finish(*pending)
    u_ref[...] = u
    if with_state:
        state_refs[N_GROUPS][0] = u[tm - POOL_HALO:, :]
    gates_base = 3 * ATTN_WIDTH + pool_width
    for lo in range(0, 2 * d_model, MXU_TILE):
        gate_ref[:, lo:lo + MXU_TILE] = jax.nn.sigmoid(proj(gates_base + lo, MXU_TILE)).astype(BF16)


def _proj(x, gain, w_in, q_gain, k_gain, seg, *, tm, n_seq, dils, qkv_dtype, keep_rows, narrow=False):
    m, d = x.shape
    pool_width = w_in.shape[1] - 3 * ATTN_WIDTH - 2 * d
    seq_len = m // n_seq
    assert seq_len % tm == 0 and all(tm % dil == 0 for dil in dils)
    blocks_per_seq = seq_len // tm
    with_state = keep_rows is not None
    seq_block = lambda i: (i // blocks_per_seq, i % blocks_per_seq)

    row = lambda w: pl.BlockSpec((tm, w), lambda i: (i, 0))
    out_shape, out_specs = [], []
    for g in range(N_GROUPS):
        out_shape.append(jax.ShapeDtypeStruct((n_seq, dils[g], 3, seq_len // dils[g], GROUP_WIDTH), qkv_dtype))
        out_specs.append(pl.BlockSpec((1, dils[g], 3, tm // dils[g], GROUP_WIDTH),
                                      lambda i: (seq_block(i)[0], 0, 0, seq_block(i)[1], 0)))
    out_shape += [jax.ShapeDtypeStruct((m, pool_width), F32), jax.ShapeDtypeStruct((m, 2 * d), BF16)]
    out_specs += [row(pool_width), row(2 * d)]
    kv_rows = kv_first = ()
    if with_state:
        kv_rows = tuple(min(r, tm) for r in keep_rows)
        kv_first = tuple(blocks_per_seq - keep_rows[g] // kv_rows[g] for g in range(N_GROUPS))
        for g in range(N_GROUPS):
            out_shape.append(jax.ShapeDtypeStruct((n_seq, 2, GROUP_WIDTH, keep_rows[g]), F32))
            out_specs.append(pl.BlockSpec(
                (1, 2, GROUP_WIDTH, kv_rows[g]),
                lambda i, first=kv_first[g]: (seq_block(i)[0], 0, 0, jnp.maximum(seq_block(i)[1] - first, 0))))
        out_shape.append(jax.ShapeDtypeStruct((n_seq, POOL_HALO, pool_width), F32))
        out_specs.append(pl.BlockSpec((1, POOL_HALO, pool_width), lambda i: (seq_block(i)[0], 0, 0)))
    if narrow:
        assert m == tm, "each weight chunk is narrowed once only on a single-step grid"
        out_shape.append(jax.ShapeDtypeStruct(w_in.shape, BF16))
        out_specs.append(_resident(w_in.shape))
    return pl.pallas_call(
        functools.partial(_proj_kernel, dils=dils, kv_rows=kv_rows, with_state=with_state, narrow=narrow,
                          pool_width=pool_width, d_model=d),
        out_shape=out_shape,
        grid=(m // tm,),
        in_specs=[row(d), _resident((1, d)), _resident(w_in.shape),
                  _resident((1, ATTN_WIDTH)), _resident((1, ATTN_WIDTH)),
                  _resident((GROUP_WIDTH, GROUP_WIDTH))],
        out_specs=out_specs,
        scratch_shapes=[pltpu.VMEM((max(1, 3 * sum(dil > 1 for dil in dils)), SLABS, tm, LANES), F32),
                        pltpu.VMEM((SLABS, tm, LANES), F32)],
        compiler_params=_params(),
        name="proj",
    )(x, gain, w_in, q_gain, k_gain, seg)


def _attn_kernel(*refs, g, dil, n_blocks, unroll, n_cast):
    qkv_ref, bucket_ref, hmask_ref, table_ref = refs[0:4]
    wide_refs = refs[4:4 + n_cast]
    out_ref = refs[4 + n_cast]
    narrow_refs = refs[5 + n_cast:5 + 2 * n_cast]
    bias_ref = refs[5 + 2 * n_cast]
    for wide_ref, narrow_ref in zip(wide_refs, narrow_refs):
        narrow_ref[...] = wide_ref[...].astype(BF16)

    @pl.when(pl.program_id(0) == 0)
    def _():
        _build_bias(bias_ref.at[1], bucket_ref, table_ref, g)
        bias_ref[0, :, 0:QBLK] = bias_ref[1, :, QBLK:2 * QBLK]
        bias_ref[0, :, QBLK:2 * QBLK] = jnp.full((HEADS_PER_GROUP * QBLK, QBLK), NEG, F32)

    first_head = lax.broadcasted_iota(jnp.int32, (QBLK, LANES), 1) < HEAD_DIM
    heads = range(HEADS_PER_GROUP)

    def per_slab(col):
        part = lambda h: jnp.broadcast_to(col[h * QBLK:(h + 1) * QBLK], (QBLK, LANES))
        return [jnp.where(first_head, part(2 * s), part(2 * s + 1)) for s in range(SLABS)]

    def body(n, carry):
        res = n // n_blocks
        i = n % n_blocks
        row0 = pl.multiple_of(i * QBLK, QBLK)
        span0 = pl.multiple_of(jnp.maximum(i - 1, 0) * QBLK, QBLK)
        q = qkv_ref[0, res, 0, pl.ds(row0, QBLK), :]
        qs = jnp.concatenate([q * hmask_ref[h] for h in heads], axis=0)
        s = lax.dot_general(qs, qkv_ref[0, res, 1, pl.ds(span0, 2 * QBLK), :], NT_DIMS,
                            preferred_element_type=F32) + bias_ref[jnp.minimum(i, 1)]
        m = jnp.max(s, axis=-1, keepdims=True)
        p = jnp.exp2(s - m)
        l = jnp.sum(p, axis=-1, keepdims=True)
        p = p.astype(BF16)
        v = qkv_ref[0, res, 2, pl.ds(span0, 2 * QBLK), :]
        p_wide = jnp.concatenate([p[h * QBLK:(h + 1) * QBLK] for h in heads], axis=1)
        v_tall = jnp.concatenate([v * hmask_ref[h] for h in heads], axis=0)
        pv = jnp.dot(p_wide, v_tall, preferred_element_type=F32)
        token0 = res + dil * row0
        dst = pl.ds(token0, QBLK) if dil == 1 else pl.ds(token0, QBLK, stride=dil)
        for s_, (m_s, l_s) in enumerate(zip(per_slab(m), per_slab(l))):
            out_ref[0, 0, s_, dst, :] = pv[:, s_ * LANES:(s_ + 1) * LANES] / l_s
            out_ref[0, 1, s_, dst, :] = m_s + jnp.log2(l_s)
        return carry

    lax.fori_loop(0, dil * n_blocks, body, 0, unroll=unroll)


def _prompt_buckets(g):
    win, dil = ATTN_GROUPS[g]
    delta = np.arange(QBLK)[:, None] + QBLK - np.arange(2 * QBLK)[None, :]
    valid = (delta >= 0) & (delta <= win // dil)
    return np.where(valid, _t5_buckets(dil * np.clip(delta, 0, win // dil)), -1).astype(np.int32)


def _prompt_attention(qkv, table, g, to_narrow=()):
    _, dil = ATTN_GROUPS[g]
    batch, _, _, sub, _ = qkv.shape
    n_blocks = sub // QBLK
    assert sub % QBLK == 0 and n_blocks >= 2 and (dil * n_blocks) % ATTN_UNROLL == 0
    assert all(w.shape[0] % (16 * batch) == 0 for w in to_narrow)
    cast_specs = [pl.BlockSpec((w.shape[0] // batch, w.shape[1]), lambda b: (b, 0)) for w in to_narrow]
    head_of_lane = np.arange(GROUP_WIDTH) // HEAD_DIM
    hmask = head_of_lane[None, None, :] == np.arange(HEADS_PER_GROUP)[:, None, None]
    in_spec = pl.BlockSpec((1, dil, 3, sub, GROUP_WIDTH), lambda b: (b, 0, 0, 0, 0))
    out_spec = pl.BlockSpec((1, 2, SLABS, sub * dil, LANES), lambda b: (b, 0, 0, 0, 0))
    out_sds = jax.ShapeDtypeStruct((batch, 2, SLABS, sub * dil, LANES), F32)
    outs = pl.pallas_call(
        functools.partial(_attn_kernel, g=g, dil=dil, n_blocks=n_blocks, unroll=ATTN_UNROLL,
                          n_cast=len(to_narrow)),
        out_shape=[out_sds] + [jax.ShapeDtypeStruct(w.shape, BF16) for w in to_narrow],
        grid=(batch,),
        in_specs=[in_spec, _resident((QBLK, 2 * QBLK)), _resident(hmask.shape), _SMEM] + cast_specs,
        out_specs=[out_spec] + cast_specs,
        scratch_shapes=[pltpu.VMEM((2, HEADS_PER_GROUP * QBLK, 2 * QBLK), F32)],
        compiler_params=_params(),
        name=f"attn_g{g}",
    )(qkv, jnp.asarray(_prompt_buckets(g)), jnp.asarray(hmask, BF16), table, *to_narrow)
    return outs[0], outs[1:]


def _sample_attn_phases(refs):
    n = N_GROUPS
    qkv_refs, cache_refs, bucket_refs, table_ref = refs[0:n], refs[n:2 * n], refs[2 * n:3 * n], refs[3 * n]
    out_refs, win_refs, bias_refs = refs[3 * n + 1:4 * n + 1], refs[4 * n + 1:5 * n + 1], refs[5 * n + 1:6 * n + 1]

    @pl.when(pl.program_id(0) == 0)
    def _():
        for g in range(N_GROUPS):
            _build_bias(bias_refs[g], bucket_refs[g], table_ref, g)

    t_new = qkv_refs[0].shape[3]
    lane_head = lax.broadcasted_iota(jnp.int32, (t_new, GROUP_WIDTH), 1) // HEAD_DIM
    new_lane = lax.broadcasted_iota(jnp.int32, (GROUP_WIDTH, LANES), 1) >= LANES - t_new
    pad = jnp.zeros((LANES - t_new, GROUP_WIDTH), F32)

    groups = []
    for g in range(N_GROUPS):
        q, k, v = (qkv_refs[g][0, 0, which] for which in range(3))
        cache_ref, win_ref = cache_refs[g], win_refs[g]
        past = cache_ref.shape[3]
        new_rows = [jnp.concatenate([pad, rows_], axis=0) for rows_ in (k, v)]
        for c in range(2):
            shifted = pltpu.roll(cache_ref[0, c], past - t_new, 1)
            if past > LANES:
                win_ref[0, c, :, 0:past - LANES] = shifted[:, 0:past - LANES]
            win_ref[0, c, :, past - LANES:past] = jnp.where(new_lane, new_rows[c].T, shifted[:, past - LANES:])
        groups.append(dict(past=past, qs=_stack_heads(q, lane_head),
                           old=[cache_ref[0, c].astype(BF16) for c in range(2)],
                           new=[rows_.astype(BF16) for rows_ in new_rows]))
    yield

    for g, grp in enumerate(groups):
        past, bias_ref = grp["past"], bias_refs[g]
        s_old = jnp.dot(grp["qs"], grp["old"][0], preferred_element_type=F32) + bias_ref[:, 0:past]
        s_new = lax.dot_general(grp["qs"], grp["new"][0], NT_DIMS,
                                preferred_element_type=F32) + bias_ref[:, past:past + LANES]
        m = jnp.maximum(jnp.max(s_old, axis=-1, keepdims=True), jnp.max(s_new, axis=-1, keepdims=True))
        p_old = jnp.exp2(s_old - m)
        p_new = jnp.exp2(s_new - m)
        grp.update(m=m, l=jnp.sum(p_old, axis=-1, keepdims=True) + jnp.sum(p_new, axis=-1, keepdims=True),
                   p_old=p_old.astype(BF16), p_new=p_new.astype(BF16))
    yield

    for g, grp in enumerate(groups):
        pv = (lax.dot_general(grp["p_old"], grp["old"][1], NT_DIMS, preferred_element_type=F32)
              + jnp.dot(grp["p_new"], grp["new"][1], preferred_element_type=F32)) / grp["l"]
        lse = jnp.broadcast_to(grp["m"] + jnp.log2(grp["l"]), pv.shape)
        rows = lambda a: [a[h * t_new:(h + 1) * t_new] for h in range(HEADS_PER_GROUP)]
        for which, val in enumerate((_head_select(rows(pv), lane_head), _head_select(rows(lse), lane_head))):
            for s_ in range(SLABS):
                out_refs[g][0, which, s_] = val[:, s_ * LANES:(s_ + 1) * LANES]


def _sample_attn_kernel(*refs):
    for _ in _sample_attn_phases(refs):
        pass


def _sample_buckets(g, past, t_new):
    win, dil = ATTN_GROUPS[g]
    lane = np.arange(past + LANES)[None, :]
    key_pos = np.where(lane < past, lane, lane - (LANES - t_new))
    delta = past + np.arange(t_new)[:, None] - key_pos
    valid = ((lane < past) | (lane >= past + LANES - t_new)) & (delta >= 0) & (delta % dil == 0) & (delta <= win)
    return np.where(valid, _t5_buckets(np.clip(delta, 0, win)), -1).astype(np.int32)


def _sample_attention_parts(qkv, caches, table):
    batch = caches[0].shape[0]
    t_new = qkv[0].shape[3] // batch
    pasts = [c.shape[3] for c in caches]
    buckets = [jnp.asarray(_sample_buckets(g, pasts[g], t_new)) for g in range(N_GROUPS)]
    cache_spec = lambda p: pl.BlockSpec((1, 2, GROUP_WIDTH, p), lambda b: (b, 0, 0, 0))
    new_spec = pl.BlockSpec((1, 1, 3, t_new, GROUP_WIDTH), lambda b: (0, 0, 0, b, 0))
    out_spec = pl.BlockSpec((1, 2, SLABS, t_new, LANES), lambda b: (0, 0, 0, b, 0))
    out_sds = jax.ShapeDtypeStruct((1, 2, SLABS, batch * t_new, LANES), F32)
    return dict(
        batch=batch,
        args=[*qkv, *caches, *buckets, table],
        out_shape=[out_sds] * N_GROUPS + [jax.ShapeDtypeStruct(c.shape, F32) for c in caches],
        in_specs=[new_spec] * N_GROUPS + [cache_spec(p) for p in pasts] + [_resident(b.shape) for b in buckets]
        + [_SMEM],
        out_specs=[out_spec] * N_GROUPS + [cache_spec(p) for p in pasts],
        scratch_shapes=[pltpu.VMEM((HEADS_PER_GROUP * t_new, p + LANES), F32) for p in pasts])


def _sample_attention(parts):
    return pl.pallas_call(
        _sample_attn_kernel, grid=(parts["batch"],), out_shape=parts["out_shape"],
        in_specs=parts["in_specs"], out_specs=parts["out_specs"], scratch_shapes=parts["scratch_shapes"],
        compiler_params=_params(), name="sample_attn")(*parts["args"])


def _back_kernel(x_ref, ol0_ref, ol1_ref, ol2_ref, u_ref, halo_ref, gate_ref,
                 wab_ref, wpg_ref, pscale_ref, wpb_ref, wout_ref, g2_ref, wup_ref, wdn_ref,
                 out_ref, ue_ref, level_ref, mixed_ref, act_ref, attn_ref, pooled_ref, merged_ref,
                 *, pos_base, blocks_per_seq):
    block = pl.program_id(0)
    tm, d_model = x_ref.shape
    n_seq = halo_ref.shape[0]
    t = tm // n_seq
    pool_width = u_ref.shape[1]
    gw = pool_width // len(POOL_WINDOWS)

    for s in range(SLABS):
        ol_refs = (ol0_ref, ol1_ref, ol2_ref)
        lses = [r[0, 1, s] for r in ol_refs]
        top = jnp.maximum(jnp.maximum(lses[0], lses[1]), lses[2])
        num = jnp.zeros_like(top)
        den = jnp.zeros_like(top)
        for ol_ref, lse in zip(ol_refs, lses):
            e = jnp.exp2(lse - top)
            num = num + e * ol_ref[0, 0, s]
            den = den + e
        attn_ref[:, s * LANES:(s + 1) * LANES] = (num / den).astype(BF16)

    start = pos_base + (block % blocks_per_seq) * t
    first, end = POOL_PAD - POOL_HALO, POOL_PAD + t
    ue_ref[:, 0:first, :] = jnp.zeros((n_seq, first, pool_width), F32)
    ue_ref[:, first:POOL_PAD, :] = jnp.where(start > 0, halo_ref[...], 0.0)
    ue_ref[:, POOL_PAD:, :] = u_ref[...].reshape(n_seq, t, pool_width)
    level_ref[:, :, 0:first, :] = jnp.zeros((2, n_seq, first, gw), F32)
    pos = start + lax.broadcasted_iota(jnp.int32, (1, t, gw), 1)
    for gi, win in enumerate(POOL_WINDOWS):
        cols = slice(gi * gw, (gi + 1) * gw)
        read = lambda lo, hi, cols=cols: ue_ref[:, lo:hi, cols]
        span = 1
        while 2 * span < win:
            buf = (span.bit_length() - 1) % 2
            level_ref[buf, :, first:, :] = read(first, end) + read(first - span, end - span)
            read = lambda lo, hi, buf=buf: level_ref[buf, :, lo:hi, :]
            span *= 2
        s = read(POOL_PAD, end) + read(POOL_PAD - span, end - span)
        cnt = jnp.minimum(pos + 1, win).astype(F32)
        d = (s / cnt - ue_ref[:, POOL_PAD:, cols]).reshape(tm, gw)
        y = jnp.dot(d.astype(BF16), wpg_ref[gi], preferred_element_type=F32)
        pooled_ref[:, cols] = (y * pscale_ref[:, cols]).astype(BF16)

    for lo in range(0, d_model, MXU_TILE):
        cols = slice(lo, lo + MXU_TILE)
        branch_a = jnp.dot(attn_ref[...], wab_ref[:, cols], preferred_element_type=F32)
        branch_b = jnp.dot(pooled_ref[...], wpb_ref[:, cols], preferred_element_type=F32)
        gate_a = gate_ref[:, cols].astype(F32)
        gate_b = gate_ref[:, d_model + lo:d_model + lo + MXU_TILE].astype(F32)
        merged_ref[:, cols] = (gate_a * branch_a + gate_b * branch_b).astype(BF16)
    for lo in range(0, d_model, MXU_TILE):
        cols = slice(lo, lo + MXU_TILE)
        mixed_ref[:, cols] = x_ref[:, cols] + jnp.dot(merged_ref[...], wout_ref[:, cols],
                                                      preferred_element_type=F32)

    _ffn_block(mixed_ref, g2_ref, wup_ref, wdn_ref, out_ref, act_ref)


def _back(x, ol, u, halo, gates, merge_w, ffn_w, *, tm, halo_block, halo_index, pos_base, blocks_per_seq):
    m, d = x.shape
    pool_width = u.shape[1]
    n_seq = halo_block[0]
    d_ff = ffn_w[2].shape[0]
    row = lambda wd: pl.BlockSpec((tm, wd), lambda i: (i, 0))
    slab = pl.BlockSpec((1, 2, SLABS, tm, LANES),
                        lambda i: (i // blocks_per_seq, 0, 0, i % blocks_per_seq, 0))
    return pl.pallas_call(
        functools.partial(_back_kernel, pos_base=pos_base, blocks_per_seq=blocks_per_seq),
        out_shape=jax.ShapeDtypeStruct((m, d), F32),
        grid=(m // tm,),
        in_specs=[row(d)] + [slab] * N_GROUPS
        + [row(pool_width), pl.BlockSpec(halo_block, halo_index), row(2 * d)]
        + [_resident(a.shape) for a in (*merge_w, *ffn_w)],
        out_specs=row(d),
        scratch_shapes=[pltpu.VMEM((n_seq, POOL_PAD + tm // n_seq, pool_width), F32),
                        pltpu.VMEM((2, n_seq, POOL_PAD + tm // n_seq, pool_width // len(POOL_WINDOWS)), F32),
                        pltpu.VMEM((tm, d), F32), pltpu.VMEM((tm, d_ff), BF16),
                        pltpu.VMEM((tm, GROUP_WIDTH), BF16), pltpu.VMEM((tm, pool_width), BF16),
                        pltpu.VMEM((tm, d), BF16)],
        compiler_params=_params(),
        name="back",
    )(x, *ol, u, halo, gates, *merge_w, *ffn_w)


def _window_in(cache):
    n_seq, rows = cache.shape[0:2]
    return jnp.transpose(cache, (0, 2, 3, 4, 1)).reshape(n_seq, 2, GROUP_WIDTH, rows)


def _window_out(kv):
    n_seq, _, _, rows = kv.shape
    return jnp.transpose(kv.reshape(n_seq, 2, HEADS_PER_GROUP, HEAD_DIM, rows), (0, 4, 1, 2, 3))


def kernel(x_prompt, x_sample, cache_kv_w128, cache_kv_w512, cache_kv_w2048, state_pool, rel_bias_table,
           norm_ffn1, ffn1_w_up, ffn1_w_down, norm_mix, w_in, q_norm, k_norm, pool_w_group, pool_scale,
           w_attn_branch, w_pool_branch, w_out, norm_ffn2, ffn2_w_up, ffn2_w_down):
    batch, seq, d_model = x_prompt.shape
    dec_batch, dec_seq, _ = x_sample.shape
    depth = norm_ffn1.shape[0]
    pool_width = state_pool.shape[-1]
    tm = ROW_BLOCK
    ms = dec_batch * dec_seq
    seg = jnp.asarray(np.kron(np.eye(HEADS_PER_GROUP), np.full((HEAD_DIM, HEAD_DIM), 1.0 / HEAD_DIM)), BF16)
    table = rel_bias_table.astype(F32)
    caches_in = (cache_kv_w128, cache_kv_w512, cache_kv_w2048)
    keep_prompt = tuple(min(win, seq) for win, _ in ATTN_GROUPS)
    dils = tuple(dil for _, dil in ATTN_GROUPS)

    xp = x_prompt.reshape(batch * seq, d_model)
    xs = x_sample.reshape(ms, d_model)
    kv_p, kv_s, pool_p, pool_s = ([], [], []), ([], [], []), [], []
    for layer in range(depth):
        gain = lambda a: a[layer].reshape(1, -1).astype(F32)
        qg, kg = gain(q_norm), gain(k_norm)
        pool_groups = pool_w_group[layer]
        late_w = (ffn2_w_down[layer], w_attn_branch[layer], pool_groups.reshape(-1, pool_groups.shape[-1]),
                  w_pool_branch[layer], w_out[layer])

        xs, (wup1, wdn1) = _ffn(xs, gain(norm_ffn1), ffn1_w_up[layer], ffn1_w_down[layer], ms, narrow=True)
        outs = _proj(xs, gain(norm_mix), w_in[layer], qg, kg, seg, tm=ms, n_seq=1, dils=(1,) * N_GROUPS,
                     qkv_dtype=F32, keep_rows=None, narrow=True)
        qkv_s, u_s, gates_s, win = outs[0:N_GROUPS], outs[N_GROUPS], outs[N_GROUPS + 1], outs[N_GROUPS + 2]
        sample_parts = _sample_attention_parts(qkv_s, [_window_in(c[layer]) for c in caches_in], table)

        if (batch * seq) // tm == dec_batch:
            xp, sample_outs = _ffn(xp, gain(norm_ffn1), wup1, wdn1, tm, rider=sample_parts)
        else:
            xp, _ = _ffn(xp, gain(norm_ffn1), wup1, wdn1, tm)
            sample_outs = _sample_attention(sample_parts)
        outs = _proj(xp, gain(norm_mix), win, qg, kg, seg, tm=2 * tm, n_seq=batch, dils=dils,
                     qkv_dtype=BF16, keep_rows=keep_prompt)
        qkv, u, gates = outs[0:N_GROUPS], outs[N_GROUPS], outs[N_GROUPS + 1]
        windows, pstate = outs[N_GROUPS + 2:2 * N_GROUPS + 2], outs[2 * N_GROUPS + 2]
        narrowing = ((ffn2_w_up[layer],), late_w) + ((),) * (N_GROUPS - 2)
        ol, narrowed = zip(*[_prompt_attention(qkv[g], table, g, to_narrow=narrowing[g])
                             for g in range(N_GROUPS)])
        (wup2,), (wdn2, wab, wpg, wpb, wout) = narrowed[0:2]
        ffn2_w = (gain(norm_ffn2), wup2, wdn2)
        merge_w = (wab, wpg.reshape(pool_groups.shape), gain(pool_scale), wpb, wout)
        xp = _back(xp, ol, u, u.reshape(-1, POOL_HALO, pool_width), gates, merge_w, ffn2_w, tm=tm,
                   halo_block=(1, POOL_HALO, pool_width),
                   halo_index=lambda blk: (jnp.maximum(blk * (tm // POOL_HALO) - 1, 0), 0, 0),
                   pos_base=0, blocks_per_seq=seq // tm)
        for g in range(N_GROUPS):
            kv_p[g].append(_window_out(windows[g]))
        pool_p.append(pstate[:, POOL_HALO - POOL_STATE:])

        ol, windows = sample_outs[0:N_GROUPS], sample_outs[N_GROUPS:2 * N_GROUPS]
        history = jnp.concatenate(
            [jnp.zeros((dec_batch, POOL_HALO - POOL_STATE, pool_width), F32), state_pool[layer]], axis=1)
        xs = _back(xs, ol, u_s, history, gates_s, merge_w, ffn2_w, tm=ms,
                   halo_block=(dec_batch, POOL_HALO, pool_width), halo_index=lambda blk: (0, 0, 0),
                   pos_base=PAST_LEN, blocks_per_seq=1)
        for g in range(N_GROUPS):
            kv_s[g].append(_window_out(windows[g]))
        ue = jnp.concatenate([state_pool[layer], u_s.reshape(dec_batch, dec_seq, pool_width)], axis=1)
        pool_s.append(ue[:, ue.shape[1] - POOL_STATE:])

    stack = lambda xs_: jnp.stack(xs_, axis=0)
    return (xp.reshape(batch, seq, d_model), xs.reshape(dec_batch, dec_seq, d_model),
            stack(kv_p[0]), stack(kv_p[1]), stack(kv_p[2]), stack(pool_p),
            stack(kv_s[0]), stack(kv_s[1]), stack(kv_s[2]), stack(pool_s))
```

```python
import functools
import math

import numpy as np
import jax
import jax.numpy as jnp
from jax import lax
from jax.experimental import pallas as pl
from jax.experimental.pallas import tpu as pltpu

HEAD_DIM = 64
HEADS_PER_GROUP = 4
GROUP_WIDTH = HEADS_PER_GROUP * HEAD_DIM
ATTN_GROUPS = ((128, 1), (512, 4), (2048, 16))
N_GROUPS = len(ATTN_GROUPS)
ATTN_WIDTH = N_GROUPS * GROUP_WIDTH
N_BUCKETS = 32
MAX_DISTANCE = 2048
POOL_WINDOWS = (2, 4, 8, 16)
POOL_STATE = max(POOL_WINDOWS) - 1
POOL_HALO = 16
POOL_PAD = POOL_HALO + 8
EPS = 1e-6
PAST_LEN = 8192
NEG = -1e30
LOG2E = math.log2(math.e)
LANES = 128
MXU_TILE = 256
SLABS = GROUP_WIDTH // LANES
SINGLE_OP_STRIDE = 4
WINDOW_RING = 3
RIDER_PHASE_GAP = 2
ROW_BLOCK = 512
QBLK = 128
ATTN_UNROLL = 16
VMEM_LIMIT = 58 * 1024 * 1024

F32 = jnp.float32
BF16 = jnp.bfloat16
NT_DIMS = (((1,), (1,)), ((), ()))


def _t5_buckets(distance):
    max_exact = N_BUCKETS // 2
    d = np.asarray(distance, dtype=np.int32)
    ratio = np.log(np.maximum(d, 1).astype(np.float32) / np.float32(max_exact))
    large = max_exact + (ratio / np.float32(math.log(MAX_DISTANCE / max_exact))
                         * (N_BUCKETS - max_exact)).astype(np.int32)
    large = np.minimum(large, N_BUCKETS - 1)
    return np.where(d < max_exact, d, large).astype(np.int32)


def _params(n_grid_dims=1):
    return pltpu.CompilerParams(dimension_semantics=("arbitrary",) * n_grid_dims,
                                vmem_limit_bytes=VMEM_LIMIT)


def _resident(shape):
    return pl.BlockSpec(shape, lambda *_: (0,) * len(shape), pipeline_mode=pl.Buffered(1))


_SMEM = pl.BlockSpec(memory_space=pltpu.SMEM)


def _rmsnorm(x, g):
    ms = jnp.mean(x * x, axis=-1, keepdims=True)
    return x * lax.rsqrt(ms + EPS) * g


def _head_select(parts, lane_head):
    out = jnp.where(lane_head == 0, parts[0], 0.0)
    for h in range(1, HEADS_PER_GROUP):
        out = jnp.where(lane_head == h, parts[h], out)
    return out


def _stack_heads(q, lane_head):
    return jnp.concatenate([jnp.where(lane_head == h, q, 0.0)
                            for h in range(HEADS_PER_GROUP)], axis=0).astype(BF16)


def _build_bias(bias_ref, bucket_ref, table_ref, g):
    buckets = bucket_ref[...]
    rows = buckets.shape[0]
    for h in range(HEADS_PER_GROUP):
        acc = jnp.full(buckets.shape, NEG, F32)
        for b in range(N_BUCKETS):
            acc = jnp.where(buckets == b, table_ref[b, g * HEADS_PER_GROUP + h] * LOG2E, acc)
        bias_ref[h * rows:(h + 1) * rows, :] = acc


def _matmul_weight(w_ref, narrow_ref, rows, cols):
    w = w_ref[rows, cols]
    if narrow_ref is not None:
        w = w.astype(BF16)
        narrow_ref[rows, cols] = w
    return w


def _ffn_block(x_ref, g_ref, wup_ref, wdn_ref, o_ref, act_ref, narrow_refs=(None, None), rider_phases=()):
    d_ff, d = wdn_ref.shape
    every = slice(None)
    rider_phases = iter(rider_phases)
    next(rider_phases, None)
    h = _rmsnorm(x_ref[...], g_ref[...]).astype(BF16)
    for chunk, lo in enumerate(range(0, d_ff, MXU_TILE)):
        w_gate = _matmul_weight(wup_ref, narrow_refs[0], every, slice(lo, lo + MXU_TILE))
        w_up = _matmul_weight(wup_ref, narrow_refs[0], every, slice(d_ff + lo, d_ff + lo + MXU_TILE))
        gate = jnp.dot(h, w_gate, preferred_element_type=F32)
        up = jnp.dot(h, w_up, preferred_element_type=F32)
        if chunk % RIDER_PHASE_GAP == RIDER_PHASE_GAP - 1:
            next(rider_phases, None)
        act_ref[:, lo:lo + MXU_TILE] = (gate * jax.nn.sigmoid(gate) * up).astype(BF16)
    for lo in range(0, d, MXU_TILE):
        cols = slice(lo, lo + MXU_TILE)
        w_down = _matmul_weight(wdn_ref, narrow_refs[1], every, cols)
        o_ref[:, cols] = x_ref[:, cols] + 0.5 * jnp.dot(act_ref[...], w_down, preferred_element_type=F32)


def _ffn_kernel(*refs, rider_counts, narrow):
    n_in, n_out = rider_counts
    n_narrow = 2 if narrow else 0
    x_ref, g_ref, wup_ref, wdn_ref = refs[0:4]
    o_ref = refs[4 + n_in]
    narrow_refs = refs[5 + n_in:5 + n_in + n_narrow] if narrow else (None, None)
    first_rider_out = 5 + n_in + n_narrow
    act_ref = refs[first_rider_out + n_out]
    rider_phases = ()
    if n_in:
        rider_phases = _sample_attn_phases((*refs[4:4 + n_in], *refs[first_rider_out:first_rider_out + n_out],
                                            *refs[first_rider_out + n_out + 1:]))
    _ffn_block(x_ref, g_ref, wup_ref, wdn_ref, o_ref, act_ref, narrow_refs, rider_phases)


def _ffn(x, gain, w_up, w_down, tm, rider=None, narrow=False):
    m, d = x.shape
    d_ff = w_down.shape[0]
    assert m % tm == 0 and d_ff % MXU_TILE == 0 and d % MXU_TILE == 0
    assert not narrow or m == tm, "each weight chunk is narrowed once only on a single-step grid"
    rider = rider or dict(args=[], in_specs=[], out_shape=[], out_specs=[], scratch_shapes=[], batch=m // tm)
    assert rider["batch"] == m // tm
    weights = (w_up, w_down)
    narrowed = [jax.ShapeDtypeStruct(w.shape, BF16) for w in weights] if narrow else []
    outs = pl.pallas_call(
        functools.partial(_ffn_kernel, rider_counts=(len(rider["args"]), len(rider["out_shape"])), narrow=narrow),
        out_shape=[jax.ShapeDtypeStruct((m, d), F32)] + narrowed + rider["out_shape"],
        grid=(m // tm,),
        in_specs=[pl.BlockSpec((tm, d), lambda i: (i, 0)),
                  _resident((1, d)), _resident((d, 2 * d_ff)), _resident((d_ff, d))] + rider["in_specs"],
        out_specs=[pl.BlockSpec((tm, d), lambda i: (i, 0))] + [_resident(w.shape) for w in narrowed]
        + rider["out_specs"],
        scratch_shapes=[pltpu.VMEM((tm, d_ff), BF16)] + rider["scratch_shapes"],
        compiler_params=_params(),
        name="ffn",
    )(x, gain, w_up, w_down, *rider["args"])
    return outs[0], outs[1:]


def _proj_kernel(x_ref, g_ref, win_ref, qn_ref, kn_ref, seg_ref, *refs,
                 dils, kv_rows, with_state, narrow, pool_width, d_model):
    qkv_refs = refs[0:N_GROUPS]
    u_ref, gate_ref = refs[N_GROUPS:N_GROUPS + 2]
    state_refs = refs[N_GROUPS + 2:]
    narrow_ref = refs[-3] if narrow else None
    slab_ref, regroup_ref = refs[-2:]
    tm = x_ref.shape[0]
    h = _rmsnorm(x_ref[...], g_ref[...]).astype(BF16)

    def proj(lo, width):
        w = _matmul_weight(win_ref, narrow_ref, slice(None), slice(lo, lo + width))
        return jnp.dot(h, w, preferred_element_type=F32)

    def head_norm(y, gain):
        ms = jnp.dot((y * y).astype(BF16), seg_ref[...], preferred_element_type=F32)
        return y * lax.rsqrt(ms + EPS) * gain

    def emit(dst_ref, which, slot, val, dil):
        if dil == 1:
            dst_ref[0, 0, which] = val.astype(dst_ref.dtype)
            return
        for s in range(SLABS):
            slab_ref[slot, s] = val[:, s * LANES:(s + 1) * LANES]
        n = tm // dil
        if dil <= SINGLE_OP_STRIDE:
            for r in range(dil):
                parts = [slab_ref[slot, s, pl.ds(r, n, stride=dil), :] for s in range(SLABS)]
                dst_ref[0, r, which] = jnp.concatenate(parts, axis=-1).astype(dst_ref.dtype)
            return
        outer = SINGLE_OP_STRIDE
        inner = dil // outer
        assert inner <= SINGLE_OP_STRIDE
        quarter = tm // outer
        for a in range(outer):
            for s in range(SLABS):
                regroup_ref[s, a * quarter:(a + 1) * quarter, :] = slab_ref[slot, s, pl.ds(a, quarter, stride=outer), :]
        for r in range(dil):
            start = (r % outer) * quarter + r // outer
            parts = [regroup_ref[s, pl.ds(start, n, stride=inner), :] for s in range(SLABS)]
            dst_ref[0, r, which] = jnp.concatenate(parts, axis=-1).astype(dst_ref.dtype)

    def finish(g, which, y):
        cols = slice(g * GROUP_WIDTH, (g + 1) * GROUP_WIDTH)
        if which == 0:
            y = head_norm(y, qn_ref[:, cols]) * (HEAD_DIM ** -0.5 * LOG2E)
        elif which == 1:
            y = head_norm(y, kn_ref[:, cols])
        emit(qkv_refs[g], which, 3 * sum(d > 1 for d in dils[:g]) + which, y, dils[g])
        if with_state and which > 0:
            state_refs[g][0, which - 1] = y.T[:, tm - kv_rows[g]:]

    pending = None
    for g in range(N_GROUPS):
        for which in range(3):
            y = proj(which * ATTN_WIDTH + g * GROUP_WIDTH, GROUP_WIDTH)
            if pending is not None:
                finish(*pending)
            pending = (g, which, y)
    u = proj(3 * ATTN_WIDTH, pool_width)
    finish(*pending)
    u_ref[...] = u
    if with_state:
        state_refs[N_GROUPS][0] = u[tm - POOL_HALO:, :]
    gates_base = 3 * ATTN_WIDTH + pool_width
    for lo in range(0, 2 * d_model, MXU_TILE):
        gate_ref[:, lo:lo + MXU_TILE] = jax.nn.sigmoid(proj(gates_base + lo, MXU_TILE)).astype(BF16)


def _proj(x, gain, w_in, q_gain, k_gain, seg, *, tm, n_seq, dils, qkv_dtype, keep_rows, narrow=False):
    m, d = x.shape
    pool_width = w_in.shape[1] - 3 * ATTN_WIDTH - 2 * d
    seq_len = m // n_seq
    assert seq_len % tm == 0 and all(tm % dil == 0 for dil in dils)
    blocks_per_seq = seq_len // tm
    with_state = keep_rows is not None
    seq_block = lambda i: (i // blocks_per_seq, i % blocks_per_seq)

    row = lambda w: pl.BlockSpec((tm, w), lambda i: (i, 0))
    out_shape, out_specs = [], []
    for g in range(N_GROUPS):
        out_shape.append(jax.ShapeDtypeStruct((n_seq, dils[g], 3, seq_len // dils[g], GROUP_WIDTH), qkv_dtype))
        out_specs.append(pl.BlockSpec((1, dils[g], 3, tm // dils[g], GROUP_WIDTH),
                                      lambda i: (seq_block(i)[0], 0, 0, seq_block(i)[1], 0)))
    out_shape += [jax.ShapeDtypeStruct((m, pool_width), F32), jax.ShapeDtypeStruct((m, 2 * d), BF16)]
    out_specs += [row(pool_width), row(2 * d)]
    kv_rows = kv_first = ()
    if with_state:
        kv_rows = tuple(min(r, tm) for r in keep_rows)
        kv_first = tuple(blocks_per_seq - keep_rows[g] // kv_rows[g] for g in range(N_GROUPS))
        for g in range(N_GROUPS):
            out_shape.append(jax.ShapeDtypeStruct((n_seq, 2, GROUP_WIDTH, keep_rows[g]), F32))
            out_specs.append(pl.BlockSpec(
                (1, 2, GROUP_WIDTH, kv_rows[g]),
                lambda i, first=kv_first[g]: (seq_block(i)[0], 0, 0, jnp.maximum(seq_block(i)[1] - first, 0))))
        out_shape.append(jax.ShapeDtypeStruct((n_seq, POOL_HALO, pool_width), F32))
        out_specs.append(pl.BlockSpec((1, POOL_HALO, pool_width), lambda i: (seq_block(i)[0], 0, 0)))
    if narrow:
        assert m == tm, "each weight chunk is narrowed once only on a single-step grid"
        out_shape.append(jax.ShapeDtypeStruct(w_in.shape, BF16))
        out_specs.append(_resident(w_in.shape))
    return pl.pallas_call(
        functools.partial(_proj_kernel, dils=dils, kv_rows=kv_rows, with_state=with_state, narrow=narrow,
                          pool_width=pool_width, d_model=d),
        out_shape=out_shape,
        grid=(m // tm,),
        in_specs=[row(d), _resident((1, d)), _resident(w_in.shape),
                  _resident((1, ATTN_WIDTH)), _resident((1, ATTN_WIDTH)),
                  _resident((GROUP_WIDTH, GROUP_WIDTH))],
        out_specs=out_specs,
        scratch_shapes=[pltpu.VMEM((max(1, 3 * sum(dil > 1 for dil in dils)), SLABS, tm, LANES), F32),
                        pltpu.VMEM((SLABS, tm, LANES), F32)],
        compiler_params=_params(),
        name="proj",
    )(x, gain, w_in, q_gain, k_gain, seg)


def _attn_kernel(*refs, g, dil, n_blocks, unroll, n_cast):
    qkv_ref, bucket_ref, hmask_ref, table_ref = refs[0:4]
    wide_refs = refs[4:4 + n_cast]
    out_ref = refs[4 + n_cast]
    narrow_refs = refs[5 + n_cast:5 + 2 * n_cast]
    bias_ref = refs[5 + 2 * n_cast]
    for wide_ref, narrow_ref in zip(wide_refs, narrow_refs):
        narrow_ref[...] = wide_ref[...].astype(BF16)

    @pl.when(pl.program_id(0) == 0)
    def _():
        _build_bias(bias_ref.at[1], bucket_ref, table_ref, g)
        bias_ref[0, :, 0:QBLK] = bias_ref[1, :, QBLK:2 * QBLK]
        bias_ref[0, :, QBLK:2 * QBLK] = jnp.full((HEADS_PER_GROUP * QBLK, QBLK), NEG, F32)

    first_head = lax.broadcasted_iota(jnp.int32, (QBLK, LANES), 1) < HEAD_DIM
    heads = range(HEADS_PER_GROUP)

    def per_slab(col):
        part = lambda h: jnp.broadcast_to(col[h * QBLK:(h + 1) * QBLK], (QBLK, LANES))
        return [jnp.where(first_head, part(2 * s), part(2 * s + 1)) for s in range(SLABS)]

    def body(n, carry):
        res = n // n_blocks
        i = n % n_blocks
        row0 = pl.multiple_of(i * QBLK, QBLK)
        span0 = pl.multiple_of(jnp.maximum(i - 1, 0) * QBLK, QBLK)
        q = qkv_ref[0, res, 0, pl.ds(row0, QBLK), :]
        qs = jnp.concatenate([q * hmask_ref[h] for h in heads], axis=0)
        s = lax.dot_general(qs, qkv_ref[0, res, 1, pl.ds(span0, 2 * QBLK), :], NT_DIMS,
                            preferred_element_type=F32) + bias_ref[jnp.minimum(i, 1)]
        m = jnp.max(s, axis=-1, keepdims=True)
        p = jnp.exp2(s - m)
        l = jnp.sum(p, axis=-1, keepdims=True)
        p = p.astype(BF16)
        v = qkv_ref[0, res, 2, pl.ds(span0, 2 * QBLK), :]
        p_wide = jnp.concatenate([p[h * QBLK:(h + 1) * QBLK] for h in heads], axis=1)
        v_tall = jnp.concatenate([v * hmask_ref[h] for h in heads], axis=0)
        pv = jnp.dot(p_wide, v_tall, preferred_element_type=F32)
        token0 = res + dil * row0
        dst = pl.ds(token0, QBLK) if dil == 1 else pl.ds(token0, QBLK, stride=dil)
        for s_, (m_s, l_s) in enumerate(zip(per_slab(m), per_slab(l))):
            out_ref[0, 0, s_, dst, :] = pv[:, s_ * LANES:(s_ + 1) * LANES] / l_s
            out_ref[0, 1, s_, dst, :] = m_s + jnp.log2(l_s)
        return carry

    lax.fori_loop(0, dil * n_blocks, body, 0, unroll=unroll)


def _prompt_buckets(g):
    win, dil = ATTN_GROUPS[g]
    delta = np.arange(QBLK)[:, None] + QBLK - np.arange(2 * QBLK)[None, :]
    valid = (delta >= 0) & (delta <= win // dil)
    return np.where(valid, _t5_buckets(dil * np.clip(delta, 0, win // dil)), -1).astype(np.int32)


def _prompt_attention(qkv, table, g, to_narrow=()):
    _, dil = ATTN_GROUPS[g]
    batch, _, _, sub, _ = qkv.shape
    n_blocks = sub // QBLK
    assert sub % QBLK == 0 and n_blocks >= 2 and (dil * n_blocks) % ATTN_UNROLL == 0
    assert all(w.shape[0] % (16 * batch) == 0 for w in to_narrow)
    cast_specs = [pl.BlockSpec((w.shape[0] // batch, w.shape[1]), lambda b: (b, 0)) for w in to_narrow]
    head_of_lane = np.arange(GROUP_WIDTH) // HEAD_DIM
    hmask = head_of_lane[None, None, :] == np.arange(HEADS_PER_GROUP)[:, None, None]
    in_spec = pl.BlockSpec((1, dil, 3, sub, GROUP_WIDTH), lambda b: (b, 0, 0, 0, 0))
    out_spec = pl.BlockSpec((1, 2, SLABS, sub * dil, LANES), lambda b: (b, 0, 0, 0, 0))
    out_sds = jax.ShapeDtypeStruct((batch, 2, SLABS, sub * dil, LANES), F32)
    outs = pl.pallas_call(
        functools.partial(_attn_kernel, g=g, dil=dil, n_blocks=n_blocks, unroll=ATTN_UNROLL,
                          n_cast=len(to_narrow)),
        out_shape=[out_sds] + [jax.ShapeDtypeStruct(w.shape, BF16) for w in to_narrow],
        grid=(batch,),
        in_specs=[in_spec, _resident((QBLK, 2 * QBLK)), _resident(hmask.shape), _SMEM] + cast_specs,
        out_specs=[out_spec] + cast_specs,
        scratch_shapes=[pltpu.VMEM((2, HEADS_PER_GROUP * QBLK, 2 * QBLK), F32)],
        compiler_params=_params(),
        name=f"attn_g{g}",
    )(qkv, jnp.asarray(_prompt_buckets(g)), jnp.asarray(hmask, BF16), table, *to_narrow)
    return outs[0], outs[1:]


def _sample_attn_phases(refs):
    n = N_GROUPS
    qkv_refs, cache_refs, bucket_refs, table_ref = refs[0:n], refs[n:2 * n], refs[2 * n:3 * n], refs[3 * n]
    out_refs, win_refs, bias_refs = refs[3 * n + 1:4 * n + 1], refs[4 * n + 1:5 * n + 1], refs[5 * n + 1:6 * n + 1]
    ring_refs, sem_ref = refs[6 * n + 1:7 * n + 1], refs[7 * n + 1]
    step, n_steps = pl.program_id(0), pl.num_programs(0)

    def window_copy(g, seq):
        slot = seq % WINDOW_RING
        return pltpu.make_async_copy(cache_refs[g].at[seq], ring_refs[g].at[slot], sem_ref.at[g, slot])

    @pl.when(step == 0)
    def _():
        for g in range(N_GROUPS):
            _build_bias(bias_refs[g], bucket_refs[g], table_ref, g)
            for seq in range(WINDOW_RING - 1):
                window_copy(g, seq).start()

    @pl.when(step + WINDOW_RING - 1 < n_steps)
    def _():
        for g in range(N_GROUPS):
            window_copy(g, step + WINDOW_RING - 1).start()

    for g in range(N_GROUPS):
        window_copy(g, step).wait()
    slot = step % WINDOW_RING

    t_new = qkv_refs[0].shape[3]
    lane_head = lax.broadcasted_iota(jnp.int32, (t_new, GROUP_WIDTH), 1) // HEAD_DIM
    new_lane = lax.broadcasted_iota(jnp.int32, (GROUP_WIDTH, LANES), 1) >= LANES - t_new
    pad = jnp.zeros((LANES - t_new, GROUP_WIDTH), F32)

    groups = []
    for g in range(N_GROUPS):
        q, k, v = (qkv_refs[g][0, 0, which] for which in range(3))
        cache_ref, win_ref = ring_refs[g].at[slot], win_refs[g]
        past = cache_ref.shape[2]
        new_rows = [jnp.concatenate([pad, rows_], axis=0) for rows_ in (k, v)]
        for c in range(2):
            shifted = pltpu.roll(cache_ref[c], past - t_new, 1)
            if past > LANES:
                win_ref[0, c, :, 0:past - LANES] = shifted[:, 0:past - LANES]
            win_ref[0, c, :, past - LANES:past] = jnp.where(new_lane, new_rows[c].T, shifted[:, past - LANES:])
        groups.append(dict(past=past, qs=_stack_heads(q, lane_head),
                           old=[cache_ref[c].astype(BF16) for c in range(2)],
                           new=[rows_.astype(BF16) for rows_ in new_rows]))
    yield

    for g, grp in enumerate(groups):
        past, bias_ref = grp["past"], bias_refs[g]
        s_old = jnp.dot(grp["qs"], grp["old"][0], preferred_element_type=F32) + bias_ref[:, 0:past]
        s_new = lax.dot_general(grp["qs"], grp["new"][0], NT_DIMS,
                                preferred_element_type=F32) + bias_ref[:, past:past + LANES]
        m = jnp.maximum(jnp.max(s_old, axis=-1, keepdims=True), jnp.max(s_new, axis=-1, keepdims=True))
        p_old = jnp.exp2(s_old - m)
        p_new = jnp.exp2(s_new - m)
        grp.update(m=m, l=jnp.sum(p_old, axis=-1, keepdims=True) + jnp.sum(p_new, axis=-1, keepdims=True),
                   p_old=p_old.astype(BF16), p_new=p_new.astype(BF16))
    yield

    for g, grp in enumerate(groups):
        pv = (lax.dot_general(grp["p_old"], grp["old"][1], NT_DIMS, preferred_element_type=F32)
              + jnp.dot(grp["p_new"], grp["new"][1], preferred_element_type=F32)) / grp["l"]
        lse = jnp.broadcast_to(grp["m"] + jnp.log2(grp["l"]), pv.shape)
        rows = lambda a: [a[h * t_new:(h + 1) * t_new] for h in range(HEADS_PER_GROUP)]
        for which, val in enumerate((_head_select(rows(pv), lane_head), _head_select(rows(lse), lane_head))):
            for s_ in range(SLABS):
                out_refs[g][0, which, s_] = val[:, s_ * LANES:(s_ + 1) * LANES]


def _sample_attn_kernel(*refs):
    for _ in _sample_attn_phases(refs):
        pass


def _sample_buckets(g, past, t_new):
    win, dil = ATTN_GROUPS[g]
    lane = np.arange(past + LANES)[None, :]
    key_pos = np.where(lane < past, lane, lane - (LANES - t_new))
    delta = past + np.arange(t_new)[:, None] - key_pos
    valid = ((lane < past) | (lane >= past + LANES - t_new)) & (delta >= 0) & (delta % dil == 0) & (delta <= win)
    return np.where(valid, _t5_buckets(np.clip(delta, 0, win)), -1).astype(np.int32)


def _sample_attention_parts(qkv, caches, table):
    batch = caches[0].shape[0]
    assert batch >= WINDOW_RING
    t_new = qkv[0].shape[3] // batch
    pasts = [c.shape[3] for c in caches]
    window_ring = [pltpu.VMEM((WINDOW_RING,) + c.shape[1:], F32) for c in caches]
    buckets = [jnp.asarray(_sample_buckets(g, pasts[g], t_new)) for g in range(N_GROUPS)]
    cache_spec = lambda p: pl.BlockSpec((1, 2, GROUP_WIDTH, p), lambda b: (b, 0, 0, 0))
    new_spec = pl.BlockSpec((1, 1, 3, t_new, GROUP_WIDTH), lambda b: (0, 0, 0, b, 0))
    out_spec = pl.BlockSpec((1, 2, SLABS, t_new, LANES), lambda b: (0, 0, 0, b, 0))
    out_sds = jax.ShapeDtypeStruct((1, 2, SLABS, batch * t_new, LANES), F32)
    return dict(
        batch=batch,
        args=[*qkv, *caches, *buckets, table],
        out_shape=[out_sds] * N_GROUPS + [jax.ShapeDtypeStruct(c.shape, F32) for c in caches],
        in_specs=[new_spec] * N_GROUPS + [pl.BlockSpec(memory_space=pl.ANY)] * N_GROUPS
        + [_resident(b.shape) for b in buckets] + [_SMEM],
        out_specs=[out_spec] * N_GROUPS + [cache_spec(p) for p in pasts],
        scratch_shapes=[pltpu.VMEM((HEADS_PER_GROUP * t_new, p + LANES), F32) for p in pasts] + window_ring
        + [pltpu.SemaphoreType.DMA((N_GROUPS, WINDOW_RING))])


def _sample_attention(parts):
    return pl.pallas_call(
        _sample_attn_kernel, grid=(parts["batch"],), out_shape=parts["out_shape"],
        in_specs=parts["in_specs"], out_specs=parts["out_specs"], scratch_shapes=parts["scratch_shapes"],
        compiler_params=_params(), name="sample_attn")(*parts["args"])


def _back_kernel(x_ref, ol0_ref, ol1_ref, ol2_ref, u_ref, halo_ref, gate_ref,
                 wab_ref, wpg_ref, pscale_ref, wpb_ref, wout_ref, g2_ref, wup_ref, wdn_ref,
                 out_ref, ue_ref, level_ref, mixed_ref, act_ref, attn_ref, pooled_ref, merged_ref,
                 ring0_ref, ring1_ref, ring2_ref, sem_ref, *, pos_base, blocks_per_seq, n_blocks):
    block = pl.program_id(0)
    tm, d_model = x_ref.shape
    n_seq = halo_ref.shape[0]
    t = tm // n_seq
    pool_width = u_ref.shape[1]
    gw = pool_width // len(POOL_WINDOWS)
    rings = (ring0_ref, ring1_ref, ring2_ref)

    def ol_copy(g, blk):
        src = (ol0_ref, ol1_ref, ol2_ref)[g].at[blk // blocks_per_seq, :, :, pl.ds((blk % blocks_per_seq) * tm, tm), :]
        slot = blk % WINDOW_RING
        return pltpu.make_async_copy(src, rings[g].at[slot], sem_ref.at[g, slot])

    @pl.when(block == 0)
    def _():
        for g in range(N_GROUPS):
            for blk in range(min(WINDOW_RING - 1, n_blocks)):
                ol_copy(g, blk).start()

    @pl.when(block + WINDOW_RING - 1 < n_blocks)
    def _():
        for g in range(N_GROUPS):
            ol_copy(g, block + WINDOW_RING - 1).start()

    for g in range(N_GROUPS):
        ol_copy(g, block).wait()
    ol_refs = [ring.at[block % WINDOW_RING] for ring in rings]

    for s in range(SLABS):
        lses = [r[1, s] for r in ol_refs]
        top = jnp.maximum(jnp.maximum(lses[0], lses[1]), lses[2])
        num = jnp.zeros_like(top)
        den = jnp.zeros_like(top)
        for ol_ref, lse in zip(ol_refs, lses):
            e = jnp.exp2(lse - top)
            num = num + e * ol_ref[0, s]
            den = den + e
        attn_ref[:, s * LANES:(s + 1) * LANES] = (num / den).astype(BF16)

    start = pos_base + (block % blocks_per_seq) * t
    first, end = POOL_PAD - POOL_HALO, POOL_PAD + t
    ue_ref[:, 0:first, :] = jnp.zeros((n_seq, first, pool_width), F32)
    ue_ref[:, first:POOL_PAD, :] = jnp.where(start > 0, halo_ref[...], 0.0)
    ue_ref[:, POOL_PAD:, :] = u_ref[...].reshape(n_seq, t, pool_width)
    level_ref[:, :, 0:first, :] = jnp.zeros((2, n_seq, first, gw), F32)
    pos = start + lax.broadcasted_iota(jnp.int32, (1, t, gw), 1)
    for gi, win in enumerate(POOL_WINDOWS):
        cols = slice(gi * gw, (gi + 1) * gw)
        read = lambda lo, hi, cols=cols: ue_ref[:, lo:hi, cols]
        span = 1
        while 2 * span < win:
            buf = (span.bit_length() - 1) % 2
            level_ref[buf, :, first:, :] = read(first, end) + read(first - span, end - span)
            read = lambda lo, hi, buf=buf: level_ref[buf, :, lo:hi, :]
            span *= 2
        s = read(POOL_PAD, end) + read(POOL_PAD - span, end - span)
        cnt = jnp.minimum(pos + 1, win).astype(F32)
        d = (s / cnt - ue_ref[:, POOL_PAD:, cols]).reshape(tm, gw)
        y = jnp.dot(d.astype(BF16), wpg_ref[gi], preferred_element_type=F32)
        pooled_ref[:, cols] = (y * pscale_ref[:, cols]).astype(BF16)

    for lo in range(0, d_model, MXU_TILE):
        cols = slice(lo, lo + MXU_TILE)
        branch_a = jnp.dot(attn_ref[...], wab_ref[:, cols], preferred_element_type=F32)
        branch_b = jnp.dot(pooled_ref[...], wpb_ref[:, cols], preferred_element_type=F32)
        gate_a = gate_ref[:, cols].astype(F32)
        gate_b = gate_ref[:, d_model + lo:d_model + lo + MXU_TILE].astype(F32)
        merged_ref[:, cols] = (gate_a * branch_a + gate_b * branch_b).astype(BF16)
    for lo in range(0, d_model, MXU_TILE):
        cols = slice(lo, lo + MXU_TILE)
        mixed_ref[:, cols] = x_ref[:, cols] + jnp.dot(merged_ref[...], wout_ref[:, cols],
                                                      preferred_element_type=F32)

    _ffn_block(mixed_ref, g2_ref, wup_ref, wdn_ref, out_ref, act_ref)


def _back(x, ol, u, halo, gates, merge_w, ffn_w, *, tm, halo_block, halo_index, pos_base, blocks_per_seq):
    m, d = x.shape
    pool_width = u.shape[1]
    n_seq = halo_block[0]
    d_ff = ffn_w[2].shape[0]
    row = lambda wd: pl.BlockSpec((tm, wd), lambda i: (i, 0))
    ol_ring = pltpu.VMEM((WINDOW_RING, 2, SLABS, tm, LANES), F32)
    return pl.pallas_call(
        functools.partial(_back_kernel, pos_base=pos_base, blocks_per_seq=blocks_per_seq, n_blocks=m // tm),
        out_shape=jax.ShapeDtypeStruct((m, d), F32),
        grid=(m // tm,),
        in_specs=[row(d)] + [pl.BlockSpec(memory_space=pl.ANY)] * N_GROUPS
        + [row(pool_width), pl.BlockSpec(halo_block, halo_index), row(2 * d)]
        + [_resident(a.shape) for a in (*merge_w, *ffn_w)],
        out_specs=row(d),
        scratch_shapes=[pltpu.VMEM((n_seq, POOL_PAD + tm // n_seq, pool_width), F32),
                        pltpu.VMEM((2, n_seq, POOL_PAD + tm // n_seq, pool_width // len(POOL_WINDOWS)), F32),
                        pltpu.VMEM((tm, d), F32), pltpu.VMEM((tm, d_ff), BF16),
                        pltpu.VMEM((tm, GROUP_WIDTH), BF16), pltpu.VMEM((tm, pool_width), BF16),
                        pltpu.VMEM((tm, d), BF16)] + [ol_ring] * N_GROUPS
        + [pltpu.SemaphoreType.DMA((N_GROUPS, WINDOW_RING))],
        compiler_params=_params(),
        name="back",
    )(x, *ol, u, halo, gates, *merge_w, *ffn_w)


def _window_in(cache):
    n_seq, rows = cache.shape[0:2]
    return jnp.transpose(cache, (0, 2, 3, 4, 1)).reshape(n_seq, 2, GROUP_WIDTH, rows)


def _window_out(kv):
    n_seq, _, _, rows = kv.shape
    return jnp.transpose(kv.reshape(n_seq, 2, HEADS_PER_GROUP, HEAD_DIM, rows), (0, 4, 1, 2, 3))


def kernel(x_prompt, x_sample, cache_kv_w128, cache_kv_w512, cache_kv_w2048, state_pool, rel_bias_table,
           norm_ffn1, ffn1_w_up, ffn1_w_down, norm_mix, w_in, q_norm, k_norm, pool_w_group, pool_scale,
           w_attn_branch, w_pool_branch, w_out, norm_ffn2, ffn2_w_up, ffn2_w_down):
    batch, seq, d_model = x_prompt.shape
    dec_batch, dec_seq, _ = x_sample.shape
    depth = norm_ffn1.shape[0]
    pool_width = state_pool.shape[-1]
    tm = ROW_BLOCK
    ms = dec_batch * dec_seq
    seg = jnp.asarray(np.kron(np.eye(HEADS_PER_GROUP), np.full((HEAD_DIM, HEAD_DIM), 1.0 / HEAD_DIM)), BF16)
    table = rel_bias_table.astype(F32)
    caches_in = (cache_kv_w128, cache_kv_w512, cache_kv_w2048)
    keep_prompt = tuple(min(win, seq) for win, _ in ATTN_GROUPS)
    dils = tuple(dil for _, dil in ATTN_GROUPS)

    xp = x_prompt.reshape(batch * seq, d_model)
    xs = x_sample.reshape(ms, d_model)
    kv_p, kv_s, pool_p, pool_s = ([], [], []), ([], [], []), [], []
    for layer in range(depth):
        gain = lambda a: a[layer].reshape(1, -1).astype(F32)
        qg, kg = gain(q_norm), gain(k_norm)
        pool_groups = pool_w_group[layer]
        late_w = (ffn2_w_down[layer], w_attn_branch[layer], pool_groups.reshape(-1, pool_groups.shape[-1]),
                  w_pool_branch[layer], w_out[layer])

        xs, (wup1, wdn1) = _ffn(xs, gain(norm_ffn1), ffn1_w_up[layer], ffn1_w_down[layer], ms, narrow=True)
        outs = _proj(xs, gain(norm_mix), w_in[layer], qg, kg, seg, tm=ms, n_seq=1, dils=(1,) * N_GROUPS,
                     qkv_dtype=F32, keep_rows=None, narrow=True)
        qkv_s, u_s, gates_s, win = outs[0:N_GROUPS], outs[N_GROUPS], outs[N_GROUPS + 1], outs[N_GROUPS + 2]
        sample_parts = _sample_attention_parts(qkv_s, [_window_in(c[layer]) for c in caches_in], table)

        if (batch * seq) // tm == dec_batch:
            xp, sample_outs = _ffn(xp, gain(norm_ffn1), wup1, wdn1, tm, rider=sample_parts)
        else:
            xp, _ = _ffn(xp, gain(norm_ffn1), wup1, wdn1, tm)
            sample_outs = _sample_attention(sample_parts)
        outs = _proj(xp, gain(norm_mix), win, qg, kg, seg, tm=2 * tm, n_seq=batch, dils=dils,
                     qkv_dtype=BF16, keep_rows=keep_prompt)
        qkv, u, gates = outs[0:N_GROUPS], outs[N_GROUPS], outs[N_GROUPS + 1]
        windows, pstate = outs[N_GROUPS + 2:2 * N_GROUPS + 2], outs[2 * N_GROUPS + 2]
        narrowing = ((ffn2_w_up[layer],), late_w) + ((),) * (N_GROUPS - 2)
        ol, narrowed = zip(*[_prompt_attention(qkv[g], table, g, to_narrow=narrowing[g])
                             for g in range(N_GROUPS)])
        (wup2,), (wdn2, wab, wpg, wpb, wout) = narrowed[0:2]
        ffn2_w = (gain(norm_ffn2), wup2, wdn2)
        merge_w = (wab, wpg.reshape(pool_groups.shape), gain(pool_scale), wpb, wout)
        xp = _back(xp, ol, u, u.reshape(-1, POOL_HALO, pool_width), gates, merge_w, ffn2_w, tm=tm,
                   halo_block=(1, POOL_HALO, pool_width),
                   halo_index=lambda blk: (jnp.maximum(blk * (tm // POOL_HALO) - 1, 0), 0, 0),
                   pos_base=0, blocks_per_seq=seq // tm)
        for g in range(N_GROUPS):
            kv_p[g].append(_window_out(windows[g]))
        pool_p.append(pstate[:, POOL_HALO - POOL_STATE:])

        ol, windows = sample_outs[0:N_GROUPS], sample_outs[N_GROUPS:2 * N_GROUPS]
        history = jnp.concatenate(
            [jnp.zeros((dec_batch, POOL_HALO - POOL_STATE, pool_width), F32), state_pool[layer]], axis=1)
        xs = _back(xs, ol, u_s, history, gates_s, merge_w, ffn2_w, tm=ms,
                   halo_block=(dec_batch, POOL_HALO, pool_width), halo_index=lambda blk: (0, 0, 0),
                   pos_base=PAST_LEN, blocks_per_seq=1)
        for g in range(N_GROUPS):
            kv_s[g].append(_window_out(windows[g]))
        ue = jnp.concatenate([state_pool[layer], u_s.reshape(dec_batch, dec_seq, pool_width)], axis=1)
        pool_s.append(ue[:, ue.shape[1] - POOL_STATE:])

    stack = lambda xs_: jnp.stack(xs_, axis=0)
    return (xp.reshape(batch, seq, d_model), xs.reshape(dec_batch, dec_seq, d_model),
            stack(kv_p[0]), stack(kv_p[1]), stack(kv_p[2]), stack(pool_p),
            stack(kv_s[0]), stack(kv_s[1]), stack(kv_s[2]), stack(pool_s))
```

```python
import functools
import math

import numpy as np
import jax
import jax.numpy as jnp
from jax import lax
from jax.experimental import pallas as pl
from jax.experimental.pallas import tpu as pltpu

HEAD_DIM = 64
HEADS_PER_GROUP = 4
GROUP_WIDTH = HEADS_PER_GROUP * HEAD_DIM
ATTN_GROUPS = ((128, 1), (512, 4), (2048, 16))
N_GROUPS = len(ATTN_GROUPS)
ATTN_WIDTH = N_GROUPS * GROUP_WIDTH
N_BUCKETS = 32
MAX_DISTANCE = 2048
POOL_WINDOWS = (2, 4, 8, 16)
POOL_STATE = max(POOL_WINDOWS) - 1
POOL_HALO = 16
POOL_PAD = POOL_HALO + 8
EPS = 1e-6
PAST_LEN = 8192
NEG = -1e30
LOG2E = math.log2(math.e)
LANES = 128
MXU_TILE = 256
SLABS = GROUP_WIDTH // LANES
SINGLE_OP_STRIDE = 4
WINDOW_RING = 3
RIDER_PHASE_GAP = 2
ROW_BLOCK = 512
QBLK = 128
ATTN_UNROLL = 16
VMEM_LIMIT = 58 * 1024 * 1024

F32 = jnp.float32
BF16 = jnp.bfloat16
NT_DIMS = (((1,), (1,)), ((), ()))


def _t5_buckets(distance):
    max_exact = N_BUCKETS // 2
    d = np.asarray(distance, dtype=np.int32)
    ratio = np.log(np.maximum(d, 1).astype(np.float32) / np.float32(max_exact))
    large = max_exact + (ratio / np.float32(math.log(MAX_DISTANCE / max_exact))
                         * (N_BUCKETS - max_exact)).astype(np.int32)
    large = np.minimum(large, N_BUCKETS - 1)
    return np.where(d < max_exact, d, large).astype(np.int32)


def _params(n_grid_dims=1):
    return pltpu.CompilerParams(dimension_semantics=("arbitrary",) * n_grid_dims,
                                vmem_limit_bytes=VMEM_LIMIT)


def _resident(shape):
    return pl.BlockSpec(shape, lambda *_: (0,) * len(shape), pipeline_mode=pl.Buffered(1))


_SMEM = pl.BlockSpec(memory_space=pltpu.SMEM)


def _rmsnorm(x, g):
    ms = jnp.mean(x * x, axis=-1, keepdims=True)
    return x * lax.rsqrt(ms + EPS) * g


def _head_select(parts, lane_head):
    out = jnp.where(lane_head == 0, parts[0], 0.0)
    for h in range(1, HEADS_PER_GROUP):
        out = jnp.where(lane_head == h, parts[h], out)
    return out


def _stack_heads(q, lane_head):
    return jnp.concatenate([jnp.where(lane_head == h, q, 0.0)
                            for h in range(HEADS_PER_GROUP)], axis=0).astype(BF16)


def _build_bias(bias_ref, bucket_ref, table_ref, g):
    buckets = bucket_ref[...]
    rows = buckets.shape[0]
    for h in range(HEADS_PER_GROUP):
        acc = jnp.full(buckets.shape, NEG, F32)
        for b in range(N_BUCKETS):
            acc = jnp.where(buckets == b, table_ref[b, g * HEADS_PER_GROUP + h] * LOG2E, acc)
        bias_ref[h * rows:(h + 1) * rows, :] = acc


def _matmul_weight(w_ref, narrow_ref, rows, cols):
    w = w_ref[rows, cols]
    if narrow_ref is not None:
        w = w.astype(BF16)
        narrow_ref[rows, cols] = w
    return w


def _ffn_block(x_ref, g_ref, wup_ref, wdn_ref, o_ref, act_ref, narrow_refs=(None, None), rider_phases=()):
    d_ff, d = wdn_ref.shape
    every = slice(None)
    rider_phases = iter(rider_phases)
    next(rider_phases, None)
    h = _rmsnorm(x_ref[...], g_ref[...]).astype(BF16)
    for chunk, lo in enumerate(range(0, d_ff, MXU_TILE)):
        w_gate = _matmul_weight(wup_ref, narrow_refs[0], every, slice(lo, lo + MXU_TILE))
        w_up = _matmul_weight(wup_ref, narrow_refs[0], every, slice(d_ff + lo, d_ff + lo + MXU_TILE))
        gate = jnp.dot(h, w_gate, preferred_element_type=F32)
        up = jnp.dot(h, w_up, preferred_element_type=F32)
        if chunk % RIDER_PHASE_GAP == RIDER_PHASE_GAP - 1:
            next(rider_phases, None)
        act_ref[:, lo:lo + MXU_TILE] = (gate * jax.nn.sigmoid(gate) * up).astype(BF16)
    for lo in range(0, d, MXU_TILE):
        cols = slice(lo, lo + MXU_TILE)
        w_down = _matmul_weight(wdn_ref, narrow_refs[1], every, cols)
        o_ref[:, cols] = x_ref[:, cols] + 0.5 * jnp.dot(act_ref[...], w_down, preferred_element_type=F32)


def _ffn_kernel(*refs, rider_counts, narrow, x_ring):
    n_in, n_out = rider_counts
    n_narrow = 2 if narrow else 0
    x_ref, g_ref, wup_ref, wdn_ref = refs[0:4]
    o_ref = refs[4 + n_in]
    narrow_refs = refs[5 + n_in:5 + n_in + n_narrow] if narrow else (None, None)
    first_rider_out = 5 + n_in + n_narrow
    act_ref = refs[first_rider_out + n_out]
    rider_scratch = refs[first_rider_out + n_out + 1:]
    if x_ring:
        (ring_ref, sem_ref), rider_scratch = rider_scratch[0:2], rider_scratch[2:]
        step, n_steps = pl.program_id(0), pl.num_programs(0)
        tm = ring_ref.shape[1]
        x_hbm = x_ref

        def x_copy(blk):
            slot = blk % WINDOW_RING
            return pltpu.make_async_copy(x_hbm.at[pl.ds(blk * tm, tm), :], ring_ref.at[slot], sem_ref.at[slot])

        @pl.when(step == 0)
        def _():
            for blk in range(WINDOW_RING - 1):
                x_copy(blk).start()

        @pl.when(step + WINDOW_RING - 1 < n_steps)
        def _():
            x_copy(step + WINDOW_RING - 1).start()

        x_copy(step).wait()
        x_ref = ring_ref.at[step % WINDOW_RING]
    rider_phases = ()
    if n_in:
        rider_phases = _sample_attn_phases((*refs[4:4 + n_in], *refs[first_rider_out:first_rider_out + n_out],
                                            *rider_scratch))
    _ffn_block(x_ref, g_ref, wup_ref, wdn_ref, o_ref, act_ref, narrow_refs, rider_phases)


def _ffn(x, gain, w_up, w_down, tm, rider=None, narrow=False):
    m, d = x.shape
    d_ff = w_down.shape[0]
    assert m % tm == 0 and d_ff % MXU_TILE == 0 and d % MXU_TILE == 0
    assert not narrow or m == tm, "each weight chunk is narrowed once only on a single-step grid"
    rider = rider or dict(args=[], in_specs=[], out_shape=[], out_specs=[], scratch_shapes=[], batch=m // tm)
    assert rider["batch"] == m // tm
    weights = (w_up, w_down)
    narrowed = [jax.ShapeDtypeStruct(w.shape, BF16) for w in weights] if narrow else []
    x_ring = m // tm >= WINDOW_RING
    x_spec = pl.BlockSpec(memory_space=pl.ANY) if x_ring else pl.BlockSpec((tm, d), lambda i: (i, 0))
    ring_scratch = [pltpu.VMEM((WINDOW_RING, tm, d), F32), pltpu.SemaphoreType.DMA((WINDOW_RING,))] if x_ring else []
    outs = pl.pallas_call(
        functools.partial(_ffn_kernel, rider_counts=(len(rider["args"]), len(rider["out_shape"])), narrow=narrow,
                          x_ring=x_ring),
        out_shape=[jax.ShapeDtypeStruct((m, d), F32)] + narrowed + rider["out_shape"],
        grid=(m // tm,),
        in_specs=[x_spec, _resident((1, d)), _resident((d, 2 * d_ff)), _resident((d_ff, d))] + rider["in_specs"],
        out_specs=[pl.BlockSpec((tm, d), lambda i: (i, 0))] + [_resident(w.shape) for w in narrowed]
        + rider["out_specs"],
        scratch_shapes=[pltpu.VMEM((tm, d_ff), BF16)] + ring_scratch + rider["scratch_shapes"],
        compiler_params=_params(),
        name="ffn",
    )(x, gain, w_up, w_down, *rider["args"])
    return outs[0], outs[1:]


def _proj_kernel(x_ref, g_ref, win_ref, qn_ref, kn_ref, seg_ref, *refs,
                 dils, kv_rows, with_state, narrow, pool_width, d_model):
    qkv_refs = refs[0:N_GROUPS]
    u_ref, gate_ref = refs[N_GROUPS:N_GROUPS + 2]
    state_refs = refs[N_GROUPS + 2:]
    narrow_ref = refs[-3] if narrow else None
    slab_ref, regroup_ref = refs[-2:]
    tm = x_ref.shape[0]
    h = _rmsnorm(x_ref[...], g_ref[...]).astype(BF16)

    def proj(lo, width):
        w = _matmul_weight(win_ref, narrow_ref, slice(None), slice(lo, lo + width))
        return jnp.dot(h, w, preferred_element_type=F32)

    def head_norm(y, gain):
        ms = jnp.dot((y * y).astype(BF16), seg_ref[...], preferred_element_type=F32)
        return y * lax.rsqrt(ms + EPS) * gain

    def emit(dst_ref, which, slot, val, dil):
        if dil == 1:
            dst_ref[0, 0, which] = val.astype(dst_ref.dtype)
            return
        for s in range(SLABS):
            slab_ref[slot, s] = val[:, s * LANES:(s + 1) * LANES]
        n = tm // dil
        if dil <= SINGLE_OP_STRIDE:
            for r in range(dil):
                parts = [slab_ref[slot, s, pl.ds(r, n, stride=dil), :] for s in range(SLABS)]
                dst_ref[0, r, which] = jnp.concatenate(parts, axis=-1).astype(dst_ref.dtype)
            return
        outer = SINGLE_OP_STRIDE
        inner = dil // outer
        assert inner <= SINGLE_OP_STRIDE
        quarter = tm // outer
        for a in range(outer):
            for s in range(SLABS):
                regroup_ref[s, a * quarter:(a + 1) * quarter, :] = slab_ref[slot, s, pl.ds(a, quarter, stride=outer), :]
        for r in range(dil):
            start = (r % outer) * quarter + r // outer
            parts = [regroup_ref[s, pl.ds(start, n, stride=inner), :] for s in range(SLABS)]
            dst_ref[0, r, which] = jnp.concatenate(parts, axis=-1).astype(dst_ref.dtype)

    def finish(g, which, y):
        cols = slice(g * GROUP_WIDTH, (g + 1) * GROUP_WIDTH)
        if which == 0:
            y = head_norm(y, qn_ref[:, cols]) * (HEAD_DIM ** -0.5 * LOG2E)
        elif which == 1:
            y = head_norm(y, kn_ref[:, cols])
        emit(qkv_refs[g], which, 3 * sum(d > 1 for d in dils[:g]) + which, y, dils[g])
        if with_state and which > 0:
            state_refs[g][0, which - 1] = y.T[:, tm - kv_rows[g]:]

    pending = None
    for g in range(N_GROUPS):
        for which in range(3):
            y = proj(which * ATTN_WIDTH + g * GROUP_WIDTH, GROUP_WIDTH)
            if pending is not None:
                finish(*pending)
            pending = (g, which, y)
    u = proj(3 * ATTN_WIDTH, pool_width)
    finish(*pending)
    u_ref[...] = u
    if with_state:
        state_refs[N_GROUPS][0] = u[tm - POOL_HALO:, :]
    gates_base = 3 * ATTN_WIDTH + pool_width
    for lo in range(0, 2 * d_model, MXU_TILE):
        gate_ref[:, lo:lo + MXU_TILE] = jax.nn.sigmoid(proj(gates_base + lo, MXU_TILE)).astype(BF16)


def _proj(x, gain, w_in, q_gain, k_gain, seg, *, tm, n_seq, dils, qkv_dtype, keep_rows, narrow=False):
    m, d = x.shape
    pool_width = w_in.shape[1] - 3 * ATTN_WIDTH - 2 * d
    seq_len = m // n_seq
    assert seq_len % tm == 0 and all(tm % dil == 0 for dil in dils)
    blocks_per_seq = seq_len // tm
    with_state = keep_rows is not None
    seq_block = lambda i: (i // blocks_per_seq, i % blocks_per_seq)

    row = lambda w: pl.BlockSpec((tm, w), lambda i: (i, 0))
    out_shape, out_specs = [], []
    for g in range(N_GROUPS):
        out_shape.append(jax.ShapeDtypeStruct((n_seq, dils[g], 3, seq_len // dils[g], GROUP_WIDTH), qkv_dtype))
        out_specs.append(pl.BlockSpec((1, dils[g], 3, tm // dils[g], GROUP_WIDTH),
                                      lambda i: (seq_block(i)[0], 0, 0, seq_block(i)[1], 0)))
    out_shape += [jax.ShapeDtypeStruct((m, pool_width), F32), jax.ShapeDtypeStruct((m, 2 * d), BF16)]
    out_specs += [row(pool_width), row(2 * d)]
    kv_rows = kv_first = ()
    if with_state:
        kv_rows = tuple(min(r, tm) for r in keep_rows)
        kv_first = tuple(blocks_per_seq - keep_rows[g] // kv_rows[g] for g in range(N_GROUPS))
        for g in range(N_GROUPS):
            out_shape.append(jax.ShapeDtypeStruct((n_seq, 2, GROUP_WIDTH, keep_rows[g]), F32))
            out_specs.append(pl.BlockSpec(
                (1, 2, GROUP_WIDTH, kv_rows[g]),
                lambda i, first=kv_first[g]: (seq_block(i)[0], 0, 0, jnp.maximum(seq_block(i)[1] - first, 0))))
        out_shape.append(jax.ShapeDtypeStruct((n_seq, POOL_HALO, pool_width), F32))
        out_specs.append(pl.BlockSpec((1, POOL_HALO, pool_width), lambda i: (seq_block(i)[0], 0, 0)))
    if narrow:
        assert m == tm, "each weight chunk is narrowed once only on a single-step grid"
        out_shape.append(jax.ShapeDtypeStruct(w_in.shape, BF16))
        out_specs.append(_resident(w_in.shape))
    return pl.pallas_call(
        functools.partial(_proj_kernel, dils=dils, kv_rows=kv_rows, with_state=with_state, narrow=narrow,
                          pool_width=pool_width, d_model=d),
        out_shape=out_shape,
        grid=(m // tm,),
        in_specs=[row(d), _resident((1, d)), _resident(w_in.shape),
                  _resident((1, ATTN_WIDTH)), _resident((1, ATTN_WIDTH)),
                  _resident((GROUP_WIDTH, GROUP_WIDTH))],
        out_specs=out_specs,
        scratch_shapes=[pltpu.VMEM((max(1, 3 * sum(dil > 1 for dil in dils)), SLABS, tm, LANES), F32),
                        pltpu.VMEM((SLABS, tm, LANES), F32)],
        compiler_params=_params(),
        name="proj",
    )(x, gain, w_in, q_gain, k_gain, seg)


def _attn_kernel(*refs, g, dil, n_blocks, unroll, n_cast):
    qkv_ref, bucket_ref, hmask_ref, table_ref = refs[0:4]
    wide_refs = refs[4:4 + n_cast]
    out_ref = refs[4 + n_cast]
    narrow_refs = refs[5 + n_cast:5 + 2 * n_cast]
    bias_ref = refs[5 + 2 * n_cast]
    for wide_ref, narrow_ref in zip(wide_refs, narrow_refs):
        narrow_ref[...] = wide_ref[...].astype(BF16)

    @pl.when(pl.program_id(0) == 0)
    def _():
        _build_bias(bias_ref.at[1], bucket_ref, table_ref, g)
        bias_ref[0, :, 0:QBLK] = bias_ref[1, :, QBLK:2 * QBLK]
        bias_ref[0, :, QBLK:2 * QBLK] = jnp.full((HEADS_PER_GROUP * QBLK, QBLK), NEG, F32)

    first_head = lax.broadcasted_iota(jnp.int32, (QBLK, LANES), 1) < HEAD_DIM
    heads = range(HEADS_PER_GROUP)

    def per_slab(col):
        part = lambda h: jnp.broadcast_to(col[h * QBLK:(h + 1) * QBLK], (QBLK, LANES))
        return [jnp.where(first_head, part(2 * s), part(2 * s + 1)) for s in range(SLABS)]

    def body(n, carry):
        res = n // n_blocks
        i = n % n_blocks
        row0 = pl.multiple_of(i * QBLK, QBLK)
        span0 = pl.multiple_of(jnp.maximum(i - 1, 0) * QBLK, QBLK)
        q = qkv_ref[0, res, 0, pl.ds(row0, QBLK), :]
        qs = jnp.concatenate([q * hmask_ref[h] for h in heads], axis=0)
        s = lax.dot_general(qs, qkv_ref[0, res, 1, pl.ds(span0, 2 * QBLK), :], NT_DIMS,
                            preferred_element_type=F32) + bias_ref[jnp.minimum(i, 1)]
        m = jnp.max(s, axis=-1, keepdims=True)
        p = jnp.exp2(s - m)
        l = jnp.sum(p, axis=-1, keepdims=True)
        p = p.astype(BF16)
        v = qkv_ref[0, res, 2, pl.ds(span0, 2 * QBLK), :]
        p_wide = jnp.concatenate([p[h * QBLK:(h + 1) * QBLK] for h in heads], axis=1)
        v_tall = jnp.concatenate([v * hmask_ref[h] for h in heads], axis=0)
        pv = jnp.dot(p_wide, v_tall, preferred_element_type=F32)
        token0 = res + dil * row0
        dst = pl.ds(token0, QBLK) if dil == 1 else pl.ds(token0, QBLK, stride=dil)
        for s_, (m_s, l_s) in enumerate(zip(per_slab(m), per_slab(l))):
            out_ref[0, 0, s_, dst, :] = pv[:, s_ * LANES:(s_ + 1) * LANES] / l_s
            out_ref[0, 1, s_, dst, :] = m_s + jnp.log2(l_s)
        return carry

    lax.fori_loop(0, dil * n_blocks, body, 0, unroll=unroll)


def _prompt_buckets(g):
    win, dil = ATTN_GROUPS[g]
    delta = np.arange(QBLK)[:, None] + QBLK - np.arange(2 * QBLK)[None, :]
    valid = (delta >= 0) & (delta <= win // dil)
    return np.where(valid, _t5_buckets(dil * np.clip(delta, 0, win // dil)), -1).astype(np.int32)


def _prompt_attention(qkv, table, g, to_narrow=()):
    _, dil = ATTN_GROUPS[g]
    batch, _, _, sub, _ = qkv.shape
    n_blocks = sub // QBLK
    assert sub % QBLK == 0 and n_blocks >= 2 and (dil * n_blocks) % ATTN_UNROLL == 0
    assert all(w.shape[0] % (16 * batch) == 0 for w in to_narrow)
    cast_specs = [pl.BlockSpec((w.shape[0] // batch, w.shape[1]), lambda b: (b, 0)) for w in to_narrow]
    head_of_lane = np.arange(GROUP_WIDTH) // HEAD_DIM
    hmask = head_of_lane[None, None, :] == np.arange(HEADS_PER_GROUP)[:, None, None]
    in_spec = pl.BlockSpec((1, dil, 3, sub, GROUP_WIDTH), lambda b: (b, 0, 0, 0, 0))
    out_spec = pl.BlockSpec((1, 2, SLABS, sub * dil, LANES), lambda b: (b, 0, 0, 0, 0))
    out_sds = jax.ShapeDtypeStruct((batch, 2, SLABS, sub * dil, LANES), F32)
    outs = pl.pallas_call(
        functools.partial(_attn_kernel, g=g, dil=dil, n_blocks=n_blocks, unroll=ATTN_UNROLL,
                          n_cast=len(to_narrow)),
        out_shape=[out_sds] + [jax.ShapeDtypeStruct(w.shape, BF16) for w in to_narrow],
        grid=(batch,),
        in_specs=[in_spec, _resident((QBLK, 2 * QBLK)), _resident(hmask.shape), _SMEM] + cast_specs,
        out_specs=[out_spec] + cast_specs,
        scratch_shapes=[pltpu.VMEM((2, HEADS_PER_GROUP * QBLK, 2 * QBLK), F32)],
        compiler_params=_params(),
        name=f"attn_g{g}",
    )(qkv, jnp.asarray(_prompt_buckets(g)), jnp.asarray(hmask, BF16), table, *to_narrow)
    return outs[0], outs[1:]


def _sample_attn_phases(refs):
    n = N_GROUPS
    qkv_refs, cache_refs, bucket_refs, table_ref = refs[0:n], refs[n:2 * n], refs[2 * n:3 * n], refs[3 * n]
    out_refs, win_refs, bias_refs = refs[3 * n + 1:4 * n + 1], refs[4 * n + 1:5 * n + 1], refs[5 * n + 1:6 * n + 1]
    ring_refs, sem_ref = refs[6 * n + 1:7 * n + 1], refs[7 * n + 1]
    step, n_steps = pl.program_id(0), pl.num_programs(0)

    def window_copy(g, seq):
        slot = seq % WINDOW_RING
        return pltpu.make_async_copy(cache_refs[g].at[seq], ring_refs[g].at[slot], sem_ref.at[g, slot])

    @pl.when(step == 0)
    def _():
        for g in range(N_GROUPS):
            _build_bias(bias_refs[g], bucket_refs[g], table_ref, g)
            for seq in range(WINDOW_RING - 1):
                window_copy(g, seq).start()

    @pl.when(step + WINDOW_RING - 1 < n_steps)
    def _():
        for g in range(N_GROUPS):
            window_copy(g, step + WINDOW_RING - 1).start()

    for g in range(N_GROUPS):
        window_copy(g, step).wait()
    slot = step % WINDOW_RING

    t_new = qkv_refs[0].shape[3]
    lane_head = lax.broadcasted_iota(jnp.int32, (t_new, GROUP_WIDTH), 1) // HEAD_DIM
    new_lane = lax.broadcasted_iota(jnp.int32, (GROUP_WIDTH, LANES), 1) >= LANES - t_new
    pad = jnp.zeros((LANES - t_new, GROUP_WIDTH), F32)

    groups = []
    for g in range(N_GROUPS):
        q, k, v = (qkv_refs[g][0, 0, which] for which in range(3))
        cache_ref, win_ref = ring_refs[g].at[slot], win_refs[g]
        past = cache_ref.shape[2]
        new_rows = [jnp.concatenate([pad, rows_], axis=0) for rows_ in (k, v)]
        for c in range(2):
            shifted = pltpu.roll(cache_ref[c], past - t_new, 1)
            if past > LANES:
                win_ref[0, c, :, 0:past - LANES] = shifted[:, 0:past - LANES]
            win_ref[0, c, :, past - LANES:past] = jnp.where(new_lane, new_rows[c].T, shifted[:, past - LANES:])
        groups.append(dict(past=past, qs=_stack_heads(q, lane_head),
                           old=[cache_ref[c].astype(BF16) for c in range(2)],
                           new=[rows_.astype(BF16) for rows_ in new_rows]))
    yield

    for g, grp in enumerate(groups):
        past, bias_ref = grp["past"], bias_refs[g]
        s_old = jnp.dot(grp["qs"], grp["old"][0], preferred_element_type=F32) + bias_ref[:, 0:past]
        s_new = lax.dot_general(grp["qs"], grp["new"][0], NT_DIMS,
                                preferred_element_type=F32) + bias_ref[:, past:past + LANES]
        m = jnp.maximum(jnp.max(s_old, axis=-1, keepdims=True), jnp.max(s_new, axis=-1, keepdims=True))
        p_old = jnp.exp2(s_old - m)
        p_new = jnp.exp2(s_new - m)
        grp.update(m=m, l=jnp.sum(p_old, axis=-1, keepdims=True) + jnp.sum(p_new, axis=-1, keepdims=True),
                   p_old=p_old.astype(BF16), p_new=p_new.astype(BF16))
    yield

    for g, grp in enumerate(groups):
        pv = (lax.dot_general(grp["p_old"], grp["old"][1], NT_DIMS, preferred_element_type=F32)
              + jnp.dot(grp["p_new"], grp["new"][1], preferred_element_type=F32)) / grp["l"]
        lse = jnp.broadcast_to(grp["m"] + jnp.log2(grp["l"]), pv.shape)
        rows = lambda a: [a[h * t_new:(h + 1) * t_new] for h in range(HEADS_PER_GROUP)]
        for which, val in enumerate((_head_select(rows(pv), lane_head), _head_select(rows(lse), lane_head))):
            for s_ in range(SLABS):
                out_refs[g][0, which, s_] = val[:, s_ * LANES:(s_ + 1) * LANES]


def _sample_attn_kernel(*refs):
    for _ in _sample_attn_phases(refs):
        pass


def _sample_buckets(g, past, t_new):
    win, dil = ATTN_GROUPS[g]
    lane = np.arange(past + LANES)[None, :]
    key_pos = np.where(lane < past, lane, lane - (LANES - t_new))
    delta = past + np.arange(t_new)[:, None] - key_pos
    valid = ((lane < past) | (lane >= past + LANES - t_new)) & (delta >= 0) & (delta % dil == 0) & (delta <= win)
    return np.where(valid, _t5_buckets(np.clip(delta, 0, win)), -1).astype(np.int32)


def _sample_attention_parts(qkv, caches, table):
    batch = caches[0].shape[0]
    assert batch >= WINDOW_RING
    t_new = qkv[0].shape[3] // batch
    pasts = [c.shape[3] for c in caches]
    window_ring = [pltpu.VMEM((WINDOW_RING,) + c.shape[1:], F32) for c in caches]
    buckets = [jnp.asarray(_sample_buckets(g, pasts[g], t_new)) for g in range(N_GROUPS)]
    cache_spec = lambda p: pl.BlockSpec((1, 2, GROUP_WIDTH, p), lambda b: (b, 0, 0, 0))
    new_spec = pl.BlockSpec((1, 1, 3, t_new, GROUP_WIDTH), lambda b: (0, 0, 0, b, 0))
    out_spec = pl.BlockSpec((1, 2, SLABS, t_new, LANES), lambda b: (0, 0, 0, b, 0))
    out_sds = jax.ShapeDtypeStruct((1, 2, SLABS, batch * t_new, LANES), F32)
    return dict(
        batch=batch,
        args=[*qkv, *caches, *buckets, table],
        out_shape=[out_sds] * N_GROUPS + [jax.ShapeDtypeStruct(c.shape, F32) for c in caches],
        in_specs=[new_spec] * N_GROUPS + [pl.BlockSpec(memory_space=pl.ANY)] * N_GROUPS
        + [_resident(b.shape) for b in buckets] + [_SMEM],
        out_specs=[out_spec] * N_GROUPS + [cache_spec(p) for p in pasts],
        scratch_shapes=[pltpu.VMEM((HEADS_PER_GROUP * t_new, p + LANES), F32) for p in pasts] + window_ring
        + [pltpu.SemaphoreType.DMA((N_GROUPS, WINDOW_RING))])


def _sample_attention(parts):
    return pl.pallas_call(
        _sample_attn_kernel, grid=(parts["batch"],), out_shape=parts["out_shape"],
        in_specs=parts["in_specs"], out_specs=parts["out_specs"], scratch_shapes=parts["scratch_shapes"],
        compiler_params=_params(), name="sample_attn")(*parts["args"])


def _back_kernel(x_ref, ol0_ref, ol1_ref, ol2_ref, u_ref, halo_ref, gate_ref,
                 wab_ref, wpg_ref, pscale_ref, wpb_ref, wout_ref, g2_ref, wup_ref, wdn_ref,
                 out_ref, ue_ref, level_ref, mixed_ref, act_ref, attn_ref, pooled_ref, merged_ref,
                 *, pos_base, blocks_per_seq):
    block = pl.program_id(0)
    tm, d_model = x_ref.shape
    n_seq = halo_ref.shape[0]
    t = tm // n_seq
    pool_width = u_ref.shape[1]
    gw = pool_width // len(POOL_WINDOWS)

    for s in range(SLABS):
        ol_refs = (ol0_ref, ol1_ref, ol2_ref)
        lses = [r[0, 1, s] for r in ol_refs]
        top = jnp.maximum(jnp.maximum(lses[0], lses[1]), lses[2])
        num = jnp.zeros_like(top)
        den = jnp.zeros_like(top)
        for ol_ref, lse in zip(ol_refs, lses):
            e = jnp.exp2(lse - top)
            num = num + e * ol_ref[0, 0, s]
            den = den + e
        attn_ref[:, s * LANES:(s + 1) * LANES] = (num / den).astype(BF16)

    start = pos_base + (block % blocks_per_seq) * t
    first, end = POOL_PAD - POOL_HALO, POOL_PAD + t
    ue_ref[:, 0:first, :] = jnp.zeros((n_seq, first, pool_width), F32)
    ue_ref[:, first:POOL_PAD, :] = jnp.where(start > 0, halo_ref[...], 0.0)
    ue_ref[:, POOL_PAD:, :] = u_ref[...].reshape(n_seq, t, pool_width)
    level_ref[:, :, 0:first, :] = jnp.zeros((2, n_seq, first, gw), F32)
    pos = start + lax.broadcasted_iota(jnp.int32, (1, t, gw), 1)
    for gi, win in enumerate(POOL_WINDOWS):
        cols = slice(gi * gw, (gi + 1) * gw)
        read = lambda lo, hi, cols=cols: ue_ref[:, lo:hi, cols]
        span = 1
        while 2 * span < win:
            buf = (span.bit_length() - 1) % 2
            level_ref[buf, :, first:, :] = read(first, end) + read(first - span, end - span)
            read = lambda lo, hi, buf=buf: level_ref[buf, :, lo:hi, :]
            span *= 2
        s = read(POOL_PAD, end) + read(POOL_PAD - span, end - span)
        cnt = jnp.minimum(pos + 1, win).astype(F32)
        d = (s / cnt - ue_ref[:, POOL_PAD:, cols]).reshape(tm, gw)
        y = jnp.dot(d.astype(BF16), wpg_ref[gi], preferred_element_type=F32)
        pooled_ref[:, cols] = (y * pscale_ref[:, cols]).astype(BF16)

    for lo in range(0, d_model, MXU_TILE):
        cols = slice(lo, lo + MXU_TILE)
        branch_a = jnp.dot(attn_ref[...], wab_ref[:, cols], preferred_element_type=F32)
        branch_b = jnp.dot(pooled_ref[...], wpb_ref[:, cols], preferred_element_type=F32)
        gate_a = gate_ref[:, cols].astype(F32)
        gate_b = gate_ref[:, d_model + lo:d_model + lo + MXU_TILE].astype(F32)
        merged_ref[:, cols] = (gate_a * branch_a + gate_b * branch_b).astype(BF16)
    for lo in range(0, d_model, MXU_TILE):
        cols = slice(lo, lo + MXU_TILE)
        mixed_ref[:, cols] = x_ref[:, cols] + jnp.dot(merged_ref[...], wout_ref[:, cols],
                                                      preferred_element_type=F32)

    _ffn_block(mixed_ref, g2_ref, wup_ref, wdn_ref, out_ref, act_ref)


def _back(x, ol, u, halo, gates, merge_w, ffn_w, *, tm, halo_block, halo_index, pos_base, blocks_per_seq):
    m, d = x.shape
    pool_width = u.shape[1]
    n_seq = halo_block[0]
    d_ff = ffn_w[2].shape[0]
    row = lambda wd: pl.BlockSpec((tm, wd), lambda i: (i, 0))
    slab = pl.BlockSpec((1, 2, SLABS, tm, LANES),
                        lambda i: (i // blocks_per_seq, 0, 0, i % blocks_per_seq, 0))
    return pl.pallas_call(
        functools.partial(_back_kernel, pos_base=pos_base, blocks_per_seq=blocks_per_seq),
        out_shape=jax.ShapeDtypeStruct((m, d), F32),
        grid=(m // tm,),
        in_specs=[row(d)] + [slab] * N_GROUPS
        + [row(pool_width), pl.BlockSpec(halo_block, halo_index), row(2 * d)]
        + [_resident(a.shape) for a in (*merge_w, *ffn_w)],
        out_specs=row(d),
        scratch_shapes=[pltpu.VMEM((n_seq, POOL_PAD + tm // n_seq, pool_width), F32),
                        pltpu.VMEM((2, n_seq, POOL_PAD + tm // n_seq, pool_width // len(POOL_WINDOWS)), F32),
                        pltpu.VMEM((tm, d), F32), pltpu.VMEM((tm, d_ff), BF16),
                        pltpu.VMEM((tm, GROUP_WIDTH), BF16), pltpu.VMEM((tm, pool_width), BF16),
                        pltpu.VMEM((tm, d), BF16)],
        compiler_params=_params(),
        name="back",
    )(x, *ol, u, halo, gates, *merge_w, *ffn_w)


def _window_in(cache):
    n_seq, rows = cache.shape[0:2]
    return jnp.transpose(cache, (0, 2, 3, 4, 1)).reshape(n_seq, 2, GROUP_WIDTH, rows)


def _window_out(kv):
    n_seq, _, _, rows = kv.shape
    return jnp.transpose(kv.reshape(n_seq, 2, HEADS_PER_GROUP, HEAD_DIM, rows), (0, 4, 1, 2, 3))


def kernel(x_prompt, x_sample, cache_kv_w128, cache_kv_w512, cache_kv_w2048, state_pool, rel_bias_table,
           norm_ffn1, ffn1_w_up, ffn1_w_down, norm_mix, w_in, q_norm, k_norm, pool_w_group, pool_scale,
           w_attn_branch, w_pool_branch, w_out, norm_ffn2, ffn2_w_up, ffn2_w_down):
    batch, seq, d_model = x_prompt.shape
    dec_batch, dec_seq, _ = x_sample.shape
    depth = norm_ffn1.shape[0]
    pool_width = state_pool.shape[-1]
    tm = ROW_BLOCK
    ms = dec_batch * dec_seq
    seg = jnp.asarray(np.kron(np.eye(HEADS_PER_GROUP), np.full((HEAD_DIM, HEAD_DIM), 1.0 / HEAD_DIM)), BF16)
    table = rel_bias_table.astype(F32)
    caches_in = (cache_kv_w128, cache_kv_w512, cache_kv_w2048)
    keep_prompt = tuple(min(win, seq) for win, _ in ATTN_GROUPS)
    dils = tuple(dil for _, dil in ATTN_GROUPS)

    xp = x_prompt.reshape(batch * seq, d_model)
    xs = x_sample.reshape(ms, d_model)
    kv_p, kv_s, pool_p, pool_s = ([], [], []), ([], [], []), [], []
    for layer in range(depth):
        gain = lambda a: a[layer].reshape(1, -1).astype(F32)
        qg, kg = gain(q_norm), gain(k_norm)
        pool_groups = pool_w_group[layer]
        late_w = (ffn2_w_down[layer], w_attn_branch[layer], pool_groups.reshape(-1, pool_groups.shape[-1]),
                  w_pool_branch[layer], w_out[layer])

        xs, (wup1, wdn1) = _ffn(xs, gain(norm_ffn1), ffn1_w_up[layer], ffn1_w_down[layer], ms, narrow=True)
        outs = _proj(xs, gain(norm_mix), w_in[layer], qg, kg, seg, tm=ms, n_seq=1, dils=(1,) * N_GROUPS,
                     qkv_dtype=F32, keep_rows=None, narrow=True)
        qkv_s, u_s, gates_s, win = outs[0:N_GROUPS], outs[N_GROUPS], outs[N_GROUPS + 1], outs[N_GROUPS + 2]
        sample_parts = _sample_attention_parts(qkv_s, [_window_in(c[layer]) for c in caches_in], table)

        if (batch * seq) // tm == dec_batch:
            xp, sample_outs = _ffn(xp, gain(norm_ffn1), wup1, wdn1, tm, rider=sample_parts)
        else:
            xp, _ = _ffn(xp, gain(norm_ffn1), wup1, wdn1, tm)
            sample_outs = _sample_attention(sample_parts)
        outs = _proj(xp, gain(norm_mix), win, qg, kg, seg, tm=2 * tm, n_seq=batch, dils=dils,
                     qkv_dtype=BF16, keep_rows=keep_prompt)
        qkv, u, gates = outs[0:N_GROUPS], outs[N_GROUPS], outs[N_GROUPS + 1]
        windows, pstate = outs[N_GROUPS + 2:2 * N_GROUPS + 2], outs[2 * N_GROUPS + 2]
        narrowing = ((ffn2_w_up[layer],), late_w) + ((),) * (N_GROUPS - 2)
        ol, narrowed = zip(*[_prompt_attention(qkv[g], table, g, to_narrow=narrowing[g])
                             for g in range(N_GROUPS)])
        (wup2,), (wdn2, wab, wpg, wpb, wout) = narrowed[0:2]
        ffn2_w = (gain(norm_ffn2), wup2, wdn2)
        merge_w = (wab, wpg.reshape(pool_groups.shape), gain(pool_scale), wpb, wout)
        xp = _back(xp, ol, u, u.reshape(-1, POOL_HALO, pool_width), gates, merge_w, ffn2_w, tm=tm,
                   halo_block=(1, POOL_HALO, pool_width),
                   halo_index=lambda blk: (jnp.maximum(blk * (tm // POOL_HALO) - 1, 0), 0, 0),
                   pos_base=0, blocks_per_seq=seq // tm)
        for g in range(N_GROUPS):
            kv_p[g].append(_window_out(windows[g]))
        pool_p.append(pstate[:, POOL_HALO - POOL_STATE:])

        ol, windows = sample_outs[0:N_GROUPS], sample_outs[N_GROUPS:2 * N_GROUPS]
        history = jnp.concatenate(
            [jnp.zeros((dec_batch, POOL_HALO - POOL_STATE, pool_width), F32), state_pool[layer]], axis=1)
        xs = _back(xs, ol, u_s, history, gates_s, merge_w, ffn2_w, tm=ms,
                   halo_block=(dec_batch, POOL_HALO, pool_width), halo_index=lambda blk: (0, 0, 0),
                   pos_base=PAST_LEN, blocks_per_seq=1)
        for g in range(N_GROUPS):
            kv_s[g].append(_window_out(windows[g]))
        ue = jnp.concatenate([state_pool[layer], u_s.reshape(dec_batch, dec_seq, pool_width)], axis=1)
        pool_s.append(ue[:, ue.shape[1] - POOL_STATE:])

    stack = lambda xs_: jnp.stack(xs_, axis=0)
    return (xp.reshape(batch, seq, d_model), xs.reshape(dec_batch, dec_seq, d_model),
            stack(kv_p[0]), stack(kv_p[1]), stack(kv_p[2]), stack(pool_p),
            stack(kv_s[0]), stack(kv_s[1]), stack(kv_s[2]), stack(pool_s))
```

```python
import functools
import math

import numpy as np
import jax
import jax.numpy as jnp
from jax import lax
from jax.experimental import pallas as pl
from jax.experimental.pallas import tpu as pltpu

HEAD_DIM = 64
HEADS_PER_GROUP = 4
GROUP_WIDTH = HEADS_PER_GROUP * HEAD_DIM
ATTN_GROUPS = ((128, 1), (512, 4), (2048, 16))
N_GROUPS = len(ATTN_GROUPS)
ATTN_WIDTH = N_GROUPS * GROUP_WIDTH
N_BUCKETS = 32
MAX_DISTANCE = 2048
POOL_WINDOWS = (2, 4, 8, 16)
POOL_STATE = max(POOL_WINDOWS) - 1
POOL_HALO = 16
POOL_PAD = POOL_HALO + 8
EPS = 1e-6
PAST_LEN = 8192
NEG = -1e30
LOG2E = math.log2(math.e)
LANES = 128
MXU_TILE = 256
SLABS = GROUP_WIDTH // LANES
SINGLE_OP_STRIDE = 4
WINDOW_RING = 3
WINDOW_DMA_PRIORITY = 1
RIDER_PHASE_GAP = 2
ROW_BLOCK = 512
QBLK = 128
ATTN_UNROLL = 16
VMEM_LIMIT = 58 * 1024 * 1024

F32 = jnp.float32
BF16 = jnp.bfloat16
NT_DIMS = (((1,), (1,)), ((), ()))


def _t5_buckets(distance):
    max_exact = N_BUCKETS // 2
    d = np.asarray(distance, dtype=np.int32)
    ratio = np.log(np.maximum(d, 1).astype(np.float32) / np.float32(max_exact))
    large = max_exact + (ratio / np.float32(math.log(MAX_DISTANCE / max_exact))
                         * (N_BUCKETS - max_exact)).astype(np.int32)
    large = np.minimum(large, N_BUCKETS - 1)
    return np.where(d < max_exact, d, large).astype(np.int32)


def _params(n_grid_dims=1):
    return pltpu.CompilerParams(dimension_semantics=("arbitrary",) * n_grid_dims,
                                vmem_limit_bytes=VMEM_LIMIT)


def _resident(shape):
    return pl.BlockSpec(shape, lambda *_: (0,) * len(shape), pipeline_mode=pl.Buffered(1))


_SMEM = pl.BlockSpec(memory_space=pltpu.SMEM)


def _rmsnorm(x, g):
    ms = jnp.mean(x * x, axis=-1, keepdims=True)
    return x * lax.rsqrt(ms + EPS) * g


def _head_select(parts, lane_head):
    out = jnp.where(lane_head == 0, parts[0], 0.0)
    for h in range(1, HEADS_PER_GROUP):
        out = jnp.where(lane_head == h, parts[h], out)
    return out


def _stack_heads(q, lane_head):
    return jnp.concatenate([jnp.where(lane_head == h, q, 0.0)
                            for h in range(HEADS_PER_GROUP)], axis=0).astype(BF16)


def _build_bias(bias_ref, bucket_ref, table_ref, g):
    buckets = bucket_ref[...]
    rows = buckets.shape[0]
    for h in range(HEADS_PER_GROUP):
        acc = jnp.full(buckets.shape, NEG, F32)
        for b in range(N_BUCKETS):
            acc = jnp.where(buckets == b, table_ref[b, g * HEADS_PER_GROUP + h] * LOG2E, acc)
        bias_ref[h * rows:(h + 1) * rows, :] = acc


def _matmul_weight(w_ref, narrow_ref, rows, cols):
    w = w_ref[rows, cols]
    if narrow_ref is not None:
        w = w.astype(BF16)
        narrow_ref[rows, cols] = w
    return w


def _ffn_block(x_ref, g_ref, wup_ref, wdn_ref, o_ref, act_ref, narrow_refs=(None, None), rider_phases=()):
    d_ff, d = wdn_ref.shape
    every = slice(None)
    rider_phases = iter(rider_phases)
    next(rider_phases, None)
    h = _rmsnorm(x_ref[...], g_ref[...]).astype(BF16)
    for chunk, lo in enumerate(range(0, d_ff, MXU_TILE)):
        w_gate = _matmul_weight(wup_ref, narrow_refs[0], every, slice(lo, lo + MXU_TILE))
        w_up = _matmul_weight(wup_ref, narrow_refs[0], every, slice(d_ff + lo, d_ff + lo + MXU_TILE))
        gate = jnp.dot(h, w_gate, preferred_element_type=F32)
        up = jnp.dot(h, w_up, preferred_element_type=F32)
        if chunk % RIDER_PHASE_GAP == RIDER_PHASE_GAP - 1:
            next(rider_phases, None)
        act_ref[:, lo:lo + MXU_TILE] = (gate * jax.nn.sigmoid(gate) * up).astype(BF16)
    for lo in range(0, d, MXU_TILE):
        cols = slice(lo, lo + MXU_TILE)
        w_down = _matmul_weight(wdn_ref, narrow_refs[1], every, cols)
        o_ref[:, cols] = x_ref[:, cols] + 0.5 * jnp.dot(act_ref[...], w_down, preferred_element_type=F32)


def _ffn_kernel(*refs, rider_counts, narrow):
    n_in, n_out = rider_counts
    n_narrow = 2 if narrow else 0
    x_ref, g_ref, wup_ref, wdn_ref = refs[0:4]
    o_ref = refs[4 + n_in]
    narrow_refs = refs[5 + n_in:5 + n_in + n_narrow] if narrow else (None, None)
    first_rider_out = 5 + n_in + n_narrow
    act_ref = refs[first_rider_out + n_out]
    rider_phases = ()
    if n_in:
        rider_phases = _sample_attn_phases((*refs[4:4 + n_in], *refs[first_rider_out:first_rider_out + n_out],
                                            *refs[first_rider_out + n_out + 1:]))
    _ffn_block(x_ref, g_ref, wup_ref, wdn_ref, o_ref, act_ref, narrow_refs, rider_phases)


def _ffn(x, gain, w_up, w_down, tm, rider=None, narrow=False):
    m, d = x.shape
    d_ff = w_down.shape[0]
    assert m % tm == 0 and d_ff % MXU_TILE == 0 and d % MXU_TILE == 0
    assert not narrow or m == tm, "each weight chunk is narrowed once only on a single-step grid"
    rider = rider or dict(args=[], in_specs=[], out_shape=[], out_specs=[], scratch_shapes=[], batch=m // tm)
    assert rider["batch"] == m // tm
    weights = (w_up, w_down)
    narrowed = [jax.ShapeDtypeStruct(w.shape, BF16) for w in weights] if narrow else []
    outs = pl.pallas_call(
        functools.partial(_ffn_kernel, rider_counts=(len(rider["args"]), len(rider["out_shape"])), narrow=narrow),
        out_shape=[jax.ShapeDtypeStruct((m, d), F32)] + narrowed + rider["out_shape"],
        grid=(m // tm,),
        in_specs=[pl.BlockSpec((tm, d), lambda i: (i, 0)),
                  _resident((1, d)), _resident((d, 2 * d_ff)), _resident((d_ff, d))] + rider["in_specs"],
        out_specs=[pl.BlockSpec((tm, d), lambda i: (i, 0))] + [_resident(w.shape) for w in narrowed]
        + rider["out_specs"],
        scratch_shapes=[pltpu.VMEM((tm, d_ff), BF16)] + rider["scratch_shapes"],
        compiler_params=_params(),
        name="ffn",
    )(x, gain, w_up, w_down, *rider["args"])
    return outs[0], outs[1:]


def _proj_kernel(x_ref, g_ref, win_ref, qn_ref, kn_ref, seg_ref, *refs,
                 dils, kv_rows, with_state, narrow, pool_width, d_model):
    qkv_refs = refs[0:N_GROUPS]
    u_ref, gate_ref = refs[N_GROUPS:N_GROUPS + 2]
    state_refs = refs[N_GROUPS + 2:]
    narrow_ref = refs[-3] if narrow else None
    slab_ref, regroup_ref = refs[-2:]
    tm = x_ref.shape[0]
    h = _rmsnorm(x_ref[...], g_ref[...]).astype(BF16)

    def proj(lo, width):
        w = _matmul_weight(win_ref, narrow_ref, slice(None), slice(lo, lo + width))
        return jnp.dot(h, w, preferred_element_type=F32)

    def head_norm(y, gain):
        ms = jnp.dot((y * y).astype(BF16), seg_ref[...], preferred_element_type=F32)
        return y * lax.rsqrt(ms + EPS) * gain

    def emit(dst_ref, which, slot, val, dil):
        if dil == 1:
            dst_ref[0, 0, which] = val.astype(dst_ref.dtype)
            return
        for s in range(SLABS):
            slab_ref[slot, s] = val[:, s * LANES:(s + 1) * LANES]
        n = tm // dil
        if dil <= SINGLE_OP_STRIDE:
            for r in range(dil):
                parts = [slab_ref[slot, s, pl.ds(r, n, stride=dil), :] for s in range(SLABS)]
                dst_ref[0, r, which] = jnp.concatenate(parts, axis=-1).astype(dst_ref.dtype)
            return
        outer = SINGLE_OP_STRIDE
        inner = dil // outer
        assert inner <= SINGLE_OP_STRIDE
        quarter = tm // outer
        for a in range(outer):
            for s in range(SLABS):
                regroup_ref[s, a * quarter:(a + 1) * quarter, :] = slab_ref[slot, s, pl.ds(a, quarter, stride=outer), :]
        for r in range(dil):
            start = (r % outer) * quarter + r // outer
            parts = [regroup_ref[s, pl.ds(start, n, stride=inner), :] for s in range(SLABS)]
            dst_ref[0, r, which] = jnp.concatenate(parts, axis=-1).astype(dst_ref.dtype)

    def finish(g, which, y):
        cols = slice(g * GROUP_WIDTH, (g + 1) * GROUP_WIDTH)
        if which == 0:
            y = head_norm(y, qn_ref[:, cols]) * (HEAD_DIM ** -0.5 * LOG2E)
        elif which == 1:
            y = head_norm(y, kn_ref[:, cols])
        emit(qkv_refs[g], which, 3 * sum(d > 1 for d in dils[:g]) + which, y, dils[g])
        if with_state and which > 0:
            state_refs[g][0, which - 1] = y.T[:, tm - kv_rows[g]:]

    pending = None
    for g in range(N_GROUPS):
        for which in range(3):
            y = proj(which * ATTN_WIDTH + g * GROUP_WIDTH, GROUP_WIDTH)
            if pending is not None:
                finish(*pending)
            pending = (g, which, y)
    u = proj(3 * ATTN_WIDTH, pool_width)
    finish(*pending)
    u_ref[...] = u
    if with_state:
        state_refs[N_GROUPS][0] = u[tm - POOL_HALO:, :]
    gates_base = 3 * ATTN_WIDTH + pool_width
    for lo in range(0, 2 * d_model, MXU_TILE):
        gate_ref[:, lo:lo + MXU_TILE] = jax.nn.sigmoid(proj(gates_base + lo, MXU_TILE)).astype(BF16)


def _proj(x, gain, w_in, q_gain, k_gain, seg, *, tm, n_seq, dils, qkv_dtype, keep_rows, narrow=False):
    m, d = x.shape
    pool_width = w_in.shape[1] - 3 * ATTN_WIDTH - 2 * d
    seq_len = m // n_seq
    assert seq_len % tm == 0 and all(tm % dil == 0 for dil in dils)
    blocks_per_seq = seq_len // tm
    with_state = keep_rows is not None
    seq_block = lambda i: (i // blocks_per_seq, i % blocks_per_seq)

    row = lambda w: pl.BlockSpec((tm, w), lambda i: (i, 0))
    out_shape, out_specs = [], []
    for g in range(N_GROUPS):
        out_shape.append(jax.ShapeDtypeStruct((n_seq, dils[g], 3, seq_len // dils[g], GROUP_WIDTH), qkv_dtype))
        out_specs.append(pl.BlockSpec((1, dils[g], 3, tm // dils[g], GROUP_WIDTH),
                                      lambda i: (seq_block(i)[0], 0, 0, seq_block(i)[1], 0)))
    out_shape += [jax.ShapeDtypeStruct((m, pool_width), F32), jax.ShapeDtypeStruct((m, 2 * d), BF16)]
    out_specs += [row(pool_width), row(2 * d)]
    kv_rows = kv_first = ()
    if with_state:
        kv_rows = tuple(min(r, tm) for r in keep_rows)
        kv_first = tuple(blocks_per_seq - keep_rows[g] // kv_rows[g] for g in range(N_GROUPS))
        for g in range(N_GROUPS):
            out_shape.append(jax.ShapeDtypeStruct((n_seq, 2, GROUP_WIDTH, keep_rows[g]), F32))
            out_specs.append(pl.BlockSpec(
                (1, 2, GROUP_WIDTH, kv_rows[g]),
                lambda i, first=kv_first[g]: (seq_block(i)[0], 0, 0, jnp.maximum(seq_block(i)[1] - first, 0))))
        out_shape.append(jax.ShapeDtypeStruct((n_seq, POOL_HALO, pool_width), F32))
        out_specs.append(pl.BlockSpec((1, POOL_HALO, pool_width), lambda i: (seq_block(i)[0], 0, 0)))
    if narrow:
        assert m == tm, "each weight chunk is narrowed once only on a single-step grid"
        out_shape.append(jax.ShapeDtypeStruct(w_in.shape, BF16))
        out_specs.append(_resident(w_in.shape))
    return pl.pallas_call(
        functools.partial(_proj_kernel, dils=dils, kv_rows=kv_rows, with_state=with_state, narrow=narrow,
                          pool_width=pool_width, d_model=d),
        out_shape=out_shape,
        grid=(m // tm,),
        in_specs=[row(d), _resident((1, d)), _resident(w_in.shape),
                  _resident((1, ATTN_WIDTH)), _resident((1, ATTN_WIDTH)),
                  _resident((GROUP_WIDTH, GROUP_WIDTH))],
        out_specs=out_specs,
        scratch_shapes=[pltpu.VMEM((max(1, 3 * sum(dil > 1 for dil in dils)), SLABS, tm, LANES), F32),
                        pltpu.VMEM((SLABS, tm, LANES), F32)],
        compiler_params=_params(),
        name="proj",
    )(x, gain, w_in, q_gain, k_gain, seg)


def _attn_kernel(*refs, g, dil, n_blocks, unroll, n_cast):
    qkv_ref, bucket_ref, hmask_ref, table_ref = refs[0:4]
    wide_refs = refs[4:4 + n_cast]
    out_ref = refs[4 + n_cast]
    narrow_refs = refs[5 + n_cast:5 + 2 * n_cast]
    bias_ref = refs[5 + 2 * n_cast]
    for wide_ref, narrow_ref in zip(wide_refs, narrow_refs):
        narrow_ref[...] = wide_ref[...].astype(BF16)

    @pl.when(pl.program_id(0) == 0)
    def _():
        _build_bias(bias_ref.at[1], bucket_ref, table_ref, g)
        bias_ref[0, :, 0:QBLK] = bias_ref[1, :, QBLK:2 * QBLK]
        bias_ref[0, :, QBLK:2 * QBLK] = jnp.full((HEADS_PER_GROUP * QBLK, QBLK), NEG, F32)

    first_head = lax.broadcasted_iota(jnp.int32, (QBLK, LANES), 1) < HEAD_DIM
    heads = range(HEADS_PER_GROUP)

    def per_slab(col):
        part = lambda h: jnp.broadcast_to(col[h * QBLK:(h + 1) * QBLK], (QBLK, LANES))
        return [jnp.where(first_head, part(2 * s), part(2 * s + 1)) for s in range(SLABS)]

    def body(n, carry):
        res = n // n_blocks
        i = n % n_blocks
        row0 = pl.multiple_of(i * QBLK, QBLK)
        span0 = pl.multiple_of(jnp.maximum(i - 1, 0) * QBLK, QBLK)
        q = qkv_ref[0, res, 0, pl.ds(row0, QBLK), :]
        qs = jnp.concatenate([q * hmask_ref[h] for h in heads], axis=0)
        s = lax.dot_general(qs, qkv_ref[0, res, 1, pl.ds(span0, 2 * QBLK), :], NT_DIMS,
                            preferred_element_type=F32) + bias_ref[jnp.minimum(i, 1)]
        m = jnp.max(s, axis=-1, keepdims=True)
        p = jnp.exp2(s - m)
        l = jnp.sum(p, axis=-1, keepdims=True)
        p = p.astype(BF16)
        v = qkv_ref[0, res, 2, pl.ds(span0, 2 * QBLK), :]
        p_wide = jnp.concatenate([p[h * QBLK:(h + 1) * QBLK] for h in heads], axis=1)
        v_tall = jnp.concatenate([v * hmask_ref[h] for h in heads], axis=0)
        pv = jnp.dot(p_wide, v_tall, preferred_element_type=F32)
        token0 = res + dil * row0
        dst = pl.ds(token0, QBLK) if dil == 1 else pl.ds(token0, QBLK, stride=dil)
        for s_, (m_s, l_s) in enumerate(zip(per_slab(m), per_slab(l))):
            out_ref[0, 0, s_, dst, :] = pv[:, s_ * LANES:(s_ + 1) * LANES] / l_s
            out_ref[0, 1, s_, dst, :] = m_s + jnp.log2(l_s)
        return carry

    lax.fori_loop(0, dil * n_blocks, body, 0, unroll=unroll)


def _prompt_buckets(g):
    win, dil = ATTN_GROUPS[g]
    delta = np.arange(QBLK)[:, None] + QBLK - np.arange(2 * QBLK)[None, :]
    valid = (delta >= 0) & (delta <= win // dil)
    return np.where(valid, _t5_buckets(dil * np.clip(delta, 0, win // dil)), -1).astype(np.int32)


def _prompt_attention(qkv, table, g, to_narrow=()):
    _, dil = ATTN_GROUPS[g]
    batch, _, _, sub, _ = qkv.shape
    n_blocks = sub // QBLK
    assert sub % QBLK == 0 and n_blocks >= 2 and (dil * n_blocks) % ATTN_UNROLL == 0
    assert all(w.shape[0] % (16 * batch) == 0 for w in to_narrow)
    cast_specs = [pl.BlockSpec((w.shape[0] // batch, w.shape[1]), lambda b: (b, 0)) for w in to_narrow]
    head_of_lane = np.arange(GROUP_WIDTH) // HEAD_DIM
    hmask = head_of_lane[None, None, :] == np.arange(HEADS_PER_GROUP)[:, None, None]
    in_spec = pl.BlockSpec((1, dil, 3, sub, GROUP_WIDTH), lambda b: (b, 0, 0, 0, 0))
    out_spec = pl.BlockSpec((1, 2, SLABS, sub * dil, LANES), lambda b: (b, 0, 0, 0, 0))
    out_sds = jax.ShapeDtypeStruct((batch, 2, SLABS, sub * dil, LANES), F32)
    outs = pl.pallas_call(
        functools.partial(_attn_kernel, g=g, dil=dil, n_blocks=n_blocks, unroll=ATTN_UNROLL,
                          n_cast=len(to_narrow)),
        out_shape=[out_sds] + [jax.ShapeDtypeStruct(w.shape, BF16) for w in to_narrow],
        grid=(batch,),
        in_specs=[in_spec, _resident((QBLK, 2 * QBLK)), _resident(hmask.shape), _SMEM] + cast_specs,
        out_specs=[out_spec] + cast_specs,
        scratch_shapes=[pltpu.VMEM((2, HEADS_PER_GROUP * QBLK, 2 * QBLK), F32)],
        compiler_params=_params(),
        name=f"attn_g{g}",
    )(qkv, jnp.asarray(_prompt_buckets(g)), jnp.asarray(hmask, BF16), table, *to_narrow)
    return outs[0], outs[1:]


def _sample_attn_phases(refs):
    n = N_GROUPS
    qkv_refs, cache_refs, bucket_refs, table_ref = refs[0:n], refs[n:2 * n], refs[2 * n:3 * n], refs[3 * n]
    out_refs, win_refs, bias_refs = refs[3 * n + 1:4 * n + 1], refs[4 * n + 1:5 * n + 1], refs[5 * n + 1:6 * n + 1]
    ring_refs, sem_ref = refs[6 * n + 1:7 * n + 1], refs[7 * n + 1]
    step, n_steps = pl.program_id(0), pl.num_programs(0)

    def window_copy(g, seq):
        slot = seq % WINDOW_RING
        return pltpu.make_async_copy(cache_refs[g].at[seq], ring_refs[g].at[slot], sem_ref.at[g, slot])

    @pl.when(step == 0)
    def _():
        for g in range(N_GROUPS):
            _build_bias(bias_refs[g], bucket_refs[g], table_ref, g)
            for seq in range(WINDOW_RING - 1):
                window_copy(g, seq).start(priority=WINDOW_DMA_PRIORITY)

    @pl.when(step + WINDOW_RING - 1 < n_steps)
    def _():
        for g in range(N_GROUPS):
            window_copy(g, step + WINDOW_RING - 1).start(priority=WINDOW_DMA_PRIORITY)

    for g in range(N_GROUPS):
        window_copy(g, step).wait()
    slot = step % WINDOW_RING

    t_new = qkv_refs[0].shape[3]
    lane_head = lax.broadcasted_iota(jnp.int32, (t_new, GROUP_WIDTH), 1) // HEAD_DIM
    new_lane = lax.broadcasted_iota(jnp.int32, (GROUP_WIDTH, LANES), 1) >= LANES - t_new
    pad = jnp.zeros((LANES - t_new, GROUP_WIDTH), F32)

    groups = []
    for g in range(N_GROUPS):
        q, k, v = (qkv_refs[g][0, 0, which] for which in range(3))
        cache_ref, win_ref = ring_refs[g].at[slot], win_refs[g]
        past = cache_ref.shape[2]
        new_rows = [jnp.concatenate([pad, rows_], axis=0) for rows_ in (k, v)]
        for c in range(2):
            shifted = pltpu.roll(cache_ref[c], past - t_new, 1)
            if past > LANES:
                win_ref[0, c, :, 0:past - LANES] = shifted[:, 0:past - LANES]
            win_ref[0, c, :, past - LANES:past] = jnp.where(new_lane, new_rows[c].T, shifted[:, past - LANES:])
        groups.append(dict(past=past, qs=_stack_heads(q, lane_head),
                           old=[cache_ref[c].astype(BF16) for c in range(2)],
                           new=[rows_.astype(BF16) for rows_ in new_rows]))
    yield

    for g, grp in enumerate(groups):
        past, bias_ref = grp["past"], bias_refs[g]
        s_old = jnp.dot(grp["qs"], grp["old"][0], preferred_element_type=F32) + bias_ref[:, 0:past]
        s_new = lax.dot_general(grp["qs"], grp["new"][0], NT_DIMS,
                                preferred_element_type=F32) + bias_ref[:, past:past + LANES]
        m = jnp.maximum(jnp.max(s_old, axis=-1, keepdims=True), jnp.max(s_new, axis=-1, keepdims=True))
        p_old = jnp.exp2(s_old - m)
        p_new = jnp.exp2(s_new - m)
        grp.update(m=m, l=jnp.sum(p_old, axis=-1, keepdims=True) + jnp.sum(p_new, axis=-1, keepdims=True),
                   p_old=p_old.astype(BF16), p_new=p_new.astype(BF16))
    yield

    for g, grp in enumerate(groups):
        pv = (lax.dot_general(grp["p_old"], grp["old"][1], NT_DIMS, preferred_element_type=F32)
              + jnp.dot(grp["p_new"], grp["new"][1], preferred_element_type=F32)) / grp["l"]
        lse = jnp.broadcast_to(grp["m"] + jnp.log2(grp["l"]), pv.shape)
        rows = lambda a: [a[h * t_new:(h + 1) * t_new] for h in range(HEADS_PER_GROUP)]
        for which, val in enumerate((_head_select(rows(pv), lane_head), _head_select(rows(lse), lane_head))):
            for s_ in range(SLABS):
                out_refs[g][0, which, s_] = val[:, s_ * LANES:(s_ + 1) * LANES]


def _sample_attn_kernel(*refs):
    for _ in _sample_attn_phases(refs):
        pass


def _sample_buckets(g, past, t_new):
    win, dil = ATTN_GROUPS[g]
    lane = np.arange(past + LANES)[None, :]
    key_pos = np.where(lane < past, lane, lane - (LANES - t_new))
    delta = past + np.arange(t_new)[:, None] - key_pos
    valid = ((lane < past) | (lane >= past + LANES - t_new)) & (delta >= 0) & (delta % dil == 0) & (delta <= win)
    return np.where(valid, _t5_buckets(np.clip(delta, 0, win)), -1).astype(np.int32)


def _sample_attention_parts(qkv, caches, table):
    batch = caches[0].shape[0]
    assert batch >= WINDOW_RING
    t_new = qkv[0].shape[3] // batch
    pasts = [c.shape[3] for c in caches]
    window_ring = [pltpu.VMEM((WINDOW_RING,) + c.shape[1:], F32) for c in caches]
    buckets = [jnp.asarray(_sample_buckets(g, pasts[g], t_new)) for g in range(N_GROUPS)]
    cache_spec = lambda p: pl.BlockSpec((1, 2, GROUP_WIDTH, p), lambda b: (b, 0, 0, 0))
    new_spec = pl.BlockSpec((1, 1, 3, t_new, GROUP_WIDTH), lambda b: (0, 0, 0, b, 0))
    out_spec = pl.BlockSpec((1, 2, SLABS, t_new, LANES), lambda b: (0, 0, 0, b, 0))
    out_sds = jax.ShapeDtypeStruct((1, 2, SLABS, batch * t_new, LANES), F32)
    return dict(
        batch=batch,
        args=[*qkv, *caches, *buckets, table],
        out_shape=[out_sds] * N_GROUPS + [jax.ShapeDtypeStruct(c.shape, F32) for c in caches],
        in_specs=[new_spec] * N_GROUPS + [pl.BlockSpec(memory_space=pl.ANY)] * N_GROUPS
        + [_resident(b.shape) for b in buckets] + [_SMEM],
        out_specs=[out_spec] * N_GROUPS + [cache_spec(p) for p in pasts],
        scratch_shapes=[pltpu.VMEM((HEADS_PER_GROUP * t_new, p + LANES), F32) for p in pasts] + window_ring
        + [pltpu.SemaphoreType.DMA((N_GROUPS, WINDOW_RING))])


def _sample_attention(parts):
    return pl.pallas_call(
        _sample_attn_kernel, grid=(parts["batch"],), out_shape=parts["out_shape"],
        in_specs=parts["in_specs"], out_specs=parts["out_specs"], scratch_shapes=parts["scratch_shapes"],
        compiler_params=_params(), name="sample_attn")(*parts["args"])


def _back_kernel(x_ref, ol0_ref, ol1_ref, ol2_ref, u_ref, halo_ref, gate_ref,
                 wab_ref, wpg_ref, pscale_ref, wpb_ref, wout_ref, g2_ref, wup_ref, wdn_ref,
                 out_ref, ue_ref, level_ref, mixed_ref, act_ref, attn_ref, pooled_ref, merged_ref,
                 *, pos_base, blocks_per_seq):
    block = pl.program_id(0)
    tm, d_model = x_ref.shape
    n_seq = halo_ref.shape[0]
    t = tm // n_seq
    pool_width = u_ref.shape[1]
    gw = pool_width // len(POOL_WINDOWS)

    for s in range(SLABS):
        ol_refs = (ol0_ref, ol1_ref, ol2_ref)
        lses = [r[0, 1, s] for r in ol_refs]
        top = jnp.maximum(jnp.maximum(lses[0], lses[1]), lses[2])
        num = jnp.zeros_like(top)
        den = jnp.zeros_like(top)
        for ol_ref, lse in zip(ol_refs, lses):
            e = jnp.exp2(lse - top)
            num = num + e * ol_ref[0, 0, s]
            den = den + e
        attn_ref[:, s * LANES:(s + 1) * LANES] = (num / den).astype(BF16)

    start = pos_base + (block % blocks_per_seq) * t
    first, end = POOL_PAD - POOL_HALO, POOL_PAD + t
    ue_ref[:, 0:first, :] = jnp.zeros((n_seq, first, pool_width), F32)
    ue_ref[:, first:POOL_PAD, :] = jnp.where(start > 0, halo_ref[...], 0.0)
    ue_ref[:, POOL_PAD:, :] = u_ref[...].reshape(n_seq, t, pool_width)
    level_ref[:, :, 0:first, :] = jnp.zeros((2, n_seq, first, gw), F32)
    pos = start + lax.broadcasted_iota(jnp.int32, (1, t, gw), 1)
    for gi, win in enumerate(POOL_WINDOWS):
        cols = slice(gi * gw, (gi + 1) * gw)
        read = lambda lo, hi, cols=cols: ue_ref[:, lo:hi, cols]
        span = 1
        while 2 * span < win:
            buf = (span.bit_length() - 1) % 2
            level_ref[buf, :, first:, :] = read(first, end) + read(first - span, end - span)
            read = lambda lo, hi, buf=buf: level_ref[buf, :, lo:hi, :]
            span *= 2
        s = read(POOL_PAD, end) + read(POOL_PAD - span, end - span)
        cnt = jnp.minimum(pos + 1, win).astype(F32)
        d = (s / cnt - ue_ref[:, POOL_PAD:, cols]).reshape(tm, gw)
        y = jnp.dot(d.astype(BF16), wpg_ref[gi], preferred_element_type=F32)
        pooled_ref[:, cols] = (y * pscale_ref[:, cols]).astype(BF16)

    for lo in range(0, d_model, MXU_TILE):
        cols = slice(lo, lo + MXU_TILE)
        branch_a = jnp.dot(attn_ref[...], wab_ref[:, cols], preferred_element_type=F32)
        branch_b = jnp.dot(pooled_ref[...], wpb_ref[:, cols], preferred_element_type=F32)
        gate_a = gate_ref[:, cols].astype(F32)
        gate_b = gate_ref[:, d_model + lo:d_model + lo + MXU_TILE].astype(F32)
        merged_ref[:, cols] = (gate_a * branch_a + gate_b * branch_b).astype(BF16)
    for lo in range(0, d_model, MXU_TILE):
        cols = slice(lo, lo + MXU_TILE)
        mixed_ref[:, cols] = x_ref[:, cols] + jnp.dot(merged_ref[...], wout_ref[:, cols],
                                                      preferred_element_type=F32)

    _ffn_block(mixed_ref, g2_ref, wup_ref, wdn_ref, out_ref, act_ref)


def _back(x, ol, u, halo, gates, merge_w, ffn_w, *, tm, halo_block, halo_index, pos_base, blocks_per_seq):
    m, d = x.shape
    pool_width = u.shape[1]
    n_seq = halo_block[0]
    d_ff = ffn_w[2].shape[0]
    row = lambda wd: pl.BlockSpec((tm, wd), lambda i: (i, 0))
    slab = pl.BlockSpec((1, 2, SLABS, tm, LANES),
                        lambda i: (i // blocks_per_seq, 0, 0, i % blocks_per_seq, 0))
    return pl.pallas_call(
        functools.partial(_back_kernel, pos_base=pos_base, blocks_per_seq=blocks_per_seq),
        out_shape=jax.ShapeDtypeStruct((m, d), F32),
        grid=(m // tm,),
        in_specs=[row(d)] + [slab] * N_GROUPS
        + [row(pool_width), pl.BlockSpec(halo_block, halo_index), row(2 * d)]
        + [_resident(a.shape) for a in (*merge_w, *ffn_w)],
        out_specs=row(d),
        scratch_shapes=[pltpu.VMEM((n_seq, POOL_PAD + tm // n_seq, pool_width), F32),
                        pltpu.VMEM((2, n_seq, POOL_PAD + tm // n_seq, pool_width // len(POOL_WINDOWS)), F32),
                        pltpu.VMEM((tm, d), F32), pltpu.VMEM((tm, d_ff), BF16),
                        pltpu.VMEM((tm, GROUP_WIDTH), BF16), pltpu.VMEM((tm, pool_width), BF16),
                        pltpu.VMEM((tm, d), BF16)],
        compiler_params=_params(),
        name="back",
    )(x, *ol, u, halo, gates, *merge_w, *ffn_w)


def _window_in(cache):
    n_seq, rows = cache.shape[0:2]
    return jnp.transpose(cache, (0, 2, 3, 4, 1)).reshape(n_seq, 2, GROUP_WIDTH, rows)


def _window_out(kv):
    n_seq, _, _, rows = kv.shape
    return jnp.transpose(kv.reshape(n_seq, 2, HEADS_PER_GROUP, HEAD_DIM, rows), (0, 4, 1, 2, 3))


def kernel(x_prompt, x_sample, cache_kv_w128, cache_kv_w512, cache_kv_w2048, state_pool, rel_bias_table,
           norm_ffn1, ffn1_w_up, ffn1_w_down, norm_mix, w_in, q_norm, k_norm, pool_w_group, pool_scale,
           w_attn_branch, w_pool_branch, w_out, norm_ffn2, ffn2_w_up, ffn2_w_down):
    batch, seq, d_model = x_prompt.shape
    dec_batch, dec_seq, _ = x_sample.shape
    depth = norm_ffn1.shape[0]
    pool_width = state_pool.shape[-1]
    tm = ROW_BLOCK
    ms = dec_batch * dec_seq
    seg = jnp.asarray(np.kron(np.eye(HEADS_PER_GROUP), np.full((HEAD_DIM, HEAD_DIM), 1.0 / HEAD_DIM)), BF16)
    table = rel_bias_table.astype(F32)
    caches_in = (cache_kv_w128, cache_kv_w512, cache_kv_w2048)
    keep_prompt = tuple(min(win, seq) for win, _ in ATTN_GROUPS)
    dils = tuple(dil for _, dil in ATTN_GROUPS)

    xp = x_prompt.reshape(batch * seq, d_model)
    xs = x_sample.reshape(ms, d_model)
    kv_p, kv_s, pool_p, pool_s = ([], [], []), ([], [], []), [], []
    for layer in range(depth):
        gain = lambda a: a[layer].reshape(1, -1).astype(F32)
        qg, kg = gain(q_norm), gain(k_norm)
        pool_groups = pool_w_group[layer]
        late_w = (ffn2_w_down[layer], w_attn_branch[layer], pool_groups.reshape(-1, pool_groups.shape[-1]),
                  w_pool_branch[layer], w_out[layer])

        xs, (wup1, wdn1) = _ffn(xs, gain(norm_ffn1), ffn1_w_up[layer], ffn1_w_down[layer], ms, narrow=True)
        outs = _proj(xs, gain(norm_mix), w_in[layer], qg, kg, seg, tm=ms, n_seq=1, dils=(1,) * N_GROUPS,
                     qkv_dtype=F32, keep_rows=None, narrow=True)
        qkv_s, u_s, gates_s, win = outs[0:N_GROUPS], outs[N_GROUPS], outs[N_GROUPS + 1], outs[N_GROUPS + 2]
        sample_parts = _sample_attention_parts(qkv_s, [_window_in(c[layer]) for c in caches_in], table)

        if (batch * seq) // tm == dec_batch:
            xp, sample_outs = _ffn(xp, gain(norm_ffn1), wup1, wdn1, tm, rider=sample_parts)
        else:
            xp, _ = _ffn(xp, gain(norm_ffn1), wup1, wdn1, tm)
            sample_outs = _sample_attention(sample_parts)
        outs = _proj(xp, gain(norm_mix), win, qg, kg, seg, tm=2 * tm, n_seq=batch, dils=dils,
                     qkv_dtype=BF16, keep_rows=keep_prompt)
        qkv, u, gates = outs[0:N_GROUPS], outs[N_GROUPS], outs[N_GROUPS + 1]
        windows, pstate = outs[N_GROUPS + 2:2 * N_GROUPS + 2], outs[2 * N_GROUPS + 2]
        narrowing = ((ffn2_w_up[layer],), late_w) + ((),) * (N_GROUPS - 2)
        ol, narrowed = zip(*[_prompt_attention(qkv[g], table, g, to_narrow=narrowing[g])
                             for g in range(N_GROUPS)])
        (wup2,), (wdn2, wab, wpg, wpb, wout) = narrowed[0:2]
        ffn2_w = (gain(norm_ffn2), wup2, wdn2)
        merge_w = (wab, wpg.reshape(pool_groups.shape), gain(pool_scale), wpb, wout)
        xp = _back(xp, ol, u, u.reshape(-1, POOL_HALO, pool_width), gates, merge_w, ffn2_w, tm=tm,
                   halo_block=(1, POOL_HALO, pool_width),
                   halo_index=lambda blk: (jnp.maximum(blk * (tm // POOL_HALO) - 1, 0), 0, 0),
                   pos_base=0, blocks_per_seq=seq // tm)
        for g in range(N_GROUPS):
            kv_p[g].append(_window_out(windows[g]))
        pool_p.append(pstate[:, POOL_HALO - POOL_STATE:])

        ol, windows = sample_outs[0:N_GROUPS], sample_outs[N_GROUPS:2 * N_GROUPS]
        history = jnp.concatenate(
            [jnp.zeros((dec_batch, POOL_HALO - POOL_STATE, pool_width), F32), state_pool[layer]], axis=1)
        xs = _back(xs, ol, u_s, history, gates_s, merge_w, ffn2_w, tm=ms,
                   halo_block=(dec_batch, POOL_HALO, pool_width), halo_index=lambda blk: (0, 0, 0),
                   pos_base=PAST_LEN, blocks_per_seq=1)
        for g in range(N_GROUPS):
            kv_s[g].append(_window_out(windows[g]))
        ue = jnp.concatenate([state_pool[layer], u_s.reshape(dec_batch, dec_seq, pool_width)], axis=1)
        pool_s.append(ue[:, ue.shape[1] - POOL_STATE:])

    stack = lambda xs_: jnp.stack(xs_, axis=0)
    return (xp.reshape(batch, seq, d_model), xs.reshape(dec_batch, dec_seq, d_model),
            stack(kv_p[0]), stack(kv_p[1]), stack(kv_p[2]), stack(pool_p),
            stack(kv_s[0]), stack(kv_s[1]), stack(kv_s[2]), stack(pool_s))
```

```python
import functools
import math

import numpy as np
import jax
import jax.numpy as jnp
from jax import lax
from jax.experimental import pallas as pl
from jax.experimental.pallas import tpu as pltpu

HEAD_DIM = 64
HEADS_PER_GROUP = 4
GROUP_WIDTH = HEADS_PER_GROUP * HEAD_DIM
ATTN_GROUPS = ((128, 1), (512, 4), (2048, 16))
N_GROUPS = len(ATTN_GROUPS)
ATTN_WIDTH = N_GROUPS * GROUP_WIDTH
N_BUCKETS = 32
MAX_DISTANCE = 2048
POOL_WINDOWS = (2, 4, 8, 16)
POOL_STATE = max(POOL_WINDOWS) - 1
POOL_HALO = 16
POOL_PAD = POOL_HALO + 8
EPS = 1e-6
PAST_LEN = 8192
NEG = -1e30
LOG2E = math.log2(math.e)
LANES = 128
MXU_TILE = 256
SLABS = GROUP_WIDTH // LANES
SINGLE_OP_STRIDE = 4
WINDOW_RING = 3
WINDOW_STAGES = 2
RIDER_PHASE_GAP = 2
ROW_BLOCK = 512
QBLK = 128
ATTN_UNROLL = 16
VMEM_LIMIT = 58 * 1024 * 1024

F32 = jnp.float32
BF16 = jnp.bfloat16
NT_DIMS = (((1,), (1,)), ((), ()))


def _t5_buckets(distance):
    max_exact = N_BUCKETS // 2
    d = np.asarray(distance, dtype=np.int32)
    ratio = np.log(np.maximum(d, 1).astype(np.float32) / np.float32(max_exact))
    large = max_exact + (ratio / np.float32(math.log(MAX_DISTANCE / max_exact))
                         * (N_BUCKETS - max_exact)).astype(np.int32)
    large = np.minimum(large, N_BUCKETS - 1)
    return np.where(d < max_exact, d, large).astype(np.int32)


def _params(n_grid_dims=1):
    return pltpu.CompilerParams(dimension_semantics=("arbitrary",) * n_grid_dims,
                                vmem_limit_bytes=VMEM_LIMIT)


def _resident(shape):
    return pl.BlockSpec(shape, lambda *_: (0,) * len(shape), pipeline_mode=pl.Buffered(1))


_SMEM = pl.BlockSpec(memory_space=pltpu.SMEM)


def _rmsnorm(x, g):
    ms = jnp.mean(x * x, axis=-1, keepdims=True)
    return x * lax.rsqrt(ms + EPS) * g


def _head_select(parts, lane_head):
    out = jnp.where(lane_head == 0, parts[0], 0.0)
    for h in range(1, HEADS_PER_GROUP):
        out = jnp.where(lane_head == h, parts[h], out)
    return out


def _stack_heads(q, lane_head):
    return jnp.concatenate([jnp.where(lane_head == h, q, 0.0)
                            for h in range(HEADS_PER_GROUP)], axis=0).astype(BF16)


def _build_bias(bias_ref, bucket_ref, table_ref, g):
    buckets = bucket_ref[...]
    rows = buckets.shape[0]
    for h in range(HEADS_PER_GROUP):
        acc = jnp.full(buckets.shape, NEG, F32)
        for b in range(N_BUCKETS):
            acc = jnp.where(buckets == b, table_ref[b, g * HEADS_PER_GROUP + h] * LOG2E, acc)
        bias_ref[h * rows:(h + 1) * rows, :] = acc


def _matmul_weight(w_ref, narrow_ref, rows, cols):
    w = w_ref[rows, cols]
    if narrow_ref is not None:
        w = w.astype(BF16)
        narrow_ref[rows, cols] = w
    return w


def _ffn_block(x_ref, g_ref, wup_ref, wdn_ref, o_ref, act_ref, narrow_refs=(None, None), rider_phases=()):
    d_ff, d = wdn_ref.shape
    every = slice(None)
    rider_phases = iter(rider_phases)
    next(rider_phases, None)
    h = _rmsnorm(x_ref[...], g_ref[...]).astype(BF16)
    for chunk, lo in enumerate(range(0, d_ff, MXU_TILE)):
        w_gate = _matmul_weight(wup_ref, narrow_refs[0], every, slice(lo, lo + MXU_TILE))
        w_up = _matmul_weight(wup_ref, narrow_refs[0], every, slice(d_ff + lo, d_ff + lo + MXU_TILE))
        gate = jnp.dot(h, w_gate, preferred_element_type=F32)
        up = jnp.dot(h, w_up, preferred_element_type=F32)
        if chunk % RIDER_PHASE_GAP == RIDER_PHASE_GAP - 1:
            next(rider_phases, None)
        act_ref[:, lo:lo + MXU_TILE] = (gate * jax.nn.sigmoid(gate) * up).astype(BF16)
    for lo in range(0, d, MXU_TILE):
        cols = slice(lo, lo + MXU_TILE)
        w_down = _matmul_weight(wdn_ref, narrow_refs[1], every, cols)
        o_ref[:, cols] = x_ref[:, cols] + 0.5 * jnp.dot(act_ref[...], w_down, preferred_element_type=F32)


def _ffn_kernel(*refs, rider_counts, narrow):
    n_in, n_out = rider_counts
    n_narrow = 2 if narrow else 0
    x_ref, g_ref, wup_ref, wdn_ref = refs[0:4]
    o_ref = refs[4 + n_in]
    narrow_refs = refs[5 + n_in:5 + n_in + n_narrow] if narrow else (None, None)
    first_rider_out = 5 + n_in + n_narrow
    act_ref = refs[first_rider_out + n_out]
    rider_phases = ()
    if n_in:
        rider_phases = _sample_attn_phases((*refs[4:4 + n_in], *refs[first_rider_out:first_rider_out + n_out],
                                            *refs[first_rider_out + n_out + 1:]))
    _ffn_block(x_ref, g_ref, wup_ref, wdn_ref, o_ref, act_ref, narrow_refs, rider_phases)


def _ffn(x, gain, w_up, w_down, tm, rider=None, narrow=False):
    m, d = x.shape
    d_ff = w_down.shape[0]
    assert m % tm == 0 and d_ff % MXU_TILE == 0 and d % MXU_TILE == 0
    assert not narrow or m == tm, "each weight chunk is narrowed once only on a single-step grid"
    rider = rider or dict(args=[], in_specs=[], out_shape=[], out_specs=[], scratch_shapes=[], batch=m // tm)
    assert rider["batch"] == m // tm
    weights = (w_up, w_down)
    narrowed = [jax.ShapeDtypeStruct(w.shape, BF16) for w in weights] if narrow else []
    outs = pl.pallas_call(
        functools.partial(_ffn_kernel, rider_counts=(len(rider["args"]), len(rider["out_shape"])), narrow=narrow),
        out_shape=[jax.ShapeDtypeStruct((m, d), F32)] + narrowed + rider["out_shape"],
        grid=(m // tm,),
        in_specs=[pl.BlockSpec((tm, d), lambda i: (i, 0)),
                  _resident((1, d)), _resident((d, 2 * d_ff)), _resident((d_ff, d))] + rider["in_specs"],
        out_specs=[pl.BlockSpec((tm, d), lambda i: (i, 0))] + [_resident(w.shape) for w in narrowed]
        + rider["out_specs"],
        scratch_shapes=[pltpu.VMEM((tm, d_ff), BF16)] + rider["scratch_shapes"],
        compiler_params=_params(),
        name="ffn",
    )(x, gain, w_up, w_down, *rider["args"])
    return outs[0], outs[1:]


def _proj_kernel(x_ref, g_ref, win_ref, qn_ref, kn_ref, seg_ref, *refs,
                 dils, kv_rows, with_state, narrow, pool_width, d_model):
    qkv_refs = refs[0:N_GROUPS]
    u_ref, gate_ref = refs[N_GROUPS:N_GROUPS + 2]
    state_refs = refs[N_GROUPS + 2:]
    narrow_ref = refs[-3] if narrow else None
    slab_ref, regroup_ref = refs[-2:]
    tm = x_ref.shape[0]
    h = _rmsnorm(x_ref[...], g_ref[...]).astype(BF16)

    def proj(lo, width):
        w = _matmul_weight(win_ref, narrow_ref, slice(None), slice(lo, lo + width))
        return jnp.dot(h, w, preferred_element_type=F32)

    def head_norm(y, gain):
        ms = jnp.dot((y * y).astype(BF16), seg_ref[...], preferred_element_type=F32)
        return y * lax.rsqrt(ms + EPS) * gain

    def emit(dst_ref, which, slot, val, dil):
        if dil == 1:
            dst_ref[0, 0, which] = val.astype(dst_ref.dtype)
            return
        for s in range(SLABS):
            slab_ref[slot, s] = val[:, s * LANES:(s + 1) * LANES]
        n = tm // dil
        if dil <= SINGLE_OP_STRIDE:
            for r in range(dil):
                parts = [slab_ref[slot, s, pl.ds(r, n, stride=dil), :] for s in range(SLABS)]
                dst_ref[0, r, which] = jnp.concatenate(parts, axis=-1).astype(dst_ref.dtype)
            return
        outer = SINGLE_OP_STRIDE
        inner = dil // outer
        assert inner <= SINGLE_OP_STRIDE
        quarter = tm // outer
        for a in range(outer):
            for s in range(SLABS):
                regroup_ref[s, a * quarter:(a + 1) * quarter, :] = slab_ref[slot, s, pl.ds(a, quarter, stride=outer), :]
        for r in range(dil):
            start = (r % outer) * quarter + r // outer
            parts = [regroup_ref[s, pl.ds(start, n, stride=inner), :] for s in range(SLABS)]
            dst_ref[0, r, which] = jnp.concatenate(parts, axis=-1).astype(dst_ref.dtype)

    def finish(g, which, y):
        cols = slice(g * GROUP_WIDTH, (g + 1) * GROUP_WIDTH)
        if which == 0:
            y = head_norm(y, qn_ref[:, cols]) * (HEAD_DIM ** -0.5 * LOG2E)
        elif which == 1:
            y = head_norm(y, kn_ref[:, cols])
        emit(qkv_refs[g], which, 3 * sum(d > 1 for d in dils[:g]) + which, y, dils[g])
        if with_state and which > 0:
            state_refs[g][0, which - 1] = y.T[:, tm - kv_rows[g]:]

    pending = None
    for g in range(N_GROUPS):
        for which in range(3):
            y = proj(which * ATTN_WIDTH + g * GROUP_WIDTH, GROUP_WIDTH)
            if pending is not None:
                finish(*pending)
            pending = (g, which, y)
    u = proj(3 * ATTN_WIDTH, pool_width)
    finish(*pending)
    u_ref[...] = u
    if with_state:
        state_refs[N_GROUPS][0] = u[tm - POOL_HALO:, :]
    gates_base = 3 * ATTN_WIDTH + pool_width
    for lo in range(0, 2 * d_model, MXU_TILE):
        gate_ref[:, lo:lo + MXU_TILE] = jax.nn.sigmoid(proj(gates_base + lo, MXU_TILE)).astype(BF16)


def _proj(x, gain, w_in, q_gain, k_gain, seg, *, tm, n_seq, dils, qkv_dtype, keep_rows, narrow=False):
    m, d = x.shape
    pool_width = w_in.shape[1] - 3 * ATTN_WIDTH - 2 * d
    seq_len = m // n_seq
    assert seq_len % tm == 0 and all(tm % dil == 0 for dil in dils)
    blocks_per_seq = seq_len // tm
    with_state = keep_rows is not None
    seq_block = lambda i: (i // blocks_per_seq, i % blocks_per_seq)

    row = lambda w: pl.BlockSpec((tm, w), lambda i: (i, 0))
    out_shape, out_specs = [], []
    for g in range(N_GROUPS):
        out_shape.append(jax.ShapeDtypeStruct((n_seq, dils[g], 3, seq_len // dils[g], GROUP_WIDTH), qkv_dtype))
        out_specs.append(pl.BlockSpec((1, dils[g], 3, tm // dils[g], GROUP_WIDTH),
                                      lambda i: (seq_block(i)[0], 0, 0, seq_block(i)[1], 0)))
    out_shape += [jax.ShapeDtypeStruct((m, pool_width), F32), jax.ShapeDtypeStruct((m, 2 * d), BF16)]
    out_specs += [row(pool_width), row(2 * d)]
    kv_rows = kv_first = ()
    if with_state:
        kv_rows = tuple(min(r, tm) for r in keep_rows)
        kv_first = tuple(blocks_per_seq - keep_rows[g] // kv_rows[g] for g in range(N_GROUPS))
        for g in range(N_GROUPS):
            out_shape.append(jax.ShapeDtypeStruct((n_seq, 2, GROUP_WIDTH, keep_rows[g]), F32))
            out_specs.append(pl.BlockSpec(
                (1, 2, GROUP_WIDTH, kv_rows[g]),
                lambda i, first=kv_first[g]: (seq_block(i)[0], 0, 0, jnp.maximum(seq_block(i)[1] - first, 0))))
        out_shape.append(jax.ShapeDtypeStruct((n_seq, POOL_HALO, pool_width), F32))
        out_specs.append(pl.BlockSpec((1, POOL_HALO, pool_width), lambda i: (seq_block(i)[0], 0, 0)))
    if narrow:
        assert m == tm, "each weight chunk is narrowed once only on a single-step grid"
        out_shape.append(jax.ShapeDtypeStruct(w_in.shape, BF16))
        out_specs.append(_resident(w_in.shape))
    return pl.pallas_call(
        functools.partial(_proj_kernel, dils=dils, kv_rows=kv_rows, with_state=with_state, narrow=narrow,
                          pool_width=pool_width, d_model=d),
        out_shape=out_shape,
        grid=(m // tm,),
        in_specs=[row(d), _resident((1, d)), _resident(w_in.shape),
                  _resident((1, ATTN_WIDTH)), _resident((1, ATTN_WIDTH)),
                  _resident((GROUP_WIDTH, GROUP_WIDTH))],
        out_specs=out_specs,
        scratch_shapes=[pltpu.VMEM((max(1, 3 * sum(dil > 1 for dil in dils)), SLABS, tm, LANES), F32),
                        pltpu.VMEM((SLABS, tm, LANES), F32)],
        compiler_params=_params(),
        name="proj",
    )(x, gain, w_in, q_gain, k_gain, seg)


def _attn_kernel(*refs, g, dil, n_blocks, unroll, n_cast):
    qkv_ref, bucket_ref, hmask_ref, table_ref = refs[0:4]
    wide_refs = refs[4:4 + n_cast]
    out_ref = refs[4 + n_cast]
    narrow_refs = refs[5 + n_cast:5 + 2 * n_cast]
    bias_ref = refs[5 + 2 * n_cast]
    for wide_ref, narrow_ref in zip(wide_refs, narrow_refs):
        narrow_ref[...] = wide_ref[...].astype(BF16)

    @pl.when(pl.program_id(0) == 0)
    def _():
        _build_bias(bias_ref.at[1], bucket_ref, table_ref, g)
        bias_ref[0, :, 0:QBLK] = bias_ref[1, :, QBLK:2 * QBLK]
        bias_ref[0, :, QBLK:2 * QBLK] = jnp.full((HEADS_PER_GROUP * QBLK, QBLK), NEG, F32)

    first_head = lax.broadcasted_iota(jnp.int32, (QBLK, LANES), 1) < HEAD_DIM
    heads = range(HEADS_PER_GROUP)

    def per_slab(col):
        part = lambda h: jnp.broadcast_to(col[h * QBLK:(h + 1) * QBLK], (QBLK, LANES))
        return [jnp.where(first_head, part(2 * s), part(2 * s + 1)) for s in range(SLABS)]

    def body(n, carry):
        res = n // n_blocks
        i = n % n_blocks
        row0 = pl.multiple_of(i * QBLK, QBLK)
        span0 = pl.multiple_of(jnp.maximum(i - 1, 0) * QBLK, QBLK)
        q = qkv_ref[0, res, 0, pl.ds(row0, QBLK), :]
        qs = jnp.concatenate([q * hmask_ref[h] for h in heads], axis=0)
        s = lax.dot_general(qs, qkv_ref[0, res, 1, pl.ds(span0, 2 * QBLK), :], NT_DIMS,
                            preferred_element_type=F32) + bias_ref[jnp.minimum(i, 1)]
        m = jnp.max(s, axis=-1, keepdims=True)
        p = jnp.exp2(s - m)
        l = jnp.sum(p, axis=-1, keepdims=True)
        p = p.astype(BF16)
        v = qkv_ref[0, res, 2, pl.ds(span0, 2 * QBLK), :]
        p_wide = jnp.concatenate([p[h * QBLK:(h + 1) * QBLK] for h in heads], axis=1)
        v_tall = jnp.concatenate([v * hmask_ref[h] for h in heads], axis=0)
        pv = jnp.dot(p_wide, v_tall, preferred_element_type=F32)
        token0 = res + dil * row0
        dst = pl.ds(token0, QBLK) if dil == 1 else pl.ds(token0, QBLK, stride=dil)
        for s_, (m_s, l_s) in enumerate(zip(per_slab(m), per_slab(l))):
            out_ref[0, 0, s_, dst, :] = pv[:, s_ * LANES:(s_ + 1) * LANES] / l_s
            out_ref[0, 1, s_, dst, :] = m_s + jnp.log2(l_s)
        return carry

    lax.fori_loop(0, dil * n_blocks, body, 0, unroll=unroll)


def _prompt_buckets(g):
    win, dil = ATTN_GROUPS[g]
    delta = np.arange(QBLK)[:, None] + QBLK - np.arange(2 * QBLK)[None, :]
    valid = (delta >= 0) & (delta <= win // dil)
    return np.where(valid, _t5_buckets(dil * np.clip(delta, 0, win // dil)), -1).astype(np.int32)


def _prompt_attention(qkv, table, g, to_narrow=()):
    _, dil = ATTN_GROUPS[g]
    batch, _, _, sub, _ = qkv.shape
    n_blocks = sub // QBLK
    assert sub % QBLK == 0 and n_blocks >= 2 and (dil * n_blocks) % ATTN_UNROLL == 0
    assert all(w.shape[0] % (16 * batch) == 0 for w in to_narrow)
    cast_specs = [pl.BlockSpec((w.shape[0] // batch, w.shape[1]), lambda b: (b, 0)) for w in to_narrow]
    head_of_lane = np.arange(GROUP_WIDTH) // HEAD_DIM
    hmask = head_of_lane[None, None, :] == np.arange(HEADS_PER_GROUP)[:, None, None]
    in_spec = pl.BlockSpec((1, dil, 3, sub, GROUP_WIDTH), lambda b: (b, 0, 0, 0, 0))
    out_spec = pl.BlockSpec((1, 2, SLABS, sub * dil, LANES), lambda b: (b, 0, 0, 0, 0))
    out_sds = jax.ShapeDtypeStruct((batch, 2, SLABS, sub * dil, LANES), F32)
    outs = pl.pallas_call(
        functools.partial(_attn_kernel, g=g, dil=dil, n_blocks=n_blocks, unroll=ATTN_UNROLL,
                          n_cast=len(to_narrow)),
        out_shape=[out_sds] + [jax.ShapeDtypeStruct(w.shape, BF16) for w in to_narrow],
        grid=(batch,),
        in_specs=[in_spec, _resident((QBLK, 2 * QBLK)), _resident(hmask.shape), _SMEM] + cast_specs,
        out_specs=[out_spec] + cast_specs,
        scratch_shapes=[pltpu.VMEM((2, HEADS_PER_GROUP * QBLK, 2 * QBLK), F32)],
        compiler_params=_params(),
        name=f"attn_g{g}",
    )(qkv, jnp.asarray(_prompt_buckets(g)), jnp.asarray(hmask, BF16), table, *to_narrow)
    return outs[0], outs[1:]


def _sample_attn_phases(refs):
    n = N_GROUPS
    qkv_refs, cache_refs, bucket_refs, table_ref = refs[0:n], refs[n:2 * n], refs[2 * n:3 * n], refs[3 * n]
    out_refs, win_refs, bias_refs = refs[3 * n + 1:4 * n + 1], refs[4 * n + 1:5 * n + 1], refs[5 * n + 1:6 * n + 1]
    ring_refs, sem_ref = refs[6 * n + 1:7 * n + 1], refs[7 * n + 1]
    stage_refs, out_sem_ref = refs[7 * n + 2:8 * n + 2], refs[8 * n + 2]
    step, n_steps = pl.program_id(0), pl.num_programs(0)

    def window_writeback(g, seq):
        slot = seq % WINDOW_STAGES
        return pltpu.make_async_copy(stage_refs[g].at[slot], win_refs[g].at[seq], out_sem_ref.at[g, slot])

    def window_copy(g, seq):
        slot = seq % WINDOW_RING
        return pltpu.make_async_copy(cache_refs[g].at[seq], ring_refs[g].at[slot], sem_ref.at[g, slot])

    @pl.when(step == 0)
    def _():
        for g in range(N_GROUPS):
            _build_bias(bias_refs[g], bucket_refs[g], table_ref, g)
            for seq in range(WINDOW_RING - 1):
                window_copy(g, seq).start()

    @pl.when(step + WINDOW_RING - 1 < n_steps)
    def _():
        for g in range(N_GROUPS):
            window_copy(g, step + WINDOW_RING - 1).start()

    @pl.when(step >= WINDOW_STAGES)
    def _():
        for g in range(N_GROUPS):
            window_writeback(g, step - WINDOW_STAGES).wait()

    for g in range(N_GROUPS):
        window_copy(g, step).wait()
    slot, stage = step % WINDOW_RING, step % WINDOW_STAGES

    t_new = qkv_refs[0].shape[3]
    lane_head = lax.broadcasted_iota(jnp.int32, (t_new, GROUP_WIDTH), 1) // HEAD_DIM
    new_lane = lax.broadcasted_iota(jnp.int32, (GROUP_WIDTH, LANES), 1) >= LANES - t_new
    pad = jnp.zeros((LANES - t_new, GROUP_WIDTH), F32)

    groups = []
    for g in range(N_GROUPS):
        q, k, v = (qkv_refs[g][0, 0, which] for which in range(3))
        cache_ref, win_ref = ring_refs[g].at[slot], stage_refs[g].at[stage]
        past = cache_ref.shape[2]
        new_rows = [jnp.concatenate([pad, rows_], axis=0) for rows_ in (k, v)]
        for c in range(2):
            shifted = pltpu.roll(cache_ref[c], past - t_new, 1)
            if past > LANES:
                win_ref[c, :, 0:past - LANES] = shifted[:, 0:past - LANES]
            win_ref[c, :, past - LANES:past] = jnp.where(new_lane, new_rows[c].T, shifted[:, past - LANES:])
        window_writeback(g, step).start()
        groups.append(dict(past=past, qs=_stack_heads(q, lane_head),
                           old=[cache_ref[c].astype(BF16) for c in range(2)],
                           new=[rows_.astype(BF16) for rows_ in new_rows]))
    yield

    for g, grp in enumerate(groups):
        past, bias_ref = grp["past"], bias_refs[g]
        s_old = jnp.dot(grp["qs"], grp["old"][0], preferred_element_type=F32) + bias_ref[:, 0:past]
        s_new = lax.dot_general(grp["qs"], grp["new"][0], NT_DIMS,
                                preferred_element_type=F32) + bias_ref[:, past:past + LANES]
        m = jnp.maximum(jnp.max(s_old, axis=-1, keepdims=True), jnp.max(s_new, axis=-1, keepdims=True))
        p_old = jnp.exp2(s_old - m)
        p_new = jnp.exp2(s_new - m)
        grp.update(m=m, l=jnp.sum(p_old, axis=-1, keepdims=True) + jnp.sum(p_new, axis=-1, keepdims=True),
                   p_old=p_old.astype(BF16), p_new=p_new.astype(BF16))
    yield

    for g, grp in enumerate(groups):
        pv = (lax.dot_general(grp["p_old"], grp["old"][1], NT_DIMS, preferred_element_type=F32)
              + jnp.dot(grp["p_new"], grp["new"][1], preferred_element_type=F32)) / grp["l"]
        lse = jnp.broadcast_to(grp["m"] + jnp.log2(grp["l"]), pv.shape)
        rows = lambda a: [a[h * t_new:(h + 1) * t_new] for h in range(HEADS_PER_GROUP)]
        for which, val in enumerate((_head_select(rows(pv), lane_head), _head_select(rows(lse), lane_head))):
            for s_ in range(SLABS):
                out_refs[g][0, which, s_] = val[:, s_ * LANES:(s_ + 1) * LANES]

    @pl.when(step == n_steps - 1)
    def _():
        for g in range(N_GROUPS):
            for back in range(WINDOW_STAGES):
                window_writeback(g, step - back).wait()


def _sample_attn_kernel(*refs):
    for _ in _sample_attn_phases(refs):
        pass


def _sample_buckets(g, past, t_new):
    win, dil = ATTN_GROUPS[g]
    lane = np.arange(past + LANES)[None, :]
    key_pos = np.where(lane < past, lane, lane - (LANES - t_new))
    delta = past + np.arange(t_new)[:, None] - key_pos
    valid = ((lane < past) | (lane >= past + LANES - t_new)) & (delta >= 0) & (delta % dil == 0) & (delta <= win)
    return np.where(valid, _t5_buckets(np.clip(delta, 0, win)), -1).astype(np.int32)


def _sample_attention_parts(qkv, caches, table):
    batch = caches[0].shape[0]
    assert batch >= WINDOW_RING
    t_new = qkv[0].shape[3] // batch
    pasts = [c.shape[3] for c in caches]
    window_ring = [pltpu.VMEM((WINDOW_RING,) + c.shape[1:], F32) for c in caches]
    buckets = [jnp.asarray(_sample_buckets(g, pasts[g], t_new)) for g in range(N_GROUPS)]
    window_stages = [pltpu.VMEM((WINDOW_STAGES,) + c.shape[1:], F32) for c in caches]
    new_spec = pl.BlockSpec((1, 1, 3, t_new, GROUP_WIDTH), lambda b: (0, 0, 0, b, 0))
    out_spec = pl.BlockSpec((1, 2, SLABS, t_new, LANES), lambda b: (0, 0, 0, b, 0))
    out_sds = jax.ShapeDtypeStruct((1, 2, SLABS, batch * t_new, LANES), F32)
    return dict(
        batch=batch,
        args=[*qkv, *caches, *buckets, table],
        out_shape=[out_sds] * N_GROUPS + [jax.ShapeDtypeStruct(c.shape, F32) for c in caches],
        in_specs=[new_spec] * N_GROUPS + [pl.BlockSpec(memory_space=pl.ANY)] * N_GROUPS
        + [_resident(b.shape) for b in buckets] + [_SMEM],
        out_specs=[out_spec] * N_GROUPS + [pl.BlockSpec(memory_space=pl.ANY)] * N_GROUPS,
        scratch_shapes=[pltpu.VMEM((HEADS_PER_GROUP * t_new, p + LANES), F32) for p in pasts] + window_ring
        + [pltpu.SemaphoreType.DMA((N_GROUPS, WINDOW_RING))] + window_stages
        + [pltpu.SemaphoreType.DMA((N_GROUPS, WINDOW_STAGES))])


def _sample_attention(parts):
    return pl.pallas_call(
        _sample_attn_kernel, grid=(parts["batch"],), out_shape=parts["out_shape"],
        in_specs=parts["in_specs"], out_specs=parts["out_specs"], scratch_shapes=parts["scratch_shapes"],
        compiler_params=_params(), name="sample_attn")(*parts["args"])


def _back_kernel(x_ref, ol0_ref, ol1_ref, ol2_ref, u_ref, halo_ref, gate_ref,
                 wab_ref, wpg_ref, pscale_ref, wpb_ref, wout_ref, g2_ref, wup_ref, wdn_ref,
                 out_ref, ue_ref, level_ref, mixed_ref, act_ref, attn_ref, pooled_ref, merged_ref,
                 *, pos_base, blocks_per_seq):
    block = pl.program_id(0)
    tm, d_model = x_ref.shape
    n_seq = halo_ref.shape[0]
    t = tm // n_seq
    pool_width = u_ref.shape[1]
    gw = pool_width // len(POOL_WINDOWS)

    for s in range(SLABS):
        ol_refs = (ol0_ref, ol1_ref, ol2_ref)
        lses = [r[0, 1, s] for r in ol_refs]
        top = jnp.maximum(jnp.maximum(lses[0], lses[1]), lses[2])
        num = jnp.zeros_like(top)
        den = jnp.zeros_like(top)
        for ol_ref, lse in zip(ol_refs, lses):
            e = jnp.exp2(lse - top)
            num = num + e * ol_ref[0, 0, s]
            den = den + e
        attn_ref[:, s * LANES:(s + 1) * LANES] = (num / den).astype(BF16)

    start = pos_base + (block % blocks_per_seq) * t
    first, end = POOL_PAD - POOL_HALO, POOL_PAD + t
    ue_ref[:, 0:first, :] = jnp.zeros((n_seq, first, pool_width), F32)
    ue_ref[:, first:POOL_PAD, :] = jnp.where(start > 0, halo_ref[...], 0.0)
    ue_ref[:, POOL_PAD:, :] = u_ref[...].reshape(n_seq, t, pool_width)
    level_ref[:, :, 0:first, :] = jnp.zeros((2, n_seq, first, gw), F32)
    pos = start + lax.broadcasted_iota(jnp.int32, (1, t, gw), 1)
    for gi, win in enumerate(POOL_WINDOWS):
        cols = slice(gi * gw, (gi + 1) * gw)
        read = lambda lo, hi, cols=cols: ue_ref[:, lo:hi, cols]
        span = 1
        while 2 * span < win:
            buf = (span.bit_length() - 1) % 2
            level_ref[buf, :, first:, :] = read(first, end) + read(first - span, end - span)
            read = lambda lo, hi, buf=buf: level_ref[buf, :, lo:hi, :]
            span *= 2
        s = read(POOL_PAD, end) + read(POOL_PAD - span, end - span)
        cnt = jnp.minimum(pos + 1, win).astype(F32)
        d = (s / cnt - ue_ref[:, POOL_PAD:, cols]).reshape(tm, gw)
        y = jnp.dot(d.astype(BF16), wpg_ref[gi], preferred_element_type=F32)
        pooled_ref[:, cols] = (y * pscale_ref[:, cols]).astype(BF16)

    for lo in range(0, d_model, MXU_TILE):
        cols = slice(lo, lo + MXU_TILE)
        branch_a = jnp.dot(attn_ref[...], wab_ref[:, cols], preferred_element_type=F32)
        branch_b = jnp.dot(pooled_ref[...], wpb_ref[:, cols], preferred_element_type=F32)
        gate_a = gate_ref[:, cols].astype(F32)
        gate_b = gate_ref[:, d_model + lo:d_model + lo + MXU_TILE].astype(F32)
        merged_ref[:, cols] = (gate_a * branch_a + gate_b * branch_b).astype(BF16)
    for lo in range(0, d_model, MXU_TILE):
        cols = slice(lo, lo + MXU_TILE)
        mixed_ref[:, cols] = x_ref[:, cols] + jnp.dot(merged_ref[...], wout_ref[:, cols],
                                                      preferred_element_type=F32)

    _ffn_block(mixed_ref, g2_ref, wup_ref, wdn_ref, out_ref, act_ref)


def _back(x, ol, u, halo, gates, merge_w, ffn_w, *, tm, halo_block, halo_index, pos_base, blocks_per_seq):
    m, d = x.shape
    pool_width = u.shape[1]
    n_seq = halo_block[0]
    d_ff = ffn_w[2].shape[0]
    row = lambda wd: pl.BlockSpec((tm, wd), lambda i: (i, 0))
    slab = pl.BlockSpec((1, 2, SLABS, tm, LANES),
                        lambda i: (i // blocks_per_seq, 0, 0, i % blocks_per_seq, 0))
    return pl.pallas_call(
        functools.partial(_back_kernel, pos_base=pos_base, blocks_per_seq=blocks_per_seq),
        out_shape=jax.ShapeDtypeStruct((m, d), F32),
        grid=(m // tm,),
        in_specs=[row(d)] + [slab] * N_GROUPS
        + [row(pool_width), pl.BlockSpec(halo_block, halo_index), row(2 * d)]
        + [_resident(a.shape) for a in (*merge_w, *ffn_w)],
        out_specs=row(d),
        scratch_shapes=[pltpu.VMEM((n_seq, POOL_PAD + tm // n_seq, pool_width), F32),
                        pltpu.VMEM((2, n_seq, POOL_PAD + tm // n_seq, pool_width // len(POOL_WINDOWS)), F32),
                        pltpu.VMEM((tm, d), F32), pltpu.VMEM((tm, d_ff), BF16),
                        pltpu.VMEM((tm, GROUP_WIDTH), BF16), pltpu.VMEM((tm, pool_width), BF16),
                        pltpu.VMEM((tm, d), BF16)],
        compiler_params=_params(),
        name="back",
    )(x, *ol, u, halo, gates, *merge_w, *ffn_w)


def _window_in(cache):
    n_seq, rows = cache.shape[0:2]
    return jnp.transpose(cache, (0, 2, 3, 4, 1)).reshape(n_seq, 2, GROUP_WIDTH, rows)


def _window_out(kv):
    n_seq, _, _, rows = kv.shape
    return jnp.transpose(kv.reshape(n_seq, 2, HEADS_PER_GROUP, HEAD_DIM, rows), (0, 4, 1, 2, 3))


def kernel(x_prompt, x_sample, cache_kv_w128, cache_kv_w512, cache_kv_w2048, state_pool, rel_bias_table,
           norm_ffn1, ffn1_w_up, ffn1_w_down, norm_mix, w_in, q_norm, k_norm, pool_w_group, pool_scale,
           w_attn_branch, w_pool_branch, w_out, norm_ffn2, ffn2_w_up, ffn2_w_down):
    batch, seq, d_model = x_prompt.shape
    dec_batch, dec_seq, _ = x_sample.shape
    depth = norm_ffn1.shape[0]
    pool_width = state_pool.shape[-1]
    tm = ROW_BLOCK
    ms = dec_batch * dec_seq
    seg = jnp.asarray(np.kron(np.eye(HEADS_PER_GROUP), np.full((HEAD_DIM, HEAD_DIM), 1.0 / HEAD_DIM)), BF16)
    table = rel_bias_table.astype(F32)
    caches_in = (cache_kv_w128, cache_kv_w512, cache_kv_w2048)
    keep_prompt = tuple(min(win, seq) for win, _ in ATTN_GROUPS)
    dils = tuple(dil for _, dil in ATTN_GROUPS)

    xp = x_prompt.reshape(batch * seq, d_model)
    xs = x_sample.reshape(ms, d_model)
    kv_p, kv_s, pool_p, pool_s = ([], [], []), ([], [], []), [], []
    for layer in range(depth):
        gain = lambda a: a[layer].reshape(1, -1).astype(F32)
        qg, kg = gain(q_norm), gain(k_norm)
        pool_groups = pool_w_group[layer]
        late_w = (ffn2_w_down[layer], w_attn_branch[layer], pool_groups.reshape(-1, pool_groups.shape[-1]),
                  w_pool_branch[layer], w_out[layer])

        xs, (wup1, wdn1) = _ffn(xs, gain(norm_ffn1), ffn1_w_up[layer], ffn1_w_down[layer], ms, narrow=True)
        outs = _proj(xs, gain(norm_mix), w_in[layer], qg, kg, seg, tm=ms, n_seq=1, dils=(1,) * N_GROUPS,
                     qkv_dtype=F32, keep_rows=None, narrow=True)
        qkv_s, u_s, gates_s, win = outs[0:N_GROUPS], outs[N_GROUPS], outs[N_GROUPS + 1], outs[N_GROUPS + 2]
        sample_parts = _sample_attention_parts(qkv_s, [_window_in(c[layer]) for c in caches_in], table)

        if (batch * seq) // tm == dec_batch:
            xp, sample_outs = _ffn(xp, gain(norm_ffn1), wup1, wdn1, tm, rider=sample_parts)
        else:
            xp, _ = _ffn(xp, gain(norm_ffn1), wup1, wdn1, tm)
            sample_outs = _sample_attention(sample_parts)
        outs = _proj(xp, gain(norm_mix), win, qg, kg, seg, tm=2 * tm, n_seq=batch, dils=dils,
                     qkv_dtype=BF16, keep_rows=keep_prompt)
        qkv, u, gates = outs[0:N_GROUPS], outs[N_GROUPS], outs[N_GROUPS + 1]
        windows, pstate = outs[N_GROUPS + 2:2 * N_GROUPS + 2], outs[2 * N_GROUPS + 2]
        narrowing = ((ffn2_w_up[layer],), late_w) + ((),) * (N_GROUPS - 2)
        ol, narrowed = zip(*[_prompt_attention(qkv[g], table, g, to_narrow=narrowing[g])
                             for g in range(N_GROUPS)])
        (wup2,), (wdn2, wab, wpg, wpb, wout) = narrowed[0:2]
        ffn2_w = (gain(norm_ffn2), wup2, wdn2)
        merge_w = (wab, wpg.reshape(pool_groups.shape), gain(pool_scale), wpb, wout)
        xp = _back(xp, ol, u, u.reshape(-1, POOL_HALO, pool_width), gates, merge_w, ffn2_w, tm=tm,
                   halo_block=(1, POOL_HALO, pool_width),
                   halo_index=lambda blk: (jnp.maximum(blk * (tm // POOL_HALO) - 1, 0), 0, 0),
                   pos_base=0, blocks_per_seq=seq // tm)
        for g in range(N_GROUPS):
            kv_p[g].append(_window_out(windows[g]))
        pool_p.append(pstate[:, POOL_HALO - POOL_STATE:])

        ol, windows = sample_outs[0:N_GROUPS], sample_outs[N_GROUPS:2 * N_GROUPS]
        history = jnp.concatenate(
            [jnp.zeros((dec_batch, POOL_HALO - POOL_STATE, pool_width), F32), state_pool[layer]], axis=1)
        xs = _back(xs, ol, u_s, history, gates_s, merge_w, ffn2_w, tm=ms,
                   halo_block=(dec_batch, POOL_HALO, pool_width), halo_index=lambda blk: (0, 0, 0),
                   pos_base=PAST_LEN, blocks_per_seq=1)
        for g in range(N_GROUPS):
            kv_s[g].append(_window_out(windows[g]))
        ue = jnp.concatenate([state_pool[layer], u_s.reshape(dec_batch, dec_seq, pool_width)], axis=1)
        pool_s.append(ue[:, ue.shape[1] - POOL_STATE:])

    stack = lambda xs_: jnp.stack(xs_, axis=0)
    return (xp.reshape(batch, seq, d_model), xs.reshape(dec_batch, dec_seq, d_model),
            stack(kv_p[0]), stack(kv_p[1]), stack(kv_p[2]), stack(pool_p),
            stack(kv_s[0]), stack(kv_s[1]), stack(kv_s[2]), stack(pool_s))
```

```python
import functools
import math

import numpy as np
import jax
import jax.numpy as jnp
from jax import lax
from jax.experimental import pallas as pl
from jax.experimental.pallas import tpu as pltpu

HEAD_DIM = 64
HEADS_PER_GROUP = 4
GROUP_WIDTH = HEADS_PER_GROUP * HEAD_DIM
ATTN_GROUPS = ((128, 1), (512, 4), (2048, 16))
N_GROUPS = len(ATTN_GROUPS)
ATTN_WIDTH = N_GROUPS * GROUP_WIDTH
N_BUCKETS = 32
MAX_DISTANCE = 2048
POOL_WINDOWS = (2, 4, 8, 16)
POOL_STATE = max(POOL_WINDOWS) - 1
POOL_HALO = 16
POOL_PAD = POOL_HALO + 8
EPS = 1e-6
PAST_LEN = 8192
NEG = -1e30
LOG2E = math.log2(math.e)
LANES = 128
MXU_TILE = 256
SLABS = GROUP_WIDTH // LANES
SINGLE_OP_STRIDE = 4
WINDOW_RING = 3
RIDER_PHASE_GAP = 2
ROW_BLOCK = 512
QBLK = 128
ATTN_UNROLL = 16
VMEM_LIMIT = 58 * 1024 * 1024

F32 = jnp.float32
BF16 = jnp.bfloat16
NT_DIMS = (((1,), (1,)), ((), ()))


def _t5_buckets(distance):
    max_exact = N_BUCKETS // 2
    d = np.asarray(distance, dtype=np.int32)
    ratio = np.log(np.maximum(d, 1).astype(np.float32) / np.float32(max_exact))
    large = max_exact + (ratio / np.float32(math.log(MAX_DISTANCE / max_exact))
                         * (N_BUCKETS - max_exact)).astype(np.int32)
    large = np.minimum(large, N_BUCKETS - 1)
    return np.where(d < max_exact, d, large).astype(np.int32)


def _params(n_grid_dims=1):
    return pltpu.CompilerParams(dimension_semantics=("arbitrary",) * n_grid_dims,
                                vmem_limit_bytes=VMEM_LIMIT)


def _resident(shape):
    return pl.BlockSpec(shape, lambda *_: (0,) * len(shape), pipeline_mode=pl.Buffered(1))


_SMEM = pl.BlockSpec(memory_space=pltpu.SMEM)


def _rmsnorm(x, g):
    ms = jnp.mean(x * x, axis=-1, keepdims=True)
    return x * lax.rsqrt(ms + EPS) * g


def _head_select(parts, lane_head):
    out = jnp.where(lane_head == 0, parts[0], 0.0)
    for h in range(1, HEADS_PER_GROUP):
        out = jnp.where(lane_head == h, parts[h], out)
    return out


def _stack_heads(q, lane_head):
    return jnp.concatenate([jnp.where(lane_head == h, q, 0.0)
                            for h in range(HEADS_PER_GROUP)], axis=0).astype(BF16)


def _build_bias(bias_ref, bucket_ref, table_ref, g):
    buckets = bucket_ref[...]
    rows = buckets.shape[0]
    for h in range(HEADS_PER_GROUP):
        acc = jnp.full(buckets.shape, NEG, F32)
        for b in range(N_BUCKETS):
            acc = jnp.where(buckets == b, table_ref[b, g * HEADS_PER_GROUP + h] * LOG2E, acc)
        bias_ref[h * rows:(h + 1) * rows, :] = acc


def _matmul_weight(w_ref, narrow_ref, rows, cols, arrive=None):
    if arrive is not None:
        arrive(cols)
    w = w_ref[rows, cols]
    if narrow_ref is not None:
        w = w.astype(BF16)
        narrow_ref[rows, cols] = w
    return w


def _ffn_block(x_ref, g_ref, wup_ref, wdn_ref, o_ref, act_ref, narrow_refs=(None, None), rider_phases=(),
               arrive=(None, None)):
    d_ff, d = wdn_ref.shape
    every = slice(None)
    rider_phases = iter(rider_phases)
    next(rider_phases, None)
    h = _rmsnorm(x_ref[...], g_ref[...]).astype(BF16)
    for chunk, lo in enumerate(range(0, d_ff, MXU_TILE)):
        w_gate = _matmul_weight(wup_ref, narrow_refs[0], every, slice(lo, lo + MXU_TILE), arrive[0])
        w_up = _matmul_weight(wup_ref, narrow_refs[0], every, slice(d_ff + lo, d_ff + lo + MXU_TILE), arrive[0])
        gate = jnp.dot(h, w_gate, preferred_element_type=F32)
        up = jnp.dot(h, w_up, preferred_element_type=F32)
        if chunk % RIDER_PHASE_GAP == RIDER_PHASE_GAP - 1:
            next(rider_phases, None)
        act_ref[:, lo:lo + MXU_TILE] = (gate * jax.nn.sigmoid(gate) * up).astype(BF16)
    for lo in range(0, d, MXU_TILE):
        cols = slice(lo, lo + MXU_TILE)
        w_down = _matmul_weight(wdn_ref, narrow_refs[1], every, cols, arrive[1])
        o_ref[:, cols] = x_ref[:, cols] + 0.5 * jnp.dot(act_ref[...], w_down, preferred_element_type=F32)


def _ffn_kernel(*refs, rider_counts, narrow):
    n_in, n_out = rider_counts
    n_narrow = 2 if narrow else 0
    x_ref, g_ref, wup_ref, wdn_ref = refs[0:4]
    o_ref = refs[4 + n_in]
    narrow_refs = refs[5 + n_in:5 + n_in + n_narrow] if narrow else (None, None)
    first_rider_out = 5 + n_in + n_narrow
    act_ref = refs[first_rider_out + n_out]
    rider_phases, arrive = (), (None, None)
    if n_in:
        rider_phases = _sample_attn_phases((*refs[4:4 + n_in], *refs[first_rider_out:first_rider_out + n_out],
                                            *refs[first_rider_out + n_out + 1:]))
    if narrow:
        d_ff = wdn_ref.shape[0]
        hbm, (wup_ref, wdn_ref), sem_ref = (wup_ref, wdn_ref), refs[-3:-1], refs[-1]
        starts = ([(0, lo + half) for lo in range(0, d_ff, MXU_TILE) for half in (0, d_ff)]
                  + [(1, lo) for lo in range(0, wdn_ref.shape[1], MXU_TILE)])
        copies = {}
        for k, (w, lo) in enumerate(starts):
            cols = (slice(None), pl.ds(lo, MXU_TILE))
            copies[w, lo] = pltpu.make_async_copy(hbm[w].at[cols], (wup_ref, wdn_ref)[w].at[cols], sem_ref.at[k])
            copies[w, lo].start()
        arrive = tuple(lambda cols, w=w: copies[w, cols.start].wait() for w in range(2))
    _ffn_block(x_ref, g_ref, wup_ref, wdn_ref, o_ref, act_ref, narrow_refs, rider_phases, arrive)


def _ffn(x, gain, w_up, w_down, tm, rider=None, narrow=False):
    m, d = x.shape
    d_ff = w_down.shape[0]
    assert m % tm == 0 and d_ff % MXU_TILE == 0 and d % MXU_TILE == 0
    assert not narrow or m == tm, "each weight chunk is narrowed once only on a single-step grid"
    rider = rider or dict(args=[], in_specs=[], out_shape=[], out_specs=[], scratch_shapes=[], batch=m // tm)
    assert rider["batch"] == m // tm
    weights = (w_up, w_down)
    narrowed = [jax.ShapeDtypeStruct(w.shape, BF16) for w in weights] if narrow else []
    weight_specs = [_resident(w.shape) for w in weights]
    streamed = []
    if narrow:
        assert not rider["args"]
        weight_specs = [pl.BlockSpec(memory_space=pl.ANY)] * 2
        streamed = [pltpu.VMEM(w.shape, F32) for w in weights] + [
            pltpu.SemaphoreType.DMA(((2 * d_ff + d) // MXU_TILE,))]
    outs = pl.pallas_call(
        functools.partial(_ffn_kernel, rider_counts=(len(rider["args"]), len(rider["out_shape"])), narrow=narrow),
        out_shape=[jax.ShapeDtypeStruct((m, d), F32)] + narrowed + rider["out_shape"],
        grid=(m // tm,),
        in_specs=[pl.BlockSpec((tm, d), lambda i: (i, 0)),
                  _resident((1, d))] + weight_specs + rider["in_specs"],
        out_specs=[pl.BlockSpec((tm, d), lambda i: (i, 0))] + [_resident(w.shape) for w in narrowed]
        + rider["out_specs"],
        scratch_shapes=[pltpu.VMEM((tm, d_ff), BF16)] + rider["scratch_shapes"] + streamed,
        compiler_params=_params(),
        name="ffn",
    )(x, gain, w_up, w_down, *rider["args"])
    return outs[0], outs[1:]


def _proj_kernel(x_ref, g_ref, win_ref, qn_ref, kn_ref, seg_ref, *refs,
                 dils, kv_rows, with_state, narrow, pool_width, d_model):
    qkv_refs = refs[0:N_GROUPS]
    u_ref, gate_ref = refs[N_GROUPS:N_GROUPS + 2]
    state_refs = refs[N_GROUPS + 2:]
    narrow_ref = refs[-3] if narrow else None
    slab_ref, regroup_ref = refs[-2:]
    tm = x_ref.shape[0]
    h = _rmsnorm(x_ref[...], g_ref[...]).astype(BF16)

    def proj(lo, width):
        w = _matmul_weight(win_ref, narrow_ref, slice(None), slice(lo, lo + width))
        return jnp.dot(h, w, preferred_element_type=F32)

    def head_norm(y, gain):
        ms = jnp.dot((y * y).astype(BF16), seg_ref[...], preferred_element_type=F32)
        return y * lax.rsqrt(ms + EPS) * gain

    def emit(dst_ref, which, slot, val, dil):
        if dil == 1:
            dst_ref[0, 0, which] = val.astype(dst_ref.dtype)
            return
        for s in range(SLABS):
            slab_ref[slot, s] = val[:, s * LANES:(s + 1) * LANES]
        n = tm // dil
        if dil <= SINGLE_OP_STRIDE:
            for r in range(dil):
                parts = [slab_ref[slot, s, pl.ds(r, n, stride=dil), :] for s in range(SLABS)]
                dst_ref[0, r, which] = jnp.concatenate(parts, axis=-1).astype(dst_ref.dtype)
            return
        outer = SINGLE_OP_STRIDE
        inner = dil // outer
        assert inner <= SINGLE_OP_STRIDE
        quarter = tm // outer
        for a in range(outer):
            for s in range(SLABS):
                regroup_ref[s, a * quarter:(a + 1) * quarter, :] = slab_ref[slot, s, pl.ds(a, quarter, stride=outer), :]
        for r in range(dil):
            start = (r % outer) * quarter + r // outer
            parts = [regroup_ref[s, pl.ds(start, n, stride=inner), :] for s in range(SLABS)]
            dst_ref[0, r, which] = jnp.concatenate(parts, axis=-1).astype(dst_ref.dtype)

    def finish(g, which, y):
        cols = slice(g * GROUP_WIDTH, (g + 1) * GROUP_WIDTH)
        if which == 0:
            y = head_norm(y, qn_ref[:, cols]) * (HEAD_DIM ** -0.5 * LOG2E)
        elif which == 1:
            y = head_norm(y, kn_ref[:, cols])
        emit(qkv_refs[g], which, 3 * sum(d > 1 for d in dils[:g]) + which, y, dils[g])
        if with_state and which > 0:
            state_refs[g][0, which - 1] = y.T[:, tm - kv_rows[g]:]

    pending = None
    for g in range(N_GROUPS):
        for which in range(3):
            y = proj(which * ATTN_WIDTH + g * GROUP_WIDTH, GROUP_WIDTH)
            if pending is not None:
                finish(*pending)
            pending = (g, which, y)
    u = proj(3 * ATTN_WIDTH, pool_width)
    finish(*pending)
    u_ref[...] = u
    if with_state:
        state_refs[N_GROUPS][0] = u[tm - POOL_HALO:, :]
    gates_base = 3 * ATTN_WIDTH + pool_width
    for lo in range(0, 2 * d_model, MXU_TILE):
        gate_ref[:, lo:lo + MXU_TILE] = jax.nn.sigmoid(proj(gates_base + lo, MXU_TILE)).astype(BF16)


def _proj(x, gain, w_in, q_gain, k_gain, seg, *, tm, n_seq, dils, qkv_dtype, keep_rows, narrow=False):
    m, d = x.shape
    pool_width = w_in.shape[1] - 3 * ATTN_WIDTH - 2 * d
    seq_len = m // n_seq
    assert seq_len % tm == 0 and all(tm % dil == 0 for dil in dils)
    blocks_per_seq = seq_len // tm
    with_state = keep_rows is not None
    seq_block = lambda i: (i // blocks_per_seq, i % blocks_per_seq)

    row = lambda w: pl.BlockSpec((tm, w), lambda i: (i, 0))
    out_shape, out_specs = [], []
    for g in range(N_GROUPS):
        out_shape.append(jax.ShapeDtypeStruct((n_seq, dils[g], 3, seq_len // dils[g], GROUP_WIDTH), qkv_dtype))
        out_specs.append(pl.BlockSpec((1, dils[g], 3, tm // dils[g], GROUP_WIDTH),
                                      lambda i: (seq_block(i)[0], 0, 0, seq_block(i)[1], 0)))
    out_shape += [jax.ShapeDtypeStruct((m, pool_width), F32), jax.ShapeDtypeStruct((m, 2 * d), BF16)]
    out_specs += [row(pool_width), row(2 * d)]
    kv_rows = kv_first = ()
    if with_state:
        kv_rows = tuple(min(r, tm) for r in keep_rows)
        kv_first = tuple(blocks_per_seq - keep_rows[g] // kv_rows[g] for g in range(N_GROUPS))
        for g in range(N_GROUPS):
            out_shape.append(jax.ShapeDtypeStruct((n_seq, 2, GROUP_WIDTH, keep_rows[g]), F32))
            out_specs.append(pl.BlockSpec(
                (1, 2, GROUP_WIDTH, kv_rows[g]),
                lambda i, first=kv_first[g]: (seq_block(i)[0], 0, 0, jnp.maximum(seq_block(i)[1] - first, 0))))
        out_shape.append(jax.ShapeDtypeStruct((n_seq, POOL_HALO, pool_width), F32))
        out_specs.append(pl.BlockSpec((1, POOL_HALO, pool_width), lambda i: (seq_block(i)[0], 0, 0)))
    if narrow:
        assert m == tm, "each weight chunk is narrowed once only on a single-step grid"
        out_shape.append(jax.ShapeDtypeStruct(w_in.shape, BF16))
        out_specs.append(_resident(w_in.shape))
    return pl.pallas_call(
        functools.partial(_proj_kernel, dils=dils, kv_rows=kv_rows, with_state=with_state, narrow=narrow,
                          pool_width=pool_width, d_model=d),
        out_shape=out_shape,
        grid=(m // tm,),
        in_specs=[row(d), _resident((1, d)), _resident(w_in.shape),
                  _resident((1, ATTN_WIDTH)), _resident((1, ATTN_WIDTH)),
                  _resident((GROUP_WIDTH, GROUP_WIDTH))],
        out_specs=out_specs,
        scratch_shapes=[pltpu.VMEM((max(1, 3 * sum(dil > 1 for dil in dils)), SLABS, tm, LANES), F32),
                        pltpu.VMEM((SLABS, tm, LANES), F32)],
        compiler_params=_params(),
        name="proj",
    )(x, gain, w_in, q_gain, k_gain, seg)


def _attn_kernel(*refs, g, dil, n_blocks, unroll, n_cast):
    qkv_ref, bucket_ref, hmask_ref, table_ref = refs[0:4]
    wide_refs = refs[4:4 + n_cast]
    out_ref = refs[4 + n_cast]
    narrow_refs = refs[5 + n_cast:5 + 2 * n_cast]
    bias_ref = refs[5 + 2 * n_cast]
    for wide_ref, narrow_ref in zip(wide_refs, narrow_refs):
        narrow_ref[...] = wide_ref[...].astype(BF16)

    @pl.when(pl.program_id(0) == 0)
    def _():
        _build_bias(bias_ref.at[1], bucket_ref, table_ref, g)
        bias_ref[0, :, 0:QBLK] = bias_ref[1, :, QBLK:2 * QBLK]
        bias_ref[0, :, QBLK:2 * QBLK] = jnp.full((HEADS_PER_GROUP * QBLK, QBLK), NEG, F32)

    first_head = lax.broadcasted_iota(jnp.int32, (QBLK, LANES), 1) < HEAD_DIM
    heads = range(HEADS_PER_GROUP)

    def per_slab(col):
        part = lambda h: jnp.broadcast_to(col[h * QBLK:(h + 1) * QBLK], (QBLK, LANES))
        return [jnp.where(first_head, part(2 * s), part(2 * s + 1)) for s in range(SLABS)]

    def body(n, carry):
        res = n // n_blocks
        i = n % n_blocks
        row0 = pl.multiple_of(i * QBLK, QBLK)
        span0 = pl.multiple_of(jnp.maximum(i - 1, 0) * QBLK, QBLK)
        q = qkv_ref[0, res, 0, pl.ds(row0, QBLK), :]
        qs = jnp.concatenate([q * hmask_ref[h] for h in heads], axis=0)
        s = lax.dot_general(qs, qkv_ref[0, res, 1, pl.ds(span0, 2 * QBLK), :], NT_DIMS,
                            preferred_element_type=F32) + bias_ref[jnp.minimum(i, 1)]
        m = jnp.max(s, axis=-1, keepdims=True)
        p = jnp.exp2(s - m)
        l = jnp.sum(p, axis=-1, keepdims=True)
        p = p.astype(BF16)
        v = qkv_ref[0, res, 2, pl.ds(span0, 2 * QBLK), :]
        p_wide = jnp.concatenate([p[h * QBLK:(h + 1) * QBLK] for h in heads], axis=1)
        v_tall = jnp.concatenate([v * hmask_ref[h] for h in heads], axis=0)
        pv = jnp.dot(p_wide, v_tall, preferred_element_type=F32)
        token0 = res + dil * row0
        dst = pl.ds(token0, QBLK) if dil == 1 else pl.ds(token0, QBLK, stride=dil)
        for s_, (m_s, l_s) in enumerate(zip(per_slab(m), per_slab(l))):
            out_ref[0, 0, s_, dst, :] = pv[:, s_ * LANES:(s_ + 1) * LANES] / l_s
            out_ref[0, 1, s_, dst, :] = m_s + jnp.log2(l_s)
        return carry

    lax.fori_loop(0, dil * n_blocks, body, 0, unroll=unroll)


def _prompt_buckets(g):
    win, dil = ATTN_GROUPS[g]
    delta = np.arange(QBLK)[:, None] + QBLK - np.arange(2 * QBLK)[None, :]
    valid = (delta >= 0) & (delta <= win // dil)
    return np.where(valid, _t5_buckets(dil * np.clip(delta, 0, win // dil)), -1).astype(np.int32)


def _prompt_attention(qkv, table, g, to_narrow=()):
    _, dil = ATTN_GROUPS[g]
    batch, _, _, sub, _ = qkv.shape
    n_blocks = sub // QBLK
    assert sub % QBLK == 0 and n_blocks >= 2 and (dil * n_blocks) % ATTN_UNROLL == 0
    assert all(w.shape[0] % (16 * batch) == 0 for w in to_narrow)
    cast_specs = [pl.BlockSpec((w.shape[0] // batch, w.shape[1]), lambda b: (b, 0)) for w in to_narrow]
    head_of_lane = np.arange(GROUP_WIDTH) // HEAD_DIM
    hmask = head_of_lane[None, None, :] == np.arange(HEADS_PER_GROUP)[:, None, None]
    in_spec = pl.BlockSpec((1, dil, 3, sub, GROUP_WIDTH), lambda b: (b, 0, 0, 0, 0))
    out_spec = pl.BlockSpec((1, 2, SLABS, sub * dil, LANES), lambda b: (b, 0, 0, 0, 0))
    out_sds = jax.ShapeDtypeStruct((batch, 2, SLABS, sub * dil, LANES), F32)
    outs = pl.pallas_call(
        functools.partial(_attn_kernel, g=g, dil=dil, n_blocks=n_blocks, unroll=ATTN_UNROLL,
                          n_cast=len(to_narrow)),
        out_shape=[out_sds] + [jax.ShapeDtypeStruct(w.shape, BF16) for w in to_narrow],
        grid=(batch,),
        in_specs=[in_spec, _resident((QBLK, 2 * QBLK)), _resident(hmask.shape), _SMEM] + cast_specs,
        out_specs=[out_spec] + cast_specs,
        scratch_shapes=[pltpu.VMEM((2, HEADS_PER_GROUP * QBLK, 2 * QBLK), F32)],
        compiler_params=_params(),
        name=f"attn_g{g}",
    )(qkv, jnp.asarray(_prompt_buckets(g)), jnp.asarray(hmask, BF16), table, *to_narrow)
    return outs[0], outs[1:]


def _sample_attn_phases(refs):
    n = N_GROUPS
    qkv_refs, cache_refs, bucket_refs, table_ref = refs[0:n], refs[n:2 * n], refs[2 * n:3 * n], refs[3 * n]
    out_refs, win_refs, bias_refs = refs[3 * n + 1:4 * n + 1], refs[4 * n + 1:5 * n + 1], refs[5 * n + 1:6 * n + 1]
    ring_refs, sem_ref = refs[6 * n + 1:7 * n + 1], refs[7 * n + 1]
    step, n_steps = pl.program_id(0), pl.num_programs(0)

    def window_copy(g, seq):
        slot = seq % WINDOW_RING
        return pltpu.make_async_copy(cache_refs[g].at[seq], ring_refs[g].at[slot], sem_ref.at[g, slot])

    @pl.when(step == 0)
    def _():
        for g in range(N_GROUPS):
            _build_bias(bias_refs[g], bucket_refs[g], table_ref, g)
            for seq in range(WINDOW_RING - 1):
                window_copy(g, seq).start()

    @pl.when(step + WINDOW_RING - 1 < n_steps)
    def _():
        for g in range(N_GROUPS):
            window_copy(g, step + WINDOW_RING - 1).start()

    for g in range(N_GROUPS):
        window_copy(g, step).wait()
    slot = step % WINDOW_RING

    t_new = qkv_refs[0].shape[3]
    lane_head = lax.broadcasted_iota(jnp.int32, (t_new, GROUP_WIDTH), 1) // HEAD_DIM
    new_lane = lax.broadcasted_iota(jnp.int32, (GROUP_WIDTH, LANES), 1) >= LANES - t_new
    pad = jnp.zeros((LANES - t_new, GROUP_WIDTH), F32)

    groups = []
    for g in range(N_GROUPS):
        q, k, v = (qkv_refs[g][0, 0, which] for which in range(3))
        cache_ref, win_ref = ring_refs[g].at[slot], win_refs[g]
        past = cache_ref.shape[2]
        new_rows = [jnp.concatenate([pad, rows_], axis=0) for rows_ in (k, v)]
        for c in range(2):
            shifted = pltpu.roll(cache_ref[c], past - t_new, 1)
            if past > LANES:
                win_ref[0, c, :, 0:past - LANES] = shifted[:, 0:past - LANES]
            win_ref[0, c, :, past - LANES:past] = jnp.where(new_lane, new_rows[c].T, shifted[:, past - LANES:])
        groups.append(dict(past=past, qs=_stack_heads(q, lane_head),
                           old=[cache_ref[c].astype(BF16) for c in range(2)],
                           new=[rows_.astype(BF16) for rows_ in new_rows]))
    yield

    for g, grp in enumerate(groups):
        past, bias_ref = grp["past"], bias_refs[g]
        s_old = jnp.dot(grp["qs"], grp["old"][0], preferred_element_type=F32) + bias_ref[:, 0:past]
        s_new = lax.dot_general(grp["qs"], grp["new"][0], NT_DIMS,
                                preferred_element_type=F32) + bias_ref[:, past:past + LANES]
        m = jnp.maximum(jnp.max(s_old, axis=-1, keepdims=True), jnp.max(s_new, axis=-1, keepdims=True))
        p_old = jnp.exp2(s_old - m)
        p_new = jnp.exp2(s_new - m)
        grp.update(m=m, l=jnp.sum(p_old, axis=-1, keepdims=True) + jnp.sum(p_new, axis=-1, keepdims=True),
                   p_old=p_old.astype(BF16), p_new=p_new.astype(BF16))
    yield

    for g, grp in enumerate(groups):
        pv = (lax.dot_general(grp["p_old"], grp["old"][1], NT_DIMS, preferred_element_type=F32)
              + jnp.dot(grp["p_new"], grp["new"][1], preferred_element_type=F32)) / grp["l"]
        lse = jnp.broadcast_to(grp["m"] + jnp.log2(grp["l"]), pv.shape)
        rows = lambda a: [a[h * t_new:(h + 1) * t_new] for h in range(HEADS_PER_GROUP)]
        for which, val in enumerate((_head_select(rows(pv), lane_head), _head_select(rows(lse), lane_head))):
            for s_ in range(SLABS):
                out_refs[g][0, which, s_] = val[:, s_ * LANES:(s_ + 1) * LANES]


def _sample_attn_kernel(*refs):
    for _ in _sample_attn_phases(refs):
        pass


def _sample_buckets(g, past, t_new):
    win, dil = ATTN_GROUPS[g]
    lane = np.arange(past + LANES)[None, :]
    key_pos = np.where(lane < past, lane, lane - (LANES - t_new))
    delta = past + np.arange(t_new)[:, None] - key_pos
    valid = ((lane < past) | (lane >= past + LANES - t_new)) & (delta >= 0) & (delta % dil == 0) & (delta <= win)
    return np.where(valid, _t5_buckets(np.clip(delta, 0, win)), -1).astype(np.int32)


def _sample_attention_parts(qkv, caches, table):
    batch = caches[0].shape[0]
    assert batch >= WINDOW_RING
    t_new = qkv[0].shape[3] // batch
    pasts = [c.shape[3] for c in caches]
    window_ring = [pltpu.VMEM((WINDOW_RING,) + c.shape[1:], F32) for c in caches]
    buckets = [jnp.asarray(_sample_buckets(g, pasts[g], t_new)) for g in range(N_GROUPS)]
    cache_spec = lambda p: pl.BlockSpec((1, 2, GROUP_WIDTH, p), lambda b: (b, 0, 0, 0))
    new_spec = pl.BlockSpec((1, 1, 3, t_new, GROUP_WIDTH), lambda b: (0, 0, 0, b, 0))
    out_spec = pl.BlockSpec((1, 2, SLABS, t_new, LANES), lambda b: (0, 0, 0, b, 0))
    out_sds = jax.ShapeDtypeStruct((1, 2, SLABS, batch * t_new, LANES), F32)
    return dict(
        batch=batch,
        args=[*qkv, *caches, *buckets, table],
        out_shape=[out_sds] * N_GROUPS + [jax.ShapeDtypeStruct(c.shape, F32) for c in caches],
        in_specs=[new_spec] * N_GROUPS + [pl.BlockSpec(memory_space=pl.ANY)] * N_GROUPS
        + [_resident(b.shape) for b in buckets] + [_SMEM],
        out_specs=[out_spec] * N_GROUPS + [cache_spec(p) for p in pasts],
        scratch_shapes=[pltpu.VMEM((HEADS_PER_GROUP * t_new, p + LANES), F32) for p in pasts] + window_ring
        + [pltpu.SemaphoreType.DMA((N_GROUPS, WINDOW_RING))])


def _sample_attention(parts):
    return pl.pallas_call(
        _sample_attn_kernel, grid=(parts["batch"],), out_shape=parts["out_shape"],
        in_specs=parts["in_specs"], out_specs=parts["out_specs"], scratch_shapes=parts["scratch_shapes"],
        compiler_params=_params(), name="sample_attn")(*parts["args"])


def _back_kernel(x_ref, ol0_ref, ol1_ref, ol2_ref, u_ref, halo_ref, gate_ref,
                 wab_ref, wpg_ref, pscale_ref, wpb_ref, wout_ref, g2_ref, wup_ref, wdn_ref,
                 out_ref, ue_ref, level_ref, mixed_ref, act_ref, attn_ref, pooled_ref, merged_ref,
                 *, pos_base, blocks_per_seq):
    block = pl.program_id(0)
    tm, d_model = x_ref.shape
    n_seq = halo_ref.shape[0]
    t = tm // n_seq
    pool_width = u_ref.shape[1]
    gw = pool_width // len(POOL_WINDOWS)

    for s in range(SLABS):
        ol_refs = (ol0_ref, ol1_ref, ol2_ref)
        lses = [r[0, 1, s] for r in ol_refs]
        top = jnp.maximum(jnp.maximum(lses[0], lses[1]), lses[2])
        num = jnp.zeros_like(top)
        den = jnp.zeros_like(top)
        for ol_ref, lse in zip(ol_refs, lses):
            e = jnp.exp2(lse - top)
            num = num + e * ol_ref[0, 0, s]
            den = den + e
        attn_ref[:, s * LANES:(s + 1) * LANES] = (num / den).astype(BF16)

    start = pos_base + (block % blocks_per_seq) * t
    first, end = POOL_PAD - POOL_HALO, POOL_PAD + t
    ue_ref[:, 0:first, :] = jnp.zeros((n_seq, first, pool_width), F32)
    ue_ref[:, first:POOL_PAD, :] = jnp.where(start > 0, halo_ref[...], 0.0)
    ue_ref[:, POOL_PAD:, :] = u_ref[...].reshape(n_seq, t, pool_width)
    level_ref[:, :, 0:first, :] = jnp.zeros((2, n_seq, first, gw), F32)
    pos = start + lax.broadcasted_iota(jnp.int32, (1, t, gw), 1)
    for gi, win in enumerate(POOL_WINDOWS):
        cols = slice(gi * gw, (gi + 1) * gw)
        read = lambda lo, hi, cols=cols: ue_ref[:, lo:hi, cols]
        span = 1
        while 2 * span < win:
            buf = (span.bit_length() - 1) % 2
            level_ref[buf, :, first:, :] = read(first, end) + read(first - span, end - span)
            read = lambda lo, hi, buf=buf: level_ref[buf, :, lo:hi, :]
            span *= 2
        s = read(POOL_PAD, end) + read(POOL_PAD - span, end - span)
        cnt = jnp.minimum(pos + 1, win).astype(F32)
        d = (s / cnt - ue_ref[:, POOL_PAD:, cols]).reshape(tm, gw)
        y = jnp.dot(d.astype(BF16), wpg_ref[gi], preferred_element_type=F32)
        pooled_ref[:, cols] = (y * pscale_ref[:, cols]).astype(BF16)

    for lo in range(0, d_model, MXU_TILE):
        cols = slice(lo, lo + MXU_TILE)
        branch_a = jnp.dot(attn_ref[...], wab_ref[:, cols], preferred_element_type=F32)
        branch_b = jnp.dot(pooled_ref[...], wpb_ref[:, cols], preferred_element_type=F32)
        gate_a = gate_ref[:, cols].astype(F32)
        gate_b = gate_ref[:, d_model + lo:d_model + lo + MXU_TILE].astype(F32)
        merged_ref[:, cols] = (gate_a * branch_a + gate_b * branch_b).astype(BF16)
    for lo in range(0, d_model, MXU_TILE):
        cols = slice(lo, lo + MXU_TILE)
        mixed_ref[:, cols] = x_ref[:, cols] + jnp.dot(merged_ref[...], wout_ref[:, cols],
                                                      preferred_element_type=F32)

    _ffn_block(mixed_ref, g2_ref, wup_ref, wdn_ref, out_ref, act_ref)


def _back(x, ol, u, halo, gates, merge_w, ffn_w, *, tm, halo_block, halo_index, pos_base, blocks_per_seq):
    m, d = x.shape
    pool_width = u.shape[1]
    n_seq = halo_block[0]
    d_ff = ffn_w[2].shape[0]
    row = lambda wd: pl.BlockSpec((tm, wd), lambda i: (i, 0))
    slab = pl.BlockSpec((1, 2, SLABS, tm, LANES),
                        lambda i: (i // blocks_per_seq, 0, 0, i % blocks_per_seq, 0))
    return pl.pallas_call(
        functools.partial(_back_kernel, pos_base=pos_base, blocks_per_seq=blocks_per_seq),
        out_shape=jax.ShapeDtypeStruct((m, d), F32),
        grid=(m // tm,),
        in_specs=[row(d)] + [slab] * N_GROUPS
        + [row(pool_width), pl.BlockSpec(halo_block, halo_index), row(2 * d)]
        + [_resident(a.shape) for a in (*merge_w, *ffn_w)],
        out_specs=row(d),
        scratch_shapes=[pltpu.VMEM((n_seq, POOL_PAD + tm // n_seq, pool_width), F32),
                        pltpu.VMEM((2, n_seq, POOL_PAD + tm // n_seq, pool_width // len(POOL_WINDOWS)), F32),
                        pltpu.VMEM((tm, d), F32), pltpu.VMEM((tm, d_ff), BF16),
                        pltpu.VMEM((tm, GROUP_WIDTH), BF16), pltpu.VMEM((tm, pool_width), BF16),
                        pltpu.VMEM((tm, d), BF16)],
        compiler_params=_params(),
        name="back",
    )(x, *ol, u, halo, gates, *merge_w, *ffn_w)


def _window_in(cache):
    n_seq, rows = cache.shape[0:2]
    return jnp.transpose(cache, (0, 2, 3, 4, 1)).reshape(n_seq, 2, GROUP_WIDTH, rows)


def _window_out(kv):
    n_seq, _, _, rows = kv.shape
    return jnp.transpose(kv.reshape(n_seq, 2, HEADS_PER_GROUP, HEAD_DIM, rows), (0, 4, 1, 2, 3))


def kernel(x_prompt, x_sample, cache_kv_w128, cache_kv_w512, cache_kv_w2048, state_pool, rel_bias_table,
           norm_ffn1, ffn1_w_up, ffn1_w_down, norm_mix, w_in, q_norm, k_norm, pool_w_group, pool_scale,
           w_attn_branch, w_pool_branch, w_out, norm_ffn2, ffn2_w_up, ffn2_w_down):
    batch, seq, d_model = x_prompt.shape
    dec_batch, dec_seq, _ = x_sample.shape
    depth = norm_ffn1.shape[0]
    pool_width = state_pool.shape[-1]
    tm = ROW_BLOCK
    ms = dec_batch * dec_seq
    seg = jnp.asarray(np.kron(np.eye(HEADS_PER_GROUP), np.full((HEAD_DIM, HEAD_DIM), 1.0 / HEAD_DIM)), BF16)
    table = rel_bias_table.astype(F32)
    caches_in = (cache_kv_w128, cache_kv_w512, cache_kv_w2048)
    keep_prompt = tuple(min(win, seq) for win, _ in ATTN_GROUPS)
    dils = tuple(dil for _, dil in ATTN_GROUPS)

    xp = x_prompt.reshape(batch * seq, d_model)
    xs = x_sample.reshape(ms, d_model)
    kv_p, kv_s, pool_p, pool_s = ([], [], []), ([], [], []), [], []
    for layer in range(depth):
        gain = lambda a: a[layer].reshape(1, -1).astype(F32)
        qg, kg = gain(q_norm), gain(k_norm)
        pool_groups = pool_w_group[layer]
        late_w = (ffn2_w_down[layer], w_attn_branch[layer], pool_groups.reshape(-1, pool_groups.shape[-1]),
                  w_pool_branch[layer], w_out[layer])

        xs, (wup1, wdn1) = _ffn(xs, gain(norm_ffn1), ffn1_w_up[layer], ffn1_w_down[layer], ms, narrow=True)
        outs = _proj(xs, gain(norm_mix), w_in[layer], qg, kg, seg, tm=ms, n_seq=1, dils=(1,) * N_GROUPS,
                     qkv_dtype=F32, keep_rows=None, narrow=True)
        qkv_s, u_s, gates_s, win = outs[0:N_GROUPS], outs[N_GROUPS], outs[N_GROUPS + 1], outs[N_GROUPS + 2]
        sample_parts = _sample_attention_parts(qkv_s, [_window_in(c[layer]) for c in caches_in], table)

        if (batch * seq) // tm == dec_batch:
            xp, sample_outs = _ffn(xp, gain(norm_ffn1), wup1, wdn1, tm, rider=sample_parts)
        else:
            xp, _ = _ffn(xp, gain(norm_ffn1), wup1, wdn1, tm)
            sample_outs = _sample_attention(sample_parts)
        outs = _proj(xp, gain(norm_mix), win, qg, kg, seg, tm=2 * tm, n_seq=batch, dils=dils,
                     qkv_dtype=BF16, keep_rows=keep_prompt)
        qkv, u, gates = outs[0:N_GROUPS], outs[N_GROUPS], outs[N_GROUPS + 1]
        windows, pstate = outs[N_GROUPS + 2:2 * N_GROUPS + 2], outs[2 * N_GROUPS + 2]
        narrowing = ((ffn2_w_up[layer],), late_w) + ((),) * (N_GROUPS - 2)
        ol, narrowed = zip(*[_prompt_attention(qkv[g], table, g, to_narrow=narrowing[g])
                             for g in range(N_GROUPS)])
        (wup2,), (wdn2, wab, wpg, wpb, wout) = narrowed[0:2]
        ffn2_w = (gain(norm_ffn2), wup2, wdn2)
        merge_w = (wab, wpg.reshape(pool_groups.shape), gain(pool_scale), wpb, wout)
        xp = _back(xp, ol, u, u.reshape(-1, POOL_HALO, pool_width), gates, merge_w, ffn2_w, tm=tm,
                   halo_block=(1, POOL_HALO, pool_width),
                   halo_index=lambda blk: (jnp.maximum(blk * (tm // POOL_HALO) - 1, 0), 0, 0),
                   pos_base=0, blocks_per_seq=seq // tm)
        for g in range(N_GROUPS):
            kv_p[g].append(_window_out(windows[g]))
        pool_p.append(pstate[:, POOL_HALO - POOL_STATE:])

        ol, windows = sample_outs[0:N_GROUPS], sample_outs[N_GROUPS:2 * N_GROUPS]
        history = jnp.concatenate(
            [jnp.zeros((dec_batch, POOL_HALO - POOL_STATE, pool_width), F32), state_pool[layer]], axis=1)
        xs = _back(xs, ol, u_s, history, gates_s, merge_w, ffn2_w, tm=ms,
                   halo_block=(dec_batch, POOL_HALO, pool_width), halo_index=lambda blk: (0, 0, 0),
                   pos_base=PAST_LEN, blocks_per_seq=1)
        for g in range(N_GROUPS):
            kv_s[g].append(_window_out(windows[g]))
        ue = jnp.concatenate([state_pool[layer], u_s.reshape(dec_batch, dec_seq, pool_width)], axis=1)
        pool_s.append(ue[:, ue.shape[1] - POOL_STATE:])

    stack = lambda xs_: jnp.stack(xs_, axis=0)
    return (xp.reshape(batch, seq, d_model), xs.reshape(dec_batch, dec_seq, d_model),
            stack(kv_p[0]), stack(kv_p[1]), stack(kv_p[2]), stack(pool_p),
            stack(kv_s[0]), stack(kv_s[1]), stack(kv_s[2]), stack(pool_s))
```

```python
import functools
import math

import numpy as np
import jax
import jax.numpy as jnp
from jax import lax
from jax.experimental import pallas as pl
from jax.experimental.pallas import tpu as pltpu

HEAD_DIM = 64
HEADS_PER_GROUP = 4
GROUP_WIDTH = HEADS_PER_GROUP * HEAD_DIM
ATTN_GROUPS = ((128, 1), (512, 4), (2048, 16))
N_GROUPS = len(ATTN_GROUPS)
ATTN_WIDTH = N_GROUPS * GROUP_WIDTH
N_BUCKETS = 32
MAX_DISTANCE = 2048
POOL_WINDOWS = (2, 4, 8, 16)
POOL_STATE = max(POOL_WINDOWS) - 1
POOL_HALO = 16
POOL_PAD = POOL_HALO + 8
EPS = 1e-6
PAST_LEN = 8192
NEG = -1e30
LOG2E = math.log2(math.e)
LANES = 128
MXU_TILE = 256
SLABS = GROUP_WIDTH // LANES
SINGLE_OP_STRIDE = 4
WINDOW_RING = 3
RIDER_PHASE_GAP = 2
ROW_BLOCK = 512
QBLK = 128
ATTN_UNROLL = 16
VMEM_LIMIT = 58 * 1024 * 1024

F32 = jnp.float32
BF16 = jnp.bfloat16
NT_DIMS = (((1,), (1,)), ((), ()))


def _t5_buckets(distance):
    max_exact = N_BUCKETS // 2
    d = np.asarray(distance, dtype=np.int32)
    ratio = np.log(np.maximum(d, 1).astype(np.float32) / np.float32(max_exact))
    large = max_exact + (ratio / np.float32(math.log(MAX_DISTANCE / max_exact))
                         * (N_BUCKETS - max_exact)).astype(np.int32)
    large = np.minimum(large, N_BUCKETS - 1)
    return np.where(d < max_exact, d, large).astype(np.int32)


def _params(n_grid_dims=1):
    return pltpu.CompilerParams(dimension_semantics=("arbitrary",) * n_grid_dims,
                                vmem_limit_bytes=VMEM_LIMIT)


def _resident(shape):
    return pl.BlockSpec(shape, lambda *_: (0,) * len(shape), pipeline_mode=pl.Buffered(1))


_SMEM = pl.BlockSpec(memory_space=pltpu.SMEM)


def _rmsnorm(x, g):
    ms = jnp.mean(x * x, axis=-1, keepdims=True)
    return x * lax.rsqrt(ms + EPS) * g


def _head_select(parts, lane_head):
    out = jnp.where(lane_head == 0, parts[0], 0.0)
    for h in range(1, HEADS_PER_GROUP):
        out = jnp.where(lane_head == h, parts[h], out)
    return out


def _stack_heads(q, lane_head):
    return jnp.concatenate([jnp.where(lane_head == h, q, 0.0)
                            for h in range(HEADS_PER_GROUP)], axis=0).astype(BF16)


def _build_bias(bias_ref, bucket_ref, table_ref, g):
    buckets = bucket_ref[...]
    rows = buckets.shape[0]
    for h in range(HEADS_PER_GROUP):
        acc = jnp.full(buckets.shape, NEG, F32)
        for b in range(N_BUCKETS):
            acc = jnp.where(buckets == b, table_ref[b, g * HEADS_PER_GROUP + h] * LOG2E, acc)
        bias_ref[h * rows:(h + 1) * rows, :] = acc


def _matmul_weight(w_ref, narrow_ref, rows, cols):
    w = w_ref[rows, cols]
    if narrow_ref is not None:
        w = w.astype(BF16)
        narrow_ref[rows, cols] = w
    return w


def _ffn_block(x_ref, g_ref, wup_ref, wdn_ref, o_ref, act_ref, narrow_refs=(None, None), rider_phases=()):
    d_ff, d = wdn_ref.shape
    every = slice(None)
    rider_phases = iter(rider_phases)
    h = _rmsnorm(x_ref[...], g_ref[...]).astype(BF16)
    next(rider_phases, None)
    for chunk, lo in enumerate(range(0, d_ff, MXU_TILE)):
        w_gate = _matmul_weight(wup_ref, narrow_refs[0], every, slice(lo, lo + MXU_TILE))
        w_up = _matmul_weight(wup_ref, narrow_refs[0], every, slice(d_ff + lo, d_ff + lo + MXU_TILE))
        gate = jnp.dot(h, w_gate, preferred_element_type=F32)
        up = jnp.dot(h, w_up, preferred_element_type=F32)
        if chunk % RIDER_PHASE_GAP == RIDER_PHASE_GAP - 1:
            next(rider_phases, None)
        act_ref[:, lo:lo + MXU_TILE] = (gate * jax.nn.sigmoid(gate) * up).astype(BF16)
    for lo in range(0, d, MXU_TILE):
        cols = slice(lo, lo + MXU_TILE)
        w_down = _matmul_weight(wdn_ref, narrow_refs[1], every, cols)
        o_ref[:, cols] = x_ref[:, cols] + 0.5 * jnp.dot(act_ref[...], w_down, preferred_element_type=F32)


def _ffn_kernel(*refs, rider_counts, narrow):
    n_in, n_out = rider_counts
    n_narrow = 2 if narrow else 0
    x_ref, g_ref, wup_ref, wdn_ref = refs[0:4]
    o_ref = refs[4 + n_in]
    narrow_refs = refs[5 + n_in:5 + n_in + n_narrow] if narrow else (None, None)
    first_rider_out = 5 + n_in + n_narrow
    act_ref = refs[first_rider_out + n_out]
    rider_phases = ()
    if n_in:
        rider_phases = _sample_attn_phases((*refs[4:4 + n_in], *refs[first_rider_out:first_rider_out + n_out],
                                            *refs[first_rider_out + n_out + 1:]))
    _ffn_block(x_ref, g_ref, wup_ref, wdn_ref, o_ref, act_ref, narrow_refs, rider_phases)


def _ffn(x, gain, w_up, w_down, tm, rider=None, narrow=False):
    m, d = x.shape
    d_ff = w_down.shape[0]
    assert m % tm == 0 and d_ff % MXU_TILE == 0 and d % MXU_TILE == 0
    assert not narrow or m == tm, "each weight chunk is narrowed once only on a single-step grid"
    rider = rider or dict(args=[], in_specs=[], out_shape=[], out_specs=[], scratch_shapes=[], batch=m // tm)
    assert rider["batch"] == m // tm
    weights = (w_up, w_down)
    narrowed = [jax.ShapeDtypeStruct(w.shape, BF16) for w in weights] if narrow else []
    outs = pl.pallas_call(
        functools.partial(_ffn_kernel, rider_counts=(len(rider["args"]), len(rider["out_shape"])), narrow=narrow),
        out_shape=[jax.ShapeDtypeStruct((m, d), F32)] + narrowed + rider["out_shape"],
        grid=(m // tm,),
        in_specs=[pl.BlockSpec((tm, d), lambda i: (i, 0)),
                  _resident((1, d)), _resident((d, 2 * d_ff)), _resident((d_ff, d))] + rider["in_specs"],
        out_specs=[pl.BlockSpec((tm, d), lambda i: (i, 0))] + [_resident(w.shape) for w in narrowed]
        + rider["out_specs"],
        scratch_shapes=[pltpu.VMEM((tm, d_ff), BF16)] + rider["scratch_shapes"],
        compiler_params=_params(),
        name="ffn",
    )(x, gain, w_up, w_down, *rider["args"])
    return outs[0], outs[1:]


def _proj_kernel(x_ref, g_ref, win_ref, qn_ref, kn_ref, seg_ref, *refs,
                 dils, kv_rows, with_state, narrow, pool_width, d_model):
    qkv_refs = refs[0:N_GROUPS]
    u_ref, gate_ref = refs[N_GROUPS:N_GROUPS + 2]
    state_refs = refs[N_GROUPS + 2:]
    narrow_ref = refs[-3] if narrow else None
    slab_ref, regroup_ref = refs[-2:]
    tm = x_ref.shape[0]
    h = _rmsnorm(x_ref[...], g_ref[...]).astype(BF16)

    def proj(lo, width):
        w = _matmul_weight(win_ref, narrow_ref, slice(None), slice(lo, lo + width))
        return jnp.dot(h, w, preferred_element_type=F32)

    def head_norm(y, gain):
        ms = jnp.dot((y * y).astype(BF16), seg_ref[...], preferred_element_type=F32)
        return y * lax.rsqrt(ms + EPS) * gain

    def emit(dst_ref, which, slot, val, dil):
        if dil == 1:
            dst_ref[0, 0, which] = val.astype(dst_ref.dtype)
            return
        for s in range(SLABS):
            slab_ref[slot, s] = val[:, s * LANES:(s + 1) * LANES]
        n = tm // dil
        if dil <= SINGLE_OP_STRIDE:
            for r in range(dil):
                parts = [slab_ref[slot, s, pl.ds(r, n, stride=dil), :] for s in range(SLABS)]
                dst_ref[0, r, which] = jnp.concatenate(parts, axis=-1).astype(dst_ref.dtype)
            return
        outer = SINGLE_OP_STRIDE
        inner = dil // outer
        assert inner <= SINGLE_OP_STRIDE
        quarter = tm // outer
        for a in range(outer):
            for s in range(SLABS):
                regroup_ref[s, a * quarter:(a + 1) * quarter, :] = slab_ref[slot, s, pl.ds(a, quarter, stride=outer), :]
        for r in range(dil):
            start = (r % outer) * quarter + r // outer
            parts = [regroup_ref[s, pl.ds(start, n, stride=inner), :] for s in range(SLABS)]
            dst_ref[0, r, which] = jnp.concatenate(parts, axis=-1).astype(dst_ref.dtype)

    def finish(g, which, y):
        cols = slice(g * GROUP_WIDTH, (g + 1) * GROUP_WIDTH)
        if which == 0:
            y = head_norm(y, qn_ref[:, cols]) * (HEAD_DIM ** -0.5 * LOG2E)
        elif which == 1:
            y = head_norm(y, kn_ref[:, cols])
        emit(qkv_refs[g], which, 3 * sum(d > 1 for d in dils[:g]) + which, y, dils[g])
        if with_state and which > 0:
            state_refs[g][0, which - 1] = y.T[:, tm - kv_rows[g]:]

    pending = None
    for g in range(N_GROUPS):
        for which in range(3):
            y = proj(which * ATTN_WIDTH + g * GROUP_WIDTH, GROUP_WIDTH)
            if pending is not None:
                finish(*pending)
            pending = (g, which, y)
    u = proj(3 * ATTN_WIDTH, pool_width)
    finish(*pending)
    u_ref[...] = u
    if with_state:
        state_refs[N_GROUPS][0] = u[tm - POOL_HALO:, :]
    gates_base = 3 * ATTN_WIDTH + pool_width
    for lo in range(0, 2 * d_model, MXU_TILE):
        gate_ref[:, lo:lo + MXU_TILE] = jax.nn.sigmoid(proj(gates_base + lo, MXU_TILE)).astype(BF16)


def _proj(x, gain, w_in, q_gain, k_gain, seg, *, tm, n_seq, dils, qkv_dtype, keep_rows, narrow=False):
    m, d = x.shape
    pool_width = w_in.shape[1] - 3 * ATTN_WIDTH - 2 * d
    seq_len = m // n_seq
    assert seq_len % tm == 0 and all(tm % dil == 0 for dil in dils)
    blocks_per_seq = seq_len // tm
    with_state = keep_rows is not None
    seq_block = lambda i: (i // blocks_per_seq, i % blocks_per_seq)

    row = lambda w: pl.BlockSpec((tm, w), lambda i: (i, 0))
    out_shape, out_specs = [], []
    for g in range(N_GROUPS):
        out_shape.append(jax.ShapeDtypeStruct((n_seq, dils[g], 3, seq_len // dils[g], GROUP_WIDTH), qkv_dtype))
        out_specs.append(pl.BlockSpec((1, dils[g], 3, tm // dils[g], GROUP_WIDTH),
                                      lambda i: (seq_block(i)[0], 0, 0, seq_block(i)[1], 0)))
    out_shape += [jax.ShapeDtypeStruct((m, pool_width), F32), jax.ShapeDtypeStruct((m, 2 * d), BF16)]
    out_specs += [row(pool_width), row(2 * d)]
    kv_rows = kv_first = ()
    if with_state:
        kv_rows = tuple(min(r, tm) for r in keep_rows)
        kv_first = tuple(blocks_per_seq - keep_rows[g] // kv_rows[g] for g in range(N_GROUPS))
        for g in range(N_GROUPS):
            out_shape.append(jax.ShapeDtypeStruct((n_seq, 2, GROUP_WIDTH, keep_rows[g]), F32))
            out_specs.append(pl.BlockSpec(
                (1, 2, GROUP_WIDTH, kv_rows[g]),
                lambda i, first=kv_first[g]: (seq_block(i)[0], 0, 0, jnp.maximum(seq_block(i)[1] - first, 0))))
        out_shape.append(jax.ShapeDtypeStruct((n_seq, POOL_HALO, pool_width), F32))
        out_specs.append(pl.BlockSpec((1, POOL_HALO, pool_width), lambda i: (seq_block(i)[0], 0, 0)))
    if narrow:
        assert m == tm, "each weight chunk is narrowed once only on a single-step grid"
        out_shape.append(jax.ShapeDtypeStruct(w_in.shape, BF16))
        out_specs.append(_resident(w_in.shape))
    return pl.pallas_call(
        functools.partial(_proj_kernel, dils=dils, kv_rows=kv_rows, with_state=with_state, narrow=narrow,
                          pool_width=pool_width, d_model=d),
        out_shape=out_shape,
        grid=(m // tm,),
        in_specs=[row(d), _resident((1, d)), _resident(w_in.shape),
                  _resident((1, ATTN_WIDTH)), _resident((1, ATTN_WIDTH)),
                  _resident((GROUP_WIDTH, GROUP_WIDTH))],
        out_specs=out_specs,
        scratch_shapes=[pltpu.VMEM((max(1, 3 * sum(dil > 1 for dil in dils)), SLABS, tm, LANES), F32),
                        pltpu.VMEM((SLABS, tm, LANES), F32)],
        compiler_params=_params(),
        name="proj",
    )(x, gain, w_in, q_gain, k_gain, seg)


def _attn_kernel(*refs, g, dil, n_blocks, unroll, n_cast):
    qkv_ref, bucket_ref, hmask_ref, table_ref = refs[0:4]
    wide_refs = refs[4:4 + n_cast]
    out_ref = refs[4 + n_cast]
    narrow_refs = refs[5 + n_cast:5 + 2 * n_cast]
    bias_ref = refs[5 + 2 * n_cast]
    for wide_ref, narrow_ref in zip(wide_refs, narrow_refs):
        narrow_ref[...] = wide_ref[...].astype(BF16)

    @pl.when(pl.program_id(0) == 0)
    def _():
        _build_bias(bias_ref.at[1], bucket_ref, table_ref, g)
        bias_ref[0, :, 0:QBLK] = bias_ref[1, :, QBLK:2 * QBLK]
        bias_ref[0, :, QBLK:2 * QBLK] = jnp.full((HEADS_PER_GROUP * QBLK, QBLK), NEG, F32)

    first_head = lax.broadcasted_iota(jnp.int32, (QBLK, LANES), 1) < HEAD_DIM
    heads = range(HEADS_PER_GROUP)

    def per_slab(col):
        part = lambda h: jnp.broadcast_to(col[h * QBLK:(h + 1) * QBLK], (QBLK, LANES))
        return [jnp.where(first_head, part(2 * s), part(2 * s + 1)) for s in range(SLABS)]

    def body(n, carry):
        res = n // n_blocks
        i = n % n_blocks
        row0 = pl.multiple_of(i * QBLK, QBLK)
        span0 = pl.multiple_of(jnp.maximum(i - 1, 0) * QBLK, QBLK)
        q = qkv_ref[0, res, 0, pl.ds(row0, QBLK), :]
        qs = jnp.concatenate([q * hmask_ref[h] for h in heads], axis=0)
        s = lax.dot_general(qs, qkv_ref[0, res, 1, pl.ds(span0, 2 * QBLK), :], NT_DIMS,
                            preferred_element_type=F32) + bias_ref[jnp.minimum(i, 1)]
        m = jnp.max(s, axis=-1, keepdims=True)
        p = jnp.exp2(s - m)
        l = jnp.sum(p, axis=-1, keepdims=True)
        p = p.astype(BF16)
        v = qkv_ref[0, res, 2, pl.ds(span0, 2 * QBLK), :]
        p_wide = jnp.concatenate([p[h * QBLK:(h + 1) * QBLK] for h in heads], axis=1)
        v_tall = jnp.concatenate([v * hmask_ref[h] for h in heads], axis=0)
        pv = jnp.dot(p_wide, v_tall, preferred_element_type=F32)
        token0 = res + dil * row0
        dst = pl.ds(token0, QBLK) if dil == 1 else pl.ds(token0, QBLK, stride=dil)
        for s_, (m_s, l_s) in enumerate(zip(per_slab(m), per_slab(l))):
            out_ref[0, 0, s_, dst, :] = pv[:, s_ * LANES:(s_ + 1) * LANES] / l_s
            out_ref[0, 1, s_, dst, :] = m_s + jnp.log2(l_s)
        return carry

    lax.fori_loop(0, dil * n_blocks, body, 0, unroll=unroll)


def _prompt_buckets(g):
    win, dil = ATTN_GROUPS[g]
    delta = np.arange(QBLK)[:, None] + QBLK - np.arange(2 * QBLK)[None, :]
    valid = (delta >= 0) & (delta <= win // dil)
    return np.where(valid, _t5_buckets(dil * np.clip(delta, 0, win // dil)), -1).astype(np.int32)


def _prompt_attention(qkv, table, g, to_narrow=()):
    _, dil = ATTN_GROUPS[g]
    batch, _, _, sub, _ = qkv.shape
    n_blocks = sub // QBLK
    assert sub % QBLK == 0 and n_blocks >= 2 and (dil * n_blocks) % ATTN_UNROLL == 0
    assert all(w.shape[0] % (16 * batch) == 0 for w in to_narrow)
    cast_specs = [pl.BlockSpec((w.shape[0] // batch, w.shape[1]), lambda b: (b, 0)) for w in to_narrow]
    head_of_lane = np.arange(GROUP_WIDTH) // HEAD_DIM
    hmask = head_of_lane[None, None, :] == np.arange(HEADS_PER_GROUP)[:, None, None]
    in_spec = pl.BlockSpec((1, dil, 3, sub, GROUP_WIDTH), lambda b: (b, 0, 0, 0, 0))
    out_spec = pl.BlockSpec((1, 2, SLABS, sub * dil, LANES), lambda b: (b, 0, 0, 0, 0))
    out_sds = jax.ShapeDtypeStruct((batch, 2, SLABS, sub * dil, LANES), F32)
    outs = pl.pallas_call(
        functools.partial(_attn_kernel, g=g, dil=dil, n_blocks=n_blocks, unroll=ATTN_UNROLL,
                          n_cast=len(to_narrow)),
        out_shape=[out_sds] + [jax.ShapeDtypeStruct(w.shape, BF16) for w in to_narrow],
        grid=(batch,),
        in_specs=[in_spec, _resident((QBLK, 2 * QBLK)), _resident(hmask.shape), _SMEM] + cast_specs,
        out_specs=[out_spec] + cast_specs,
        scratch_shapes=[pltpu.VMEM((2, HEADS_PER_GROUP * QBLK, 2 * QBLK), F32)],
        compiler_params=_params(),
        name=f"attn_g{g}",
    )(qkv, jnp.asarray(_prompt_buckets(g)), jnp.asarray(hmask, BF16), table, *to_narrow)
    return outs[0], outs[1:]


def _sample_attn_phases(refs):
    n = N_GROUPS
    qkv_refs, cache_refs, bucket_refs, table_ref = refs[0:n], refs[n:2 * n], refs[2 * n:3 * n], refs[3 * n]
    out_refs, win_refs, bias_refs = refs[3 * n + 1:4 * n + 1], refs[4 * n + 1:5 * n + 1], refs[5 * n + 1:6 * n + 1]
    ring_refs, sem_ref = refs[6 * n + 1:7 * n + 1], refs[7 * n + 1]
    step, n_steps = pl.program_id(0), pl.num_programs(0)

    def window_copy(g, seq):
        slot = seq % WINDOW_RING
        return pltpu.make_async_copy(cache_refs[g].at[seq], ring_refs[g].at[slot], sem_ref.at[g, slot])

    @pl.when(step == 0)
    def _():
        for g in range(N_GROUPS):
            _build_bias(bias_refs[g], bucket_refs[g], table_ref, g)
            for seq in range(WINDOW_RING - 1):
                window_copy(g, seq).start()

    @pl.when(step + WINDOW_RING - 1 < n_steps)
    def _():
        for g in range(N_GROUPS):
            window_copy(g, step + WINDOW_RING - 1).start()

    for g in range(N_GROUPS):
        window_copy(g, step).wait()
    slot = step % WINDOW_RING

    t_new = qkv_refs[0].shape[3]
    lane_head = lax.broadcasted_iota(jnp.int32, (t_new, GROUP_WIDTH), 1) // HEAD_DIM
    new_lane = lax.broadcasted_iota(jnp.int32, (GROUP_WIDTH, LANES), 1) >= LANES - t_new
    pad = jnp.zeros((LANES - t_new, GROUP_WIDTH), F32)

    groups = []
    for g in range(N_GROUPS):
        q, k, v = (qkv_refs[g][0, 0, which] for which in range(3))
        cache_ref, win_ref = ring_refs[g].at[slot], win_refs[g]
        past = cache_ref.shape[2]
        new_rows = [jnp.concatenate([pad, rows_], axis=0) for rows_ in (k, v)]
        for c in range(2):
            shifted = pltpu.roll(cache_ref[c], past - t_new, 1)
            if past > LANES:
                win_ref[0, c, :, 0:past - LANES] = shifted[:, 0:past - LANES]
            win_ref[0, c, :, past - LANES:past] = jnp.where(new_lane, new_rows[c].T, shifted[:, past - LANES:])
        groups.append(dict(past=past, qs=_stack_heads(q, lane_head),
                           old=[cache_ref[c].astype(BF16) for c in range(2)],
                           new=[rows_.astype(BF16) for rows_ in new_rows]))
    yield

    for g, grp in enumerate(groups):
        past, bias_ref = grp["past"], bias_refs[g]
        s_old = jnp.dot(grp["qs"], grp["old"][0], preferred_element_type=F32) + bias_ref[:, 0:past]
        s_new = lax.dot_general(grp["qs"], grp["new"][0], NT_DIMS,
                                preferred_element_type=F32) + bias_ref[:, past:past + LANES]
        m = jnp.maximum(jnp.max(s_old, axis=-1, keepdims=True), jnp.max(s_new, axis=-1, keepdims=True))
        p_old = jnp.exp2(s_old - m)
        p_new = jnp.exp2(s_new - m)
        grp.update(m=m, l=jnp.sum(p_old, axis=-1, keepdims=True) + jnp.sum(p_new, axis=-1, keepdims=True),
                   p_old=p_old.astype(BF16), p_new=p_new.astype(BF16))
    yield

    for g, grp in enumerate(groups):
        pv = (lax.dot_general(grp["p_old"], grp["old"][1], NT_DIMS, preferred_element_type=F32)
              + jnp.dot(grp["p_new"], grp["new"][1], preferred_element_type=F32)) / grp["l"]
        lse = jnp.broadcast_to(grp["m"] + jnp.log2(grp["l"]), pv.shape)
        rows = lambda a: [a[h * t_new:(h + 1) * t_new] for h in range(HEADS_PER_GROUP)]
        for which, val in enumerate((_head_select(rows(pv), lane_head), _head_select(rows(lse), lane_head))):
            for s_ in range(SLABS):
                out_refs[g][0, which, s_] = val[:, s_ * LANES:(s_ + 1) * LANES]


def _sample_attn_kernel(*refs):
    for _ in _sample_attn_phases(refs):
        pass


def _sample_buckets(g, past, t_new):
    win, dil = ATTN_GROUPS[g]
    lane = np.arange(past + LANES)[None, :]
    key_pos = np.where(lane < past, lane, lane - (LANES - t_new))
    delta = past + np.arange(t_new)[:, None] - key_pos
    valid = ((lane < past) | (lane >= past + LANES - t_new)) & (delta >= 0) & (delta % dil == 0) & (delta <= win)
    return np.where(valid, _t5_buckets(np.clip(delta, 0, win)), -1).astype(np.int32)


def _sample_attention_parts(qkv, caches, table):
    batch = caches[0].shape[0]
    assert batch >= WINDOW_RING
    t_new = qkv[0].shape[3] // batch
    pasts = [c.shape[3] for c in caches]
    window_ring = [pltpu.VMEM((WINDOW_RING,) + c.shape[1:], F32) for c in caches]
    buckets = [jnp.asarray(_sample_buckets(g, pasts[g], t_new)) for g in range(N_GROUPS)]
    cache_spec = lambda p: pl.BlockSpec((1, 2, GROUP_WIDTH, p), lambda b: (b, 0, 0, 0))
    new_spec = pl.BlockSpec((1, 1, 3, t_new, GROUP_WIDTH), lambda b: (0, 0, 0, b, 0))
    out_spec = pl.BlockSpec((1, 2, SLABS, t_new, LANES), lambda b: (0, 0, 0, b, 0))
    out_sds = jax.ShapeDtypeStruct((1, 2, SLABS, batch * t_new, LANES), F32)
    return dict(
        batch=batch,
        args=[*qkv, *caches, *buckets, table],
        out_shape=[out_sds] * N_GROUPS + [jax.ShapeDtypeStruct(c.shape, F32) for c in caches],
        in_specs=[new_spec] * N_GROUPS + [pl.BlockSpec(memory_space=pl.ANY)] * N_GROUPS
        + [_resident(b.shape) for b in buckets] + [_SMEM],
        out_specs=[out_spec] * N_GROUPS + [cache_spec(p) for p in pasts],
        scratch_shapes=[pltpu.VMEM((HEADS_PER_GROUP * t_new, p + LANES), F32) for p in pasts] + window_ring
        + [pltpu.SemaphoreType.DMA((N_GROUPS, WINDOW_RING))])


def _sample_attention(parts):
    return pl.pallas_call(
        _sample_attn_kernel, grid=(parts["batch"],), out_shape=parts["out_shape"],
        in_specs=parts["in_specs"], out_specs=parts["out_specs"], scratch_shapes=parts["scratch_shapes"],
        compiler_params=_params(), name="sample_attn")(*parts["args"])


def _back_kernel(x_ref, ol0_ref, ol1_ref, ol2_ref, u_ref, halo_ref, gate_ref,
                 wab_ref, wpg_ref, pscale_ref, wpb_ref, wout_ref, g2_ref, wup_ref, wdn_ref,
                 out_ref, ue_ref, level_ref, mixed_ref, act_ref, attn_ref, pooled_ref, merged_ref,
                 *, pos_base, blocks_per_seq):
    block = pl.program_id(0)
    tm, d_model = x_ref.shape
    n_seq = halo_ref.shape[0]
    t = tm // n_seq
    pool_width = u_ref.shape[1]
    gw = pool_width // len(POOL_WINDOWS)

    for s in range(SLABS):
        ol_refs = (ol0_ref, ol1_ref, ol2_ref)
        lses = [r[0, 1, s] for r in ol_refs]
        top = jnp.maximum(jnp.maximum(lses[0], lses[1]), lses[2])
        num = jnp.zeros_like(top)
        den = jnp.zeros_like(top)
        for ol_ref, lse in zip(ol_refs, lses):
            e = jnp.exp2(lse - top)
            num = num + e * ol_ref[0, 0, s]
            den = den + e
        attn_ref[:, s * LANES:(s + 1) * LANES] = (num / den).astype(BF16)

    start = pos_base + (block % blocks_per_seq) * t
    first, end = POOL_PAD - POOL_HALO, POOL_PAD + t
    ue_ref[:, 0:first, :] = jnp.zeros((n_seq, first, pool_width), F32)
    ue_ref[:, first:POOL_PAD, :] = jnp.where(start > 0, halo_ref[...], 0.0)
    ue_ref[:, POOL_PAD:, :] = u_ref[...].reshape(n_seq, t, pool_width)
    level_ref[:, :, 0:first, :] = jnp.zeros((2, n_seq, first, gw), F32)
    pos = start + lax.broadcasted_iota(jnp.int32, (1, t, gw), 1)
    for gi, win in enumerate(POOL_WINDOWS):
        cols = slice(gi * gw, (gi + 1) * gw)
        read = lambda lo, hi, cols=cols: ue_ref[:, lo:hi, cols]
        span = 1
        while 2 * span < win:
            buf = (span.bit_length() - 1) % 2
            level_ref[buf, :, first:, :] = read(first, end) + read(first - span, end - span)
            read = lambda lo, hi, buf=buf: level_ref[buf, :, lo:hi, :]
            span *= 2
        s = read(POOL_PAD, end) + read(POOL_PAD - span, end - span)
        cnt = jnp.minimum(pos + 1, win).astype(F32)
        d = (s / cnt - ue_ref[:, POOL_PAD:, cols]).reshape(tm, gw)
        y = jnp.dot(d.astype(BF16), wpg_ref[gi], preferred_element_type=F32)
        pooled_ref[:, cols] = (y * pscale_ref[:, cols]).astype(BF16)

    for lo in range(0, d_model, MXU_TILE):
        cols = slice(lo, lo + MXU_TILE)
        branch_a = jnp.dot(attn_ref[...], wab_ref[:, cols], preferred_element_type=F32)
        branch_b = jnp.dot(pooled_ref[...], wpb_ref[:, cols], preferred_element_type=F32)
        gate_a = gate_ref[:, cols].astype(F32)
        gate_b = gate_ref[:, d_model + lo:d_model + lo + MXU_TILE].astype(F32)
        merged_ref[:, cols] = (gate_a * branch_a + gate_b * branch_b).astype(BF16)
    for lo in range(0, d_model, MXU_TILE):
        cols = slice(lo, lo + MXU_TILE)
        mixed_ref[:, cols] = x_ref[:, cols] + jnp.dot(merged_ref[...], wout_ref[:, cols],
                                                      preferred_element_type=F32)

    _ffn_block(mixed_ref, g2_ref, wup_ref, wdn_ref, out_ref, act_ref)


def _back(x, ol, u, halo, gates, merge_w, ffn_w, *, tm, halo_block, halo_index, pos_base, blocks_per_seq):
    m, d = x.shape
    pool_width = u.shape[1]
    n_seq = halo_block[0]
    d_ff = ffn_w[2].shape[0]
    row = lambda wd: pl.BlockSpec((tm, wd), lambda i: (i, 0))
    slab = pl.BlockSpec((1, 2, SLABS, tm, LANES),
                        lambda i: (i // blocks_per_seq, 0, 0, i % blocks_per_seq, 0))
    return pl.pallas_call(
        functools.partial(_back_kernel, pos_base=pos_base, blocks_per_seq=blocks_per_seq),
        out_shape=jax.ShapeDtypeStruct((m, d), F32),
        grid=(m // tm,),
        in_specs=[row(d)] + [slab] * N_GROUPS
        + [row(pool_width), pl.BlockSpec(halo_block, halo_index), row(2 * d)]
        + [_resident(a.shape) for a in (*merge_w, *ffn_w)],
        out_specs=row(d),
        scratch_shapes=[pltpu.VMEM((n_seq, POOL_PAD + tm // n_seq, pool_width), F32),
                        pltpu.VMEM((2, n_seq, POOL_PAD + tm // n_seq, pool_width // len(POOL_WINDOWS)), F32),
                        pltpu.VMEM((tm, d), F32), pltpu.VMEM((tm, d_ff), BF16),
                        pltpu.VMEM((tm, GROUP_WIDTH), BF16), pltpu.VMEM((tm, pool_width), BF16),
                        pltpu.VMEM((tm, d), BF16)],
        compiler_params=_params(),
        name="back",
    )(x, *ol, u, halo, gates, *merge_w, *ffn_w)


def _window_in(cache):
    n_seq, rows = cache.shape[0:2]
    return jnp.transpose(cache, (0, 2, 3, 4, 1)).reshape(n_seq, 2, GROUP_WIDTH, rows)


def _window_out(kv):
    n_seq, _, _, rows = kv.shape
    return jnp.transpose(kv.reshape(n_seq, 2, HEADS_PER_GROUP, HEAD_DIM, rows), (0, 4, 1, 2, 3))


def kernel(x_prompt, x_sample, cache_kv_w128, cache_kv_w512, cache_kv_w2048, state_pool, rel_bias_table,
           norm_ffn1, ffn1_w_up, ffn1_w_down, norm_mix, w_in, q_norm, k_norm, pool_w_group, pool_scale,
           w_attn_branch, w_pool_branch, w_out, norm_ffn2, ffn2_w_up, ffn2_w_down):
    batch, seq, d_model = x_prompt.shape
    dec_batch, dec_seq, _ = x_sample.shape
    depth = norm_ffn1.shape[0]
    pool_width = state_pool.shape[-1]
    tm = ROW_BLOCK
    ms = dec_batch * dec_seq
    seg = jnp.asarray(np.kron(np.eye(HEADS_PER_GROUP), np.full((HEAD_DIM, HEAD_DIM), 1.0 / HEAD_DIM)), BF16)
    table = rel_bias_table.astype(F32)
    caches_in = (cache_kv_w128, cache_kv_w512, cache_kv_w2048)
    keep_prompt = tuple(min(win, seq) for win, _ in ATTN_GROUPS)
    dils = tuple(dil for _, dil in ATTN_GROUPS)

    xp = x_prompt.reshape(batch * seq, d_model)
    xs = x_sample.reshape(ms, d_model)
    kv_p, kv_s, pool_p, pool_s = ([], [], []), ([], [], []), [], []
    for layer in range(depth):
        gain = lambda a: a[layer].reshape(1, -1).astype(F32)
        qg, kg = gain(q_norm), gain(k_norm)
        pool_groups = pool_w_group[layer]
        late_w = (ffn2_w_down[layer], w_attn_branch[layer], pool_groups.reshape(-1, pool_groups.shape[-1]),
                  w_pool_branch[layer], w_out[layer])

        xs, (wup1, wdn1) = _ffn(xs, gain(norm_ffn1), ffn1_w_up[layer], ffn1_w_down[layer], ms, narrow=True)
        outs = _proj(xs, gain(norm_mix), w_in[layer], qg, kg, seg, tm=ms, n_seq=1, dils=(1,) * N_GROUPS,
                     qkv_dtype=F32, keep_rows=None, narrow=True)
        qkv_s, u_s, gates_s, win = outs[0:N_GROUPS], outs[N_GROUPS], outs[N_GROUPS + 1], outs[N_GROUPS + 2]
        sample_parts = _sample_attention_parts(qkv_s, [_window_in(c[layer]) for c in caches_in], table)

        if (batch * seq) // tm == dec_batch:
            xp, sample_outs = _ffn(xp, gain(norm_ffn1), wup1, wdn1, tm, rider=sample_parts)
        else:
            xp, _ = _ffn(xp, gain(norm_ffn1), wup1, wdn1, tm)
            sample_outs = _sample_attention(sample_parts)
        outs = _proj(xp, gain(norm_mix), win, qg, kg, seg, tm=2 * tm, n_seq=batch, dils=dils,
                     qkv_dtype=BF16, keep_rows=keep_prompt)
        qkv, u, gates = outs[0:N_GROUPS], outs[N_GROUPS], outs[N_GROUPS + 1]
        windows, pstate = outs[N_GROUPS + 2:2 * N_GROUPS + 2], outs[2 * N_GROUPS + 2]
        narrowing = ((ffn2_w_up[layer],), late_w) + ((),) * (N_GROUPS - 2)
        ol, narrowed = zip(*[_prompt_attention(qkv[g], table, g, to_narrow=narrowing[g])
                             for g in range(N_GROUPS)])
        (wup2,), (wdn2, wab, wpg, wpb, wout) = narrowed[0:2]
        ffn2_w = (gain(norm_ffn2), wup2, wdn2)
        merge_w = (wab, wpg.reshape(pool_groups.shape), gain(pool_scale), wpb, wout)
        xp = _back(xp, ol, u, u.reshape(-1, POOL_HALO, pool_width), gates, merge_w, ffn2_w, tm=tm,
                   halo_block=(1, POOL_HALO, pool_width),
                   halo_index=lambda blk: (jnp.maximum(blk * (tm // POOL_HALO) - 1, 0), 0, 0),
                   pos_base=0, blocks_per_seq=seq // tm)
        for g in range(N_GROUPS):
            kv_p[g].append(_window_out(windows[g]))
        pool_p.append(pstate[:, POOL_HALO - POOL_STATE:])

        ol, windows = sample_outs[0:N_GROUPS], sample_outs[N_GROUPS:2 * N_GROUPS]
        history = jnp.concatenate(
            [jnp.zeros((dec_batch, POOL_HALO - POOL_STATE, pool_width), F32), state_pool[layer]], axis=1)
        xs = _back(xs, ol, u_s, history, gates_s, merge_w, ffn2_w, tm=ms,
                   halo_block=(dec_batch, POOL_HALO, pool_width), halo_index=lambda blk: (0, 0, 0),
                   pos_base=PAST_LEN, blocks_per_seq=1)
        for g in range(N_GROUPS):
            kv_s[g].append(_window_out(windows[g]))
        ue = jnp.concatenate([state_pool[layer], u_s.reshape(dec_batch, dec_seq, pool_width)], axis=1)
        pool_s.append(ue[:, ue.shape[1] - POOL_STATE:])

    stack = lambda xs_: jnp.stack(xs_, axis=0)
    return (xp.reshape(batch, seq, d_model), xs.reshape(dec_batch, dec_seq, d_model),
            stack(kv_p[0]), stack(kv_p[1]), stack(kv_p[2]), stack(pool_p),
            stack(kv_s[0]), stack(kv_s[1]), stack(kv_s[2]), stack(pool_s))
```
